```python
import math
import jax, jax.numpy as jnp
from jax import lax
import numpy as np

D_MODEL = 1024
BATCH = 4
SEQ = 8192
DEPTH = 2

CHUNK = 64
N_A_LAYERS = DEPTH // 2
N_B_LAYERS = DEPTH - N_A_LAYERS
CONV_WIDTH = 31
N_HEADS = 8
QK_NOPE_DIM = 128
QK_ROPE_DIM = 64
V_HEAD_DIM = 128
Q_LORA_RANK = 384
KV_LORA_RANK = 256
ROPE_THETA = 10000.0
Q_BLOCK = 128
N_EXPERTS = 32
TOP_K = 4
D_EXPERT = 1024
SWIGLU_ALPHA = 1.702
SWIGLU_LIMIT = 7.0
EXPERT_BLOCK = 256
EPS = 1e-6

kernel_name = "yoco_conformer_mla_moe_adaln"


def rms_norm(x, g):
    xf = x.astype(jnp.float32)
    y = xf * lax.rsqrt(jnp.mean(xf * xf, axis=-1, keepdims=True) + EPS)
    return (y * g.astype(jnp.float32)).astype(x.dtype)


def layer_norm(x, g, b):
    xf = x.astype(jnp.float32)
    mu = jnp.mean(xf, axis=-1, keepdims=True)
    d = xf - mu
    var = jnp.mean(d * d, axis=-1, keepdims=True)
    return (d * lax.rsqrt(var + EPS) * g.astype(jnp.float32) + b.astype(jnp.float32)).astype(x.dtype)


def modulate(h, shift, scale):
    return h * (1 + scale[:, None, :]) + shift[:, None, :]


def rope_tables(positions):
    half = QK_ROPE_DIM // 2
    inv = jnp.exp(-(2.0 * math.log(ROPE_THETA) / QK_ROPE_DIM) * jnp.arange(half, dtype=jnp.float32))
    ang = positions.astype(jnp.float32)[..., None] * inv
    return jnp.cos(ang), jnp.sin(ang)


def apply_rope(x, cos, sin):
    half = x.shape[-1] // 2
    xf = x.astype(jnp.float32)
    x1, x2 = xf[..., :half], xf[..., half:]
    return jnp.concatenate([x1 * cos - x2 * sin, x2 * cos + x1 * sin], axis=-1).astype(x.dtype)


def conformer_conv(h, pw1_w, pw1_b, dw_w, dw_b, ln_g, ln_b, pw2_w, pw2_b):
    u = h @ pw1_w + pw1_b
    a, g = jnp.split(u, 2, axis=-1)
    u = a * jax.nn.sigmoid(g)
    u = lax.conv_general_dilated(
        u, dw_w[:, None, :].astype(u.dtype), window_strides=(1,),
        padding=[(CONV_WIDTH - 1, 0)], dimension_numbers=('NWC', 'WIO', 'NWC'),
        feature_group_count=D_MODEL) + dw_b
    u = jax.nn.silu(layer_norm(u, ln_g, ln_b))
    return u @ pw2_w + pw2_b


def mla_shared_kv(h, cos, sin, kv_norm_g, w_dkv, ckv_norm_g, w_uk, w_uv):
    hn = rms_norm(h, kv_norm_g)
    lat = hn @ w_dkv
    c_kv = rms_norm(lat[..., :KV_LORA_RANK], ckv_norm_g)
    k_rope = apply_rope(lat[..., KV_LORA_RANK:], cos, sin)
    k_nope = jnp.einsum('bsr,rhd->bshd', c_kv, w_uk)
    v = jnp.einsum('bsr,rhd->bshd', c_kv, w_uv)
    return k_nope, k_rope, v


def mla_attention(h, k_nope, k_rope, v, cos, sin, w_dq, cq_norm_g, w_uq, w_o):
    b, s, _ = h.shape
    c_q = rms_norm(h @ w_dq, cq_norm_g)
    q = jnp.einsum('bsr,rhd->bshd', c_q, w_uq)
    q_nope = q[..., :QK_NOPE_DIM]
    q_rope = apply_rope(q[..., QK_NOPE_DIM:], cos[:, :, None, :], sin[:, :, None, :])
    n_qb = s // Q_BLOCK

    def to_blocks(t):
        return jnp.moveaxis(t.reshape(b, n_qb, Q_BLOCK, *t.shape[2:]), 1, 0)

    key_chunk = jnp.arange(s) // CHUNK
    scale = (QK_NOPE_DIM + QK_ROPE_DIM) ** -0.5

    def attend(args):
        qn, qr, qb = args
        sc = (jnp.einsum('bqhd,bkhd->bhqk', qn, k_nope)
              + jnp.einsum('bqhr,bkr->bhqk', qr, k_rope)).astype(jnp.float32) * scale
        q_chunk = (qb * Q_BLOCK + jnp.arange(Q_BLOCK)) // CHUNK
        sc = jnp.where(key_chunk[None, :] <= q_chunk[:, None], sc, -jnp.inf)
        p = jax.nn.softmax(sc, axis=-1).astype(v.dtype)
        return jnp.einsum('bhqk,bkhd->bqhd', p, v)

    o = lax.map(attend, (to_blocks(q_nope), to_blocks(q_rope), jnp.arange(n_qb)))
    o = jnp.moveaxis(o, 0, 1).reshape(b, s, N_HEADS * V_HEAD_DIM)
    return o @ w_o


def moe_ffn(h, router_w, router_b, w_gu, b_gu, w_dn, b_dn):
    b, s, d = h.shape
    n = b * s
    t = h.reshape(n, d)
    logits = (t @ router_w + router_b).astype(jnp.float32)
    top_logit, top_idx = lax.top_k(logits, TOP_K)
    gates = jax.nn.softmax(top_logit, axis=-1)
    n_assign = n * TOP_K
    flat_e = top_idx.reshape(-1)
    order = jnp.argsort(flat_e)
    e_sorted = flat_e[order]
    tok_sorted = (order // TOP_K).astype(jnp.int32)
    gate_sorted = gates.reshape(-1)[order]
    counts = jnp.bincount(flat_e, length=N_EXPERTS)
    padded = (counts + EXPERT_BLOCK - 1) // EXPERT_BLOCK * EXPERT_BLOCK
    pad_end = jnp.cumsum(padded)
    pad_start = pad_end - padded
    start = jnp.cumsum(counts) - counts
    dest = pad_start[e_sorted] + jnp.arange(n_assign) - start[e_sorted]
    n_rows = -(-(n_assign + N_EXPERTS * (EXPERT_BLOCK - 1)) // EXPERT_BLOCK) * EXPERT_BLOCK
    n_blocks = n_rows // EXPERT_BLOCK
    row_tok = jnp.zeros((n_rows,), jnp.int32).at[dest].set(tok_sorted)
    row_gate = jnp.zeros((n_rows,), h.dtype).at[dest].set(gate_sorted.astype(h.dtype))
    block_expert = jnp.minimum(
        jnp.searchsorted(pad_end, jnp.arange(n_blocks) * EXPERT_BLOCK, side='right'), N_EXPERTS - 1)
    x_blocks = t[row_tok].reshape(n_blocks, EXPERT_BLOCK, d)

    def expert_block(args):
        xb, e = args
        gu = xb @ w_gu[e] + b_gu[e]
        g = jnp.minimum(gu[:, :D_EXPERT], SWIGLU_LIMIT)
        u = jnp.clip(gu[:, D_EXPERT:], -SWIGLU_LIMIT, SWIGLU_LIMIT)
        a = (u + 1) * g * jax.nn.sigmoid(SWIGLU_ALPHA * g)
        return a @ w_dn[e] + b_dn[e]

    y_rows = lax.map(expert_block, (x_blocks, block_expert)).reshape(n_rows, d)
    out = jnp.zeros((n, d), h.dtype).at[row_tok].add(y_rows * row_gate[:, None])
    return out.reshape(b, s, d)


def setup_inputs(seed: int = 0) -> dict:
    key = jax.random.key(seed)
    ks = jax.random.split(key, 32)
    D = D_MODEL

    def nrm(k, shape, scale):
        return jax.random.normal(k, shape, jnp.float32) * scale

    def gain(k, shape):
        return 1.0 + 0.02 * jax.random.normal(k, shape, jnp.float32)

    offsets = jax.random.randint(ks[2], (BATCH, 1), 0, 64, dtype=jnp.int32) * CHUNK
    positions = offsets + jnp.arange(SEQ, dtype=jnp.int32)[None, :]
    return {
        "x": nrm(ks[0], (BATCH, SEQ, D), 1.0),
        "c": nrm(ks[1], (BATCH, D), 1.0),
        "positions": positions,
        "mod_w": nrm(ks[3], (DEPTH, D, 6 * D), D ** -0.5),
        "mod_b": nrm(ks[4], (DEPTH, 6 * D), 0.02),
        "norm1_g": gain(ks[5], (DEPTH, D)),
        "norm2_g": gain(ks[6], (DEPTH, D)),
        "conv_pw1_w": nrm(ks[7], (N_A_LAYERS, D, 2 * D), D ** -0.5),
        "conv_pw1_b": nrm(ks[8], (N_A_LAYERS, 2 * D), 0.02),
        "conv_dw_w": nrm(ks[9], (N_A_LAYERS, CONV_WIDTH, D), CONV_WIDTH ** -0.5),
        "conv_dw_b": nrm(ks[10], (N_A_LAYERS, D), 0.02),
        "conv_ln_g": gain(ks[11], (N_A_LAYERS, D)),
        "conv_ln_b": nrm(ks[12], (N_A_LAYERS, D), 0.02),
        "conv_pw2_w": nrm(ks[13], (N_A_LAYERS, D, D), D ** -0.5),
        "conv_pw2_b": nrm(ks[14], (N_A_LAYERS, D), 0.02),
        "kv_norm_g": gain(ks[15], (D,)),
        "w_dkv": nrm(ks[16], (D, KV_LORA_RANK + QK_ROPE_DIM), D ** -0.5),
        "ckv_norm_g": gain(ks[17], (KV_LORA_RANK,)),
        "w_uk": nrm(ks[18], (KV_LORA_RANK, N_HEADS, QK_NOPE_DIM), KV_LORA_RANK ** -0.5),
        "w_uv": nrm(ks[19], (KV_LORA_RANK, N_HEADS, V_HEAD_DIM), KV_LORA_RANK ** -0.5),
        "w_dq": nrm(ks[20], (N_B_LAYERS, D, Q_LORA_RANK), D ** -0.5),
        "cq_norm_g": gain(ks[21], (N_B_LAYERS, Q_LORA_RANK)),
        "w_uq": nrm(ks[22], (N_B_LAYERS, Q_LORA_RANK, N_HEADS, QK_NOPE_DIM + QK_ROPE_DIM), Q_LORA_RANK ** -0.5),
        "w_o": nrm(ks[23], (N_B_LAYERS, N_HEADS * V_HEAD_DIM, D), (N_HEADS * V_HEAD_DIM) ** -0.5),
        "router_w": nrm(ks[24], (DEPTH, D, N_EXPERTS), D ** -0.5),
        "router_b": nrm(ks[25], (DEPTH, N_EXPERTS), 0.01),
        "exp_w_gu": nrm(ks[26], (DEPTH, N_EXPERTS, D, 2 * D_EXPERT), D ** -0.5),
        "exp_b_gu": nrm(ks[27], (DEPTH, N_EXPERTS, 2 * D_EXPERT), 0.02),
        "exp_w_dn": nrm(ks[28], (DEPTH, N_EXPERTS, D_EXPERT, D), D_EXPERT ** -0.5),
        "exp_b_dn": nrm(ks[29], (DEPTH, N_EXPERTS, D), 0.02),
        "final_g": gain(ks[30], (D,)),
    }


def reference(x, c, positions, mod_w, mod_b, norm1_g, norm2_g,
              conv_pw1_w, conv_pw1_b, conv_dw_w, conv_dw_b, conv_ln_g, conv_ln_b,
              conv_pw2_w, conv_pw2_b, kv_norm_g, w_dkv, ckv_norm_g, w_uk, w_uv,
              w_dq, cq_norm_g, w_uq, w_o, router_w, router_b,
              exp_w_gu, exp_b_gu, exp_w_dn, exp_b_dn, final_g):
    cos, sin = rope_tables(positions)
    c_act = jax.nn.silu(c)
    k_nope = k_rope = v = None
    for l in range(DEPTH):
        mod = c_act @ mod_w[l] + mod_b[l]
        shift1, scale1, gate1, shift2, scale2, gate2 = jnp.split(mod, 6, axis=-1)
        hn = modulate(rms_norm(x, norm1_g[l]), shift1, scale1)
        if l < N_A_LAYERS:
            y = conformer_conv(hn, conv_pw1_w[l], conv_pw1_b[l], conv_dw_w[l], conv_dw_b[l],
                               conv_ln_g[l], conv_ln_b[l], conv_pw2_w[l], conv_pw2_b[l])
        else:
            j = l - N_A_LAYERS
            y = mla_attention(hn, k_nope, k_rope, v, cos, sin,
                              w_dq[j], cq_norm_g[j], w_uq[j], w_o[j])
        x = x + gate1[:, None, :] * y
        hn = modulate(rms_norm(x, norm2_g[l]), shift2, scale2)
        x = x + gate2[:, None, :] * moe_ffn(hn, router_w[l], router_b[l],
                                            exp_w_gu[l], exp_b_gu[l], exp_w_dn[l], exp_b_dn[l])
        if l == N_A_LAYERS - 1:
            k_nope, k_rope, v = mla_shared_kv(x, cos, sin, kv_norm_g, w_dkv, ckv_norm_g, w_uk, w_uv)
    return rms_norm(x, final_g)
```

```python
import functools
import math

import jax
import jax.numpy as jnp
from jax import lax
from jax.experimental import pallas as pl
from jax.experimental.pallas import tpu as pltpu

CHUNK = 64
TOP_K = 4
ROPE_THETA = 10000.0
SWIGLU_ALPHA = 1.702
SWIGLU_LIMIT = 7.0
EPS = 1e-6

LANES = 128
SUBLANES = 8
VMEM_LIMIT_BYTES = 56 * 1024 * 1024

ROW_ALIGN = SUBLANES
MOE_TILE = 256
EXPERT_ROWS = 512
CONV_HALO = 32

F32 = jnp.float32
BF16 = jnp.bfloat16
I32 = jnp.int32


def _cparams(sem):
    return pltpu.CompilerParams(dimension_semantics=sem, vmem_limit_bytes=VMEM_LIMIT_BYTES)


def _rms(x, g):
    return x * lax.rsqrt(jnp.mean(x * x, axis=-1, keepdims=True) + EPS) * g


def _round_up(a, m):
    return (a + m - 1) // m * m


def _mod_kernel(c_ref, w_ref, b_ref, o_ref):
    c = c_ref[...]
    ca = c * jax.nn.sigmoid(c)
    o_ref[...] = jnp.dot(ca, w_ref[...], preferred_element_type=F32) + b_ref[...]


def _modulation(c, mod_w, mod_b):
    depth, d, d6 = mod_w.shape
    b = c.shape[0]
    bp = _round_up(b, SUBLANES)
    cp = jnp.zeros((bp, d), F32).at[:b].set(c)
    tn = d6 // 4
    out = pl.pallas_call(
        _mod_kernel,
        out_shape=jax.ShapeDtypeStruct((depth, bp, d6), F32),
        grid=(depth, d6 // tn),
        in_specs=[
            pl.BlockSpec((bp, d), lambda l, j: (0, 0)),
            pl.BlockSpec((None, d, tn), lambda l, j: (l, 0, j)),
            pl.BlockSpec((None, 1, tn), lambda l, j: (l, 0, j)),
        ],
        out_specs=pl.BlockSpec((None, bp, tn), lambda l, j: (l, 0, j)),
        compiler_params=_cparams(("arbitrary", "arbitrary")),
        name="adaln_modulation",
    )(cp, mod_w, mod_b.reshape(depth, 1, d6))
    return out[:, :b]


def _pre_moe(x1, g2, sh2, sc2, rwt, rb, hn_ref, idxt_ref, route_ref, cnt_ref):
    ts = x1.shape[0]
    n_exp = rwt.shape[0]
    hn = _rms(x1, g2) * (1.0 + sc2) + sh2
    hn_ref[...] = hn.astype(BF16)
    logits = lax.dot_general(rwt, hn, (((1,), (1,)), ((), ())),
                             precision=lax.Precision.HIGHEST,
                             preferred_element_type=F32) + rb
    e_iota = lax.broadcasted_iota(I32, (n_exp, ts), 0)
    vals, idxs = [], []
    cur = logits
    for _ in range(TOP_K):
        m = jnp.max(cur, axis=0, keepdims=True)
        i = jnp.min(jnp.where(cur == m, e_iota, n_exp), axis=0, keepdims=True)
        vals.append(m)
        idxs.append(i)
        cur = jnp.where(e_iota == i, -jnp.inf, cur)
    exps = [jnp.exp(v - vals[0]) for v in vals]
    den = exps[0]
    for e in exps[1:]:
        den = den + e
    gates = [e / den for e in exps]
    idxt_ref[...] = jnp.concatenate(idxs, axis=0)
    rows = jnp.concatenate([i.astype(F32) for i in idxs] + gates
                           + [jnp.zeros((LANES - 2 * TOP_K, ts), F32)], axis=0)
    route_ref[...] = rows.T
    onehot = jnp.zeros((n_exp, ts), F32)
    for i in idxs:
        onehot = onehot + (e_iota == i).astype(F32)
    sel = (lax.broadcasted_iota(I32, (SUBLANES, ts), 1) // MOE_TILE
           == lax.broadcasted_iota(I32, (SUBLANES, ts), 0)).astype(BF16)
    cnt = lax.dot_general(sel, onehot.astype(BF16), (((1,), (1,)), ((), ())),
                          preferred_element_type=F32)
    cnt_ref[...] = cnt.astype(I32)


def _pre_moe_specs(b, s, d, ts, n_exp):
    ns = s // ts
    out_shape = [
        jax.ShapeDtypeStruct((b, s, d), F32),
        jax.ShapeDtypeStruct((b, s, d), BF16),
        jax.ShapeDtypeStruct((TOP_K, b * s), I32),
        jax.ShapeDtypeStruct((b * s, LANES), F32),
        jax.ShapeDtypeStruct((b * ns, SUBLANES, n_exp), I32),
    ]
    out_specs = [
        pl.BlockSpec((None, ts, d), lambda i, j: (i, j, 0)),
        pl.BlockSpec((None, ts, d), lambda i, j: (i, j, 0)),
        pl.BlockSpec((TOP_K, ts), lambda i, j: (0, i * ns + j)),
        pl.BlockSpec((ts, LANES), lambda i, j: (i * ns + j, 0)),
        pl.BlockSpec((None, SUBLANES, n_exp), lambda i, j: (i * ns + j, 0, 0)),
    ]
    return out_shape, out_specs


def _vec_spec(d):
    return pl.BlockSpec((1, d), lambda i, j: (0, 0))


def _bvec_spec(d):
    return pl.BlockSpec((None, 1, d), lambda i, j: (i, 0, 0))


def _mix0_kernel(x_ref, n1g_ref, sh1_ref, sc1_ref, gt1_ref,
                 pw1w_ref, pw1b_ref, dww_ref, dwb_ref, lng_ref, lnb_ref,
                 pw2w_ref, pw2b_ref, n2g_ref, sh2_ref, sc2_ref, rwt_ref, rb_ref,
                 x1_ref, hn_ref, idxt_ref, route_ref, cnt_ref, buf_ref, conv_ref):
    ts, d = x_ref.shape
    width = dww_ref.shape[0]
    x = x_ref[...]
    hn = _rms(x, n1g_ref[...]) * (1.0 + sc1_ref[...]) + sh1_ref[...]
    hb = hn.astype(BF16)
    a = jnp.dot(hb, pw1w_ref[:, :d], preferred_element_type=F32) + pw1b_ref[:, :d]
    g = jnp.dot(hb, pw1w_ref[:, d:], preferred_element_type=F32) + pw1b_ref[:, d:]
    glu = a * jax.nn.sigmoid(g)

    @pl.when(pl.program_id(1) == 0)
    def _():
        buf_ref[0:CONV_HALO, :] = jnp.zeros((CONV_HALO, d), F32)

    buf_ref[CONV_HALO:, :] = glu
    base = CONV_HALO - (width - 1)
    rc = 32
    lc = min(512, d)
    for r0 in range(0, ts, rc):
        for c0 in range(0, d, lc):
            acc = jnp.zeros((rc, lc), F32)
            for k in range(width):
                acc = acc + dww_ref[k:k + 1, c0:c0 + lc] * buf_ref[base + r0 + k:base + r0 + k + rc, c0:c0 + lc]
            conv_ref[r0:r0 + rc, c0:c0 + lc] = acc
    buf_ref[0:CONV_HALO, :] = buf_ref[ts:ts + CONV_HALO, :]
    u = conv_ref[...] + dwb_ref[...]
    mu = jnp.mean(u, axis=-1, keepdims=True)
    dlt = u - mu
    var = jnp.mean(dlt * dlt, axis=-1, keepdims=True)
    u = dlt * lax.rsqrt(var + EPS) * lng_ref[...] + lnb_ref[...]
    u = u * jax.nn.sigmoid(u)
    y = jnp.dot(u.astype(BF16), pw2w_ref[...], preferred_element_type=F32) + pw2b_ref[...]
    x1 = x + gt1_ref[...] * y
    x1_ref[...] = x1
    _pre_moe(x1, n2g_ref[...], sh2_ref[...], sc2_ref[...], rwt_ref[...], rb_ref[...],
             hn_ref, idxt_ref, route_ref, cnt_ref)


def _mix0(x, n1g, sh1, sc1, gt1, pw1w, pw1b, dww, dwb, lng, lnb, pw2w, pw2b,
          n2g, sh2, sc2, rwt, rb, ts):
    b, s, d = x.shape
    n_exp = rwt.shape[0]
    width = dww.shape[0]
    assert width - 1 <= CONV_HALO and ts % MOE_TILE == 0 and ts // MOE_TILE <= SUBLANES
    out_shape, out_specs = _pre_moe_specs(b, s, d, ts, n_exp)
    full = lambda shp: pl.BlockSpec(shp, lambda i, j: (0,) * len(shp))
    return pl.pallas_call(
        _mix0_kernel,
        out_shape=out_shape,
        grid=(b, s // ts),
        in_specs=[
            pl.BlockSpec((None, ts, d), lambda i, j: (i, j, 0)),
            _vec_spec(d), _bvec_spec(d), _bvec_spec(d), _bvec_spec(d),
            full((d, 2 * d)), full((1, 2 * d)), full((width, d)), full((1, d)),
            full((1, d)), full((1, d)), full((d, d)), full((1, d)),
            _vec_spec(d), _bvec_spec(d), _bvec_spec(d),
            full((n_exp, d)), full((n_exp, 1)),
        ],
        out_specs=out_specs,
        scratch_shapes=[pltpu.VMEM((CONV_HALO + ts, d), F32), pltpu.VMEM((ts, d), F32)],
        compiler_params=_cparams(("arbitrary", "arbitrary")),
        name="conformer_mixer",
    )(x, n1g, sh1, sc1, gt1, pw1w, pw1b, dww, dwb, lng, lnb, pw2w, pw2b,
      n2g, sh2, sc2, rwt, rb)


def _mix1_kernel(x_ref, o_ref, gt1_ref, wo_ref, n2g_ref, sh2_ref, sc2_ref, rwt_ref, rb_ref,
                 x1_ref, hn_ref, idxt_ref, route_ref, cnt_ref):
    y = jnp.dot(o_ref[...], wo_ref[...], preferred_element_type=F32)
    x1 = x_ref[...] + gt1_ref[...] * y
    x1_ref[...] = x1
    _pre_moe(x1, n2g_ref[...], sh2_ref[...], sc2_ref[...], rwt_ref[...], rb_ref[...],
             hn_ref, idxt_ref, route_ref, cnt_ref)


def _mix1(x, o, gt1, wo, n2g, sh2, sc2, rwt, rb, ts):
    b, s, d = x.shape
    n_exp = rwt.shape[0]
    do = o.shape[-1]
    assert ts % MOE_TILE == 0 and ts // MOE_TILE <= SUBLANES
    out_shape, out_specs = _pre_moe_specs(b, s, d, ts, n_exp)
    full = lambda shp: pl.BlockSpec(shp, lambda i, j: (0,) * len(shp))
    return pl.pallas_call(
        _mix1_kernel,
        out_shape=out_shape,
        grid=(b, s // ts),
        in_specs=[
            pl.BlockSpec((None, ts, d), lambda i, j: (i, j, 0)),
            pl.BlockSpec((None, ts, do), lambda i, j: (i, j, 0)),
            _bvec_spec(d), full((do, d)),
            _vec_spec(d), _bvec_spec(d), _bvec_spec(d),
            full((n_exp, d)), full((n_exp, 1)),
        ],
        out_specs=out_specs,
        compiler_params=_cparams(("arbitrary", "arbitrary")),
        name="attn_out_mixer",
    )(x, o, gt1, wo, n2g, sh2, sc2, rwt, rb)


def _moe_dims(n_tok, n_exp):
    n_tiles = n_tok // MOE_TILE
    loc_rows = _round_up(TOP_K * MOE_TILE + n_exp * (ROW_ALIGN - 1), LANES)
    n_chunks = loc_rows // ROW_ALIGN
    max_rows = TOP_K * n_tok + n_tiles * n_exp * (ROW_ALIGN - 1) + n_exp * (EXPERT_ROWS - ROW_ALIGN)
    max_blocks = -(-max_rows // EXPERT_ROWS)
    return n_tiles, loc_rows, n_chunks, max_blocks


def _routing_tables(cnt, n_chunks, max_blocks):
    n_tiles, n_exp = cnt.shape
    cnt8 = _round_up(cnt, ROW_ALIGN)
    off = jnp.cumsum(cnt8, axis=1) - cnt8
    seg_len = cnt8.sum(axis=0)
    seg_pad = _round_up(seg_len, EXPERT_ROWS)
    seg_end = jnp.cumsum(seg_pad)
    seg_start = seg_end - seg_pad
    run_start = seg_start[None, :] + jnp.cumsum(cnt8, axis=0) - cnt8
    n_used = (cnt8.sum(axis=1) // ROW_ALIGN).astype(I32)
    c_row = jnp.arange(n_chunks, dtype=I32) * ROW_ALIGN
    local_end = off + cnt8
    e_of_chunk = jnp.minimum(
        (c_row[None, :, None] >= local_end[:, None, :]).sum(axis=-1), n_exp - 1)
    delta = jnp.take_along_axis(run_start - off, e_of_chunk, axis=1)
    tab = (delta + c_row[None, :]) // ROW_ALIGN
    tab = jnp.where(jnp.arange(n_chunks)[None, :] < n_used[:, None], tab, 0).astype(I32)
    n_blocks = (seg_end[-1] // EXPERT_ROWS).astype(I32)
    blk_row = jnp.arange(max_blocks, dtype=I32) * EXPERT_ROWS
    blk_row = jnp.minimum(blk_row, seg_end[-1] - EXPERT_ROWS)
    blk_exp = jnp.minimum(jnp.searchsorted(seg_end, blk_row, side='right'), n_exp - 1).astype(I32)
    per = EXPERT_ROWS // ROW_ALIGN
    j = jnp.arange(per, dtype=I32)
    n_tail = (seg_pad - seg_len) // ROW_ALIGN
    tail_chunk = ((seg_start + seg_len) // ROW_ALIGN)[:, None] + j[None, :]
    valid = (j[None, :] < n_tail[:, None]).reshape(-1)
    order = jnp.argsort(jnp.logical_not(valid), stable=True)
    ztab = tail_chunk.reshape(-1)[order].astype(I32)
    ztab = jnp.where(jnp.arange(ztab.shape[0]) < n_tail.sum(), ztab, 0)
    n_zero = n_tail.sum().astype(I32)
    return tab.reshape(-1), n_used, ztab, n_zero.reshape(1), blk_exp, n_blocks.reshape(1)


def _dispatch_kernel(n_exp, tab_ref, nused_ref, ztab_ref, nzero_ref, nblk_ref,
                     x_ref, idxt_ref, xs_ref, loc_ref, zero_ref, sem, zsem):
    i = pl.program_id(0)
    n = pl.num_programs(0)
    slot = i % 2
    tile = x_ref.shape[0]
    loc_rows = loc_ref.shape[1]
    n_chunks = loc_rows // ROW_ALIGN

    def chunk_copy(sl, c, dst):
        return pltpu.make_async_copy(
            loc_ref.at[sl, pl.ds(pl.multiple_of(c * ROW_ALIGN, ROW_ALIGN), ROW_ALIGN)],
            xs_ref.at[pl.ds(pl.multiple_of(dst * ROW_ALIGN, ROW_ALIGN), ROW_ALIGN)],
            sem.at[sl])

    def wait_slot(sl, count):
        def body(c, carry):
            chunk_copy(sl, 0, 0).wait()
            return carry
        lax.fori_loop(0, count, body, 0)

    @pl.when(i >= 2)
    def _():
        wait_slot(slot, nused_ref[i - 2])

    idx = idxt_ref[...]
    e_iota = lax.broadcasted_iota(I32, (n_exp, tile), 0)
    hits = [idx[k:k + 1, :] == e_iota for k in range(TOP_K)]
    onehot = jnp.zeros((n_exp, tile), F32)
    for h in hits:
        onehot = onehot + h.astype(F32)
    upper = (lax.broadcasted_iota(I32, (tile, tile), 0)
             < lax.broadcasted_iota(I32, (tile, tile), 1)).astype(BF16)
    pre = jnp.dot(onehot.astype(BF16), upper, preferred_element_type=F32)
    cnt = jnp.sum(onehot, axis=1, keepdims=True)
    cnt8 = jnp.ceil(cnt / ROW_ALIGN) * ROW_ALIGN
    lower = (lax.broadcasted_iota(I32, (n_exp, n_exp), 1)
             < lax.broadcasted_iota(I32, (n_exp, n_exp), 0)).astype(BF16)
    off = jnp.dot(lower, jnp.broadcast_to(cnt8, (n_exp, LANES)).astype(BF16),
                  preferred_element_type=F32)[:, 0:1]
    base = off + pre
    r_iota = lax.broadcasted_iota(I32, (loc_rows, tile), 0)
    perm = jnp.zeros((loc_rows, tile), F32)
    for h in hits:
        dest = jnp.sum(jnp.where(h, base, 0.0), axis=0, keepdims=True).astype(I32)
        perm = perm + (r_iota == dest).astype(F32)
    loc_ref[slot] = jnp.dot(perm.astype(BF16), x_ref[...], preferred_element_type=F32)

    def issue(c, carry):
        chunk_copy(slot, c, tab_ref[i * n_chunks + c]).start()
        return carry
    lax.fori_loop(0, nused_ref[i], issue, 0)

    @pl.when(i == n - 1)
    def _():
        zero_ref[...] = jnp.zeros(zero_ref.shape, F32)
        max_blocks = xs_ref.shape[0] // EXPERT_ROWS

        def zcopy(dst):
            return pltpu.make_async_copy(
                zero_ref.at[pl.ds(0, ROW_ALIGN)],
                xs_ref.at[pl.ds(pl.multiple_of(dst * ROW_ALIGN, ROW_ALIGN), ROW_ALIGN)],
                zsem.at[0])

        def bcopy(blk):
            return pltpu.make_async_copy(
                zero_ref,
                xs_ref.at[pl.ds(pl.multiple_of(blk * EXPERT_ROWS, EXPERT_ROWS), EXPERT_ROWS)],
                zsem.at[1])

        def zissue(c, carry):
            zcopy(ztab_ref[c]).start()
            return carry
        lax.fori_loop(0, nzero_ref[0], zissue, 0)

        def bissue(blk, carry):
            bcopy(blk).start()
            return carry
        lax.fori_loop(nblk_ref[0], max_blocks, bissue, 0)

        @pl.when(i >= 1)
        def _():
            wait_slot(1 - slot, nused_ref[i - 1])
        wait_slot(slot, nused_ref[i])

        def zwait(c, carry):
            zcopy(0).wait()
            return carry
        lax.fori_loop(0, nzero_ref[0], zwait, 0)

        def bwait(blk, carry):
            bcopy(0).wait()
            return carry
        lax.fori_loop(nblk_ref[0], max_blocks, bwait, 0)


def _dispatch(hn2, idxt, tab, n_used, ztab, n_zero, n_blocks, n_exp, loc_rows, max_blocks):
    n_tok, d = hn2.shape
    n_tiles = n_tok // MOE_TILE
    return pl.pallas_call(
        functools.partial(_dispatch_kernel, n_exp),
        out_shape=jax.ShapeDtypeStruct((max_blocks * EXPERT_ROWS, d), F32),
        grid_spec=pltpu.PrefetchScalarGridSpec(
            num_scalar_prefetch=5,
            grid=(n_tiles,),
            in_specs=[
                pl.BlockSpec((MOE_TILE, d), lambda i, *_: (i, 0)),
                pl.BlockSpec((TOP_K, MOE_TILE), lambda i, *_: (0, i)),
            ],
            out_specs=pl.BlockSpec(memory_space=pl.ANY),
            scratch_shapes=[
                pltpu.VMEM((2, loc_rows, d), F32),
                pltpu.VMEM((EXPERT_ROWS, d), F32),
                pltpu.SemaphoreType.DMA((2,)),
                pltpu.SemaphoreType.DMA((2,)),
            ],
        ),
        compiler_params=_cparams(("arbitrary",)),
        name="moe_dispatch",
    )(tab, n_used, ztab, n_zero, n_blocks, hn2, idxt)


def _expert_kernel(be_ref, nb_ref, x_ref, wgu_ref, bgu_ref, wdn_ref, bdn_ref, y_ref):
    de = wdn_ref.shape[0]

    @pl.when(pl.program_id(0) < nb_ref[0])
    def _():
        xb = x_ref[...].astype(BF16)
        g = jnp.dot(xb, wgu_ref[:, :de], preferred_element_type=F32) + bgu_ref[:, :de]
        u = jnp.dot(xb, wgu_ref[:, de:], preferred_element_type=F32) + bgu_ref[:, de:]
        g = jnp.minimum(g, SWIGLU_LIMIT)
        u = jnp.clip(u, -SWIGLU_LIMIT, SWIGLU_LIMIT)
        a = (u + 1.0) * g * jax.nn.sigmoid(SWIGLU_ALPHA * g)
        y_ref[...] = jnp.dot(a.astype(BF16), wdn_ref[...], preferred_element_type=F32) + bdn_ref[...]

    @pl.when(pl.program_id(0) >= nb_ref[0])
    def _():
        y_ref[...] = jnp.zeros(y_ref.shape, y_ref.dtype)


def _experts(xs, blk_exp, n_blocks, wgu, bgu, wdn, bdn):
    rows, d = xs.shape
    n_exp, _, de2 = wgu.shape
    de = de2 // 2
    max_blocks = rows // EXPERT_ROWS
    row_map = lambda b, be, nb: (jnp.minimum(b, nb[0] - 1), 0)
    exp_map = lambda b, be, nb: (be[b], 0, 0)
    return pl.pallas_call(
        _expert_kernel,
        out_shape=jax.ShapeDtypeStruct((rows, d), F32),
        grid_spec=pltpu.PrefetchScalarGridSpec(
            num_scalar_prefetch=2,
            grid=(max_blocks,),
            in_specs=[
                pl.BlockSpec((EXPERT_ROWS, d), row_map),
                pl.BlockSpec((None, d, de2), exp_map),
                pl.BlockSpec((None, 1, de2), exp_map),
                pl.BlockSpec((None, de, d), exp_map),
                pl.BlockSpec((None, 1, d), exp_map),
            ],
            out_specs=pl.BlockSpec((EXPERT_ROWS, d), lambda b, be, nb: (b, 0)),
        ),
        compiler_params=_cparams(("arbitrary",)),
        name="moe_experts",
    )(blk_exp, n_blocks, xs, wgu, bgu.reshape(n_exp, 1, de2), wdn, bdn.reshape(n_exp, 1, d))


def _combine_kernel(final_norm, n_exp, tab_ref, nused_ref,
                    route_ref, x1_ref, g2_ref, fg_ref, ys_ref, out_ref, loc_ref, sem):
    i = pl.program_id(0)
    n = pl.num_programs(0)
    slot = i % 2
    tile = x1_ref.shape[0]
    loc_rows = loc_ref.shape[1]
    n_chunks = loc_rows // ROW_ALIGN

    def chunk_copy(sl, c, src):
        return pltpu.make_async_copy(
            ys_ref.at[pl.ds(pl.multiple_of(src * ROW_ALIGN, ROW_ALIGN), ROW_ALIGN)],
            loc_ref.at[sl, pl.ds(pl.multiple_of(c * ROW_ALIGN, ROW_ALIGN), ROW_ALIGN)],
            sem.at[sl])

    def issue_tile(t, sl):
        def body(c, carry):
            chunk_copy(sl, c, tab_ref[t * n_chunks + c]).start()
            return carry
        lax.fori_loop(0, nused_ref[t], body, 0)

    @pl.when(i == 0)
    def _():
        issue_tile(0, 0)

    @pl.when(i + 1 < n)
    def _():
        issue_tile(i + 1, 1 - slot)

    route = route_ref[...]
    e_iota = lax.broadcasted_iota(I32, (tile, n_exp), 1)
    hits = [route[:, k:k + 1].astype(I32) == e_iota for k in range(TOP_K)]
    gates = [route[:, TOP_K + k:TOP_K + k + 1] for k in range(TOP_K)]
    onehot = jnp.zeros((tile, n_exp), F32)
    for h in hits:
        onehot = onehot + h.astype(F32)
    lower = (lax.broadcasted_iota(I32, (tile, tile), 1)
             < lax.broadcasted_iota(I32, (tile, tile), 0)).astype(BF16)
    pre = jnp.dot(lower, onehot.astype(BF16), preferred_element_type=F32)
    cnt = jnp.sum(onehot, axis=0, keepdims=True)
    cnt8 = jnp.ceil(cnt / ROW_ALIGN) * ROW_ALIGN
    upper = (lax.broadcasted_iota(I32, (n_exp, n_exp), 0)
             < lax.broadcasted_iota(I32, (n_exp, n_exp), 1)).astype(BF16)
    off = jnp.dot(jnp.broadcast_to(cnt8, (SUBLANES, n_exp)).astype(BF16), upper,
                  preferred_element_type=F32)[0:1, :]
    base = off + pre
    r_iota = lax.broadcasted_iota(I32, (tile, loc_rows), 1)
    comb = jnp.zeros((tile, loc_rows), F32)
    for h, gk in zip(hits, gates):
        dest = jnp.sum(jnp.where(h, base, 0.0), axis=1, keepdims=True).astype(I32)
        comb = comb + jnp.where(r_iota == dest, gk, 0.0)

    def wbody(c, carry):
        chunk_copy(slot, 0, 0).wait()
        return carry
    lax.fori_loop(0, nused_ref[i], wbody, 0)

    used_rows = nused_ref[i] * ROW_ALIGN
    row_ok = lax.broadcasted_iota(I32, (loc_rows, 1), 0) < used_rows
    yl = jnp.where(row_ok, loc_ref[slot], 0.0).astype(BF16)
    moe = jnp.dot(comb.astype(BF16), yl, preferred_element_type=F32)
    out = x1_ref[...] + g2_ref[...] * moe
    if final_norm:
        out = _rms(out, fg_ref[...])
    out_ref[...] = out


def _combine(ys, route, x1, gate2, final_g, tab, n_used, n_exp, loc_rows, final_norm):
    b, s, d = x1.shape
    n_tok = b * s
    n_tiles = n_tok // MOE_TILE
    tiles_per_seq = s // MOE_TILE
    out = pl.pallas_call(
        functools.partial(_combine_kernel, final_norm, n_exp),
        out_shape=jax.ShapeDtypeStruct((n_tok, d), F32),
        grid_spec=pltpu.PrefetchScalarGridSpec(
            num_scalar_prefetch=2,
            grid=(n_tiles,),
            in_specs=[
                pl.BlockSpec((MOE_TILE, LANES), lambda i, *_: (i, 0)),
                pl.BlockSpec((MOE_TILE, d), lambda i, *_: (i, 0)),
                pl.BlockSpec((None, 1, d), lambda i, *_: (i // tiles_per_seq, 0, 0)),
                pl.BlockSpec((1, d), lambda i, *_: (0, 0)),
                pl.BlockSpec(memory_space=pl.ANY),
            ],
            out_specs=pl.BlockSpec((MOE_TILE, d), lambda i, *_: (i, 0)),
            scratch_shapes=[
                pltpu.VMEM((2, loc_rows, d), F32),
                pltpu.SemaphoreType.DMA((2,)),
            ],
        ),
        compiler_params=_cparams(("arbitrary",)),
        name="moe_combine",
    )(tab, n_used, route, x1.reshape(n_tok, d), gate2, final_g, ys)
    return out.reshape(b, s, d)


def _moe(x1, hn2, idxt, route, cnt, gate2, final_g, wgu, bgu, wdn, bdn, final_norm):
    b, s, d = x1.shape
    n_tok = b * s
    n_exp = wgu.shape[0]
    n_tiles, loc_rows, n_chunks, max_blocks = _moe_dims(n_tok, n_exp)
    tab, n_used, ztab, n_zero, blk_exp, n_blocks = _routing_tables(cnt, n_chunks, max_blocks)
    xs = _dispatch(hn2.reshape(n_tok, d), idxt, tab, n_used, ztab, n_zero, n_blocks,
                   n_exp, loc_rows, max_blocks)
    ys = _experts(xs, blk_exp, n_blocks, wgu, bgu, wdn, bdn)
    return _combine(ys, route, x1, gate2, final_g, tab, n_used, n_exp, loc_rows, final_norm)


def _qkv_kernel(n_heads, scale, x_ref, cc_ref, ss_ref, kvg_ref, wdkv_ref, ckvg_ref, wuk_ref, wuv_ref,
                n1g_ref, sh1_ref, sc1_ref, wdq_ref, cqg_ref, wuq_ref,
                q_ref, k_ref, v_ref):
    x = x_ref[...]
    cc = cc_ref[...]
    ss = ss_ref[...]
    r_kv = ckvg_ref.shape[1]
    hk = _rms(x, kvg_ref[...]).astype(BF16)
    lat = jnp.dot(hk, wdkv_ref[...], preferred_element_type=F32)
    ckv = _rms(lat[:, :r_kv], ckvg_ref[...]).astype(BF16)
    krot = lat[:, r_kv:r_kv + LANES] * cc + lat[:, r_kv + LANES:r_kv + 2 * LANES] * ss
    kn = jnp.dot(ckv, wuk_ref[...], preferred_element_type=F32)
    vv = jnp.dot(ckv, wuv_ref[...], preferred_element_type=F32)
    hq = (_rms(x, n1g_ref[...]) * (1.0 + sc1_ref[...]) + sh1_ref[...]).astype(BF16)
    cq = _rms(jnp.dot(hq, wdq_ref[...], preferred_element_type=F32), cqg_ref[...]).astype(BF16)
    qq = jnp.dot(cq, wuq_ref[...], preferred_element_type=F32) * scale
    hd = n_heads * LANES
    for h in range(n_heads):
        sl = slice(h * LANES, (h + 1) * LANES)
        k_ref[h, :, 0:LANES] = kn[:, sl].astype(BF16)
        k_ref[h, :, LANES:2 * LANES] = krot.astype(BF16)
        v_ref[h] = vv[:, sl].astype(BF16)
        q_ref[h, :, 0:LANES] = qq[:, sl].astype(BF16)
        qrot = qq[:, hd + h * LANES:hd + (h + 1) * LANES] * cc \
            + qq[:, 2 * hd + h * LANES:2 * hd + (h + 1) * LANES] * ss
        q_ref[h, :, LANES:2 * LANES] = qrot.astype(BF16)


def _qkv(x, cc, ss, kvg, wdkv, ckvg, wuk, wuv, n1g, sh1, sc1, wdq, cqg, wuq, n_heads, scale, ts):
    b, s, d = x.shape
    full = lambda a: pl.BlockSpec(a.shape, lambda i, j: (0,) * a.ndim)
    hspec = lambda w: pl.BlockSpec((None, n_heads, ts, w), lambda i, j: (i, 0, j, 0))
    return pl.pallas_call(
        functools.partial(_qkv_kernel, n_heads, scale),
        out_shape=[
            jax.ShapeDtypeStruct((b, n_heads, s, 2 * LANES), BF16),
            jax.ShapeDtypeStruct((b, n_heads, s, 2 * LANES), BF16),
            jax.ShapeDtypeStruct((b, n_heads, s, LANES), BF16),
        ],
        grid=(b, s // ts),
        in_specs=[
            pl.BlockSpec((None, ts, d), lambda i, j: (i, j, 0)),
            pl.BlockSpec((None, ts, LANES), lambda i, j: (i, j, 0)),
            pl.BlockSpec((None, ts, LANES), lambda i, j: (i, j, 0)),
            full(kvg), full(wdkv), full(ckvg), full(wuk), full(wuv),
            full(n1g), _bvec_spec(d), _bvec_spec(d), full(wdq), full(cqg), full(wuq),
        ],
        out_specs=[hspec(2 * LANES), hspec(2 * LANES), hspec(LANES)],
        compiler_params=_cparams(("arbitrary", "arbitrary")),
        name="mla_qkv",
    )(x, cc, ss, kvg, wdkv, ckvg, wuk, wuv, n1g, sh1, sc1, wdq, cqg, wuq)


def _attn_kernel(q_ref, k_ref, v_ref, o_ref, m_ref, l_ref, acc_ref):
    tq = q_ref.shape[0]
    qi = pl.program_id(2)
    q = q_ref[...]
    m_ref[...] = jnp.full(m_ref.shape, -jnp.inf, F32)
    l_ref[...] = jnp.zeros(l_ref.shape, F32)
    acc_ref[...] = jnp.zeros(acc_ref.shape, F32)

    def tile_step(ki, masked):
        start = pl.multiple_of(ki * tq, tq)
        k = k_ref[pl.ds(start, tq), :]
        v = v_ref[pl.ds(start, tq), :]
        s = lax.dot_general(q, k, (((1,), (1,)), ((), ())), preferred_element_type=F32)
        if masked:
            qc = lax.broadcasted_iota(I32, (tq, tq), 0) // CHUNK
            kc = lax.broadcasted_iota(I32, (tq, tq), 1) // CHUNK
            s = jnp.where(kc <= qc, s, -jnp.inf)
        m_old = m_ref[...]
        m_new = jnp.maximum(m_old, jnp.max(s, axis=-1, keepdims=True))
        p = jnp.exp(s - m_new)
        alpha = jnp.exp(m_old - m_new)
        l_ref[...] = alpha * l_ref[...] + jnp.sum(p, axis=-1, keepdims=True)
        acc_ref[...] = alpha * acc_ref[...] + jnp.dot(p.astype(BF16), v, preferred_element_type=F32)
        m_ref[...] = m_new

    def body(ki, carry):
        tile_step(ki, False)
        return carry
    lax.fori_loop(0, qi, body, 0)
    tile_step(qi, True)
    o_ref[...] = (acc_ref[...] / l_ref[...]).astype(o_ref.dtype)


def _attention(q, k, v, tq):
    b, h, s, dk = q.shape
    dv = v.shape[-1]
    assert tq % CHUNK == 0 and s % tq == 0
    return pl.pallas_call(
        _attn_kernel,
        out_shape=jax.ShapeDtypeStruct((b, s, h * dv), BF16),
        grid=(b, h, s // tq),
        in_specs=[
            pl.BlockSpec((None, None, tq, dk), lambda i, j, t: (i, j, t, 0)),
            pl.BlockSpec((None, None, s, dk), lambda i, j, t: (i, j, 0, 0)),
            pl.BlockSpec((None, None, s, dv), lambda i, j, t: (i, j, 0, 0)),
        ],
        out_specs=pl.BlockSpec((None, tq, dv), lambda i, j, t: (i, t, j)),
        scratch_shapes=[pltpu.VMEM((tq, 1), F32), pltpu.VMEM((tq, 1), F32), pltpu.VMEM((tq, dv), F32)],
        compiler_params=_cparams(("arbitrary", "arbitrary", "arbitrary")),
        name="mla_attention",
    )(q, k, v)


def _swap_halves(w):
    half = w.shape[-1] // 2
    return jnp.concatenate([w[..., half:], w[..., :half]], axis=-1)


def _pad_lanes(w):
    pad = LANES - w.shape[-1]
    return jnp.concatenate([w, jnp.zeros(w.shape[:-1] + (pad,), w.dtype)], axis=-1)


def kernel(x, c, positions, mod_w, mod_b, norm1_g, norm2_g, conv_pw1_w, conv_pw1_b, conv_dw_w, conv_dw_b, conv_ln_g, conv_ln_b, conv_pw2_w, conv_pw2_b, kv_norm_g, w_dkv, ckv_norm_g, w_uk, w_uv, w_dq, cq_norm_g, w_uq, w_o, router_w, router_b, exp_w_gu, exp_b_gu, exp_w_dn, exp_b_dn, final_g):
    b, s, d = x.shape
    n_heads, nope = w_uk.shape[1], w_uk.shape[2]
    r_kv = ckv_norm_g.shape[0]
    rope = w_dkv.shape[1] - r_kv
    vdim = w_uv.shape[2]
    n_exp = router_w.shape[2]
    assert nope == LANES and vdim == LANES and rope <= LANES and d % LANES == 0
    ts = min(512, s)
    tq = min(512, s)

    mod = _modulation(c, mod_w, mod_b)
    mods = [[m.reshape(b, 1, d) for m in jnp.split(mod[l], 6, axis=-1)] for l in range(2)]
    row = lambda v: v.reshape(1, -1)

    half = rope // 2
    inv = jnp.exp(-(2.0 * math.log(ROPE_THETA) / rope) * jnp.arange(half, dtype=F32))
    ang = positions.astype(F32)[..., None] * inv
    cos, sin = jnp.cos(ang), jnp.sin(ang)
    cc = _pad_lanes(jnp.concatenate([cos, cos], axis=-1))
    ss = _pad_lanes(jnp.concatenate([-sin, sin], axis=-1))

    sh1, sc1, gt1, sh2, sc2, gt2 = mods[0]
    x1, hn2, idxt, route, cnt = _mix0(
        x, row(norm1_g[0]), sh1, sc1, gt1,
        conv_pw1_w[0].astype(BF16), row(conv_pw1_b[0]), conv_dw_w[0], row(conv_dw_b[0]),
        row(conv_ln_g[0]), row(conv_ln_b[0]), conv_pw2_w[0].astype(BF16), row(conv_pw2_b[0]),
        row(norm2_g[0]), sh2, sc2, router_w[0].T, router_b[0].reshape(n_exp, 1), ts)
    cnt = cnt[:, :ts // MOE_TILE].reshape(-1, n_exp)
    x2 = _moe(x1, hn2, idxt, route, cnt, gt2, row(final_g),
              exp_w_gu[0].astype(BF16), exp_b_gu[0], exp_w_dn[0].astype(BF16), exp_b_dn[0], False)

    sh1, sc1, gt1, sh2, sc2, gt2 = mods[1]
    wdkv_rope = w_dkv[:, r_kv:]
    wdkv_ext = jnp.concatenate(
        [w_dkv[:, :r_kv], _pad_lanes(wdkv_rope), _pad_lanes(_swap_halves(wdkv_rope))], axis=-1)
    wuq = w_uq[0]
    r_q = wuq.shape[0]
    wuq_rope = wuq[:, :, nope:]
    wuq_ext = jnp.concatenate([
        wuq[:, :, :nope].reshape(r_q, n_heads * LANES),
        _pad_lanes(wuq_rope).reshape(r_q, n_heads * LANES),
        _pad_lanes(_swap_halves(wuq_rope)).reshape(r_q, n_heads * LANES)], axis=-1)
    scale = float((nope + rope) ** -0.5)
    q, k, v = _qkv(
        x2, cc, ss, row(kv_norm_g), wdkv_ext.astype(BF16), row(ckv_norm_g),
        w_uk.reshape(r_kv, n_heads * nope).astype(BF16), w_uv.reshape(r_kv, n_heads * vdim).astype(BF16),
        row(norm1_g[1]), sh1, sc1, w_dq[0].astype(BF16), row(cq_norm_g[0]), wuq_ext.astype(BF16),
        n_heads, scale, ts)
    o = _attention(q, k, v, tq)

    x3, hn2, idxt, route, cnt = _mix1(
        x2, o, gt1, w_o[0].astype(BF16), row(norm2_g[1]), sh2, sc2,
        router_w[1].T, router_b[1].reshape(n_exp, 1), ts)
    cnt = cnt[:, :ts // MOE_TILE].reshape(-1, n_exp)
    return _moe(x3, hn2, idxt, route, cnt, gt2, row(final_g),
                exp_w_gu[1].astype(BF16), exp_b_gu[1], exp_w_dn[1].astype(BF16), exp_b_dn[1], True)
```

```python
import functools
import math

import jax
import jax.numpy as jnp
from jax import lax
from jax.experimental import pallas as pl
from jax.experimental.pallas import tpu as pltpu

CHUNK = 64
TOP_K = 4
ROPE_THETA = 10000.0
SWIGLU_ALPHA = 1.702
SWIGLU_LIMIT = 7.0
EPS = 1e-6

LANES = 128
SUBLANES = 8
VMEM_LIMIT_BYTES = 56 * 1024 * 1024

ROW_ALIGN = SUBLANES
MOE_TILE = 256
EXPERT_ROWS = 512
CONV_HALO = 32

F32 = jnp.float32
BF16 = jnp.bfloat16
I32 = jnp.int32


def _cparams(sem):
    return pltpu.CompilerParams(dimension_semantics=sem, vmem_limit_bytes=VMEM_LIMIT_BYTES)


def _rms(x, g):
    return x * lax.rsqrt(jnp.mean(x * x, axis=-1, keepdims=True) + EPS) * g


def _round_up(a, m):
    return (a + m - 1) // m * m


def _mod_kernel(c_ref, w_ref, b_ref, o_ref):
    c = c_ref[...]
    ca = c * jax.nn.sigmoid(c)
    o_ref[...] = jnp.dot(ca, w_ref[...], preferred_element_type=F32) + b_ref[...]


def _modulation(c, mod_w, mod_b):
    depth, d, d6 = mod_w.shape
    b = c.shape[0]
    bp = _round_up(b, SUBLANES)
    cp = jnp.zeros((bp, d), F32).at[:b].set(c)
    tn = d6 // 4
    out = pl.pallas_call(
        _mod_kernel,
        out_shape=jax.ShapeDtypeStruct((depth, bp, d6), F32),
        grid=(depth, d6 // tn),
        in_specs=[
            pl.BlockSpec((bp, d), lambda l, j: (0, 0)),
            pl.BlockSpec((None, d, tn), lambda l, j: (l, 0, j)),
            pl.BlockSpec((None, 1, tn), lambda l, j: (l, 0, j)),
        ],
        out_specs=pl.BlockSpec((None, bp, tn), lambda l, j: (l, 0, j)),
        compiler_params=_cparams(("arbitrary", "arbitrary")),
        name="adaln_modulation",
    )(cp, mod_w, mod_b.reshape(depth, 1, d6))
    return out[:, :b]


def _pre_moe(x1, g2, sh2, sc2, rwt, rb, hn_ref, idxt_ref, route_ref, cnt_ref):
    ts = x1.shape[0]
    n_exp = rwt.shape[0]
    hn = _rms(x1, g2) * (1.0 + sc2) + sh2
    hn_ref[...] = hn.astype(BF16)
    logits = lax.dot_general(rwt, hn, (((1,), (1,)), ((), ())),
                             precision=lax.Precision.HIGHEST,
                             preferred_element_type=F32) + rb
    e_iota = lax.broadcasted_iota(I32, (n_exp, ts), 0)
    vals, idxs = [], []
    cur = logits
    for _ in range(TOP_K):
        m = jnp.max(cur, axis=0, keepdims=True)
        i = jnp.min(jnp.where(cur == m, e_iota, n_exp), axis=0, keepdims=True)
        vals.append(m)
        idxs.append(i)
        cur = jnp.where(e_iota == i, -jnp.inf, cur)
    exps = [jnp.exp(v - vals[0]) for v in vals]
    den = exps[0]
    for e in exps[1:]:
        den = den + e
    gates = [e / den for e in exps]
    idxt_ref[...] = jnp.concatenate(idxs, axis=0)
    rows = jnp.concatenate([i.astype(F32) for i in idxs] + gates
                           + [jnp.zeros((LANES - 2 * TOP_K, ts), F32)], axis=0)
    route_ref[...] = rows.T
    onehot = jnp.zeros((n_exp, ts), F32)
    for i in idxs:
        onehot = onehot + (e_iota == i).astype(F32)
    sel = (lax.broadcasted_iota(I32, (SUBLANES, ts), 1) // MOE_TILE
           == lax.broadcasted_iota(I32, (SUBLANES, ts), 0)).astype(BF16)
    cnt = lax.dot_general(sel, onehot.astype(BF16), (((1,), (1,)), ((), ())),
                          preferred_element_type=F32)
    cnt_ref[...] = cnt.astype(I32)


def _pre_moe_specs(b, s, d, ts, n_exp):
    ns = s // ts
    out_shape = [
        jax.ShapeDtypeStruct((b, s, d), F32),
        jax.ShapeDtypeStruct((b, s, d), BF16),
        jax.ShapeDtypeStruct((TOP_K, b * s), I32),
        jax.ShapeDtypeStruct((b * s, LANES), F32),
        jax.ShapeDtypeStruct((b * ns, SUBLANES, n_exp), I32),
    ]
    out_specs = [
        pl.BlockSpec((None, ts, d), lambda i, j: (i, j, 0)),
        pl.BlockSpec((None, ts, d), lambda i, j: (i, j, 0)),
        pl.BlockSpec((TOP_K, ts), lambda i, j: (0, i * ns + j)),
        pl.BlockSpec((ts, LANES), lambda i, j: (i * ns + j, 0)),
        pl.BlockSpec((None, SUBLANES, n_exp), lambda i, j: (i * ns + j, 0, 0)),
    ]
    return out_shape, out_specs


def _vec_spec(d):
    return pl.BlockSpec((1, d), lambda i, j: (0, 0))


def _bvec_spec(d):
    return pl.BlockSpec((None, 1, d), lambda i, j: (i, 0, 0))


def _mix0_kernel(x_ref, n1g_ref, sh1_ref, sc1_ref, gt1_ref,
                 pw1w_ref, pw1b_ref, dww_ref, dwb_ref, lng_ref, lnb_ref,
                 pw2w_ref, pw2b_ref, n2g_ref, sh2_ref, sc2_ref, rwt_ref, rb_ref,
                 x1_ref, hn_ref, idxt_ref, route_ref, cnt_ref, buf_ref, conv_ref):
    ts, d = x_ref.shape
    width = dww_ref.shape[0]
    x = x_ref[...]
    hn = _rms(x, n1g_ref[...]) * (1.0 + sc1_ref[...]) + sh1_ref[...]
    hb = hn.astype(BF16)
    a = jnp.dot(hb, pw1w_ref[:, :d], preferred_element_type=F32) + pw1b_ref[:, :d]
    g = jnp.dot(hb, pw1w_ref[:, d:], preferred_element_type=F32) + pw1b_ref[:, d:]
    glu = a * jax.nn.sigmoid(g)

    @pl.when(pl.program_id(1) == 0)
    def _():
        buf_ref[0:CONV_HALO, :] = jnp.zeros((CONV_HALO, d), F32)

    buf_ref[CONV_HALO:, :] = glu
    base = CONV_HALO - (width - 1)
    rc = 32
    lc = min(512, d)
    for r0 in range(0, ts, rc):
        for c0 in range(0, d, lc):
            acc = jnp.zeros((rc, lc), F32)
            for k in range(width):
                acc = acc + dww_ref[k:k + 1, c0:c0 + lc] * buf_ref[base + r0 + k:base + r0 + k + rc, c0:c0 + lc]
            conv_ref[r0:r0 + rc, c0:c0 + lc] = acc
    buf_ref[0:CONV_HALO, :] = buf_ref[ts:ts + CONV_HALO, :]
    u = conv_ref[...] + dwb_ref[...]
    mu = jnp.mean(u, axis=-1, keepdims=True)
    dlt = u - mu
    var = jnp.mean(dlt * dlt, axis=-1, keepdims=True)
    u = dlt * lax.rsqrt(var + EPS) * lng_ref[...] + lnb_ref[...]
    u = u * jax.nn.sigmoid(u)
    y = jnp.dot(u.astype(BF16), pw2w_ref[...], preferred_element_type=F32) + pw2b_ref[...]
    x1 = x + gt1_ref[...] * y
    x1_ref[...] = x1
    _pre_moe(x1, n2g_ref[...], sh2_ref[...], sc2_ref[...], rwt_ref[...], rb_ref[...],
             hn_ref, idxt_ref, route_ref, cnt_ref)


def _mix0(x, n1g, sh1, sc1, gt1, pw1w, pw1b, dww, dwb, lng, lnb, pw2w, pw2b,
          n2g, sh2, sc2, rwt, rb, ts):
    b, s, d = x.shape
    n_exp = rwt.shape[0]
    width = dww.shape[0]
    assert width - 1 <= CONV_HALO and ts % MOE_TILE == 0 and ts // MOE_TILE <= SUBLANES
    out_shape, out_specs = _pre_moe_specs(b, s, d, ts, n_exp)
    full = lambda shp: pl.BlockSpec(shp, lambda i, j: (0,) * len(shp))
    return pl.pallas_call(
        _mix0_kernel,
        out_shape=out_shape,
        grid=(b, s // ts),
        in_specs=[
            pl.BlockSpec((None, ts, d), lambda i, j: (i, j, 0)),
            _vec_spec(d), _bvec_spec(d), _bvec_spec(d), _bvec_spec(d),
            full((d, 2 * d)), full((1, 2 * d)), full((width, d)), full((1, d)),
            full((1, d)), full((1, d)), full((d, d)), full((1, d)),
            _vec_spec(d), _bvec_spec(d), _bvec_spec(d),
            full((n_exp, d)), full((n_exp, 1)),
        ],
        out_specs=out_specs,
        scratch_shapes=[pltpu.VMEM((CONV_HALO + ts, d), F32), pltpu.VMEM((ts, d), F32)],
        compiler_params=_cparams(("arbitrary", "arbitrary")),
        name="conformer_mixer",
    )(x, n1g, sh1, sc1, gt1, pw1w, pw1b, dww, dwb, lng, lnb, pw2w, pw2b,
      n2g, sh2, sc2, rwt, rb)


def _mix1_kernel(x_ref, o_ref, gt1_ref, wo_ref, n2g_ref, sh2_ref, sc2_ref, rwt_ref, rb_ref,
                 x1_ref, hn_ref, idxt_ref, route_ref, cnt_ref):
    y = jnp.dot(o_ref[...], wo_ref[...], preferred_element_type=F32)
    x1 = x_ref[...] + gt1_ref[...] * y
    x1_ref[...] = x1
    _pre_moe(x1, n2g_ref[...], sh2_ref[...], sc2_ref[...], rwt_ref[...], rb_ref[...],
             hn_ref, idxt_ref, route_ref, cnt_ref)


def _mix1(x, o, gt1, wo, n2g, sh2, sc2, rwt, rb, ts):
    b, s, d = x.shape
    n_exp = rwt.shape[0]
    do = o.shape[-1]
    assert ts % MOE_TILE == 0 and ts // MOE_TILE <= SUBLANES
    out_shape, out_specs = _pre_moe_specs(b, s, d, ts, n_exp)
    full = lambda shp: pl.BlockSpec(shp, lambda i, j: (0,) * len(shp))
    return pl.pallas_call(
        _mix1_kernel,
        out_shape=out_shape,
        grid=(b, s // ts),
        in_specs=[
            pl.BlockSpec((None, ts, d), lambda i, j: (i, j, 0)),
            pl.BlockSpec((None, ts, do), lambda i, j: (i, j, 0)),
            _bvec_spec(d), full((do, d)),
            _vec_spec(d), _bvec_spec(d), _bvec_spec(d),
            full((n_exp, d)), full((n_exp, 1)),
        ],
        out_specs=out_specs,
        compiler_params=_cparams(("arbitrary", "arbitrary")),
        name="attn_out_mixer",
    )(x, o, gt1, wo, n2g, sh2, sc2, rwt, rb)


def _moe_dims(n_tok, n_exp):
    n_tiles = n_tok // MOE_TILE
    loc_rows = _round_up(TOP_K * MOE_TILE + n_exp * (ROW_ALIGN - 1), LANES)
    n_chunks = loc_rows // ROW_ALIGN
    max_rows = TOP_K * n_tok + n_tiles * n_exp * (ROW_ALIGN - 1) + n_exp * (EXPERT_ROWS - ROW_ALIGN)
    max_blocks = -(-max_rows // EXPERT_ROWS)
    return n_tiles, loc_rows, n_chunks, max_blocks


def _routing_tables(cnt, n_chunks, max_blocks):
    n_tiles, n_exp = cnt.shape
    cnt8 = _round_up(cnt, ROW_ALIGN)
    off = jnp.cumsum(cnt8, axis=1) - cnt8
    seg_len = cnt8.sum(axis=0)
    seg_pad = _round_up(seg_len, EXPERT_ROWS)
    seg_end = jnp.cumsum(seg_pad)
    seg_start = seg_end - seg_pad
    run_start = seg_start[None, :] + jnp.cumsum(cnt8, axis=0) - cnt8
    n_used = (cnt8.sum(axis=1) // ROW_ALIGN).astype(I32)
    c_row = jnp.arange(n_chunks, dtype=I32) * ROW_ALIGN
    local_end = off + cnt8
    in_run = ((c_row[None, :, None] >= off[:, None, :])
              & (c_row[None, :, None] < local_end[:, None, :]))
    delta = jnp.sum(jnp.where(in_run, (run_start - off)[:, None, :], 0), axis=-1)
    tab = ((delta + c_row[None, :]) // ROW_ALIGN).astype(I32)
    n_blocks = (seg_end[-1] // EXPERT_ROWS).astype(I32)
    blk_row = jnp.arange(max_blocks, dtype=I32) * EXPERT_ROWS
    blk_row = jnp.minimum(blk_row, seg_end[-1] - EXPERT_ROWS)
    blk_exp = jnp.minimum((blk_row[:, None] >= seg_end[None, :]).sum(axis=-1), n_exp - 1).astype(I32)
    per = EXPERT_ROWS // ROW_ALIGN
    n_tail = (seg_pad - seg_len) // ROW_ALIGN
    z_end = jnp.cumsum(n_tail)
    z_start = z_end - n_tail
    pos = jnp.arange(n_exp * per, dtype=I32)
    in_tail = (pos[:, None] >= z_start[None, :]) & (pos[:, None] < z_end[None, :])
    first = (seg_start + seg_len) // ROW_ALIGN - z_start
    ztab = (jnp.sum(jnp.where(in_tail, first[None, :], 0), axis=-1) + pos).astype(I32)
    ztab = jnp.where(pos < z_end[-1], ztab, 0)
    n_zero = z_end[-1].astype(I32)
    return tab.reshape(-1), n_used, ztab, n_zero.reshape(1), blk_exp, n_blocks.reshape(1)


def _dispatch_kernel(n_exp, tab_ref, nused_ref, ztab_ref, nzero_ref, nblk_ref,
                     x_ref, idxt_ref, xs_ref, loc_ref, zero_ref, sem, zsem):
    i = pl.program_id(0)
    n = pl.num_programs(0)
    slot = i % 2
    tile = x_ref.shape[0]
    loc_rows = loc_ref.shape[1]
    n_chunks = loc_rows // ROW_ALIGN

    def chunk_copy(sl, c, dst):
        return pltpu.make_async_copy(
            loc_ref.at[sl, pl.ds(pl.multiple_of(c * ROW_ALIGN, ROW_ALIGN), ROW_ALIGN)],
            xs_ref.at[pl.ds(pl.multiple_of(dst * ROW_ALIGN, ROW_ALIGN), ROW_ALIGN)],
            sem.at[sl])

    def wait_slot(sl, count):
        def body(c, carry):
            chunk_copy(sl, 0, 0).wait()
            return carry
        lax.fori_loop(0, count, body, 0)

    @pl.when(i >= 2)
    def _():
        wait_slot(slot, nused_ref[i - 2])

    idx = idxt_ref[...]
    e_iota = lax.broadcasted_iota(I32, (n_exp, tile), 0)
    hits = [idx[k:k + 1, :] == e_iota for k in range(TOP_K)]
    onehot = jnp.zeros((n_exp, tile), F32)
    for h in hits:
        onehot = onehot + h.astype(F32)
    upper = (lax.broadcasted_iota(I32, (tile, tile), 0)
             < lax.broadcasted_iota(I32, (tile, tile), 1)).astype(BF16)
    pre = jnp.dot(onehot.astype(BF16), upper, preferred_element_type=F32)
    cnt = jnp.sum(onehot, axis=1, keepdims=True)
    cnt8 = jnp.ceil(cnt / ROW_ALIGN) * ROW_ALIGN
    lower = (lax.broadcasted_iota(I32, (n_exp, n_exp), 1)
             < lax.broadcasted_iota(I32, (n_exp, n_exp), 0)).astype(BF16)
    off = jnp.dot(lower, jnp.broadcast_to(cnt8, (n_exp, LANES)).astype(BF16),
                  preferred_element_type=F32)[:, 0:1]
    base = off + pre
    r_iota = lax.broadcasted_iota(I32, (loc_rows, tile), 0)
    perm = jnp.zeros((loc_rows, tile), F32)
    for h in hits:
        dest = jnp.sum(jnp.where(h, base, 0.0), axis=0, keepdims=True).astype(I32)
        perm = perm + (r_iota == dest).astype(F32)
    loc_ref[slot] = jnp.dot(perm.astype(BF16), x_ref[...], preferred_element_type=F32)

    def issue(c, carry):
        chunk_copy(slot, c, tab_ref[i * n_chunks + c]).start()
        return carry
    lax.fori_loop(0, nused_ref[i], issue, 0)

    @pl.when(i == n - 1)
    def _():
        zero_ref[...] = jnp.zeros(zero_ref.shape, F32)
        max_blocks = xs_ref.shape[0] // EXPERT_ROWS

        def zcopy(dst):
            return pltpu.make_async_copy(
                zero_ref.at[pl.ds(0, ROW_ALIGN)],
                xs_ref.at[pl.ds(pl.multiple_of(dst * ROW_ALIGN, ROW_ALIGN), ROW_ALIGN)],
                zsem.at[0])

        def bcopy(blk):
            return pltpu.make_async_copy(
                zero_ref,
                xs_ref.at[pl.ds(pl.multiple_of(blk * EXPERT_ROWS, EXPERT_ROWS), EXPERT_ROWS)],
                zsem.at[1])

        def zissue(c, carry):
            zcopy(ztab_ref[c]).start()
            return carry
        lax.fori_loop(0, nzero_ref[0], zissue, 0)

        def bissue(blk, carry):
            bcopy(blk).start()
            return carry
        lax.fori_loop(nblk_ref[0], max_blocks, bissue, 0)

        @pl.when(i >= 1)
        def _():
            wait_slot(1 - slot, nused_ref[i - 1])
        wait_slot(slot, nused_ref[i])

        def zwait(c, carry):
            zcopy(0).wait()
            return carry
        lax.fori_loop(0, nzero_ref[0], zwait, 0)

        def bwait(blk, carry):
            bcopy(0).wait()
            return carry
        lax.fori_loop(nblk_ref[0], max_blocks, bwait, 0)


def _dispatch(hn2, idxt, tab, n_used, ztab, n_zero, n_blocks, n_exp, loc_rows, max_blocks):
    n_tok, d = hn2.shape
    n_tiles = n_tok // MOE_TILE
    return pl.pallas_call(
        functools.partial(_dispatch_kernel, n_exp),
        out_shape=jax.ShapeDtypeStruct((max_blocks * EXPERT_ROWS, d), F32),
        grid_spec=pltpu.PrefetchScalarGridSpec(
            num_scalar_prefetch=5,
            grid=(n_tiles,),
            in_specs=[
                pl.BlockSpec((MOE_TILE, d), lambda i, *_: (i, 0)),
                pl.BlockSpec((TOP_K, MOE_TILE), lambda i, *_: (0, i)),
            ],
            out_specs=pl.BlockSpec(memory_space=pl.ANY),
            scratch_shapes=[
                pltpu.VMEM((2, loc_rows, d), F32),
                pltpu.VMEM((EXPERT_ROWS, d), F32),
                pltpu.SemaphoreType.DMA((2,)),
                pltpu.SemaphoreType.DMA((2,)),
            ],
        ),
        compiler_params=_cparams(("arbitrary",)),
        name="moe_dispatch",
    )(tab, n_used, ztab, n_zero, n_blocks, hn2, idxt)


def _expert_kernel(be_ref, nb_ref, x_ref, wgu_ref, bgu_ref, wdn_ref, bdn_ref, y_ref):
    de = wdn_ref.shape[0]

    @pl.when(pl.program_id(0) < nb_ref[0])
    def _():
        xb = x_ref[...].astype(BF16)
        g = jnp.dot(xb, wgu_ref[:, :de], preferred_element_type=F32) + bgu_ref[:, :de]
        u = jnp.dot(xb, wgu_ref[:, de:], preferred_element_type=F32) + bgu_ref[:, de:]
        g = jnp.minimum(g, SWIGLU_LIMIT)
        u = jnp.clip(u, -SWIGLU_LIMIT, SWIGLU_LIMIT)
        a = (u + 1.0) * g * jax.nn.sigmoid(SWIGLU_ALPHA * g)
        y_ref[...] = jnp.dot(a.astype(BF16), wdn_ref[...], preferred_element_type=F32) + bdn_ref[...]

    @pl.when(pl.program_id(0) >= nb_ref[0])
    def _():
        y_ref[...] = jnp.zeros(y_ref.shape, y_ref.dtype)


def _experts(xs, blk_exp, n_blocks, wgu, bgu, wdn, bdn):
    rows, d = xs.shape
    n_exp, _, de2 = wgu.shape
    de = de2 // 2
    max_blocks = rows // EXPERT_ROWS
    row_map = lambda b, be, nb: (jnp.minimum(b, nb[0] - 1), 0)
    exp_map = lambda b, be, nb: (be[b], 0, 0)
    return pl.pallas_call(
        _expert_kernel,
        out_shape=jax.ShapeDtypeStruct((rows, d), F32),
        grid_spec=pltpu.PrefetchScalarGridSpec(
            num_scalar_prefetch=2,
            grid=(max_blocks,),
            in_specs=[
                pl.BlockSpec((EXPERT_ROWS, d), row_map),
                pl.BlockSpec((None, d, de2), exp_map),
                pl.BlockSpec((None, 1, de2), exp_map),
                pl.BlockSpec((None, de, d), exp_map),
                pl.BlockSpec((None, 1, d), exp_map),
            ],
            out_specs=pl.BlockSpec((EXPERT_ROWS, d), lambda b, be, nb: (b, 0)),
        ),
        compiler_params=_cparams(("arbitrary",)),
        name="moe_experts",
    )(blk_exp, n_blocks, xs, wgu, bgu.reshape(n_exp, 1, de2), wdn, bdn.reshape(n_exp, 1, d))


def _combine_kernel(final_norm, n_exp, tab_ref, nused_ref,
                    route_ref, x1_ref, g2_ref, fg_ref, ys_ref, out_ref, loc_ref, sem):
    i = pl.program_id(0)
    n = pl.num_programs(0)
    slot = i % 2
    tile = x1_ref.shape[0]
    loc_rows = loc_ref.shape[1]
    n_chunks = loc_rows // ROW_ALIGN

    def chunk_copy(sl, c, src):
        return pltpu.make_async_copy(
            ys_ref.at[pl.ds(pl.multiple_of(src * ROW_ALIGN, ROW_ALIGN), ROW_ALIGN)],
            loc_ref.at[sl, pl.ds(pl.multiple_of(c * ROW_ALIGN, ROW_ALIGN), ROW_ALIGN)],
            sem.at[sl])

    def issue_tile(t, sl):
        def body(c, carry):
            chunk_copy(sl, c, tab_ref[t * n_chunks + c]).start()
            return carry
        lax.fori_loop(0, nused_ref[t], body, 0)

    @pl.when(i == 0)
    def _():
        issue_tile(0, 0)

    @pl.when(i + 1 < n)
    def _():
        issue_tile(i + 1, 1 - slot)

    route = route_ref[...]
    e_iota = lax.broadcasted_iota(I32, (tile, n_exp), 1)
    hits = [route[:, k:k + 1].astype(I32) == e_iota for k in range(TOP_K)]
    gates = [route[:, TOP_K + k:TOP_K + k + 1] for k in range(TOP_K)]
    onehot = jnp.zeros((tile, n_exp), F32)
    for h in hits:
        onehot = onehot + h.astype(F32)
    lower = (lax.broadcasted_iota(I32, (tile, tile), 1)
             < lax.broadcasted_iota(I32, (tile, tile), 0)).astype(BF16)
    pre = jnp.dot(lower, onehot.astype(BF16), preferred_element_type=F32)
    cnt = jnp.sum(onehot, axis=0, keepdims=True)
    cnt8 = jnp.ceil(cnt / ROW_ALIGN) * ROW_ALIGN
    upper = (lax.broadcasted_iota(I32, (n_exp, n_exp), 0)
             < lax.broadcasted_iota(I32, (n_exp, n_exp), 1)).astype(BF16)
    off = jnp.dot(jnp.broadcast_to(cnt8, (SUBLANES, n_exp)).astype(BF16), upper,
                  preferred_element_type=F32)[0:1, :]
    base = off + pre
    r_iota = lax.broadcasted_iota(I32, (tile, loc_rows), 1)
    comb = jnp.zeros((tile, loc_rows), F32)
    for h, gk in zip(hits, gates):
        dest = jnp.sum(jnp.where(h, base, 0.0), axis=1, keepdims=True).astype(I32)
        comb = comb + jnp.where(r_iota == dest, gk, 0.0)

    def wbody(c, carry):
        chunk_copy(slot, 0, 0).wait()
        return carry
    lax.fori_loop(0, nused_ref[i], wbody, 0)

    used_rows = nused_ref[i] * ROW_ALIGN
    row_ok = lax.broadcasted_iota(I32, (loc_rows, 1), 0) < used_rows
    yl = jnp.where(row_ok, loc_ref[slot], 0.0).astype(BF16)
    moe = jnp.dot(comb.astype(BF16), yl, preferred_element_type=F32)
    out = x1_ref[...] + g2_ref[...] * moe
    if final_norm:
        out = _rms(out, fg_ref[...])
    out_ref[...] = out


def _combine(ys, route, x1, gate2, final_g, tab, n_used, n_exp, loc_rows, final_norm):
    b, s, d = x1.shape
    n_tok = b * s
    n_tiles = n_tok // MOE_TILE
    tiles_per_seq = s // MOE_TILE
    out = pl.pallas_call(
        functools.partial(_combine_kernel, final_norm, n_exp),
        out_shape=jax.ShapeDtypeStruct((n_tok, d), F32),
        grid_spec=pltpu.PrefetchScalarGridSpec(
            num_scalar_prefetch=2,
            grid=(n_tiles,),
            in_specs=[
                pl.BlockSpec((MOE_TILE, LANES), lambda i, *_: (i, 0)),
                pl.BlockSpec((MOE_TILE, d), lambda i, *_: (i, 0)),
                pl.BlockSpec((None, 1, d), lambda i, *_: (i // tiles_per_seq, 0, 0)),
                pl.BlockSpec((1, d), lambda i, *_: (0, 0)),
                pl.BlockSpec(memory_space=pl.ANY),
            ],
            out_specs=pl.BlockSpec((MOE_TILE, d), lambda i, *_: (i, 0)),
            scratch_shapes=[
                pltpu.VMEM((2, loc_rows, d), F32),
                pltpu.SemaphoreType.DMA((2,)),
            ],
        ),
        compiler_params=_cparams(("arbitrary",)),
        name="moe_combine",
    )(tab, n_used, route, x1.reshape(n_tok, d), gate2, final_g, ys)
    return out.reshape(b, s, d)


def _moe(x1, hn2, idxt, route, cnt, gate2, final_g, wgu, bgu, wdn, bdn, final_norm):
    b, s, d = x1.shape
    n_tok = b * s
    n_exp = wgu.shape[0]
    n_tiles, loc_rows, n_chunks, max_blocks = _moe_dims(n_tok, n_exp)
    tab, n_used, ztab, n_zero, blk_exp, n_blocks = _routing_tables(cnt, n_chunks, max_blocks)
    xs = _dispatch(hn2.reshape(n_tok, d), idxt, tab, n_used, ztab, n_zero, n_blocks,
                   n_exp, loc_rows, max_blocks)
    ys = _experts(xs, blk_exp, n_blocks, wgu, bgu, wdn, bdn)
    return _combine(ys, route, x1, gate2, final_g, tab, n_used, n_exp, loc_rows, final_norm)


def _qkv_kernel(n_heads, scale, x_ref, cc_ref, ss_ref, kvg_ref, wdkv_ref, ckvg_ref, wuk_ref, wuv_ref,
                n1g_ref, sh1_ref, sc1_ref, wdq_ref, cqg_ref, wuq_ref,
                q_ref, k_ref, v_ref):
    x = x_ref[...]
    cc = cc_ref[...]
    ss = ss_ref[...]
    r_kv = ckvg_ref.shape[1]
    hk = _rms(x, kvg_ref[...]).astype(BF16)
    lat = jnp.dot(hk, wdkv_ref[...], preferred_element_type=F32)
    ckv = _rms(lat[:, :r_kv], ckvg_ref[...]).astype(BF16)
    krot = lat[:, r_kv:r_kv + LANES] * cc + lat[:, r_kv + LANES:r_kv + 2 * LANES] * ss
    kn = jnp.dot(ckv, wuk_ref[...], preferred_element_type=F32)
    vt = lax.dot_general(wuv_ref[...], ckv, (((1,), (1,)), ((), ())), preferred_element_type=F32)
    hq = (_rms(x, n1g_ref[...]) * (1.0 + sc1_ref[...]) + sh1_ref[...]).astype(BF16)
    cq = _rms(jnp.dot(hq, wdq_ref[...], preferred_element_type=F32), cqg_ref[...]).astype(BF16)
    qq = jnp.dot(cq, wuq_ref[...], preferred_element_type=F32) * scale
    hd = n_heads * LANES
    for h in range(n_heads):
        sl = slice(h * LANES, (h + 1) * LANES)
        k_ref[h, :, 0:LANES] = kn[:, sl].astype(BF16)
        k_ref[h, :, LANES:2 * LANES] = krot.astype(BF16)
        v_ref[h] = vt[h * LANES:(h + 1) * LANES, :].astype(BF16)
        q_ref[h, :, 0:LANES] = qq[:, sl].astype(BF16)
        qrot = qq[:, hd + h * LANES:hd + (h + 1) * LANES] * cc \
            + qq[:, 2 * hd + h * LANES:2 * hd + (h + 1) * LANES] * ss
        q_ref[h, :, LANES:2 * LANES] = qrot.astype(BF16)


def _qkv(x, cc, ss, kvg, wdkv, ckvg, wuk, wuv, n1g, sh1, sc1, wdq, cqg, wuq, n_heads, scale, ts):
    b, s, d = x.shape
    full = lambda a: pl.BlockSpec(a.shape, lambda i, j: (0,) * a.ndim)
    hspec = lambda w: pl.BlockSpec((None, n_heads, ts, w), lambda i, j: (i, 0, j, 0))
    return pl.pallas_call(
        functools.partial(_qkv_kernel, n_heads, scale),
        out_shape=[
            jax.ShapeDtypeStruct((b, n_heads, s, 2 * LANES), BF16),
            jax.ShapeDtypeStruct((b, n_heads, s, 2 * LANES), BF16),
            jax.ShapeDtypeStruct((b, n_heads, s // ts, LANES, ts), BF16),
        ],
        grid=(b, s // ts),
        in_specs=[
            pl.BlockSpec((None, ts, d), lambda i, j: (i, j, 0)),
            pl.BlockSpec((None, ts, LANES), lambda i, j: (i, j, 0)),
            pl.BlockSpec((None, ts, LANES), lambda i, j: (i, j, 0)),
            full(kvg), full(wdkv), full(ckvg), full(wuk), full(wuv),
            full(n1g), _bvec_spec(d), _bvec_spec(d), full(wdq), full(cqg), full(wuq),
        ],
        out_specs=[hspec(2 * LANES), hspec(2 * LANES),
                   pl.BlockSpec((None, n_heads, None, LANES, ts), lambda i, j: (i, 0, j, 0, 0))],
        compiler_params=_cparams(("arbitrary", "arbitrary")),
        name="mla_qkv",
    )(x, cc, ss, kvg, wdkv, ckvg, wuk, wuv, n1g, sh1, sc1, wdq, cqg, wuq)


def _attn_kernel(q_ref, k_ref, vt_ref, o_ref, m_ref, l_ref, acc_ref):
    tq = q_ref.shape[0]
    qi = pl.program_id(2)
    q = q_ref[...]
    m_ref[...] = jnp.full(m_ref.shape, -jnp.inf, F32)
    l_ref[...] = jnp.zeros(l_ref.shape, F32)
    acc_ref[...] = jnp.zeros(acc_ref.shape, F32)

    def scores(ki, masked):
        start = pl.multiple_of(ki * tq, tq)
        k = k_ref[pl.ds(start, tq), :]
        st = lax.dot_general(k, q, (((1,), (1,)), ((), ())), preferred_element_type=F32)
        if masked:
            kc = lax.broadcasted_iota(I32, (tq, tq), 0) // CHUNK
            qc = lax.broadcasted_iota(I32, (tq, tq), 1) // CHUNK
            st = jnp.where(kc <= qc, st, -jnp.inf)
        return st

    def update(ki, st, m_old, l_old, acc_old):
        m_new = jnp.maximum(m_old, jnp.max(st, axis=0, keepdims=True))
        p = jnp.exp2(st - m_new)
        alpha = jnp.exp2(m_old - m_new)
        l_new = alpha * l_old + jnp.sum(p, axis=0, keepdims=True)
        acc_new = alpha * acc_old + jnp.dot(vt_ref[ki], p.astype(BF16), preferred_element_type=F32)
        return m_new, l_new, acc_new

    def step(tiles):
        sts = [scores(ki, masked) for ki, masked in tiles]
        state = (m_ref[...], l_ref[...], acc_ref[...])
        for (ki, _), st in zip(tiles, sts):
            state = update(ki, st, *state)
        m_ref[...], l_ref[...], acc_ref[...] = state

    def body(j, carry):
        step([(2 * j, False), (2 * j + 1, False)])
        return carry
    lax.fori_loop(0, qi // 2, body, 0)

    @pl.when(qi % 2 == 1)
    def _():
        step([(qi - 1, False), (qi, True)])

    @pl.when(qi % 2 == 0)
    def _():
        step([(qi, True)])

    o_ref[...] = (acc_ref[...] / l_ref[...]).T.astype(o_ref.dtype)


def _attention(q, k, vt, tq):
    b, h, s, dk = q.shape
    nk, dv, tk = vt.shape[2:]
    assert tq % CHUNK == 0 and s % tq == 0 and tk == tq
    return pl.pallas_call(
        _attn_kernel,
        out_shape=jax.ShapeDtypeStruct((b, s, h * dv), BF16),
        grid=(b, h, s // tq),
        in_specs=[
            pl.BlockSpec((None, None, tq, dk), lambda i, j, t: (i, j, t, 0)),
            pl.BlockSpec((None, None, s, dk), lambda i, j, t: (i, j, 0, 0)),
            pl.BlockSpec((None, None, nk, dv, tk), lambda i, j, t: (i, j, 0, 0, 0)),
        ],
        out_specs=pl.BlockSpec((None, tq, dv), lambda i, j, t: (i, t, j)),
        scratch_shapes=[pltpu.VMEM((1, tq), F32), pltpu.VMEM((1, tq), F32), pltpu.VMEM((dv, tq), F32)],
        compiler_params=_cparams(("arbitrary", "arbitrary", "arbitrary")),
        name="mla_attention",
    )(q, k, vt)


def _swap_halves(w):
    half = w.shape[-1] // 2
    return jnp.concatenate([w[..., half:], w[..., :half]], axis=-1)


def _pad_lanes(w):
    pad = LANES - w.shape[-1]
    return jnp.concatenate([w, jnp.zeros(w.shape[:-1] + (pad,), w.dtype)], axis=-1)


def kernel(x, c, positions, mod_w, mod_b, norm1_g, norm2_g, conv_pw1_w, conv_pw1_b, conv_dw_w, conv_dw_b, conv_ln_g, conv_ln_b, conv_pw2_w, conv_pw2_b, kv_norm_g, w_dkv, ckv_norm_g, w_uk, w_uv, w_dq, cq_norm_g, w_uq, w_o, router_w, router_b, exp_w_gu, exp_b_gu, exp_w_dn, exp_b_dn, final_g):
    b, s, d = x.shape
    n_heads, nope = w_uk.shape[1], w_uk.shape[2]
    r_kv = ckv_norm_g.shape[0]
    rope = w_dkv.shape[1] - r_kv
    vdim = w_uv.shape[2]
    n_exp = router_w.shape[2]
    assert nope == LANES and vdim == LANES and rope <= LANES and d % LANES == 0
    ts = min(512, s)
    tq = min(512, s)

    mod = _modulation(c, mod_w, mod_b)
    mods = [[m.reshape(b, 1, d) for m in jnp.split(mod[l], 6, axis=-1)] for l in range(2)]
    row = lambda v: v.reshape(1, -1)

    half = rope // 2
    inv = jnp.exp(-(2.0 * math.log(ROPE_THETA) / rope) * jnp.arange(half, dtype=F32))
    ang = positions.astype(F32)[..., None] * inv
    cos, sin = jnp.cos(ang), jnp.sin(ang)
    cc = _pad_lanes(jnp.concatenate([cos, cos], axis=-1))
    ss = _pad_lanes(jnp.concatenate([-sin, sin], axis=-1))

    sh1, sc1, gt1, sh2, sc2, gt2 = mods[0]
    x1, hn2, idxt, route, cnt = _mix0(
        x, row(norm1_g[0]), sh1, sc1, gt1,
        conv_pw1_w[0].astype(BF16), row(conv_pw1_b[0]), conv_dw_w[0], row(conv_dw_b[0]),
        row(conv_ln_g[0]), row(conv_ln_b[0]), conv_pw2_w[0].astype(BF16), row(conv_pw2_b[0]),
        row(norm2_g[0]), sh2, sc2, router_w[0].T, router_b[0].reshape(n_exp, 1), ts)
    cnt = cnt[:, :ts // MOE_TILE].reshape(-1, n_exp)
    x2 = _moe(x1, hn2, idxt, route, cnt, gt2, row(final_g),
              exp_w_gu[0].astype(BF16), exp_b_gu[0], exp_w_dn[0].astype(BF16), exp_b_dn[0], False)

    sh1, sc1, gt1, sh2, sc2, gt2 = mods[1]
    wdkv_rope = w_dkv[:, r_kv:]
    wdkv_ext = jnp.concatenate(
        [w_dkv[:, :r_kv], _pad_lanes(wdkv_rope), _pad_lanes(_swap_halves(wdkv_rope))], axis=-1)
    wuq = w_uq[0]
    r_q = wuq.shape[0]
    wuq_rope = wuq[:, :, nope:]
    wuq_ext = jnp.concatenate([
        wuq[:, :, :nope].reshape(r_q, n_heads * LANES),
        _pad_lanes(wuq_rope).reshape(r_q, n_heads * LANES),
        _pad_lanes(_swap_halves(wuq_rope)).reshape(r_q, n_heads * LANES)], axis=-1)
    scale = float((nope + rope) ** -0.5 * math.log2(math.e))
    q, k, v = _qkv(
        x2, cc, ss, row(kv_norm_g), wdkv_ext.astype(BF16), row(ckv_norm_g),
        w_uk.reshape(r_kv, n_heads * nope).astype(BF16), w_uv.reshape(r_kv, n_heads * vdim).T.astype(BF16),
        row(norm1_g[1]), sh1, sc1, w_dq[0].astype(BF16), row(cq_norm_g[0]), wuq_ext.astype(BF16),
        n_heads, scale, ts)
    o = _attention(q, k, v, tq)

    x3, hn2, idxt, route, cnt = _mix1(
        x2, o, gt1, w_o[0].astype(BF16), row(norm2_g[1]), sh2, sc2,
        router_w[1].T, router_b[1].reshape(n_exp, 1), ts)
    cnt = cnt[:, :ts // MOE_TILE].reshape(-1, n_exp)
    return _moe(x3, hn2, idxt, route, cnt, gt2, row(final_g),
                exp_w_gu[1].astype(BF16), exp_b_gu[1], exp_w_dn[1].astype(BF16), exp_b_dn[1], True)
```

```python
import functools
import math

import jax
import jax.numpy as jnp
from jax import lax
from jax.experimental import pallas as pl
from jax.experimental.pallas import tpu as pltpu

CHUNK = 64
TOP_K = 4
ROPE_THETA = 10000.0
SWIGLU_ALPHA = 1.702
SWIGLU_LIMIT = 7.0
EPS = 1e-6

LANES = 128
SUBLANES = 8
VMEM_LIMIT_BYTES = 56 * 1024 * 1024

ROW_ALIGN = SUBLANES
MOE_TILE = 256
EXPERT_ROWS = 512
CONV_HALO = 32

F32 = jnp.float32
BF16 = jnp.bfloat16
I32 = jnp.int32


def _cparams(sem):
    return pltpu.CompilerParams(dimension_semantics=sem, vmem_limit_bytes=VMEM_LIMIT_BYTES)


def _rms(x, g):
    return x * lax.rsqrt(jnp.mean(x * x, axis=-1, keepdims=True) + EPS) * g


def _round_up(a, m):
    return (a + m - 1) // m * m


def _mod_kernel(c_ref, w_ref, b_ref, o_ref):
    c = c_ref[...]
    ca = c * jax.nn.sigmoid(c)
    o_ref[...] = jnp.dot(ca, w_ref[...], preferred_element_type=F32) + b_ref[...]


def _modulation(c, mod_w, mod_b):
    depth, d, d6 = mod_w.shape
    b = c.shape[0]
    bp = _round_up(b, SUBLANES)
    cp = jnp.zeros((bp, d), F32).at[:b].set(c)
    tn = d6 // 4
    out = pl.pallas_call(
        _mod_kernel,
        out_shape=jax.ShapeDtypeStruct((depth, bp, d6), F32),
        grid=(depth, d6 // tn),
        in_specs=[
            pl.BlockSpec((bp, d), lambda l, j: (0, 0)),
            pl.BlockSpec((None, d, tn), lambda l, j: (l, 0, j)),
            pl.BlockSpec((None, 1, tn), lambda l, j: (l, 0, j)),
        ],
        out_specs=pl.BlockSpec((None, bp, tn), lambda l, j: (l, 0, j)),
        compiler_params=_cparams(("arbitrary", "arbitrary")),
        name="adaln_modulation",
    )(cp, mod_w, mod_b.reshape(depth, 1, d6))
    return out[:, :b]


def _pre_moe(x1, g2, sh2, sc2, rwt, rb, hn_ref, idxt_ref, route_ref, cnt_ref):
    ts = x1.shape[0]
    n_exp = rwt.shape[0]
    hn = _rms(x1, g2) * (1.0 + sc2) + sh2
    hn_ref[...] = hn.astype(BF16)
    logits = lax.dot_general(rwt, hn, (((1,), (1,)), ((), ())),
                             precision=lax.Precision.HIGHEST,
                             preferred_element_type=F32) + rb
    e_iota = lax.broadcasted_iota(I32, (n_exp, ts), 0)
    vals, idxs = [], []
    cur = logits
    for _ in range(TOP_K):
        m = jnp.max(cur, axis=0, keepdims=True)
        i = jnp.min(jnp.where(cur == m, e_iota, n_exp), axis=0, keepdims=True)
        vals.append(m)
        idxs.append(i)
        cur = jnp.where(e_iota == i, -jnp.inf, cur)
    exps = [jnp.exp(v - vals[0]) for v in vals]
    den = exps[0]
    for e in exps[1:]:
        den = den + e
    gates = [e / den for e in exps]
    idxt_ref[...] = jnp.concatenate(idxs, axis=0)
    rows = jnp.concatenate([i.astype(F32) for i in idxs] + gates
                           + [jnp.zeros((LANES - 2 * TOP_K, ts), F32)], axis=0)
    route_ref[...] = rows.T
    onehot = jnp.zeros((n_exp, ts), F32)
    for i in idxs:
        onehot = onehot + (e_iota == i).astype(F32)
    sel = (lax.broadcasted_iota(I32, (SUBLANES, ts), 1) // MOE_TILE
           == lax.broadcasted_iota(I32, (SUBLANES, ts), 0)).astype(BF16)
    cnt = lax.dot_general(sel, onehot.astype(BF16), (((1,), (1,)), ((), ())),
                          preferred_element_type=F32)
    cnt_ref[...] = cnt.astype(I32)


def _pre_moe_specs(b, s, d, ts, n_exp):
    ns = s // ts
    out_shape = [
        jax.ShapeDtypeStruct((b, s, d), F32),
        jax.ShapeDtypeStruct((b, s, d), BF16),
        jax.ShapeDtypeStruct((TOP_K, b * s), I32),
        jax.ShapeDtypeStruct((b * s, LANES), F32),
        jax.ShapeDtypeStruct((b * ns, SUBLANES, n_exp), I32),
    ]
    out_specs = [
        pl.BlockSpec((None, ts, d), lambda i, j: (i, j, 0)),
        pl.BlockSpec((None, ts, d), lambda i, j: (i, j, 0)),
        pl.BlockSpec((TOP_K, ts), lambda i, j: (0, i * ns + j)),
        pl.BlockSpec((ts, LANES), lambda i, j: (i * ns + j, 0)),
        pl.BlockSpec((None, SUBLANES, n_exp), lambda i, j: (i * ns + j, 0, 0)),
    ]
    return out_shape, out_specs


def _vec_spec(d):
    return pl.BlockSpec((1, d), lambda i, j: (0, 0))


def _bvec_spec(d):
    return pl.BlockSpec((None, 1, d), lambda i, j: (i, 0, 0))


def _mix0_kernel(x_ref, n1g_ref, sh1_ref, sc1_ref, gt1_ref,
                 pw1w_ref, pw1b_ref, dww_ref, dwb_ref, lng_ref, lnb_ref,
                 pw2w_ref, pw2b_ref, n2g_ref, sh2_ref, sc2_ref, rwt_ref, rb_ref,
                 x1_ref, hn_ref, idxt_ref, route_ref, cnt_ref, buf_ref, conv_ref, shift_ref):
    ts, d = x_ref.shape
    width = dww_ref.shape[0]
    x = x_ref[...]
    hn = _rms(x, n1g_ref[...]) * (1.0 + sc1_ref[...]) + sh1_ref[...]
    hb = hn.astype(BF16)
    a = jnp.dot(hb, pw1w_ref[:, :d], preferred_element_type=F32) + pw1b_ref[:, :d]
    g = jnp.dot(hb, pw1w_ref[:, d:], preferred_element_type=F32) + pw1b_ref[:, d:]
    glu = a * jax.nn.sigmoid(g)

    @pl.when(pl.program_id(1) == 0)
    def _():
        buf_ref[0:CONV_HALO, :] = jnp.zeros((CONV_HALO, d), F32)

    buf_ref[CONV_HALO:, :] = glu
    base = CONV_HALO - (width - 1)
    rc = 32
    lc = min(512, d)
    sh_rows = shift_ref.shape[1]
    for c0 in range(0, d, lc):
        for r in range(1, SUBLANES):
            shift_ref[r - 1] = buf_ref[r:r + sh_rows, c0:c0 + lc]
        for r0 in range(0, ts, rc):
            acc = jnp.zeros((rc, lc), F32)
            for k in range(width):
                q8, r = divmod(base + k, SUBLANES)
                lo = q8 * SUBLANES + r0
                if r == 0:
                    win = buf_ref[lo:lo + rc, c0:c0 + lc]
                else:
                    win = shift_ref[r - 1, lo:lo + rc, :]
                acc = acc + dww_ref[k:k + 1, c0:c0 + lc] * win
            conv_ref[r0:r0 + rc, c0:c0 + lc] = acc
    buf_ref[0:CONV_HALO, :] = buf_ref[ts:ts + CONV_HALO, :]
    u = conv_ref[...] + dwb_ref[...]
    mu = jnp.mean(u, axis=-1, keepdims=True)
    dlt = u - mu
    var = jnp.mean(dlt * dlt, axis=-1, keepdims=True)
    u = dlt * lax.rsqrt(var + EPS) * lng_ref[...] + lnb_ref[...]
    u = u * jax.nn.sigmoid(u)
    y = jnp.dot(u.astype(BF16), pw2w_ref[...], preferred_element_type=F32) + pw2b_ref[...]
    x1 = x + gt1_ref[...] * y
    x1_ref[...] = x1
    _pre_moe(x1, n2g_ref[...], sh2_ref[...], sc2_ref[...], rwt_ref[...], rb_ref[...],
             hn_ref, idxt_ref, route_ref, cnt_ref)


def _mix0(x, n1g, sh1, sc1, gt1, pw1w, pw1b, dww, dwb, lng, lnb, pw2w, pw2b,
          n2g, sh2, sc2, rwt, rb, ts):
    b, s, d = x.shape
    n_exp = rwt.shape[0]
    width = dww.shape[0]
    assert width - 1 <= CONV_HALO and ts % MOE_TILE == 0 and ts // MOE_TILE <= SUBLANES
    out_shape, out_specs = _pre_moe_specs(b, s, d, ts, n_exp)
    full = lambda shp: pl.BlockSpec(shp, lambda i, j: (0,) * len(shp))
    return pl.pallas_call(
        _mix0_kernel,
        out_shape=out_shape,
        grid=(b, s // ts),
        in_specs=[
            pl.BlockSpec((None, ts, d), lambda i, j: (i, j, 0)),
            _vec_spec(d), _bvec_spec(d), _bvec_spec(d), _bvec_spec(d),
            full((d, 2 * d)), full((1, 2 * d)), full((width, d)), full((1, d)),
            full((1, d)), full((1, d)), full((d, d)), full((1, d)),
            _vec_spec(d), _bvec_spec(d), _bvec_spec(d),
            full((n_exp, d)), full((n_exp, 1)),
        ],
        out_specs=out_specs,
        scratch_shapes=[
            pltpu.VMEM((CONV_HALO + ts, d), F32),
            pltpu.VMEM((ts, d), F32),
            pltpu.VMEM((SUBLANES - 1, CONV_HALO + ts - SUBLANES, min(512, d)), F32),
        ],
        compiler_params=_cparams(("arbitrary", "arbitrary")),
        name="conformer_mixer",
    )(x, n1g, sh1, sc1, gt1, pw1w, pw1b, dww, dwb, lng, lnb, pw2w, pw2b,
      n2g, sh2, sc2, rwt, rb)


def _mix1_kernel(x_ref, o_ref, gt1_ref, wo_ref, n2g_ref, sh2_ref, sc2_ref, rwt_ref, rb_ref,
                 x1_ref, hn_ref, idxt_ref, route_ref, cnt_ref):
    y = jnp.dot(o_ref[...], wo_ref[...], preferred_element_type=F32)
    x1 = x_ref[...] + gt1_ref[...] * y
    x1_ref[...] = x1
    _pre_moe(x1, n2g_ref[...], sh2_ref[...], sc2_ref[...], rwt_ref[...], rb_ref[...],
             hn_ref, idxt_ref, route_ref, cnt_ref)


def _mix1(x, o, gt1, wo, n2g, sh2, sc2, rwt, rb, ts):
    b, s, d = x.shape
    n_exp = rwt.shape[0]
    do = o.shape[-1]
    assert ts % MOE_TILE == 0 and ts // MOE_TILE <= SUBLANES
    out_shape, out_specs = _pre_moe_specs(b, s, d, ts, n_exp)
    full = lambda shp: pl.BlockSpec(shp, lambda i, j: (0,) * len(shp))
    return pl.pallas_call(
        _mix1_kernel,
        out_shape=out_shape,
        grid=(b, s // ts),
        in_specs=[
            pl.BlockSpec((None, ts, d), lambda i, j: (i, j, 0)),
            pl.BlockSpec((None, ts, do), lambda i, j: (i, j, 0)),
            _bvec_spec(d), full((do, d)),
            _vec_spec(d), _bvec_spec(d), _bvec_spec(d),
            full((n_exp, d)), full((n_exp, 1)),
        ],
        out_specs=out_specs,
        compiler_params=_cparams(("arbitrary", "arbitrary")),
        name="attn_out_mixer",
    )(x, o, gt1, wo, n2g, sh2, sc2, rwt, rb)


def _moe_dims(n_tok, n_exp):
    n_tiles = n_tok // MOE_TILE
    loc_rows = _round_up(TOP_K * MOE_TILE + n_exp * (ROW_ALIGN - 1), LANES)
    n_chunks = loc_rows // ROW_ALIGN
    max_rows = TOP_K * n_tok + n_tiles * n_exp * (ROW_ALIGN - 1) + n_exp * (EXPERT_ROWS - ROW_ALIGN)
    max_blocks = -(-max_rows // EXPERT_ROWS)
    return n_tiles, loc_rows, n_chunks, max_blocks


def _routing_tables(cnt, n_chunks, max_blocks):
    n_tiles, n_exp = cnt.shape
    cnt8 = _round_up(cnt, ROW_ALIGN)
    off = jnp.cumsum(cnt8, axis=1) - cnt8
    seg_len = cnt8.sum(axis=0)
    seg_pad = _round_up(seg_len, EXPERT_ROWS)
    seg_end = jnp.cumsum(seg_pad)
    seg_start = seg_end - seg_pad
    run_start = seg_start[None, :] + jnp.cumsum(cnt8, axis=0) - cnt8
    n_used = (cnt8.sum(axis=1) // ROW_ALIGN).astype(I32)
    c_row = jnp.arange(n_chunks, dtype=I32) * ROW_ALIGN
    local_end = off + cnt8
    in_run = ((c_row[None, :, None] >= off[:, None, :])
              & (c_row[None, :, None] < local_end[:, None, :]))
    delta = jnp.sum(jnp.where(in_run, (run_start - off)[:, None, :], 0), axis=-1)
    tab = ((delta + c_row[None, :]) // ROW_ALIGN).astype(I32)
    n_blocks = (seg_end[-1] // EXPERT_ROWS).astype(I32)
    blk_row = jnp.arange(max_blocks, dtype=I32) * EXPERT_ROWS
    blk_row = jnp.minimum(blk_row, seg_end[-1] - EXPERT_ROWS)
    blk_exp = jnp.minimum((blk_row[:, None] >= seg_end[None, :]).sum(axis=-1), n_exp - 1).astype(I32)
    per = EXPERT_ROWS // ROW_ALIGN
    n_tail = (seg_pad - seg_len) // ROW_ALIGN
    z_end = jnp.cumsum(n_tail)
    z_start = z_end - n_tail
    pos = jnp.arange(n_exp * per, dtype=I32)
    in_tail = (pos[:, None] >= z_start[None, :]) & (pos[:, None] < z_end[None, :])
    first = (seg_start + seg_len) // ROW_ALIGN - z_start
    ztab = (jnp.sum(jnp.where(in_tail, first[None, :], 0), axis=-1) + pos).astype(I32)
    ztab = jnp.where(pos < z_end[-1], ztab, 0)
    n_zero = z_end[-1].astype(I32)
    return tab.reshape(-1), n_used, ztab, n_zero.reshape(1), blk_exp, n_blocks.reshape(1)


def _dispatch_kernel(n_exp, tab_ref, nused_ref, ztab_ref, nzero_ref, nblk_ref,
                     x_ref, idxt_ref, xs_ref, loc_ref, zero_ref, sem, zsem):
    i = pl.program_id(0)
    n = pl.num_programs(0)
    slot = i % 2
    tile = x_ref.shape[0]
    loc_rows = loc_ref.shape[1]
    n_chunks = loc_rows // ROW_ALIGN

    def chunk_copy(sl, c, dst):
        return pltpu.make_async_copy(
            loc_ref.at[sl, pl.ds(pl.multiple_of(c * ROW_ALIGN, ROW_ALIGN), ROW_ALIGN)],
            xs_ref.at[pl.ds(pl.multiple_of(dst * ROW_ALIGN, ROW_ALIGN), ROW_ALIGN)],
            sem.at[sl])

    def wait_slot(sl, count):
        def body(c, carry):
            chunk_copy(sl, 0, 0).wait()
            return carry
        lax.fori_loop(0, count, body, 0)

    @pl.when(i >= 2)
    def _():
        wait_slot(slot, nused_ref[i - 2])

    idx = idxt_ref[...]
    e_iota = lax.broadcasted_iota(I32, (n_exp, tile), 0)
    hits = [idx[k:k + 1, :] == e_iota for k in range(TOP_K)]
    onehot = jnp.zeros((n_exp, tile), F32)
    for h in hits:
        onehot = onehot + h.astype(F32)
    upper = (lax.broadcasted_iota(I32, (tile, tile), 0)
             < lax.broadcasted_iota(I32, (tile, tile), 1)).astype(BF16)
    pre = jnp.dot(onehot.astype(BF16), upper, preferred_element_type=F32)
    cnt = jnp.sum(onehot, axis=1, keepdims=True)
    cnt8 = jnp.ceil(cnt / ROW_ALIGN) * ROW_ALIGN
    lower = (lax.broadcasted_iota(I32, (n_exp, n_exp), 1)
             < lax.broadcasted_iota(I32, (n_exp, n_exp), 0)).astype(BF16)
    off = jnp.dot(lower, jnp.broadcast_to(cnt8, (n_exp, LANES)).astype(BF16),
                  preferred_element_type=F32)[:, 0:1]
    base = off + pre
    r_iota = lax.broadcasted_iota(I32, (loc_rows, tile), 0)
    perm = jnp.zeros((loc_rows, tile), F32)
    for h in hits:
        dest = jnp.sum(jnp.where(h, base, 0.0), axis=0, keepdims=True).astype(I32)
        perm = perm + (r_iota == dest).astype(F32)
    loc_ref[slot] = jnp.dot(perm.astype(BF16), x_ref[...], preferred_element_type=F32)

    def issue(c, carry):
        chunk_copy(slot, c, tab_ref[i * n_chunks + c]).start()
        return carry
    lax.fori_loop(0, nused_ref[i], issue, 0)

    @pl.when(i == n - 1)
    def _():
        zero_ref[...] = jnp.zeros(zero_ref.shape, F32)
        max_blocks = xs_ref.shape[0] // EXPERT_ROWS

        def zcopy(dst):
            return pltpu.make_async_copy(
                zero_ref.at[pl.ds(0, ROW_ALIGN)],
                xs_ref.at[pl.ds(pl.multiple_of(dst * ROW_ALIGN, ROW_ALIGN), ROW_ALIGN)],
                zsem.at[0])

        def bcopy(blk):
            return pltpu.make_async_copy(
                zero_ref,
                xs_ref.at[pl.ds(pl.multiple_of(blk * EXPERT_ROWS, EXPERT_ROWS), EXPERT_ROWS)],
                zsem.at[1])

        def zissue(c, carry):
            zcopy(ztab_ref[c]).start()
            return carry
        lax.fori_loop(0, nzero_ref[0], zissue, 0)

        def bissue(blk, carry):
            bcopy(blk).start()
            return carry
        lax.fori_loop(nblk_ref[0], max_blocks, bissue, 0)

        @pl.when(i >= 1)
        def _():
            wait_slot(1 - slot, nused_ref[i - 1])
        wait_slot(slot, nused_ref[i])

        def zwait(c, carry):
            zcopy(0).wait()
            return carry
        lax.fori_loop(0, nzero_ref[0], zwait, 0)

        def bwait(blk, carry):
            bcopy(0).wait()
            return carry
        lax.fori_loop(nblk_ref[0], max_blocks, bwait, 0)


def _dispatch(hn2, idxt, tab, n_used, ztab, n_zero, n_blocks, n_exp, loc_rows, max_blocks):
    n_tok, d = hn2.shape
    n_tiles = n_tok // MOE_TILE
    return pl.pallas_call(
        functools.partial(_dispatch_kernel, n_exp),
        out_shape=jax.ShapeDtypeStruct((max_blocks * EXPERT_ROWS, d), F32),
        grid_spec=pltpu.PrefetchScalarGridSpec(
            num_scalar_prefetch=5,
            grid=(n_tiles,),
            in_specs=[
                pl.BlockSpec((MOE_TILE, d), lambda i, *_: (i, 0)),
                pl.BlockSpec((TOP_K, MOE_TILE), lambda i, *_: (0, i)),
            ],
            out_specs=pl.BlockSpec(memory_space=pl.ANY),
            scratch_shapes=[
                pltpu.VMEM((2, loc_rows, d), F32),
                pltpu.VMEM((EXPERT_ROWS, d), F32),
                pltpu.SemaphoreType.DMA((2,)),
                pltpu.SemaphoreType.DMA((2,)),
            ],
        ),
        compiler_params=_cparams(("arbitrary",)),
        name="moe_dispatch",
    )(tab, n_used, ztab, n_zero, n_blocks, hn2, idxt)


def _expert_kernel(be_ref, nb_ref, x_ref, wgu_ref, bgu_ref, wdn_ref, bdn_ref, y_ref,
                   wgu_bf, wdn_bf):
    de = wdn_ref.shape[0]
    b = pl.program_id(0)

    @pl.when((b == 0) | (be_ref[b] != be_ref[jnp.maximum(b - 1, 0)]))
    def _():
        cw = 512
        for c0 in range(0, wgu_ref.shape[1], cw):
            wgu_bf[:, c0:c0 + cw] = wgu_ref[:, c0:c0 + cw].astype(BF16)
        for c0 in range(0, wdn_ref.shape[1], cw):
            wdn_bf[:, c0:c0 + cw] = wdn_ref[:, c0:c0 + cw].astype(BF16)

    @pl.when(b < nb_ref[0])
    def _():
        xb = x_ref[...].astype(BF16)
        g = jnp.dot(xb, wgu_bf[:, :de], preferred_element_type=F32) + bgu_ref[:, :de]
        u = jnp.dot(xb, wgu_bf[:, de:], preferred_element_type=F32) + bgu_ref[:, de:]
        g = jnp.minimum(g, SWIGLU_LIMIT)
        u = jnp.clip(u, -SWIGLU_LIMIT, SWIGLU_LIMIT)
        a = (u + 1.0) * g * jax.nn.sigmoid(SWIGLU_ALPHA * g)
        y_ref[...] = jnp.dot(a.astype(BF16), wdn_bf[...], preferred_element_type=F32) + bdn_ref[...]

    @pl.when(pl.program_id(0) >= nb_ref[0])
    def _():
        y_ref[...] = jnp.zeros(y_ref.shape, y_ref.dtype)


def _experts(xs, blk_exp, n_blocks, layer, wgu, bgu, wdn, bdn):
    rows, d = xs.shape
    depth, n_exp, _, de2 = wgu.shape
    de = de2 // 2
    max_blocks = rows // EXPERT_ROWS
    row_map = lambda b, be, nb: (jnp.minimum(b, nb[0] - 1), 0)
    exp_map = lambda b, be, nb: (layer, be[b], 0, 0)
    return pl.pallas_call(
        _expert_kernel,
        out_shape=jax.ShapeDtypeStruct((rows, d), F32),
        grid_spec=pltpu.PrefetchScalarGridSpec(
            num_scalar_prefetch=2,
            grid=(max_blocks,),
            in_specs=[
                pl.BlockSpec((EXPERT_ROWS, d), row_map),
                pl.BlockSpec((None, None, d, de2), exp_map),
                pl.BlockSpec((None, None, 1, de2), exp_map),
                pl.BlockSpec((None, None, de, d), exp_map),
                pl.BlockSpec((None, None, 1, d), exp_map),
            ],
            out_specs=pl.BlockSpec((EXPERT_ROWS, d), lambda b, be, nb: (b, 0)),
            scratch_shapes=[pltpu.VMEM((d, de2), BF16), pltpu.VMEM((de, d), BF16)],
        ),
        compiler_params=_cparams(("arbitrary",)),
        name="moe_experts",
    )(blk_exp, n_blocks, xs, wgu, bgu.reshape(depth, n_exp, 1, de2), wdn,
      bdn.reshape(depth, n_exp, 1, d))


def _combine_kernel(final_norm, n_exp, tab_ref, nused_ref,
                    route_ref, x1_ref, g2_ref, fg_ref, ys_ref, out_ref, loc_ref, sem):
    i = pl.program_id(0)
    n = pl.num_programs(0)
    slot = i % 2
    tile = x1_ref.shape[0]
    loc_rows = loc_ref.shape[1]
    n_chunks = loc_rows // ROW_ALIGN

    def chunk_copy(sl, c, src):
        return pltpu.make_async_copy(
            ys_ref.at[pl.ds(pl.multiple_of(src * ROW_ALIGN, ROW_ALIGN), ROW_ALIGN)],
            loc_ref.at[sl, pl.ds(pl.multiple_of(c * ROW_ALIGN, ROW_ALIGN), ROW_ALIGN)],
            sem.at[sl])

    def issue_tile(t, sl):
        def body(c, carry):
            chunk_copy(sl, c, tab_ref[t * n_chunks + c]).start()
            return carry
        lax.fori_loop(0, nused_ref[t], body, 0)

    @pl.when(i == 0)
    def _():
        issue_tile(0, 0)

    @pl.when(i + 1 < n)
    def _():
        issue_tile(i + 1, 1 - slot)

    route = route_ref[...]
    e_iota = lax.broadcasted_iota(I32, (tile, n_exp), 1)
    hits = [route[:, k:k + 1].astype(I32) == e_iota for k in range(TOP_K)]
    gates = [route[:, TOP_K + k:TOP_K + k + 1] for k in range(TOP_K)]
    onehot = jnp.zeros((tile, n_exp), F32)
    for h in hits:
        onehot = onehot + h.astype(F32)
    lower = (lax.broadcasted_iota(I32, (tile, tile), 1)
             < lax.broadcasted_iota(I32, (tile, tile), 0)).astype(BF16)
    pre = jnp.dot(lower, onehot.astype(BF16), preferred_element_type=F32)
    cnt = jnp.sum(onehot, axis=0, keepdims=True)
    cnt8 = jnp.ceil(cnt / ROW_ALIGN) * ROW_ALIGN
    upper = (lax.broadcasted_iota(I32, (n_exp, n_exp), 0)
             < lax.broadcasted_iota(I32, (n_exp, n_exp), 1)).astype(BF16)
    off = jnp.dot(jnp.broadcast_to(cnt8, (SUBLANES, n_exp)).astype(BF16), upper,
                  preferred_element_type=F32)[0:1, :]
    base = off + pre
    r_iota = lax.broadcasted_iota(I32, (tile, loc_rows), 1)
    comb = jnp.zeros((tile, loc_rows), F32)
    for h, gk in zip(hits, gates):
        dest = jnp.sum(jnp.where(h, base, 0.0), axis=1, keepdims=True).astype(I32)
        comb = comb + jnp.where(r_iota == dest, gk, 0.0)

    def wbody(c, carry):
        chunk_copy(slot, 0, 0).wait()
        return carry
    lax.fori_loop(0, nused_ref[i], wbody, 0)

    used_rows = nused_ref[i] * ROW_ALIGN
    row_ok = lax.broadcasted_iota(I32, (loc_rows, 1), 0) < used_rows
    yl = jnp.where(row_ok, loc_ref[slot], 0.0).astype(BF16)
    moe = jnp.dot(comb.astype(BF16), yl, preferred_element_type=F32)
    out = x1_ref[...] + g2_ref[...] * moe
    if final_norm:
        out = _rms(out, fg_ref[...])
    out_ref[...] = out


def _combine(ys, route, x1, gate2, final_g, tab, n_used, n_exp, loc_rows, final_norm):
    b, s, d = x1.shape
    n_tok = b * s
    n_tiles = n_tok // MOE_TILE
    tiles_per_seq = s // MOE_TILE
    out = pl.pallas_call(
        functools.partial(_combine_kernel, final_norm, n_exp),
        out_shape=jax.ShapeDtypeStruct((n_tok, d), F32),
        grid_spec=pltpu.PrefetchScalarGridSpec(
            num_scalar_prefetch=2,
            grid=(n_tiles,),
            in_specs=[
                pl.BlockSpec((MOE_TILE, LANES), lambda i, *_: (i, 0)),
                pl.BlockSpec((MOE_TILE, d), lambda i, *_: (i, 0)),
                pl.BlockSpec((None, 1, d), lambda i, *_: (i // tiles_per_seq, 0, 0)),
                pl.BlockSpec((1, d), lambda i, *_: (0, 0)),
                pl.BlockSpec(memory_space=pl.ANY),
            ],
            out_specs=pl.BlockSpec((MOE_TILE, d), lambda i, *_: (i, 0)),
            scratch_shapes=[
                pltpu.VMEM((2, loc_rows, d), F32),
                pltpu.SemaphoreType.DMA((2,)),
            ],
        ),
        compiler_params=_cparams(("arbitrary",)),
        name="moe_combine",
    )(tab, n_used, route, x1.reshape(n_tok, d), gate2, final_g, ys)
    return out.reshape(b, s, d)


def _moe(x1, hn2, idxt, route, cnt, gate2, final_g, layer, wgu, bgu, wdn, bdn, final_norm):
    b, s, d = x1.shape
    n_tok = b * s
    n_exp = wgu.shape[1]
    n_tiles, loc_rows, n_chunks, max_blocks = _moe_dims(n_tok, n_exp)
    tab, n_used, ztab, n_zero, blk_exp, n_blocks = _routing_tables(cnt, n_chunks, max_blocks)
    xs = _dispatch(hn2.reshape(n_tok, d), idxt, tab, n_used, ztab, n_zero, n_blocks,
                   n_exp, loc_rows, max_blocks)
    ys = _experts(xs, blk_exp, n_blocks, layer, wgu, bgu, wdn, bdn)
    return _combine(ys, route, x1, gate2, final_g, tab, n_used, n_exp, loc_rows, final_norm)


def _qkv_kernel(n_heads, scale, x_ref, cc_ref, ss_ref, kvg_ref, wdkv_ref, ckvg_ref, wuk_ref, wuv_ref,
                n1g_ref, sh1_ref, sc1_ref, wdq_ref, cqg_ref, wuq_ref,
                q_ref, k_ref, v_ref):
    x = x_ref[...]
    cc = cc_ref[...]
    ss = ss_ref[...]
    r_kv = ckvg_ref.shape[1]
    hk = _rms(x, kvg_ref[...]).astype(BF16)
    lat = jnp.dot(hk, wdkv_ref[...], preferred_element_type=F32)
    ckv = _rms(lat[:, :r_kv], ckvg_ref[...]).astype(BF16)
    krot = lat[:, r_kv:r_kv + LANES] * cc + lat[:, r_kv + LANES:r_kv + 2 * LANES] * ss
    kn = jnp.dot(ckv, wuk_ref[...], preferred_element_type=F32)
    vt = lax.dot_general(wuv_ref[...], ckv, (((1,), (1,)), ((), ())), preferred_element_type=F32)
    hq = (_rms(x, n1g_ref[...]) * (1.0 + sc1_ref[...]) + sh1_ref[...]).astype(BF16)
    cq = _rms(jnp.dot(hq, wdq_ref[...], preferred_element_type=F32), cqg_ref[...]).astype(BF16)
    qq = jnp.dot(cq, wuq_ref[...], preferred_element_type=F32) * scale
    hd = n_heads * LANES
    for h in range(n_heads):
        sl = slice(h * LANES, (h + 1) * LANES)
        k_ref[h, :, 0:LANES] = kn[:, sl].astype(BF16)
        k_ref[h, :, LANES:2 * LANES] = krot.astype(BF16)
        v_ref[h] = vt[h * LANES:(h + 1) * LANES, :].astype(BF16)
        q_ref[h, :, 0:LANES] = qq[:, sl].astype(BF16)
        qrot = qq[:, hd + h * LANES:hd + (h + 1) * LANES] * cc \
            + qq[:, 2 * hd + h * LANES:2 * hd + (h + 1) * LANES] * ss
        q_ref[h, :, LANES:2 * LANES] = qrot.astype(BF16)


def _qkv(x, cc, ss, kvg, wdkv, ckvg, wuk, wuv, n1g, sh1, sc1, wdq, cqg, wuq, n_heads, scale, ts):
    b, s, d = x.shape
    full = lambda a: pl.BlockSpec(a.shape, lambda i, j: (0,) * a.ndim)
    hspec = lambda w: pl.BlockSpec((None, n_heads, ts, w), lambda i, j: (i, 0, j, 0))
    return pl.pallas_call(
        functools.partial(_qkv_kernel, n_heads, scale),
        out_shape=[
            jax.ShapeDtypeStruct((b, n_heads, s, 2 * LANES), BF16),
            jax.ShapeDtypeStruct((b, n_heads, s, 2 * LANES), BF16),
            jax.ShapeDtypeStruct((b, n_heads, s // ts, LANES, ts), BF16),
        ],
        grid=(b, s // ts),
        in_specs=[
            pl.BlockSpec((None, ts, d), lambda i, j: (i, j, 0)),
            pl.BlockSpec((None, ts, LANES), lambda i, j: (i, j, 0)),
            pl.BlockSpec((None, ts, LANES), lambda i, j: (i, j, 0)),
            full(kvg), full(wdkv), full(ckvg), full(wuk), full(wuv),
            full(n1g), _bvec_spec(d), _bvec_spec(d), full(wdq), full(cqg), full(wuq),
        ],
        out_specs=[hspec(2 * LANES), hspec(2 * LANES),
                   pl.BlockSpec((None, n_heads, None, LANES, ts), lambda i, j: (i, 0, j, 0, 0))],
        compiler_params=_cparams(("arbitrary", "arbitrary")),
        name="mla_qkv",
    )(x, cc, ss, kvg, wdkv, ckvg, wuk, wuv, n1g, sh1, sc1, wdq, cqg, wuq)


def _attn_kernel(q_ref, k_ref, vt_ref, o_ref, m_ref, l_ref, acc_ref):
    tq = q_ref.shape[0]
    qi = pl.program_id(2)
    q = q_ref[...]
    m_ref[...] = jnp.full(m_ref.shape, -jnp.inf, F32)
    l_ref[...] = jnp.zeros(l_ref.shape, F32)
    acc_ref[...] = jnp.zeros(acc_ref.shape, F32)

    def scores(ki, masked):
        start = pl.multiple_of(ki * tq, tq)
        k = k_ref[pl.ds(start, tq), :]
        st = lax.dot_general(k, q, (((1,), (1,)), ((), ())), preferred_element_type=F32)
        if masked:
            kc = lax.broadcasted_iota(I32, (tq, tq), 0) // CHUNK
            qc = lax.broadcasted_iota(I32, (tq, tq), 1) // CHUNK
            st = jnp.where(kc <= qc, st, -jnp.inf)
        return st

    def update(ki, st, m_old, l_old, acc_old):
        m_new = jnp.maximum(m_old, jnp.max(st, axis=0, keepdims=True))
        p = jnp.exp2(st - m_new)
        alpha = jnp.exp2(m_old - m_new)
        l_new = alpha * l_old + jnp.sum(p, axis=0, keepdims=True)
        acc_new = alpha * acc_old + jnp.dot(vt_ref[ki], p.astype(BF16), preferred_element_type=F32)
        return m_new, l_new, acc_new

    def step(tiles):
        sts = [scores(ki, masked) for ki, masked in tiles]
        state = (m_ref[...], l_ref[...], acc_ref[...])
        for (ki, _), st in zip(tiles, sts):
            state = update(ki, st, *state)
        m_ref[...], l_ref[...], acc_ref[...] = state

    def body(j, carry):
        step([(2 * j, False), (2 * j + 1, False)])
        return carry
    lax.fori_loop(0, qi // 2, body, 0)

    @pl.when(qi % 2 == 1)
    def _():
        step([(qi - 1, False), (qi, True)])

    @pl.when(qi % 2 == 0)
    def _():
        step([(qi, True)])

    o_ref[...] = (acc_ref[...] / l_ref[...]).T.astype(o_ref.dtype)


def _attention(q, k, vt, tq):
    b, h, s, dk = q.shape
    nk, dv, tk = vt.shape[2:]
    assert tq % CHUNK == 0 and s % tq == 0 and tk == tq
    return pl.pallas_call(
        _attn_kernel,
        out_shape=jax.ShapeDtypeStruct((b, s, h * dv), BF16),
        grid=(b, h, s // tq),
        in_specs=[
            pl.BlockSpec((None, None, tq, dk), lambda i, j, t: (i, j, t, 0)),
            pl.BlockSpec((None, None, s, dk), lambda i, j, t: (i, j, 0, 0)),
            pl.BlockSpec((None, None, nk, dv, tk), lambda i, j, t: (i, j, 0, 0, 0)),
        ],
        out_specs=pl.BlockSpec((None, tq, dv), lambda i, j, t: (i, t, j)),
        scratch_shapes=[pltpu.VMEM((1, tq), F32), pltpu.VMEM((1, tq), F32), pltpu.VMEM((dv, tq), F32)],
        compiler_params=_cparams(("arbitrary", "arbitrary", "arbitrary")),
        name="mla_attention",
    )(q, k, vt)


def _swap_halves(w):
    half = w.shape[-1] // 2
    return jnp.concatenate([w[..., half:], w[..., :half]], axis=-1)


def _pad_lanes(w):
    pad = LANES - w.shape[-1]
    return jnp.concatenate([w, jnp.zeros(w.shape[:-1] + (pad,), w.dtype)], axis=-1)


def kernel(x, c, positions, mod_w, mod_b, norm1_g, norm2_g, conv_pw1_w, conv_pw1_b, conv_dw_w, conv_dw_b, conv_ln_g, conv_ln_b, conv_pw2_w, conv_pw2_b, kv_norm_g, w_dkv, ckv_norm_g, w_uk, w_uv, w_dq, cq_norm_g, w_uq, w_o, router_w, router_b, exp_w_gu, exp_b_gu, exp_w_dn, exp_b_dn, final_g):
    b, s, d = x.shape
    n_heads, nope = w_uk.shape[1], w_uk.shape[2]
    r_kv = ckv_norm_g.shape[0]
    rope = w_dkv.shape[1] - r_kv
    vdim = w_uv.shape[2]
    n_exp = router_w.shape[2]
    assert nope == LANES and vdim == LANES and rope <= LANES and d % LANES == 0
    ts = min(512, s)
    tq = min(512, s)

    mod = _modulation(c, mod_w, mod_b)
    mods = [[m.reshape(b, 1, d) for m in jnp.split(mod[l], 6, axis=-1)] for l in range(2)]
    row = lambda v: v.reshape(1, -1)

    half = rope // 2
    inv = jnp.exp(-(2.0 * math.log(ROPE_THETA) / rope) * jnp.arange(half, dtype=F32))
    ang = positions.astype(F32)[..., None] * inv
    cos, sin = jnp.cos(ang), jnp.sin(ang)
    cc = _pad_lanes(jnp.concatenate([cos, cos], axis=-1))
    ss = _pad_lanes(jnp.concatenate([-sin, sin], axis=-1))

    sh1, sc1, gt1, sh2, sc2, gt2 = mods[0]
    x1, hn2, idxt, route, cnt = _mix0(
        x, row(norm1_g[0]), sh1, sc1, gt1,
        conv_pw1_w[0].astype(BF16), row(conv_pw1_b[0]), conv_dw_w[0], row(conv_dw_b[0]),
        row(conv_ln_g[0]), row(conv_ln_b[0]), conv_pw2_w[0].astype(BF16), row(conv_pw2_b[0]),
        row(norm2_g[0]), sh2, sc2, router_w[0].T, router_b[0].reshape(n_exp, 1), ts)
    cnt = cnt[:, :ts // MOE_TILE].reshape(-1, n_exp)
    x2 = _moe(x1, hn2, idxt, route, cnt, gt2, row(final_g),
              0, exp_w_gu, exp_b_gu, exp_w_dn, exp_b_dn, False)

    sh1, sc1, gt1, sh2, sc2, gt2 = mods[1]
    wdkv_rope = w_dkv[:, r_kv:]
    wdkv_ext = jnp.concatenate(
        [w_dkv[:, :r_kv], _pad_lanes(wdkv_rope), _pad_lanes(_swap_halves(wdkv_rope))], axis=-1)
    wuq = w_uq[0]
    r_q = wuq.shape[0]
    wuq_rope = wuq[:, :, nope:]
    wuq_ext = jnp.concatenate([
        wuq[:, :, :nope].reshape(r_q, n_heads * LANES),
        _pad_lanes(wuq_rope).reshape(r_q, n_heads * LANES),
        _pad_lanes(_swap_halves(wuq_rope)).reshape(r_q, n_heads * LANES)], axis=-1)
    scale = float((nope + rope) ** -0.5 * math.log2(math.e))
    q, k, v = _qkv(
        x2, cc, ss, row(kv_norm_g), wdkv_ext.astype(BF16), row(ckv_norm_g),
        w_uk.reshape(r_kv, n_heads * nope).astype(BF16), w_uv.reshape(r_kv, n_heads * vdim).T.astype(BF16),
        row(norm1_g[1]), sh1, sc1, w_dq[0].astype(BF16), row(cq_norm_g[0]), wuq_ext.astype(BF16),
        n_heads, scale, ts)
    o = _attention(q, k, v, tq)

    x3, hn2, idxt, route, cnt = _mix1(
        x2, o, gt1, w_o[0].astype(BF16), row(norm2_g[1]), sh2, sc2,
        router_w[1].T, router_b[1].reshape(n_exp, 1), ts)
    cnt = cnt[:, :ts // MOE_TILE].reshape(-1, n_exp)
    return _moe(x3, hn2, idxt, route, cnt, gt2, row(final_g),
                1, exp_w_gu, exp_b_gu, exp_w_dn, exp_b_dn, True)
```

```python
import functools
import math

import jax
import jax.numpy as jnp
from jax import lax
from jax.experimental import pallas as pl
from jax.experimental.pallas import tpu as pltpu

CHUNK = 64
TOP_K = 4
ROPE_THETA = 10000.0
SWIGLU_ALPHA = 1.702
SWIGLU_LIMIT = 7.0
EPS = 1e-6

LANES = 128
SUBLANES = 8
VMEM_LIMIT_BYTES = 56 * 1024 * 1024

ROW_ALIGN = SUBLANES
MOE_TILE = 256
EXPERT_ROWS = 512
CONV_HALO = 32
V_ONES_ROWS = 2 * SUBLANES

F32 = jnp.float32
BF16 = jnp.bfloat16
I32 = jnp.int32


def _cparams(sem):
    return pltpu.CompilerParams(dimension_semantics=sem, vmem_limit_bytes=VMEM_LIMIT_BYTES)


def _rms(x, g):
    return x * lax.rsqrt(jnp.mean(x * x, axis=-1, keepdims=True) + EPS) * g


def _round_up(a, m):
    return (a + m - 1) // m * m


def _mod_kernel(c_ref, w_ref, b_ref, o_ref):
    c = c_ref[...]
    ca = c * jax.nn.sigmoid(c)
    o_ref[...] = jnp.dot(ca, w_ref[...], preferred_element_type=F32) + b_ref[...]


def _modulation(c, mod_w, mod_b):
    depth, d, d6 = mod_w.shape
    b = c.shape[0]
    bp = _round_up(b, SUBLANES)
    cp = jnp.zeros((bp, d), F32).at[:b].set(c)
    tn = d6 // 4
    out = pl.pallas_call(
        _mod_kernel,
        out_shape=jax.ShapeDtypeStruct((depth, bp, d6), F32),
        grid=(depth, d6 // tn),
        in_specs=[
            pl.BlockSpec((bp, d), lambda l, j: (0, 0)),
            pl.BlockSpec((None, d, tn), lambda l, j: (l, 0, j)),
            pl.BlockSpec((None, 1, tn), lambda l, j: (l, 0, j)),
        ],
        out_specs=pl.BlockSpec((None, bp, tn), lambda l, j: (l, 0, j)),
        compiler_params=_cparams(("arbitrary", "arbitrary")),
        name="adaln_modulation",
    )(cp, mod_w, mod_b.reshape(depth, 1, d6))
    return out[:, :b]


def _pre_moe(x1, g2, sh2, sc2, rwt, rb, hn_ref, idxt_ref, route_ref, cnt_ref):
    ts = x1.shape[0]
    n_exp = rwt.shape[0]
    hn = _rms(x1, g2) * (1.0 + sc2) + sh2
    hn_ref[...] = hn.astype(BF16)
    logits = lax.dot_general(rwt, hn, (((1,), (1,)), ((), ())),
                             precision=lax.Precision.HIGHEST,
                             preferred_element_type=F32) + rb
    e_iota = lax.broadcasted_iota(I32, (n_exp, ts), 0)
    vals, idxs = [], []
    cur = logits
    for _ in range(TOP_K):
        m = jnp.max(cur, axis=0, keepdims=True)
        i = jnp.min(jnp.where(cur == m, e_iota, n_exp), axis=0, keepdims=True)
        vals.append(m)
        idxs.append(i)
        cur = jnp.where(e_iota == i, -jnp.inf, cur)
    exps = [jnp.exp(v - vals[0]) for v in vals]
    den = exps[0]
    for e in exps[1:]:
        den = den + e
    gates = [e / den for e in exps]
    idxt_ref[...] = jnp.concatenate(idxs, axis=0)
    rows = jnp.concatenate([i.astype(F32) for i in idxs] + gates
                           + [jnp.zeros((LANES - 2 * TOP_K, ts), F32)], axis=0)
    route_ref[...] = rows.T
    onehot = jnp.zeros((n_exp, ts), F32)
    for i in idxs:
        onehot = onehot + (e_iota == i).astype(F32)
    sel = (lax.broadcasted_iota(I32, (SUBLANES, ts), 1) // MOE_TILE
           == lax.broadcasted_iota(I32, (SUBLANES, ts), 0)).astype(BF16)
    cnt = lax.dot_general(sel, onehot.astype(BF16), (((1,), (1,)), ((), ())),
                          preferred_element_type=F32)
    cnt_ref[...] = cnt.astype(I32)


def _pre_moe_specs(b, s, d, ts, n_exp):
    ns = s // ts
    out_shape = [
        jax.ShapeDtypeStruct((b, s, d), F32),
        jax.ShapeDtypeStruct((b, s, d), BF16),
        jax.ShapeDtypeStruct((TOP_K, b * s), I32),
        jax.ShapeDtypeStruct((b * s, LANES), F32),
        jax.ShapeDtypeStruct((b * ns, SUBLANES, n_exp), I32),
    ]
    out_specs = [
        pl.BlockSpec((None, ts, d), lambda i, j: (i, j, 0)),
        pl.BlockSpec((None, ts, d), lambda i, j: (i, j, 0)),
        pl.BlockSpec((TOP_K, ts), lambda i, j: (0, i * ns + j)),
        pl.BlockSpec((ts, LANES), lambda i, j: (i * ns + j, 0)),
        pl.BlockSpec((None, SUBLANES, n_exp), lambda i, j: (i * ns + j, 0, 0)),
    ]
    return out_shape, out_specs


def _vec_spec(d):
    return pl.BlockSpec((1, d), lambda i, j: (0, 0))


def _bvec_spec(d):
    return pl.BlockSpec((None, 1, d), lambda i, j: (i, 0, 0))


def _mix0_kernel(x_ref, n1g_ref, sh1_ref, sc1_ref, gt1_ref,
                 pw1w_ref, pw1b_ref, dww_ref, dwb_ref, lng_ref, lnb_ref,
                 pw2w_ref, pw2b_ref, n2g_ref, sh2_ref, sc2_ref, rwt_ref, rb_ref,
                 x1_ref, hn_ref, idxt_ref, route_ref, cnt_ref, buf_ref, conv_ref, shift_ref):
    ts, d = x_ref.shape
    width = dww_ref.shape[0]
    x = x_ref[...]
    hn = _rms(x, n1g_ref[...]) * (1.0 + sc1_ref[...]) + sh1_ref[...]
    hb = hn.astype(BF16)
    a = jnp.dot(hb, pw1w_ref[:, :d], preferred_element_type=F32) + pw1b_ref[:, :d]
    g = jnp.dot(hb, pw1w_ref[:, d:], preferred_element_type=F32) + pw1b_ref[:, d:]
    glu = a * jax.nn.sigmoid(g)

    @pl.when(pl.program_id(1) == 0)
    def _():
        buf_ref[0:CONV_HALO, :] = jnp.zeros((CONV_HALO, d), F32)

    buf_ref[CONV_HALO:, :] = glu
    base = CONV_HALO - (width - 1)
    rc = 32
    lc = min(512, d)
    sh_rows = shift_ref.shape[1]
    for c0 in range(0, d, lc):
        for r in range(1, SUBLANES):
            shift_ref[r - 1] = buf_ref[r:r + sh_rows, c0:c0 + lc]
        for r0 in range(0, ts, rc):
            acc = jnp.zeros((rc, lc), F32)
            for k in range(width):
                q8, r = divmod(base + k, SUBLANES)
                lo = q8 * SUBLANES + r0
                if r == 0:
                    win = buf_ref[lo:lo + rc, c0:c0 + lc]
                else:
                    win = shift_ref[r - 1, lo:lo + rc, :]
                acc = acc + dww_ref[k:k + 1, c0:c0 + lc] * win
            conv_ref[r0:r0 + rc, c0:c0 + lc] = acc
    buf_ref[0:CONV_HALO, :] = buf_ref[ts:ts + CONV_HALO, :]
    u = conv_ref[...] + dwb_ref[...]
    mu = jnp.mean(u, axis=-1, keepdims=True)
    dlt = u - mu
    var = jnp.mean(dlt * dlt, axis=-1, keepdims=True)
    u = dlt * lax.rsqrt(var + EPS) * lng_ref[...] + lnb_ref[...]
    u = u * jax.nn.sigmoid(u)
    y = jnp.dot(u.astype(BF16), pw2w_ref[...], preferred_element_type=F32) + pw2b_ref[...]
    x1 = x + gt1_ref[...] * y
    x1_ref[...] = x1
    _pre_moe(x1, n2g_ref[...], sh2_ref[...], sc2_ref[...], rwt_ref[...], rb_ref[...],
             hn_ref, idxt_ref, route_ref, cnt_ref)


def _mix0(x, n1g, sh1, sc1, gt1, pw1w, pw1b, dww, dwb, lng, lnb, pw2w, pw2b,
          n2g, sh2, sc2, rwt, rb, ts):
    b, s, d = x.shape
    n_exp = rwt.shape[0]
    width = dww.shape[0]
    assert width - 1 <= CONV_HALO and ts % MOE_TILE == 0 and ts // MOE_TILE <= SUBLANES
    out_shape, out_specs = _pre_moe_specs(b, s, d, ts, n_exp)
    full = lambda shp: pl.BlockSpec(shp, lambda i, j: (0,) * len(shp))
    return pl.pallas_call(
        _mix0_kernel,
        out_shape=out_shape,
        grid=(b, s // ts),
        in_specs=[
            pl.BlockSpec((None, ts, d), lambda i, j: (i, j, 0)),
            _vec_spec(d), _bvec_spec(d), _bvec_spec(d), _bvec_spec(d),
            full((d, 2 * d)), full((1, 2 * d)), full((width, d)), full((1, d)),
            full((1, d)), full((1, d)), full((d, d)), full((1, d)),
            _vec_spec(d), _bvec_spec(d), _bvec_spec(d),
            full((n_exp, d)), full((n_exp, 1)),
        ],
        out_specs=out_specs,
        scratch_shapes=[
            pltpu.VMEM((CONV_HALO + ts, d), F32),
            pltpu.VMEM((ts, d), F32),
            pltpu.VMEM((SUBLANES - 1, CONV_HALO + ts - SUBLANES, min(512, d)), F32),
        ],
        compiler_params=_cparams(("arbitrary", "arbitrary")),
        name="conformer_mixer",
    )(x, n1g, sh1, sc1, gt1, pw1w, pw1b, dww, dwb, lng, lnb, pw2w, pw2b,
      n2g, sh2, sc2, rwt, rb)


def _mix1_kernel(x_ref, o_ref, gt1_ref, wo_ref, n2g_ref, sh2_ref, sc2_ref, rwt_ref, rb_ref,
                 x1_ref, hn_ref, idxt_ref, route_ref, cnt_ref):
    y = jnp.dot(o_ref[...], wo_ref[...], preferred_element_type=F32)
    x1 = x_ref[...] + gt1_ref[...] * y
    x1_ref[...] = x1
    _pre_moe(x1, n2g_ref[...], sh2_ref[...], sc2_ref[...], rwt_ref[...], rb_ref[...],
             hn_ref, idxt_ref, route_ref, cnt_ref)


def _mix1(x, o, gt1, wo, n2g, sh2, sc2, rwt, rb, ts):
    b, s, d = x.shape
    n_exp = rwt.shape[0]
    do = o.shape[-1]
    assert ts % MOE_TILE == 0 and ts // MOE_TILE <= SUBLANES
    out_shape, out_specs = _pre_moe_specs(b, s, d, ts, n_exp)
    full = lambda shp: pl.BlockSpec(shp, lambda i, j: (0,) * len(shp))
    return pl.pallas_call(
        _mix1_kernel,
        out_shape=out_shape,
        grid=(b, s // ts),
        in_specs=[
            pl.BlockSpec((None, ts, d), lambda i, j: (i, j, 0)),
            pl.BlockSpec((None, ts, do), lambda i, j: (i, j, 0)),
            _bvec_spec(d), full((do, d)),
            _vec_spec(d), _bvec_spec(d), _bvec_spec(d),
            full((n_exp, d)), full((n_exp, 1)),
        ],
        out_specs=out_specs,
        compiler_params=_cparams(("arbitrary", "arbitrary")),
        name="attn_out_mixer",
    )(x, o, gt1, wo, n2g, sh2, sc2, rwt, rb)


def _moe_dims(n_tok, n_exp):
    n_tiles = n_tok // MOE_TILE
    loc_rows = _round_up(TOP_K * MOE_TILE + n_exp * (ROW_ALIGN - 1), LANES)
    n_chunks = loc_rows // ROW_ALIGN
    max_rows = TOP_K * n_tok + n_tiles * n_exp * (ROW_ALIGN - 1) + n_exp * (EXPERT_ROWS - ROW_ALIGN)
    max_blocks = -(-max_rows // EXPERT_ROWS)
    return n_tiles, loc_rows, n_chunks, max_blocks


def _routing_tables(cnt, n_chunks, max_blocks):
    n_tiles, n_exp = cnt.shape
    cnt8 = _round_up(cnt, ROW_ALIGN)
    off = jnp.cumsum(cnt8, axis=1) - cnt8
    seg_len = cnt8.sum(axis=0)
    seg_pad = _round_up(seg_len, EXPERT_ROWS)
    seg_end = jnp.cumsum(seg_pad)
    seg_start = seg_end - seg_pad
    run_start = seg_start[None, :] + jnp.cumsum(cnt8, axis=0) - cnt8
    n_used = (cnt8.sum(axis=1) // ROW_ALIGN).astype(I32)
    c_row = jnp.arange(n_chunks, dtype=I32) * ROW_ALIGN
    local_end = off + cnt8
    in_run = ((c_row[None, :, None] >= off[:, None, :])
              & (c_row[None, :, None] < local_end[:, None, :]))
    delta = jnp.sum(jnp.where(in_run, (run_start - off)[:, None, :], 0), axis=-1)
    tab = ((delta + c_row[None, :]) // ROW_ALIGN).astype(I32)
    n_blocks = (seg_end[-1] // EXPERT_ROWS).astype(I32)
    blk_row = jnp.arange(max_blocks, dtype=I32) * EXPERT_ROWS
    blk_row = jnp.minimum(blk_row, seg_end[-1] - EXPERT_ROWS)
    blk_exp = jnp.minimum((blk_row[:, None] >= seg_end[None, :]).sum(axis=-1), n_exp - 1).astype(I32)
    per = EXPERT_ROWS // ROW_ALIGN
    n_tail = (seg_pad - seg_len) // ROW_ALIGN
    z_end = jnp.cumsum(n_tail)
    z_start = z_end - n_tail
    pos = jnp.arange(n_exp * per, dtype=I32)
    in_tail = (pos[:, None] >= z_start[None, :]) & (pos[:, None] < z_end[None, :])
    first = (seg_start + seg_len) // ROW_ALIGN - z_start
    ztab = (jnp.sum(jnp.where(in_tail, first[None, :], 0), axis=-1) + pos).astype(I32)
    ztab = jnp.where(pos < z_end[-1], ztab, 0)
    n_zero = z_end[-1].astype(I32)
    return tab.reshape(-1), n_used, ztab, n_zero.reshape(1), blk_exp, n_blocks.reshape(1)


def _dispatch_kernel(n_exp, tab_ref, nused_ref, ztab_ref, nzero_ref, nblk_ref,
                     x_ref, idxt_ref, xs_ref, loc_ref, zero_ref, sem, zsem):
    i = pl.program_id(0)
    n = pl.num_programs(0)
    slot = i % 2
    tile = x_ref.shape[0]
    loc_rows = loc_ref.shape[1]
    n_chunks = loc_rows // ROW_ALIGN

    def chunk_copy(sl, c, dst):
        return pltpu.make_async_copy(
            loc_ref.at[sl, pl.ds(pl.multiple_of(c * ROW_ALIGN, ROW_ALIGN), ROW_ALIGN)],
            xs_ref.at[pl.ds(pl.multiple_of(dst * ROW_ALIGN, ROW_ALIGN), ROW_ALIGN)],
            sem.at[sl])

    def wait_slot(sl, count):
        @pl.when(count > 0)
        def _():
            rows = count * ROW_ALIGN
            pltpu.make_async_copy(loc_ref.at[sl, pl.ds(0, rows)], xs_ref.at[pl.ds(0, rows)],
                                  sem.at[sl]).wait()

    @pl.when(i >= 2)
    def _():
        wait_slot(slot, nused_ref[i - 2])

    idx = idxt_ref[...]
    e_iota = lax.broadcasted_iota(I32, (n_exp, tile), 0)
    hits = [idx[k:k + 1, :] == e_iota for k in range(TOP_K)]
    onehot = jnp.zeros((n_exp, tile), F32)
    for h in hits:
        onehot = onehot + h.astype(F32)
    upper = (lax.broadcasted_iota(I32, (tile, tile), 0)
             < lax.broadcasted_iota(I32, (tile, tile), 1)).astype(BF16)
    pre = jnp.dot(onehot.astype(BF16), upper, preferred_element_type=F32)
    cnt = jnp.sum(onehot, axis=1, keepdims=True)
    cnt8 = jnp.ceil(cnt / ROW_ALIGN) * ROW_ALIGN
    lower = (lax.broadcasted_iota(I32, (n_exp, n_exp), 1)
             < lax.broadcasted_iota(I32, (n_exp, n_exp), 0)).astype(BF16)
    off = jnp.dot(lower, jnp.broadcast_to(cnt8, (n_exp, LANES)).astype(BF16),
                  preferred_element_type=F32)[:, 0:1]
    base = off + pre
    r_iota = lax.broadcasted_iota(I32, (loc_rows, tile), 0)
    perm = jnp.zeros((loc_rows, tile), F32)
    for h in hits:
        dest = jnp.sum(jnp.where(h, base, 0.0), axis=0, keepdims=True).astype(I32)
        perm = perm + (r_iota == dest).astype(F32)
    loc_ref[slot] = jnp.dot(perm.astype(BF16), x_ref[...], preferred_element_type=F32)

    def issue(c, carry):
        chunk_copy(slot, c, tab_ref[i * n_chunks + c]).start()
        return carry
    lax.fori_loop(0, nused_ref[i], issue, 0)

    @pl.when(i == n - 1)
    def _():
        zero_ref[...] = jnp.zeros(zero_ref.shape, F32)
        max_blocks = xs_ref.shape[0] // EXPERT_ROWS

        def zcopy(dst):
            return pltpu.make_async_copy(
                zero_ref.at[pl.ds(0, ROW_ALIGN)],
                xs_ref.at[pl.ds(pl.multiple_of(dst * ROW_ALIGN, ROW_ALIGN), ROW_ALIGN)],
                zsem.at[0])

        def bcopy(blk):
            return pltpu.make_async_copy(
                zero_ref,
                xs_ref.at[pl.ds(pl.multiple_of(blk * EXPERT_ROWS, EXPERT_ROWS), EXPERT_ROWS)],
                zsem.at[1])

        def zissue(c, carry):
            zcopy(ztab_ref[c]).start()
            return carry
        lax.fori_loop(0, nzero_ref[0], zissue, 0)

        def bissue(blk, carry):
            bcopy(blk).start()
            return carry
        lax.fori_loop(nblk_ref[0], max_blocks, bissue, 0)

        @pl.when(i >= 1)
        def _():
            wait_slot(1 - slot, nused_ref[i - 1])
        wait_slot(slot, nused_ref[i])

        def wait_rows(rows, s):
            @pl.when(rows > 0)
            def _():
                pltpu.make_async_copy(xs_ref.at[pl.ds(0, rows)], xs_ref.at[pl.ds(0, rows)],
                                      zsem.at[s]).wait()
        wait_rows(nzero_ref[0] * ROW_ALIGN, 0)
        wait_rows((max_blocks - nblk_ref[0]) * EXPERT_ROWS, 1)


def _dispatch(hn2, idxt, tab, n_used, ztab, n_zero, n_blocks, n_exp, loc_rows, max_blocks):
    n_tok, d = hn2.shape
    n_tiles = n_tok // MOE_TILE
    return pl.pallas_call(
        functools.partial(_dispatch_kernel, n_exp),
        out_shape=jax.ShapeDtypeStruct((max_blocks * EXPERT_ROWS, d), F32),
        grid_spec=pltpu.PrefetchScalarGridSpec(
            num_scalar_prefetch=5,
            grid=(n_tiles,),
            in_specs=[
                pl.BlockSpec((MOE_TILE, d), lambda i, *_: (i, 0)),
                pl.BlockSpec((TOP_K, MOE_TILE), lambda i, *_: (0, i)),
            ],
            out_specs=pl.BlockSpec(memory_space=pl.ANY),
            scratch_shapes=[
                pltpu.VMEM((2, loc_rows, d), F32),
                pltpu.VMEM((EXPERT_ROWS, d), F32),
                pltpu.SemaphoreType.DMA((2,)),
                pltpu.SemaphoreType.DMA((2,)),
            ],
        ),
        compiler_params=_cparams(("arbitrary",)),
        name="moe_dispatch",
    )(tab, n_used, ztab, n_zero, n_blocks, hn2, idxt)


def _expert_kernel(be_ref, nb_ref, x_ref, wgu_ref, bgu_ref, wdn_ref, bdn_ref, y_ref,
                   wgu_bf, wdn_bf):
    de = wdn_ref.shape[0]
    b = pl.program_id(0)

    @pl.when((b == 0) | (be_ref[b] != be_ref[jnp.maximum(b - 1, 0)]))
    def _():
        cw = 512
        for c0 in range(0, wgu_ref.shape[1], cw):
            wgu_bf[:, c0:c0 + cw] = wgu_ref[:, c0:c0 + cw].astype(BF16)
        for c0 in range(0, wdn_ref.shape[1], cw):
            wdn_bf[:, c0:c0 + cw] = wdn_ref[:, c0:c0 + cw].astype(BF16)

    @pl.when(b < nb_ref[0])
    def _():
        xb = x_ref[...].astype(BF16)
        g = jnp.dot(xb, wgu_bf[:, :de], preferred_element_type=F32) + bgu_ref[:, :de]
        u = jnp.dot(xb, wgu_bf[:, de:], preferred_element_type=F32) + bgu_ref[:, de:]
        g = jnp.minimum(g, SWIGLU_LIMIT)
        u = jnp.clip(u, -SWIGLU_LIMIT, SWIGLU_LIMIT)
        a = (u + 1.0) * g * jax.nn.sigmoid(SWIGLU_ALPHA * g)
        y_ref[...] = jnp.dot(a.astype(BF16), wdn_bf[...], preferred_element_type=F32) + bdn_ref[...]

    @pl.when(pl.program_id(0) >= nb_ref[0])
    def _():
        y_ref[...] = jnp.zeros(y_ref.shape, y_ref.dtype)


def _experts(xs, blk_exp, n_blocks, layer, wgu, bgu, wdn, bdn):
    rows, d = xs.shape
    depth, n_exp, _, de2 = wgu.shape
    de = de2 // 2
    max_blocks = rows // EXPERT_ROWS
    row_map = lambda b, be, nb: (jnp.minimum(b, nb[0] - 1), 0)
    exp_map = lambda b, be, nb: (layer, be[b], 0, 0)
    return pl.pallas_call(
        _expert_kernel,
        out_shape=jax.ShapeDtypeStruct((rows, d), F32),
        grid_spec=pltpu.PrefetchScalarGridSpec(
            num_scalar_prefetch=2,
            grid=(max_blocks,),
            in_specs=[
                pl.BlockSpec((EXPERT_ROWS, d), row_map),
                pl.BlockSpec((None, None, d, de2), exp_map),
                pl.BlockSpec((None, None, 1, de2), exp_map),
                pl.BlockSpec((None, None, de, d), exp_map),
                pl.BlockSpec((None, None, 1, d), exp_map),
            ],
            out_specs=pl.BlockSpec((EXPERT_ROWS, d), lambda b, be, nb: (b, 0)),
            scratch_shapes=[pltpu.VMEM((d, de2), BF16), pltpu.VMEM((de, d), BF16)],
        ),
        compiler_params=_cparams(("arbitrary",)),
        name="moe_experts",
    )(blk_exp, n_blocks, xs, wgu, bgu.reshape(depth, n_exp, 1, de2), wdn,
      bdn.reshape(depth, n_exp, 1, d))


def _combine_kernel(final_norm, n_exp, tab_ref, nused_ref,
                    route_ref, x1_ref, g2_ref, fg_ref, ys_ref, out_ref, loc_ref, sem):
    i = pl.program_id(0)
    n = pl.num_programs(0)
    slot = i % 2
    tile = x1_ref.shape[0]
    loc_rows = loc_ref.shape[1]
    n_chunks = loc_rows // ROW_ALIGN

    def chunk_copy(sl, c, src):
        return pltpu.make_async_copy(
            ys_ref.at[pl.ds(pl.multiple_of(src * ROW_ALIGN, ROW_ALIGN), ROW_ALIGN)],
            loc_ref.at[sl, pl.ds(pl.multiple_of(c * ROW_ALIGN, ROW_ALIGN), ROW_ALIGN)],
            sem.at[sl])

    def issue_tile(t, sl):
        def body(c, carry):
            chunk_copy(sl, c, tab_ref[t * n_chunks + c]).start()
            return carry
        lax.fori_loop(0, nused_ref[t], body, 0)

    @pl.when(i == 0)
    def _():
        issue_tile(0, 0)

    @pl.when(i + 1 < n)
    def _():
        issue_tile(i + 1, 1 - slot)

    route = route_ref[...]
    e_iota = lax.broadcasted_iota(I32, (tile, n_exp), 1)
    hits = [route[:, k:k + 1].astype(I32) == e_iota for k in range(TOP_K)]
    gates = [route[:, TOP_K + k:TOP_K + k + 1] for k in range(TOP_K)]
    onehot = jnp.zeros((tile, n_exp), F32)
    for h in hits:
        onehot = onehot + h.astype(F32)
    lower = (lax.broadcasted_iota(I32, (tile, tile), 1)
             < lax.broadcasted_iota(I32, (tile, tile), 0)).astype(BF16)
    pre = jnp.dot(lower, onehot.astype(BF16), preferred_element_type=F32)
    cnt = jnp.sum(onehot, axis=0, keepdims=True)
    cnt8 = jnp.ceil(cnt / ROW_ALIGN) * ROW_ALIGN
    upper = (lax.broadcasted_iota(I32, (n_exp, n_exp), 0)
             < lax.broadcasted_iota(I32, (n_exp, n_exp), 1)).astype(BF16)
    off = jnp.dot(jnp.broadcast_to(cnt8, (SUBLANES, n_exp)).astype(BF16), upper,
                  preferred_element_type=F32)[0:1, :]
    base = off + pre
    r_iota = lax.broadcasted_iota(I32, (tile, loc_rows), 1)
    comb = jnp.zeros((tile, loc_rows), F32)
    for h, gk in zip(hits, gates):
        dest = jnp.sum(jnp.where(h, base, 0.0), axis=1, keepdims=True).astype(I32)
        comb = comb + jnp.where(r_iota == dest, gk, 0.0)

    @pl.when(nused_ref[i] > 0)
    def _():
        rows = nused_ref[i] * ROW_ALIGN
        pltpu.make_async_copy(ys_ref.at[pl.ds(0, rows)], loc_ref.at[slot, pl.ds(0, rows)],
                              sem.at[slot]).wait()

    used_rows = nused_ref[i] * ROW_ALIGN
    row_ok = lax.broadcasted_iota(I32, (loc_rows, 1), 0) < used_rows
    yl = jnp.where(row_ok, loc_ref[slot], 0.0).astype(BF16)
    moe = jnp.dot(comb.astype(BF16), yl, preferred_element_type=F32)
    out = x1_ref[...] + g2_ref[...] * moe
    if final_norm:
        out = _rms(out, fg_ref[...])
    out_ref[...] = out


def _combine(ys, route, x1, gate2, final_g, tab, n_used, n_exp, loc_rows, final_norm):
    b, s, d = x1.shape
    n_tok = b * s
    n_tiles = n_tok // MOE_TILE
    tiles_per_seq = s // MOE_TILE
    out = pl.pallas_call(
        functools.partial(_combine_kernel, final_norm, n_exp),
        out_shape=jax.ShapeDtypeStruct((n_tok, d), F32),
        grid_spec=pltpu.PrefetchScalarGridSpec(
            num_scalar_prefetch=2,
            grid=(n_tiles,),
            in_specs=[
                pl.BlockSpec((MOE_TILE, LANES), lambda i, *_: (i, 0)),
                pl.BlockSpec((MOE_TILE, d), lambda i, *_: (i, 0)),
                pl.BlockSpec((None, 1, d), lambda i, *_: (i // tiles_per_seq, 0, 0)),
                pl.BlockSpec((1, d), lambda i, *_: (0, 0)),
                pl.BlockSpec(memory_space=pl.ANY),
            ],
            out_specs=pl.BlockSpec((MOE_TILE, d), lambda i, *_: (i, 0)),
            scratch_shapes=[
                pltpu.VMEM((2, loc_rows, d), F32),
                pltpu.SemaphoreType.DMA((2,)),
            ],
        ),
        compiler_params=_cparams(("arbitrary",)),
        name="moe_combine",
    )(tab, n_used, route, x1.reshape(n_tok, d), gate2, final_g, ys)
    return out.reshape(b, s, d)


def _moe(x1, hn2, idxt, route, cnt, gate2, final_g, layer, wgu, bgu, wdn, bdn, final_norm):
    b, s, d = x1.shape
    n_tok = b * s
    n_exp = wgu.shape[1]
    n_tiles, loc_rows, n_chunks, max_blocks = _moe_dims(n_tok, n_exp)
    tab, n_used, ztab, n_zero, blk_exp, n_blocks = _routing_tables(cnt, n_chunks, max_blocks)
    xs = _dispatch(hn2.reshape(n_tok, d), idxt, tab, n_used, ztab, n_zero, n_blocks,
                   n_exp, loc_rows, max_blocks)
    ys = _experts(xs, blk_exp, n_blocks, layer, wgu, bgu, wdn, bdn)
    return _combine(ys, route, x1, gate2, final_g, tab, n_used, n_exp, loc_rows, final_norm)


def _qkv_kernel(n_heads, scale, x_ref, cc_ref, ss_ref, kvg_ref, wdkv_ref, ckvg_ref, wuk_ref, wuv_ref,
                n1g_ref, sh1_ref, sc1_ref, wdq_ref, cqg_ref, wuq_ref,
                q_ref, k_ref, v_ref):
    x = x_ref[...]
    cc = cc_ref[...]
    ss = ss_ref[...]
    r_kv = ckvg_ref.shape[1]
    hk = _rms(x, kvg_ref[...]).astype(BF16)
    lat = jnp.dot(hk, wdkv_ref[...], preferred_element_type=F32)
    ckv = _rms(lat[:, :r_kv], ckvg_ref[...]).astype(BF16)
    krot = lat[:, r_kv:r_kv + LANES] * cc + lat[:, r_kv + LANES:r_kv + 2 * LANES] * ss
    kn = jnp.dot(ckv, wuk_ref[...], preferred_element_type=F32)
    vt = lax.dot_general(wuv_ref[...], ckv, (((1,), (1,)), ((), ())), preferred_element_type=F32)
    hq = (_rms(x, n1g_ref[...]) * (1.0 + sc1_ref[...]) + sh1_ref[...]).astype(BF16)
    cq = _rms(jnp.dot(hq, wdq_ref[...], preferred_element_type=F32), cqg_ref[...]).astype(BF16)
    qq = jnp.dot(cq, wuq_ref[...], preferred_element_type=F32) * scale
    hd = n_heads * LANES
    ts = x.shape[0]
    ones_rows = (lax.broadcasted_iota(I32, (V_ONES_ROWS, ts), 0) == 0).astype(BF16)
    for h in range(n_heads):
        sl = slice(h * LANES, (h + 1) * LANES)
        k_ref[h, :, 0:LANES] = kn[:, sl].astype(BF16)
        k_ref[h, :, LANES:2 * LANES] = krot.astype(BF16)
        v_ref[h, 0:LANES, :] = vt[h * LANES:(h + 1) * LANES, :].astype(BF16)
        v_ref[h, LANES:LANES + V_ONES_ROWS, :] = ones_rows
        q_ref[h, :, 0:LANES] = qq[:, sl].astype(BF16)
        qrot = qq[:, hd + h * LANES:hd + (h + 1) * LANES] * cc \
            + qq[:, 2 * hd + h * LANES:2 * hd + (h + 1) * LANES] * ss
        q_ref[h, :, LANES:2 * LANES] = qrot.astype(BF16)


def _qkv(x, cc, ss, kvg, wdkv, ckvg, wuk, wuv, n1g, sh1, sc1, wdq, cqg, wuq, n_heads, scale, ts):
    b, s, d = x.shape
    full = lambda a: pl.BlockSpec(a.shape, lambda i, j: (0,) * a.ndim)
    hspec = lambda w: pl.BlockSpec((None, n_heads, ts, w), lambda i, j: (i, 0, j, 0))
    return pl.pallas_call(
        functools.partial(_qkv_kernel, n_heads, scale),
        out_shape=[
            jax.ShapeDtypeStruct((b, n_heads, s, 2 * LANES), BF16),
            jax.ShapeDtypeStruct((b, n_heads, s, 2 * LANES), BF16),
            jax.ShapeDtypeStruct((b, n_heads, s // ts, LANES + V_ONES_ROWS, ts), BF16),
        ],
        grid=(b, s // ts),
        in_specs=[
            pl.BlockSpec((None, ts, d), lambda i, j: (i, j, 0)),
            pl.BlockSpec((None, ts, LANES), lambda i, j: (i, j, 0)),
            pl.BlockSpec((None, ts, LANES), lambda i, j: (i, j, 0)),
            full(kvg), full(wdkv), full(ckvg), full(wuk), full(wuv),
            full(n1g), _bvec_spec(d), _bvec_spec(d), full(wdq), full(cqg), full(wuq),
        ],
        out_specs=[hspec(2 * LANES), hspec(2 * LANES),
                   pl.BlockSpec((None, n_heads, None, LANES + V_ONES_ROWS, ts),
                                lambda i, j: (i, 0, j, 0, 0))],
        compiler_params=_cparams(("arbitrary", "arbitrary")),
        name="mla_qkv",
    )(x, cc, ss, kvg, wdkv, ckvg, wuk, wuv, n1g, sh1, sc1, wdq, cqg, wuq)


def _attn_kernel(q_ref, k_ref, vt_ref, o_ref, m_ref, acc_ref):
    tq = q_ref.shape[0]
    dv = o_ref.shape[1]
    qi = pl.program_id(2)
    q = q_ref[...]
    m_ref[...] = jnp.full(m_ref.shape, -jnp.inf, F32)
    acc_ref[...] = jnp.zeros(acc_ref.shape, F32)

    def scores(ki, masked):
        start = pl.multiple_of(ki * tq, tq)
        k = k_ref[pl.ds(start, tq), :]
        st = lax.dot_general(k, q, (((1,), (1,)), ((), ())), preferred_element_type=F32)
        if masked:
            kc = lax.broadcasted_iota(I32, (tq, tq), 0) // CHUNK
            qc = lax.broadcasted_iota(I32, (tq, tq), 1) // CHUNK
            st = jnp.where(kc <= qc, st, -jnp.inf)
        return st

    def update(ki, st, m_old, acc_old):
        m_new = jnp.maximum(m_old, jnp.max(st, axis=0, keepdims=True))
        p = jnp.exp2((st - m_new).astype(BF16))
        alpha = jnp.exp2(m_old - m_new)
        acc_new = alpha * acc_old + jnp.dot(vt_ref[ki], p, preferred_element_type=F32)
        return m_new, acc_new

    def step(tiles):
        sts = [scores(ki, masked) for ki, masked in tiles]
        state = (m_ref[...], acc_ref[...])
        for (ki, _), st in zip(tiles, sts):
            state = update(ki, st, *state)
        m_ref[...], acc_ref[...] = state

    def body(j, carry):
        step([(2 * j, False), (2 * j + 1, False)])
        return carry
    lax.fori_loop(0, qi // 2, body, 0)

    @pl.when(qi % 2 == 1)
    def _():
        step([(qi - 1, False), (qi, True)])

    @pl.when(qi % 2 == 0)
    def _():
        step([(qi, True)])

    o_ref[...] = (acc_ref[0:dv, :] / acc_ref[dv:dv + 1, :]).T.astype(o_ref.dtype)


def _attention(q, k, vt, tq):
    b, h, s, dk = q.shape
    nk, dv_ext, tk = vt.shape[2:]
    dv = dv_ext - V_ONES_ROWS
    assert tq % CHUNK == 0 and s % tq == 0 and tk == tq
    return pl.pallas_call(
        _attn_kernel,
        out_shape=jax.ShapeDtypeStruct((b, s, h * dv), BF16),
        grid=(b, h, s // tq),
        in_specs=[
            pl.BlockSpec((None, None, tq, dk), lambda i, j, t: (i, j, t, 0)),
            pl.BlockSpec((None, None, s, dk), lambda i, j, t: (i, j, 0, 0)),
            pl.BlockSpec((None, None, nk, dv_ext, tk), lambda i, j, t: (i, j, 0, 0, 0)),
        ],
        out_specs=pl.BlockSpec((None, tq, dv), lambda i, j, t: (i, t, j)),
        scratch_shapes=[pltpu.VMEM((1, tq), F32), pltpu.VMEM((dv_ext, tq), F32)],
        compiler_params=_cparams(("arbitrary", "arbitrary", "arbitrary")),
        name="mla_attention",
    )(q, k, vt)


def _swap_halves(w):
    half = w.shape[-1] // 2
    return jnp.concatenate([w[..., half:], w[..., :half]], axis=-1)


def _pad_lanes(w):
    pad = LANES - w.shape[-1]
    return jnp.concatenate([w, jnp.zeros(w.shape[:-1] + (pad,), w.dtype)], axis=-1)


def kernel(x, c, positions, mod_w, mod_b, norm1_g, norm2_g, conv_pw1_w, conv_pw1_b, conv_dw_w, conv_dw_b, conv_ln_g, conv_ln_b, conv_pw2_w, conv_pw2_b, kv_norm_g, w_dkv, ckv_norm_g, w_uk, w_uv, w_dq, cq_norm_g, w_uq, w_o, router_w, router_b, exp_w_gu, exp_b_gu, exp_w_dn, exp_b_dn, final_g):
    b, s, d = x.shape
    n_heads, nope = w_uk.shape[1], w_uk.shape[2]
    r_kv = ckv_norm_g.shape[0]
    rope = w_dkv.shape[1] - r_kv
    vdim = w_uv.shape[2]
    n_exp = router_w.shape[2]
    assert nope == LANES and vdim == LANES and rope <= LANES and d % LANES == 0
    ts = min(512, s)
    tq = min(512, s)

    mod = _modulation(c, mod_w, mod_b)
    mods = [[m.reshape(b, 1, d) for m in jnp.split(mod[l], 6, axis=-1)] for l in range(2)]
    row = lambda v: v.reshape(1, -1)

    half = rope // 2
    inv = jnp.exp(-(2.0 * math.log(ROPE_THETA) / rope) * jnp.arange(half, dtype=F32))
    ang = positions.astype(F32)[..., None] * inv
    cos, sin = jnp.cos(ang), jnp.sin(ang)
    cc = _pad_lanes(jnp.concatenate([cos, cos], axis=-1))
    ss = _pad_lanes(jnp.concatenate([-sin, sin], axis=-1))

    sh1, sc1, gt1, sh2, sc2, gt2 = mods[0]
    x1, hn2, idxt, route, cnt = _mix0(
        x, row(norm1_g[0]), sh1, sc1, gt1,
        conv_pw1_w[0].astype(BF16), row(conv_pw1_b[0]), conv_dw_w[0], row(conv_dw_b[0]),
        row(conv_ln_g[0]), row(conv_ln_b[0]), conv_pw2_w[0].astype(BF16), row(conv_pw2_b[0]),
        row(norm2_g[0]), sh2, sc2, router_w[0].T, router_b[0].reshape(n_exp, 1), ts)
    cnt = cnt[:, :ts // MOE_TILE].reshape(-1, n_exp)
    x2 = _moe(x1, hn2, idxt, route, cnt, gt2, row(final_g),
              0, exp_w_gu, exp_b_gu, exp_w_dn, exp_b_dn, False)

    sh1, sc1, gt1, sh2, sc2, gt2 = mods[1]
    wdkv_rope = w_dkv[:, r_kv:]
    wdkv_ext = jnp.concatenate(
        [w_dkv[:, :r_kv], _pad_lanes(wdkv_rope), _pad_lanes(_swap_halves(wdkv_rope))], axis=-1)
    wuq = w_uq[0]
    r_q = wuq.shape[0]
    wuq_rope = wuq[:, :, nope:]
    wuq_ext = jnp.concatenate([
        wuq[:, :, :nope].reshape(r_q, n_heads * LANES),
        _pad_lanes(wuq_rope).reshape(r_q, n_heads * LANES),
        _pad_lanes(_swap_halves(wuq_rope)).reshape(r_q, n_heads * LANES)], axis=-1)
    scale = float((nope + rope) ** -0.5 * math.log2(math.e))
    q, k, v = _qkv(
        x2, cc, ss, row(kv_norm_g), wdkv_ext.astype(BF16), row(ckv_norm_g),
        w_uk.reshape(r_kv, n_heads * nope).astype(BF16), w_uv.reshape(r_kv, n_heads * vdim).T.astype(BF16),
        row(norm1_g[1]), sh1, sc1, w_dq[0].astype(BF16), row(cq_norm_g[0]), wuq_ext.astype(BF16),
        n_heads, scale, ts)
    o = _attention(q, k, v, tq)

    x3, hn2, idxt, route, cnt = _mix1(
        x2, o, gt1, w_o[0].astype(BF16), row(norm2_g[1]), sh2, sc2,
        router_w[1].T, router_b[1].reshape(n_exp, 1), ts)
    cnt = cnt[:, :ts // MOE_TILE].reshape(-1, n_exp)
    return _moe(x3, hn2, idxt, route, cnt, gt2, row(final_g),
                1, exp_w_gu, exp_b_gu, exp_w_dn, exp_b_dn, True)
```

```python
import functools
import math

import jax
import jax.numpy as jnp
from jax import lax
from jax.experimental import pallas as pl
from jax.experimental.pallas import tpu as pltpu

CHUNK = 64
TOP_K = 4
ROPE_THETA = 10000.0
SWIGLU_ALPHA = 1.702
SWIGLU_LIMIT = 7.0
EPS = 1e-6

LANES = 128
SUBLANES = 8
VMEM_LIMIT_BYTES = 56 * 1024 * 1024

ROW_ALIGN = SUBLANES
MOE_TILE = 256
EXPERT_ROWS = 512
CONV_HALO = 32
V_ONES_ROWS = 2 * SUBLANES
ATTN_UNROLL = 4

F32 = jnp.float32
BF16 = jnp.bfloat16
I32 = jnp.int32


def _cparams(sem):
    return pltpu.CompilerParams(dimension_semantics=sem, vmem_limit_bytes=VMEM_LIMIT_BYTES)


def _rms(x, g):
    return x * lax.rsqrt(jnp.mean(x * x, axis=-1, keepdims=True) + EPS) * g


def _round_up(a, m):
    return (a + m - 1) // m * m


def _mod_kernel(c_ref, w_ref, b_ref, o_ref):
    c = c_ref[...]
    ca = c * jax.nn.sigmoid(c)
    o_ref[...] = jnp.dot(ca, w_ref[...], preferred_element_type=F32) + b_ref[...]


def _modulation(c, mod_w, mod_b):
    depth, d, d6 = mod_w.shape
    b = c.shape[0]
    bp = _round_up(b, SUBLANES)
    cp = jnp.zeros((bp, d), F32).at[:b].set(c)
    tn = d6 // 4
    out = pl.pallas_call(
        _mod_kernel,
        out_shape=jax.ShapeDtypeStruct((depth, bp, d6), F32),
        grid=(depth, d6 // tn),
        in_specs=[
            pl.BlockSpec((bp, d), lambda l, j: (0, 0)),
            pl.BlockSpec((None, d, tn), lambda l, j: (l, 0, j)),
            pl.BlockSpec((None, 1, tn), lambda l, j: (l, 0, j)),
        ],
        out_specs=pl.BlockSpec((None, bp, tn), lambda l, j: (l, 0, j)),
        compiler_params=_cparams(("arbitrary", "arbitrary")),
        name="adaln_modulation",
    )(cp, mod_w, mod_b.reshape(depth, 1, d6))
    return out[:, :b]


def _pre_moe(x1, g2, sh2, sc2, rwt, rb, hn_ref, idxt_ref, route_ref, cnt_ref):
    ts = x1.shape[0]
    n_exp = rwt.shape[0]
    hn = _rms(x1, g2) * (1.0 + sc2) + sh2
    hn_ref[...] = hn.astype(BF16)
    logits = lax.dot_general(rwt, hn, (((1,), (1,)), ((), ())),
                             precision=lax.Precision.HIGHEST,
                             preferred_element_type=F32) + rb
    e_iota = lax.broadcasted_iota(I32, (n_exp, ts), 0)
    vals, idxs = [], []
    cur = logits
    for _ in range(TOP_K):
        m = jnp.max(cur, axis=0, keepdims=True)
        i = jnp.min(jnp.where(cur == m, e_iota, n_exp), axis=0, keepdims=True)
        vals.append(m)
        idxs.append(i)
        cur = jnp.where(e_iota == i, -jnp.inf, cur)
    exps = [jnp.exp(v - vals[0]) for v in vals]
    den = exps[0]
    for e in exps[1:]:
        den = den + e
    gates = [e / den for e in exps]
    idxt_ref[...] = jnp.concatenate(idxs, axis=0)
    rows = jnp.concatenate([i.astype(F32) for i in idxs] + gates
                           + [jnp.zeros((LANES - 2 * TOP_K, ts), F32)], axis=0)
    route_ref[...] = rows.T
    onehot = jnp.zeros((n_exp, ts), F32)
    for i in idxs:
        onehot = onehot + (e_iota == i).astype(F32)
    sel = (lax.broadcasted_iota(I32, (SUBLANES, ts), 1) // MOE_TILE
           == lax.broadcasted_iota(I32, (SUBLANES, ts), 0)).astype(BF16)
    cnt = lax.dot_general(sel, onehot.astype(BF16), (((1,), (1,)), ((), ())),
                          preferred_element_type=F32)
    cnt_ref[...] = cnt.astype(I32)


def _pre_moe_specs(b, s, d, ts, n_exp):
    ns = s // ts
    out_shape = [
        jax.ShapeDtypeStruct((b, s, d), F32),
        jax.ShapeDtypeStruct((b, s, d), BF16),
        jax.ShapeDtypeStruct((TOP_K, b * s), I32),
        jax.ShapeDtypeStruct((b * s, LANES), F32),
        jax.ShapeDtypeStruct((b * ns, SUBLANES, n_exp), I32),
    ]
    out_specs = [
        pl.BlockSpec((None, ts, d), lambda i, j: (i, j, 0)),
        pl.BlockSpec((None, ts, d), lambda i, j: (i, j, 0)),
        pl.BlockSpec((TOP_K, ts), lambda i, j: (0, i * ns + j)),
        pl.BlockSpec((ts, LANES), lambda i, j: (i * ns + j, 0)),
        pl.BlockSpec((None, SUBLANES, n_exp), lambda i, j: (i * ns + j, 0, 0)),
    ]
    return out_shape, out_specs


def _vec_spec(d):
    return pl.BlockSpec((1, d), lambda i, j: (0, 0))


def _bvec_spec(d):
    return pl.BlockSpec((None, 1, d), lambda i, j: (i, 0, 0))


def _mix0_kernel(x_ref, n1g_ref, sh1_ref, sc1_ref, gt1_ref,
                 pw1w_ref, pw1b_ref, dww_ref, dwb_ref, lng_ref, lnb_ref,
                 pw2w_ref, pw2b_ref, n2g_ref, sh2_ref, sc2_ref, rwt_ref, rb_ref,
                 x1_ref, hn_ref, idxt_ref, route_ref, cnt_ref, buf_ref, conv_ref, shift_ref):
    ts, d = x_ref.shape
    width = dww_ref.shape[0]
    x = x_ref[...]
    hn = _rms(x, n1g_ref[...]) * (1.0 + sc1_ref[...]) + sh1_ref[...]
    hb = hn.astype(BF16)
    a = jnp.dot(hb, pw1w_ref[:, :d], preferred_element_type=F32) + pw1b_ref[:, :d]
    g = jnp.dot(hb, pw1w_ref[:, d:], preferred_element_type=F32) + pw1b_ref[:, d:]
    glu = a * jax.nn.sigmoid(g)

    @pl.when(pl.program_id(1) == 0)
    def _():
        buf_ref[0:CONV_HALO, :] = jnp.zeros((CONV_HALO, d), F32)

    buf_ref[CONV_HALO:, :] = glu
    base = CONV_HALO - (width - 1)
    rc = 32
    lc = min(512, d)
    sh_rows = shift_ref.shape[1]
    for c0 in range(0, d, lc):
        for r in range(1, SUBLANES):
            shift_ref[r - 1] = buf_ref[r:r + sh_rows, c0:c0 + lc]
        for r0 in range(0, ts, rc):
            acc = jnp.zeros((rc, lc), F32)
            for k in range(width):
                q8, r = divmod(base + k, SUBLANES)
                lo = q8 * SUBLANES + r0
                if r == 0:
                    win = buf_ref[lo:lo + rc, c0:c0 + lc]
                else:
                    win = shift_ref[r - 1, lo:lo + rc, :]
                acc = acc + dww_ref[k:k + 1, c0:c0 + lc] * win
            conv_ref[r0:r0 + rc, c0:c0 + lc] = acc
    buf_ref[0:CONV_HALO, :] = buf_ref[ts:ts + CONV_HALO, :]
    u = conv_ref[...] + dwb_ref[...]
    mu = jnp.mean(u, axis=-1, keepdims=True)
    dlt = u - mu
    var = jnp.mean(dlt * dlt, axis=-1, keepdims=True)
    u = dlt * lax.rsqrt(var + EPS) * lng_ref[...] + lnb_ref[...]
    u = u * jax.nn.sigmoid(u)
    y = jnp.dot(u.astype(BF16), pw2w_ref[...], preferred_element_type=F32) + pw2b_ref[...]
    x1 = x + gt1_ref[...] * y
    x1_ref[...] = x1
    _pre_moe(x1, n2g_ref[...], sh2_ref[...], sc2_ref[...], rwt_ref[...], rb_ref[...],
             hn_ref, idxt_ref, route_ref, cnt_ref)


def _mix0(x, n1g, sh1, sc1, gt1, pw1w, pw1b, dww, dwb, lng, lnb, pw2w, pw2b,
          n2g, sh2, sc2, rwt, rb, ts):
    b, s, d = x.shape
    n_exp = rwt.shape[0]
    width = dww.shape[0]
    assert width - 1 <= CONV_HALO and ts % MOE_TILE == 0 and ts // MOE_TILE <= SUBLANES
    out_shape, out_specs = _pre_moe_specs(b, s, d, ts, n_exp)
    full = lambda shp: pl.BlockSpec(shp, lambda i, j: (0,) * len(shp))
    return pl.pallas_call(
        _mix0_kernel,
        out_shape=out_shape,
        grid=(b, s // ts),
        in_specs=[
            pl.BlockSpec((None, ts, d), lambda i, j: (i, j, 0)),
            _vec_spec(d), _bvec_spec(d), _bvec_spec(d), _bvec_spec(d),
            full((d, 2 * d)), full((1, 2 * d)), full((width, d)), full((1, d)),
            full((1, d)), full((1, d)), full((d, d)), full((1, d)),
            _vec_spec(d), _bvec_spec(d), _bvec_spec(d),
            full((n_exp, d)), full((n_exp, 1)),
        ],
        out_specs=out_specs,
        scratch_shapes=[
            pltpu.VMEM((CONV_HALO + ts, d), F32),
            pltpu.VMEM((ts, d), F32),
            pltpu.VMEM((SUBLANES - 1, CONV_HALO + ts - SUBLANES, min(512, d)), F32),
        ],
        compiler_params=_cparams(("arbitrary", "arbitrary")),
        name="conformer_mixer",
    )(x, n1g, sh1, sc1, gt1, pw1w, pw1b, dww, dwb, lng, lnb, pw2w, pw2b,
      n2g, sh2, sc2, rwt, rb)


def _mix1_kernel(x_ref, o_ref, gt1_ref, wo_ref, n2g_ref, sh2_ref, sc2_ref, rwt_ref, rb_ref,
                 x1_ref, hn_ref, idxt_ref, route_ref, cnt_ref):
    y = jnp.dot(o_ref[...], wo_ref[...], preferred_element_type=F32)
    x1 = x_ref[...] + gt1_ref[...] * y
    x1_ref[...] = x1
    _pre_moe(x1, n2g_ref[...], sh2_ref[...], sc2_ref[...], rwt_ref[...], rb_ref[...],
             hn_ref, idxt_ref, route_ref, cnt_ref)


def _mix1(x, o, gt1, wo, n2g, sh2, sc2, rwt, rb, ts):
    b, s, d = x.shape
    n_exp = rwt.shape[0]
    do = o.shape[-1]
    assert ts % MOE_TILE == 0 and ts // MOE_TILE <= SUBLANES
    out_shape, out_specs = _pre_moe_specs(b, s, d, ts, n_exp)
    full = lambda shp: pl.BlockSpec(shp, lambda i, j: (0,) * len(shp))
    return pl.pallas_call(
        _mix1_kernel,
        out_shape=out_shape,
        grid=(b, s // ts),
        in_specs=[
            pl.BlockSpec((None, ts, d), lambda i, j: (i, j, 0)),
            pl.BlockSpec((None, ts, do), lambda i, j: (i, j, 0)),
            _bvec_spec(d), full((do, d)),
            _vec_spec(d), _bvec_spec(d), _bvec_spec(d),
            full((n_exp, d)), full((n_exp, 1)),
        ],
        out_specs=out_specs,
        compiler_params=_cparams(("arbitrary", "arbitrary")),
        name="attn_out_mixer",
    )(x, o, gt1, wo, n2g, sh2, sc2, rwt, rb)


def _moe_dims(n_tok, n_exp):
    n_tiles = n_tok // MOE_TILE
    loc_rows = _round_up(TOP_K * MOE_TILE + n_exp * (ROW_ALIGN - 1), LANES)
    n_chunks = loc_rows // ROW_ALIGN
    max_rows = TOP_K * n_tok + n_tiles * n_exp * (ROW_ALIGN - 1) + n_exp * (EXPERT_ROWS - ROW_ALIGN)
    max_blocks = -(-max_rows // EXPERT_ROWS)
    return n_tiles, loc_rows, n_chunks, max_blocks


def _routing_tables(cnt, n_chunks, max_blocks):
    n_tiles, n_exp = cnt.shape
    cnt8 = _round_up(cnt, ROW_ALIGN)
    off = jnp.cumsum(cnt8, axis=1) - cnt8
    seg_len = cnt8.sum(axis=0)
    seg_pad = _round_up(seg_len, EXPERT_ROWS)
    seg_end = jnp.cumsum(seg_pad)
    seg_start = seg_end - seg_pad
    run_start = seg_start[None, :] + jnp.cumsum(cnt8, axis=0) - cnt8
    n_used = (cnt8.sum(axis=1) // ROW_ALIGN).astype(I32)
    c_row = jnp.arange(n_chunks, dtype=I32) * ROW_ALIGN
    local_end = off + cnt8
    in_run = ((c_row[None, :, None] >= off[:, None, :])
              & (c_row[None, :, None] < local_end[:, None, :]))
    delta = jnp.sum(jnp.where(in_run, (run_start - off)[:, None, :], 0), axis=-1)
    tab = ((delta + c_row[None, :]) // ROW_ALIGN).astype(I32)
    n_blocks = (seg_end[-1] // EXPERT_ROWS).astype(I32)
    blk_row = jnp.arange(max_blocks, dtype=I32) * EXPERT_ROWS
    blk_row = jnp.minimum(blk_row, seg_end[-1] - EXPERT_ROWS)
    blk_exp = jnp.minimum((blk_row[:, None] >= seg_end[None, :]).sum(axis=-1), n_exp - 1).astype(I32)
    per = EXPERT_ROWS // ROW_ALIGN
    n_tail = (seg_pad - seg_len) // ROW_ALIGN
    z_end = jnp.cumsum(n_tail)
    z_start = z_end - n_tail
    pos = jnp.arange(n_exp * per, dtype=I32)
    in_tail = (pos[:, None] >= z_start[None, :]) & (pos[:, None] < z_end[None, :])
    first = (seg_start + seg_len) // ROW_ALIGN - z_start
    ztab = (jnp.sum(jnp.where(in_tail, first[None, :], 0), axis=-1) + pos).astype(I32)
    ztab = jnp.where(pos < z_end[-1], ztab, 0)
    n_zero = z_end[-1].astype(I32)
    return tab.reshape(-1), n_used, ztab, n_zero.reshape(1), blk_exp, n_blocks.reshape(1)


def _dispatch_kernel(n_exp, tab_ref, nused_ref, ztab_ref, nzero_ref, nblk_ref,
                     x_ref, idxt_ref, xs_ref, loc_ref, zero_ref, sem, zsem):
    i = pl.program_id(0)
    n = pl.num_programs(0)
    slot = i % 2
    tile = x_ref.shape[0]
    loc_rows = loc_ref.shape[1]
    n_chunks = loc_rows // ROW_ALIGN

    def chunk_copy(sl, c, dst):
        return pltpu.make_async_copy(
            loc_ref.at[sl, pl.ds(pl.multiple_of(c * ROW_ALIGN, ROW_ALIGN), ROW_ALIGN)],
            xs_ref.at[pl.ds(pl.multiple_of(dst * ROW_ALIGN, ROW_ALIGN), ROW_ALIGN)],
            sem.at[sl])

    def wait_slot(sl, count):
        @pl.when(count > 0)
        def _():
            rows = count * ROW_ALIGN
            pltpu.make_async_copy(loc_ref.at[sl, pl.ds(0, rows)], xs_ref.at[pl.ds(0, rows)],
                                  sem.at[sl]).wait()

    @pl.when(i >= 2)
    def _():
        wait_slot(slot, nused_ref[i - 2])

    idx = idxt_ref[...]
    e_iota = lax.broadcasted_iota(I32, (n_exp, tile), 0)
    hits = [idx[k:k + 1, :] == e_iota for k in range(TOP_K)]
    onehot = jnp.zeros((n_exp, tile), F32)
    for h in hits:
        onehot = onehot + h.astype(F32)
    upper = (lax.broadcasted_iota(I32, (tile, tile), 0)
             < lax.broadcasted_iota(I32, (tile, tile), 1)).astype(BF16)
    pre = jnp.dot(onehot.astype(BF16), upper, preferred_element_type=F32)
    cnt = jnp.sum(onehot, axis=1, keepdims=True)
    cnt8 = jnp.ceil(cnt / ROW_ALIGN) * ROW_ALIGN
    lower = (lax.broadcasted_iota(I32, (n_exp, n_exp), 1)
             < lax.broadcasted_iota(I32, (n_exp, n_exp), 0)).astype(BF16)
    off = jnp.dot(lower, jnp.broadcast_to(cnt8, (n_exp, LANES)).astype(BF16),
                  preferred_element_type=F32)[:, 0:1]
    base = off + pre
    r_iota = lax.broadcasted_iota(I32, (loc_rows, tile), 0)
    perm = jnp.zeros((loc_rows, tile), F32)
    for h in hits:
        dest = jnp.sum(jnp.where(h, base, 0.0), axis=0, keepdims=True).astype(I32)
        perm = perm + (r_iota == dest).astype(F32)
    loc_ref[slot] = jnp.dot(perm.astype(BF16), x_ref[...], preferred_element_type=F32)

    def issue(c, carry):
        chunk_copy(slot, c, tab_ref[i * n_chunks + c]).start()
        return carry
    lax.fori_loop(0, nused_ref[i], issue, 0)

    @pl.when(i == n - 1)
    def _():
        zero_ref[...] = jnp.zeros(zero_ref.shape, F32)
        max_blocks = xs_ref.shape[0] // EXPERT_ROWS

        def zcopy(dst):
            return pltpu.make_async_copy(
                zero_ref.at[pl.ds(0, ROW_ALIGN)],
                xs_ref.at[pl.ds(pl.multiple_of(dst * ROW_ALIGN, ROW_ALIGN), ROW_ALIGN)],
                zsem.at[0])

        def bcopy(blk):
            return pltpu.make_async_copy(
                zero_ref,
                xs_ref.at[pl.ds(pl.multiple_of(blk * EXPERT_ROWS, EXPERT_ROWS), EXPERT_ROWS)],
                zsem.at[1])

        def zissue(c, carry):
            zcopy(ztab_ref[c]).start()
            return carry
        lax.fori_loop(0, nzero_ref[0], zissue, 0)

        def bissue(blk, carry):
            bcopy(blk).start()
            return carry
        lax.fori_loop(nblk_ref[0], max_blocks, bissue, 0)

        @pl.when(i >= 1)
        def _():
            wait_slot(1 - slot, nused_ref[i - 1])
        wait_slot(slot, nused_ref[i])

        def wait_rows(rows, s):
            @pl.when(rows > 0)
            def _():
                pltpu.make_async_copy(xs_ref.at[pl.ds(0, rows)], xs_ref.at[pl.ds(0, rows)],
                                      zsem.at[s]).wait()
        wait_rows(nzero_ref[0] * ROW_ALIGN, 0)
        wait_rows((max_blocks - nblk_ref[0]) * EXPERT_ROWS, 1)


def _dispatch(hn2, idxt, tab, n_used, ztab, n_zero, n_blocks, n_exp, loc_rows, max_blocks):
    n_tok, d = hn2.shape
    n_tiles = n_tok // MOE_TILE
    return pl.pallas_call(
        functools.partial(_dispatch_kernel, n_exp),
        out_shape=jax.ShapeDtypeStruct((max_blocks * EXPERT_ROWS, d), F32),
        grid_spec=pltpu.PrefetchScalarGridSpec(
            num_scalar_prefetch=5,
            grid=(n_tiles,),
            in_specs=[
                pl.BlockSpec((MOE_TILE, d), lambda i, *_: (i, 0)),
                pl.BlockSpec((TOP_K, MOE_TILE), lambda i, *_: (0, i)),
            ],
            out_specs=pl.BlockSpec(memory_space=pl.ANY),
            scratch_shapes=[
                pltpu.VMEM((2, loc_rows, d), F32),
                pltpu.VMEM((EXPERT_ROWS, d), F32),
                pltpu.SemaphoreType.DMA((2,)),
                pltpu.SemaphoreType.DMA((2,)),
            ],
        ),
        compiler_params=_cparams(("arbitrary",)),
        name="moe_dispatch",
    )(tab, n_used, ztab, n_zero, n_blocks, hn2, idxt)


def _expert_kernel(be_ref, nb_ref, x_ref, wgu_ref, bgu_ref, wdn_ref, bdn_ref, y_ref,
                   wgu_bf, wdn_bf):
    de = wdn_ref.shape[0]
    b = pl.program_id(0)

    @pl.when((b == 0) | (be_ref[b] != be_ref[jnp.maximum(b - 1, 0)]))
    def _():
        cw = 512
        for c0 in range(0, wgu_ref.shape[1], cw):
            wgu_bf[:, c0:c0 + cw] = wgu_ref[:, c0:c0 + cw].astype(BF16)
        for c0 in range(0, wdn_ref.shape[1], cw):
            wdn_bf[:, c0:c0 + cw] = wdn_ref[:, c0:c0 + cw].astype(BF16)

    @pl.when(b < nb_ref[0])
    def _():
        xb = x_ref[...].astype(BF16)
        g = jnp.dot(xb, wgu_bf[:, :de], preferred_element_type=F32) + bgu_ref[:, :de]
        u = jnp.dot(xb, wgu_bf[:, de:], preferred_element_type=F32) + bgu_ref[:, de:]
        g = jnp.minimum(g, SWIGLU_LIMIT)
        u = jnp.clip(u, -SWIGLU_LIMIT, SWIGLU_LIMIT)
        a = (u + 1.0) * g * jax.nn.sigmoid(SWIGLU_ALPHA * g)
        y_ref[...] = jnp.dot(a.astype(BF16), wdn_bf[...], preferred_element_type=F32) + bdn_ref[...]

    @pl.when(pl.program_id(0) >= nb_ref[0])
    def _():
        y_ref[...] = jnp.zeros(y_ref.shape, y_ref.dtype)


def _experts(xs, blk_exp, n_blocks, layer, wgu, bgu, wdn, bdn):
    rows, d = xs.shape
    depth, n_exp, _, de2 = wgu.shape
    de = de2 // 2
    max_blocks = rows // EXPERT_ROWS
    row_map = lambda b, be, nb: (jnp.minimum(b, nb[0] - 1), 0)
    exp_map = lambda b, be, nb: (layer, be[b], 0, 0)
    return pl.pallas_call(
        _expert_kernel,
        out_shape=jax.ShapeDtypeStruct((rows, d), F32),
        grid_spec=pltpu.PrefetchScalarGridSpec(
            num_scalar_prefetch=2,
            grid=(max_blocks,),
            in_specs=[
                pl.BlockSpec((EXPERT_ROWS, d), row_map),
                pl.BlockSpec((None, None, d, de2), exp_map),
                pl.BlockSpec((None, None, 1, de2), exp_map),
                pl.BlockSpec((None, None, de, d), exp_map),
                pl.BlockSpec((None, None, 1, d), exp_map),
            ],
            out_specs=pl.BlockSpec((EXPERT_ROWS, d), lambda b, be, nb: (b, 0)),
            scratch_shapes=[pltpu.VMEM((d, de2), BF16), pltpu.VMEM((de, d), BF16)],
        ),
        compiler_params=_cparams(("arbitrary",)),
        name="moe_experts",
    )(blk_exp, n_blocks, xs, wgu, bgu.reshape(depth, n_exp, 1, de2), wdn,
      bdn.reshape(depth, n_exp, 1, d))


def _combine_kernel(final_norm, n_exp, tab_ref, nused_ref,
                    route_ref, x1_ref, g2_ref, fg_ref, ys_ref, out_ref, loc_ref, sem):
    i = pl.program_id(0)
    n = pl.num_programs(0)
    slot = i % 2
    tile = x1_ref.shape[0]
    loc_rows = loc_ref.shape[1]
    n_chunks = loc_rows // ROW_ALIGN

    def chunk_copy(sl, c, src):
        return pltpu.make_async_copy(
            ys_ref.at[pl.ds(pl.multiple_of(src * ROW_ALIGN, ROW_ALIGN), ROW_ALIGN)],
            loc_ref.at[sl, pl.ds(pl.multiple_of(c * ROW_ALIGN, ROW_ALIGN), ROW_ALIGN)],
            sem.at[sl])

    def issue_tile(t, sl):
        def body(c, carry):
            chunk_copy(sl, c, tab_ref[t * n_chunks + c]).start()
            return carry
        lax.fori_loop(0, nused_ref[t], body, 0)

    @pl.when(i == 0)
    def _():
        issue_tile(0, 0)

    @pl.when(i + 1 < n)
    def _():
        issue_tile(i + 1, 1 - slot)

    route = route_ref[...]
    e_iota = lax.broadcasted_iota(I32, (tile, n_exp), 1)
    hits = [route[:, k:k + 1].astype(I32) == e_iota for k in range(TOP_K)]
    gates = [route[:, TOP_K + k:TOP_K + k + 1] for k in range(TOP_K)]
    onehot = jnp.zeros((tile, n_exp), F32)
    for h in hits:
        onehot = onehot + h.astype(F32)
    lower = (lax.broadcasted_iota(I32, (tile, tile), 1)
             < lax.broadcasted_iota(I32, (tile, tile), 0)).astype(BF16)
    pre = jnp.dot(lower, onehot.astype(BF16), preferred_element_type=F32)
    cnt = jnp.sum(onehot, axis=0, keepdims=True)
    cnt8 = jnp.ceil(cnt / ROW_ALIGN) * ROW_ALIGN
    upper = (lax.broadcasted_iota(I32, (n_exp, n_exp), 0)
             < lax.broadcasted_iota(I32, (n_exp, n_exp), 1)).astype(BF16)
    off = jnp.dot(jnp.broadcast_to(cnt8, (SUBLANES, n_exp)).astype(BF16), upper,
                  preferred_element_type=F32)[0:1, :]
    base = off + pre
    r_iota = lax.broadcasted_iota(I32, (tile, loc_rows), 1)
    comb = jnp.zeros((tile, loc_rows), F32)
    for h, gk in zip(hits, gates):
        dest = jnp.sum(jnp.where(h, base, 0.0), axis=1, keepdims=True).astype(I32)
        comb = comb + jnp.where(r_iota == dest, gk, 0.0)

    @pl.when(nused_ref[i] > 0)
    def _():
        rows = nused_ref[i] * ROW_ALIGN
        pltpu.make_async_copy(ys_ref.at[pl.ds(0, rows)], loc_ref.at[slot, pl.ds(0, rows)],
                              sem.at[slot]).wait()

    used_rows = nused_ref[i] * ROW_ALIGN
    row_ok = lax.broadcasted_iota(I32, (loc_rows, 1), 0) < used_rows
    yl = jnp.where(row_ok, loc_ref[slot], 0.0).astype(BF16)
    moe = jnp.dot(comb.astype(BF16), yl, preferred_element_type=F32)
    out = x1_ref[...] + g2_ref[...] * moe
    if final_norm:
        out = _rms(out, fg_ref[...])
    out_ref[...] = out


def _combine(ys, route, x1, gate2, final_g, tab, n_used, n_exp, loc_rows, final_norm):
    b, s, d = x1.shape
    n_tok = b * s
    n_tiles = n_tok // MOE_TILE
    tiles_per_seq = s // MOE_TILE
    out = pl.pallas_call(
        functools.partial(_combine_kernel, final_norm, n_exp),
        out_shape=jax.ShapeDtypeStruct((n_tok, d), F32),
        grid_spec=pltpu.PrefetchScalarGridSpec(
            num_scalar_prefetch=2,
            grid=(n_tiles,),
            in_specs=[
                pl.BlockSpec((MOE_TILE, LANES), lambda i, *_: (i, 0)),
                pl.BlockSpec((MOE_TILE, d), lambda i, *_: (i, 0)),
                pl.BlockSpec((None, 1, d), lambda i, *_: (i // tiles_per_seq, 0, 0)),
                pl.BlockSpec((1, d), lambda i, *_: (0, 0)),
                pl.BlockSpec(memory_space=pl.ANY),
            ],
            out_specs=pl.BlockSpec((MOE_TILE, d), lambda i, *_: (i, 0)),
            scratch_shapes=[
                pltpu.VMEM((2, loc_rows, d), F32),
                pltpu.SemaphoreType.DMA((2,)),
            ],
        ),
        compiler_params=_cparams(("arbitrary",)),
        name="moe_combine",
    )(tab, n_used, route, x1.reshape(n_tok, d), gate2, final_g, ys)
    return out.reshape(b, s, d)


def _moe(x1, hn2, idxt, route, cnt, gate2, final_g, layer, wgu, bgu, wdn, bdn, final_norm):
    b, s, d = x1.shape
    n_tok = b * s
    n_exp = wgu.shape[1]
    n_tiles, loc_rows, n_chunks, max_blocks = _moe_dims(n_tok, n_exp)
    tab, n_used, ztab, n_zero, blk_exp, n_blocks = _routing_tables(cnt, n_chunks, max_blocks)
    xs = _dispatch(hn2.reshape(n_tok, d), idxt, tab, n_used, ztab, n_zero, n_blocks,
                   n_exp, loc_rows, max_blocks)
    ys = _experts(xs, blk_exp, n_blocks, layer, wgu, bgu, wdn, bdn)
    return _combine(ys, route, x1, gate2, final_g, tab, n_used, n_exp, loc_rows, final_norm)


def _qkv_kernel(n_heads, scale, x_ref, cc_ref, ss_ref, kvg_ref, wdkv_ref, ckvg_ref, wuk_ref, wuv_ref,
                n1g_ref, sh1_ref, sc1_ref, wdq_ref, cqg_ref, wuq_ref,
                q_ref, k_ref, v_ref):
    x = x_ref[...]
    cc = cc_ref[...]
    ss = ss_ref[...]
    r_kv = ckvg_ref.shape[1]
    hk = _rms(x, kvg_ref[...]).astype(BF16)
    lat = jnp.dot(hk, wdkv_ref[...], preferred_element_type=F32)
    ckv = _rms(lat[:, :r_kv], ckvg_ref[...]).astype(BF16)
    krot = lat[:, r_kv:r_kv + LANES] * cc + lat[:, r_kv + LANES:r_kv + 2 * LANES] * ss
    kn = jnp.dot(ckv, wuk_ref[...], preferred_element_type=F32)
    vt = lax.dot_general(wuv_ref[...], ckv, (((1,), (1,)), ((), ())), preferred_element_type=F32)
    hq = (_rms(x, n1g_ref[...]) * (1.0 + sc1_ref[...]) + sh1_ref[...]).astype(BF16)
    cq = _rms(jnp.dot(hq, wdq_ref[...], preferred_element_type=F32), cqg_ref[...]).astype(BF16)
    qq = jnp.dot(cq, wuq_ref[...], preferred_element_type=F32) * scale
    hd = n_heads * LANES
    ts = x.shape[0]
    ones_rows = (lax.broadcasted_iota(I32, (V_ONES_ROWS, ts), 0) == 0).astype(BF16)
    for h in range(n_heads):
        sl = slice(h * LANES, (h + 1) * LANES)
        k_ref[h, :, 0:LANES] = kn[:, sl].astype(BF16)
        k_ref[h, :, LANES:2 * LANES] = krot.astype(BF16)
        v_ref[h, 0:LANES, :] = vt[h * LANES:(h + 1) * LANES, :].astype(BF16)
        v_ref[h, LANES:LANES + V_ONES_ROWS, :] = ones_rows
        q_ref[h, :, 0:LANES] = qq[:, sl].astype(BF16)
        qrot = qq[:, hd + h * LANES:hd + (h + 1) * LANES] * cc \
            + qq[:, 2 * hd + h * LANES:2 * hd + (h + 1) * LANES] * ss
        q_ref[h, :, LANES:2 * LANES] = qrot.astype(BF16)


def _qkv(x, cc, ss, kvg, wdkv, ckvg, wuk, wuv, n1g, sh1, sc1, wdq, cqg, wuq, n_heads, scale, ts):
    b, s, d = x.shape
    full = lambda a: pl.BlockSpec(a.shape, lambda i, j: (0,) * a.ndim)
    hspec = lambda w: pl.BlockSpec((None, n_heads, ts, w), lambda i, j: (i, 0, j, 0))
    return pl.pallas_call(
        functools.partial(_qkv_kernel, n_heads, scale),
        out_shape=[
            jax.ShapeDtypeStruct((b, n_heads, s, 2 * LANES), BF16),
            jax.ShapeDtypeStruct((b, n_heads, s, 2 * LANES), BF16),
            jax.ShapeDtypeStruct((b, n_heads, s // ts, LANES + V_ONES_ROWS, ts), BF16),
        ],
        grid=(b, s // ts),
        in_specs=[
            pl.BlockSpec((None, ts, d), lambda i, j: (i, j, 0)),
            pl.BlockSpec((None, ts, LANES), lambda i, j: (i, j, 0)),
            pl.BlockSpec((None, ts, LANES), lambda i, j: (i, j, 0)),
            full(kvg), full(wdkv), full(ckvg), full(wuk), full(wuv),
            full(n1g), _bvec_spec(d), _bvec_spec(d), full(wdq), full(cqg), full(wuq),
        ],
        out_specs=[hspec(2 * LANES), hspec(2 * LANES),
                   pl.BlockSpec((None, n_heads, None, LANES + V_ONES_ROWS, ts),
                                lambda i, j: (i, 0, j, 0, 0))],
        compiler_params=_cparams(("arbitrary", "arbitrary")),
        name="mla_qkv",
    )(x, cc, ss, kvg, wdkv, ckvg, wuk, wuv, n1g, sh1, sc1, wdq, cqg, wuq)


def _attn_kernel(q_ref, k_ref, vt_ref, o_ref, m_ref, acc_ref):
    tq = q_ref.shape[0]
    tk = vt_ref.shape[2]
    ratio = tq // tk
    dv = o_ref.shape[1]
    qi = pl.program_id(2)
    q = q_ref[...]
    m_ref[...] = jnp.full(m_ref.shape, -jnp.inf, F32)
    acc_ref[...] = jnp.zeros(acc_ref.shape, F32)

    def scores(ki, diag):
        start = pl.multiple_of(ki * tk, tk)
        k = k_ref[pl.ds(start, tk), :]
        st = lax.dot_general(k, q, (((1,), (1,)), ((), ())), preferred_element_type=F32)
        if diag is not None:
            kc = (lax.broadcasted_iota(I32, (tk, tq), 0) + diag * tk) // CHUNK
            qc = lax.broadcasted_iota(I32, (tk, tq), 1) // CHUNK
            st = jnp.where(kc <= qc, st, -jnp.inf)
        return st

    def update(ki, st, m_old, acc_old):
        m_new = jnp.maximum(m_old, jnp.max(st, axis=0, keepdims=True))
        p = jnp.exp2((st - m_new).astype(BF16))
        alpha = jnp.exp2(m_old - m_new)
        acc_new = alpha * acc_old + jnp.dot(vt_ref[ki], p, preferred_element_type=F32)
        return m_new, acc_new

    def step(tiles):
        sts = [scores(ki, diag) for ki, diag in tiles]
        state = (m_ref[...], acc_ref[...])
        for (ki, _), st in zip(tiles, sts):
            state = update(ki, st, *state)
        m_ref[...], acc_ref[...] = state

    def body(j, carry):
        step([(ATTN_UNROLL * j + t, None) for t in range(ATTN_UNROLL)])
        return carry
    n_below = qi * ratio
    trips = n_below // ATTN_UNROLL
    lax.fori_loop(0, trips, body, 0)

    for rem in range(0, ATTN_UNROLL, math.gcd(ratio, ATTN_UNROLL)):
        @pl.when(n_below % ATTN_UNROLL == rem)
        def _(rem=rem):
            step([(trips * ATTN_UNROLL + t, None) for t in range(rem)]
                 + [(n_below + t, t) for t in range(ratio)])

    o_ref[...] = (acc_ref[0:dv, :] / acc_ref[dv:dv + 1, :]).T.astype(o_ref.dtype)


def _attention(q, k, vt, tq):
    b, h, s, dk = q.shape
    nk, dv_ext, tk = vt.shape[2:]
    dv = dv_ext - V_ONES_ROWS
    assert tk % CHUNK == 0 and s % tq == 0 and tq % tk == 0
    return pl.pallas_call(
        _attn_kernel,
        out_shape=jax.ShapeDtypeStruct((b, s, h * dv), BF16),
        grid=(b, h, s // tq),
        in_specs=[
            pl.BlockSpec((None, None, tq, dk), lambda i, j, t: (i, j, t, 0)),
            pl.BlockSpec((None, None, s, dk), lambda i, j, t: (i, j, 0, 0)),
            pl.BlockSpec((None, None, nk, dv_ext, tk), lambda i, j, t: (i, j, 0, 0, 0)),
        ],
        out_specs=pl.BlockSpec((None, tq, dv), lambda i, j, t: (i, t, j)),
        scratch_shapes=[pltpu.VMEM((1, tq), F32), pltpu.VMEM((dv_ext, tq), F32)],
        compiler_params=_cparams(("arbitrary", "arbitrary", "arbitrary")),
        name="mla_attention",
    )(q, k, vt)


def _swap_halves(w):
    half = w.shape[-1] // 2
    return jnp.concatenate([w[..., half:], w[..., :half]], axis=-1)


def _pad_lanes(w):
    pad = LANES - w.shape[-1]
    return jnp.concatenate([w, jnp.zeros(w.shape[:-1] + (pad,), w.dtype)], axis=-1)


def kernel(x, c, positions, mod_w, mod_b, norm1_g, norm2_g, conv_pw1_w, conv_pw1_b, conv_dw_w, conv_dw_b, conv_ln_g, conv_ln_b, conv_pw2_w, conv_pw2_b, kv_norm_g, w_dkv, ckv_norm_g, w_uk, w_uv, w_dq, cq_norm_g, w_uq, w_o, router_w, router_b, exp_w_gu, exp_b_gu, exp_w_dn, exp_b_dn, final_g):
    b, s, d = x.shape
    n_heads, nope = w_uk.shape[1], w_uk.shape[2]
    r_kv = ckv_norm_g.shape[0]
    rope = w_dkv.shape[1] - r_kv
    vdim = w_uv.shape[2]
    n_exp = router_w.shape[2]
    assert nope == LANES and vdim == LANES and rope <= LANES and d % LANES == 0
    ts = min(512, s)
    tq = min(1024, s)

    mod = _modulation(c, mod_w, mod_b)
    mods = [[m.reshape(b, 1, d) for m in jnp.split(mod[l], 6, axis=-1)] for l in range(2)]
    row = lambda v: v.reshape(1, -1)

    half = rope // 2
    inv = jnp.exp(-(2.0 * math.log(ROPE_THETA) / rope) * jnp.arange(half, dtype=F32))
    ang = positions.astype(F32)[..., None] * inv
    cos, sin = jnp.cos(ang), jnp.sin(ang)
    cc = _pad_lanes(jnp.concatenate([cos, cos], axis=-1))
    ss = _pad_lanes(jnp.concatenate([-sin, sin], axis=-1))

    sh1, sc1, gt1, sh2, sc2, gt2 = mods[0]
    x1, hn2, idxt, route, cnt = _mix0(
        x, row(norm1_g[0]), sh1, sc1, gt1,
        conv_pw1_w[0].astype(BF16), row(conv_pw1_b[0]), conv_dw_w[0], row(conv_dw_b[0]),
        row(conv_ln_g[0]), row(conv_ln_b[0]), conv_pw2_w[0].astype(BF16), row(conv_pw2_b[0]),
        row(norm2_g[0]), sh2, sc2, router_w[0].T, router_b[0].reshape(n_exp, 1), ts)
    cnt = cnt[:, :ts // MOE_TILE].reshape(-1, n_exp)
    x2 = _moe(x1, hn2, idxt, route, cnt, gt2, row(final_g),
              0, exp_w_gu, exp_b_gu, exp_w_dn, exp_b_dn, False)

    sh1, sc1, gt1, sh2, sc2, gt2 = mods[1]
    wdkv_rope = w_dkv[:, r_kv:]
    wdkv_ext = jnp.concatenate(
        [w_dkv[:, :r_kv], _pad_lanes(wdkv_rope), _pad_lanes(_swap_halves(wdkv_rope))], axis=-1)
    wuq = w_uq[0]
    r_q = wuq.shape[0]
    wuq_rope = wuq[:, :, nope:]
    wuq_ext = jnp.concatenate([
        wuq[:, :, :nope].reshape(r_q, n_heads * LANES),
        _pad_lanes(wuq_rope).reshape(r_q, n_heads * LANES),
        _pad_lanes(_swap_halves(wuq_rope)).reshape(r_q, n_heads * LANES)], axis=-1)
    scale = float((nope + rope) ** -0.5 * math.log2(math.e))
    q, k, v = _qkv(
        x2, cc, ss, row(kv_norm_g), wdkv_ext.astype(BF16), row(ckv_norm_g),
        w_uk.reshape(r_kv, n_heads * nope).astype(BF16), w_uv.reshape(r_kv, n_heads * vdim).T.astype(BF16),
        row(norm1_g[1]), sh1, sc1, w_dq[0].astype(BF16), row(cq_norm_g[0]), wuq_ext.astype(BF16),
        n_heads, scale, ts)
    o = _attention(q, k, v, tq)

    x3, hn2, idxt, route, cnt = _mix1(
        x2, o, gt1, w_o[0].astype(BF16), row(norm2_g[1]), sh2, sc2,
        router_w[1].T, router_b[1].reshape(n_exp, 1), ts)
    cnt = cnt[:, :ts // MOE_TILE].reshape(-1, n_exp)
    return _moe(x3, hn2, idxt, route, cnt, gt2, row(final_g),
                1, exp_w_gu, exp_b_gu, exp_w_dn, exp_b_dn, True)
```

```python
import functools
import math

import jax
import jax.numpy as jnp
from jax import lax
from jax.experimental import pallas as pl
from jax.experimental.pallas import tpu as pltpu

CHUNK = 64
TOP_K = 4
ROPE_THETA = 10000.0
SWIGLU_ALPHA = 1.702
SWIGLU_LIMIT = 7.0
EPS = 1e-6

LANES = 128
SUBLANES = 8
VMEM_LIMIT_BYTES = 56 * 1024 * 1024

ROW_ALIGN = SUBLANES
MOE_TILE = 256
EXPERT_ROWS = 512
CONV_HALO = 32
V_ONES_ROWS = 2 * SUBLANES
ATTN_UNROLL = 4
ATTN_MAX_SLACK = 60.0

F32 = jnp.float32
BF16 = jnp.bfloat16
I32 = jnp.int32


def _cparams(sem):
    return pltpu.CompilerParams(dimension_semantics=sem, vmem_limit_bytes=VMEM_LIMIT_BYTES)


def _rms(x, g):
    return x * lax.rsqrt(jnp.mean(x * x, axis=-1, keepdims=True) + EPS) * g


def _round_up(a, m):
    return (a + m - 1) // m * m


def _mod_kernel(c_ref, w_ref, b_ref, o_ref):
    c = c_ref[...]
    ca = c * jax.nn.sigmoid(c)
    o_ref[...] = jnp.dot(ca, w_ref[...], preferred_element_type=F32) + b_ref[...]


def _modulation(c, mod_w, mod_b):
    depth, d, d6 = mod_w.shape
    b = c.shape[0]
    bp = _round_up(b, SUBLANES)
    cp = jnp.zeros((bp, d), F32).at[:b].set(c)
    tn = d6 // 4
    out = pl.pallas_call(
        _mod_kernel,
        out_shape=jax.ShapeDtypeStruct((depth, bp, d6), F32),
        grid=(depth, d6 // tn),
        in_specs=[
            pl.BlockSpec((bp, d), lambda l, j: (0, 0)),
            pl.BlockSpec((None, d, tn), lambda l, j: (l, 0, j)),
            pl.BlockSpec((None, 1, tn), lambda l, j: (l, 0, j)),
        ],
        out_specs=pl.BlockSpec((None, bp, tn), lambda l, j: (l, 0, j)),
        compiler_params=_cparams(("arbitrary", "arbitrary")),
        name="adaln_modulation",
    )(cp, mod_w, mod_b.reshape(depth, 1, d6))
    return out[:, :b]


def _pre_moe(x1, g2, sh2, sc2, rwt, rb, hn_ref, idxt_ref, route_ref, cnt_ref):
    ts = x1.shape[0]
    n_exp = rwt.shape[0]
    hn = _rms(x1, g2) * (1.0 + sc2) + sh2
    hn_ref[...] = hn.astype(BF16)
    logits = lax.dot_general(rwt, hn, (((1,), (1,)), ((), ())),
                             precision=lax.Precision.HIGHEST,
                             preferred_element_type=F32) + rb
    e_iota = lax.broadcasted_iota(I32, (n_exp, ts), 0)
    vals, idxs = [], []
    cur = logits
    for _ in range(TOP_K):
        m = jnp.max(cur, axis=0, keepdims=True)
        i = jnp.min(jnp.where(cur == m, e_iota, n_exp), axis=0, keepdims=True)
        vals.append(m)
        idxs.append(i)
        cur = jnp.where(e_iota == i, -jnp.inf, cur)
    exps = [jnp.exp(v - vals[0]) for v in vals]
    den = exps[0]
    for e in exps[1:]:
        den = den + e
    gates = [e / den for e in exps]
    idxt_ref[...] = jnp.concatenate(idxs, axis=0)
    rows = jnp.concatenate([i.astype(F32) for i in idxs] + gates
                           + [jnp.zeros((LANES - 2 * TOP_K, ts), F32)], axis=0)
    route_ref[...] = rows.T
    onehot = jnp.zeros((n_exp, ts), F32)
    for i in idxs:
        onehot = onehot + (e_iota == i).astype(F32)
    sel = (lax.broadcasted_iota(I32, (SUBLANES, ts), 1) // MOE_TILE
           == lax.broadcasted_iota(I32, (SUBLANES, ts), 0)).astype(BF16)
    cnt = lax.dot_general(sel, onehot.astype(BF16), (((1,), (1,)), ((), ())),
                          preferred_element_type=F32)
    cnt_ref[...] = cnt.astype(I32)


def _pre_moe_specs(b, s, d, ts, n_exp):
    ns = s // ts
    out_shape = [
        jax.ShapeDtypeStruct((b, s, d), F32),
        jax.ShapeDtypeStruct((b, s, d), BF16),
        jax.ShapeDtypeStruct((TOP_K, b * s), I32),
        jax.ShapeDtypeStruct((b * s, LANES), F32),
        jax.ShapeDtypeStruct((b * ns, SUBLANES, n_exp), I32),
    ]
    out_specs = [
        pl.BlockSpec((None, ts, d), lambda i, j: (i, j, 0)),
        pl.BlockSpec((None, ts, d), lambda i, j: (i, j, 0)),
        pl.BlockSpec((TOP_K, ts), lambda i, j: (0, i * ns + j)),
        pl.BlockSpec((ts, LANES), lambda i, j: (i * ns + j, 0)),
        pl.BlockSpec((None, SUBLANES, n_exp), lambda i, j: (i * ns + j, 0, 0)),
    ]
    return out_shape, out_specs


def _vec_spec(d):
    return pl.BlockSpec((1, d), lambda i, j: (0, 0))


def _bvec_spec(d):
    return pl.BlockSpec((None, 1, d), lambda i, j: (i, 0, 0))


def _mix0_kernel(x_ref, n1g_ref, sh1_ref, sc1_ref, gt1_ref,
                 pw1w_ref, pw1b_ref, dww_ref, dwb_ref, lng_ref, lnb_ref,
                 pw2w_ref, pw2b_ref, n2g_ref, sh2_ref, sc2_ref, rwt_ref, rb_ref,
                 x1_ref, hn_ref, idxt_ref, route_ref, cnt_ref, buf_ref, conv_ref, shift_ref):
    ts, d = x_ref.shape
    width = dww_ref.shape[0]
    x = x_ref[...]
    hn = _rms(x, n1g_ref[...]) * (1.0 + sc1_ref[...]) + sh1_ref[...]
    hb = hn.astype(BF16)
    a = jnp.dot(hb, pw1w_ref[:, :d], preferred_element_type=F32) + pw1b_ref[:, :d]
    g = jnp.dot(hb, pw1w_ref[:, d:], preferred_element_type=F32) + pw1b_ref[:, d:]
    glu = a * jax.nn.sigmoid(g)

    @pl.when(pl.program_id(1) == 0)
    def _():
        buf_ref[0:CONV_HALO, :] = jnp.zeros((CONV_HALO, d), F32)

    buf_ref[CONV_HALO:, :] = glu
    base = CONV_HALO - (width - 1)
    rc = 32
    lc = min(512, d)
    sh_rows = shift_ref.shape[1]
    for c0 in range(0, d, lc):
        for r in range(1, SUBLANES):
            shift_ref[r - 1] = buf_ref[r:r + sh_rows, c0:c0 + lc]
        for r0 in range(0, ts, rc):
            acc = jnp.zeros((rc, lc), F32)
            for k in range(width):
                q8, r = divmod(base + k, SUBLANES)
                lo = q8 * SUBLANES + r0
                if r == 0:
                    win = buf_ref[lo:lo + rc, c0:c0 + lc]
                else:
                    win = shift_ref[r - 1, lo:lo + rc, :]
                acc = acc + dww_ref[k:k + 1, c0:c0 + lc] * win
            conv_ref[r0:r0 + rc, c0:c0 + lc] = acc
    buf_ref[0:CONV_HALO, :] = buf_ref[ts:ts + CONV_HALO, :]
    u = conv_ref[...] + dwb_ref[...]
    mu = jnp.mean(u, axis=-1, keepdims=True)
    dlt = u - mu
    var = jnp.mean(dlt * dlt, axis=-1, keepdims=True)
    u = dlt * lax.rsqrt(var + EPS) * lng_ref[...] + lnb_ref[...]
    u = u * jax.nn.sigmoid(u)
    y = jnp.dot(u.astype(BF16), pw2w_ref[...], preferred_element_type=F32) + pw2b_ref[...]
    x1 = x + gt1_ref[...] * y
    x1_ref[...] = x1
    _pre_moe(x1, n2g_ref[...], sh2_ref[...], sc2_ref[...], rwt_ref[...], rb_ref[...],
             hn_ref, idxt_ref, route_ref, cnt_ref)


def _mix0(x, n1g, sh1, sc1, gt1, pw1w, pw1b, dww, dwb, lng, lnb, pw2w, pw2b,
          n2g, sh2, sc2, rwt, rb, ts):
    b, s, d = x.shape
    n_exp = rwt.shape[0]
    width = dww.shape[0]
    assert width - 1 <= CONV_HALO and ts % MOE_TILE == 0 and ts // MOE_TILE <= SUBLANES
    out_shape, out_specs = _pre_moe_specs(b, s, d, ts, n_exp)
    full = lambda shp: pl.BlockSpec(shp, lambda i, j: (0,) * len(shp))
    return pl.pallas_call(
        _mix0_kernel,
        out_shape=out_shape,
        grid=(b, s // ts),
        in_specs=[
            pl.BlockSpec((None, ts, d), lambda i, j: (i, j, 0)),
            _vec_spec(d), _bvec_spec(d), _bvec_spec(d), _bvec_spec(d),
            full((d, 2 * d)), full((1, 2 * d)), full((width, d)), full((1, d)),
            full((1, d)), full((1, d)), full((d, d)), full((1, d)),
            _vec_spec(d), _bvec_spec(d), _bvec_spec(d),
            full((n_exp, d)), full((n_exp, 1)),
        ],
        out_specs=out_specs,
        scratch_shapes=[
            pltpu.VMEM((CONV_HALO + ts, d), F32),
            pltpu.VMEM((ts, d), F32),
            pltpu.VMEM((SUBLANES - 1, CONV_HALO + ts - SUBLANES, min(512, d)), F32),
        ],
        compiler_params=_cparams(("arbitrary", "arbitrary")),
        name="conformer_mixer",
    )(x, n1g, sh1, sc1, gt1, pw1w, pw1b, dww, dwb, lng, lnb, pw2w, pw2b,
      n2g, sh2, sc2, rwt, rb)


def _mix1_kernel(x_ref, o_ref, gt1_ref, wo_ref, n2g_ref, sh2_ref, sc2_ref, rwt_ref, rb_ref,
                 x1_ref, hn_ref, idxt_ref, route_ref, cnt_ref):
    y = jnp.dot(o_ref[...], wo_ref[...], preferred_element_type=F32)
    x1 = x_ref[...] + gt1_ref[...] * y
    x1_ref[...] = x1
    _pre_moe(x1, n2g_ref[...], sh2_ref[...], sc2_ref[...], rwt_ref[...], rb_ref[...],
             hn_ref, idxt_ref, route_ref, cnt_ref)


def _mix1(x, o, gt1, wo, n2g, sh2, sc2, rwt, rb, ts):
    b, s, d = x.shape
    n_exp = rwt.shape[0]
    do = o.shape[-1]
    assert ts % MOE_TILE == 0 and ts // MOE_TILE <= SUBLANES
    out_shape, out_specs = _pre_moe_specs(b, s, d, ts, n_exp)
    full = lambda shp: pl.BlockSpec(shp, lambda i, j: (0,) * len(shp))
    return pl.pallas_call(
        _mix1_kernel,
        out_shape=out_shape,
        grid=(b, s // ts),
        in_specs=[
            pl.BlockSpec((None, ts, d), lambda i, j: (i, j, 0)),
            pl.BlockSpec((None, ts, do), lambda i, j: (i, j, 0)),
            _bvec_spec(d), full((do, d)),
            _vec_spec(d), _bvec_spec(d), _bvec_spec(d),
            full((n_exp, d)), full((n_exp, 1)),
        ],
        out_specs=out_specs,
        compiler_params=_cparams(("arbitrary", "arbitrary")),
        name="attn_out_mixer",
    )(x, o, gt1, wo, n2g, sh2, sc2, rwt, rb)


def _moe_dims(n_tok, n_exp):
    n_tiles = n_tok // MOE_TILE
    loc_rows = _round_up(TOP_K * MOE_TILE + n_exp * (ROW_ALIGN - 1), LANES)
    n_chunks = loc_rows // ROW_ALIGN
    max_rows = TOP_K * n_tok + n_tiles * n_exp * (ROW_ALIGN - 1) + n_exp * (EXPERT_ROWS - ROW_ALIGN)
    max_blocks = -(-max_rows // EXPERT_ROWS)
    return n_tiles, loc_rows, n_chunks, max_blocks


def _routing_tables(cnt, n_chunks, max_blocks):
    n_tiles, n_exp = cnt.shape
    cnt8 = _round_up(cnt, ROW_ALIGN)
    off = jnp.cumsum(cnt8, axis=1) - cnt8
    seg_len = cnt8.sum(axis=0)
    seg_pad = _round_up(seg_len, EXPERT_ROWS)
    seg_end = jnp.cumsum(seg_pad)
    seg_start = seg_end - seg_pad
    run_start = seg_start[None, :] + jnp.cumsum(cnt8, axis=0) - cnt8
    n_used = (cnt8.sum(axis=1) // ROW_ALIGN).astype(I32)
    c_row = jnp.arange(n_chunks, dtype=I32) * ROW_ALIGN
    local_end = off + cnt8
    in_run = ((c_row[None, :, None] >= off[:, None, :])
              & (c_row[None, :, None] < local_end[:, None, :]))
    delta = jnp.sum(jnp.where(in_run, (run_start - off)[:, None, :], 0), axis=-1)
    tab = ((delta + c_row[None, :]) // ROW_ALIGN).astype(I32)
    n_blocks = (seg_end[-1] // EXPERT_ROWS).astype(I32)
    blk_row = jnp.arange(max_blocks, dtype=I32) * EXPERT_ROWS
    blk_row = jnp.minimum(blk_row, seg_end[-1] - EXPERT_ROWS)
    blk_exp = jnp.minimum((blk_row[:, None] >= seg_end[None, :]).sum(axis=-1), n_exp - 1).astype(I32)
    per = EXPERT_ROWS // ROW_ALIGN
    n_tail = (seg_pad - seg_len) // ROW_ALIGN
    z_end = jnp.cumsum(n_tail)
    z_start = z_end - n_tail
    pos = jnp.arange(n_exp * per, dtype=I32)
    in_tail = (pos[:, None] >= z_start[None, :]) & (pos[:, None] < z_end[None, :])
    first = (seg_start + seg_len) // ROW_ALIGN - z_start
    ztab = (jnp.sum(jnp.where(in_tail, first[None, :], 0), axis=-1) + pos).astype(I32)
    ztab = jnp.where(pos < z_end[-1], ztab, 0)
    n_zero = z_end[-1].astype(I32)
    return tab.reshape(-1), n_used, ztab, n_zero.reshape(1), blk_exp, n_blocks.reshape(1)


def _dispatch_kernel(n_exp, tab_ref, nused_ref, ztab_ref, nzero_ref, nblk_ref,
                     x_ref, idxt_ref, xs_ref, loc_ref, zero_ref, sem, zsem):
    i = pl.program_id(0)
    n = pl.num_programs(0)
    slot = i % 2
    tile = x_ref.shape[0]
    loc_rows = loc_ref.shape[1]
    n_chunks = loc_rows // ROW_ALIGN

    def chunk_copy(sl, c, dst):
        return pltpu.make_async_copy(
            loc_ref.at[sl, pl.ds(pl.multiple_of(c * ROW_ALIGN, ROW_ALIGN), ROW_ALIGN)],
            xs_ref.at[pl.ds(pl.multiple_of(dst * ROW_ALIGN, ROW_ALIGN), ROW_ALIGN)],
            sem.at[sl])

    def wait_slot(sl, count):
        @pl.when(count > 0)
        def _():
            rows = count * ROW_ALIGN
            pltpu.make_async_copy(loc_ref.at[sl, pl.ds(0, rows)], xs_ref.at[pl.ds(0, rows)],
                                  sem.at[sl]).wait()

    @pl.when(i >= 2)
    def _():
        wait_slot(slot, nused_ref[i - 2])

    idx = idxt_ref[...]
    e_iota = lax.broadcasted_iota(I32, (n_exp, tile), 0)
    hits = [idx[k:k + 1, :] == e_iota for k in range(TOP_K)]
    onehot = jnp.zeros((n_exp, tile), F32)
    for h in hits:
        onehot = onehot + h.astype(F32)
    upper = (lax.broadcasted_iota(I32, (tile, tile), 0)
             < lax.broadcasted_iota(I32, (tile, tile), 1)).astype(BF16)
    pre = jnp.dot(onehot.astype(BF16), upper, preferred_element_type=F32)
    cnt = jnp.sum(onehot, axis=1, keepdims=True)
    cnt8 = jnp.ceil(cnt / ROW_ALIGN) * ROW_ALIGN
    lower = (lax.broadcasted_iota(I32, (n_exp, n_exp), 1)
             < lax.broadcasted_iota(I32, (n_exp, n_exp), 0)).astype(BF16)
    off = jnp.dot(lower, jnp.broadcast_to(cnt8, (n_exp, LANES)).astype(BF16),
                  preferred_element_type=F32)[:, 0:1]
    base = off + pre
    r_iota = lax.broadcasted_iota(I32, (loc_rows, tile), 0)
    perm = jnp.zeros((loc_rows, tile), F32)
    for h in hits:
        dest = jnp.sum(jnp.where(h, base, 0.0), axis=0, keepdims=True).astype(I32)
        perm = perm + (r_iota == dest).astype(F32)
    loc_ref[slot] = jnp.dot(perm.astype(BF16), x_ref[...], preferred_element_type=F32)

    def issue(c, carry):
        chunk_copy(slot, c, tab_ref[i * n_chunks + c]).start()
        return carry
    lax.fori_loop(0, nused_ref[i], issue, 0)

    @pl.when(i == n - 1)
    def _():
        zero_ref[...] = jnp.zeros(zero_ref.shape, F32)
        max_blocks = xs_ref.shape[0] // EXPERT_ROWS

        def zcopy(dst):
            return pltpu.make_async_copy(
                zero_ref.at[pl.ds(0, ROW_ALIGN)],
                xs_ref.at[pl.ds(pl.multiple_of(dst * ROW_ALIGN, ROW_ALIGN), ROW_ALIGN)],
                zsem.at[0])

        def bcopy(blk):
            return pltpu.make_async_copy(
                zero_ref,
                xs_ref.at[pl.ds(pl.multiple_of(blk * EXPERT_ROWS, EXPERT_ROWS), EXPERT_ROWS)],
                zsem.at[1])

        def zissue(c, carry):
            zcopy(ztab_ref[c]).start()
            return carry
        lax.fori_loop(0, nzero_ref[0], zissue, 0)

        def bissue(blk, carry):
            bcopy(blk).start()
            return carry
        lax.fori_loop(nblk_ref[0], max_blocks, bissue, 0)

        @pl.when(i >= 1)
        def _():
            wait_slot(1 - slot, nused_ref[i - 1])
        wait_slot(slot, nused_ref[i])

        def wait_rows(rows, s):
            @pl.when(rows > 0)
            def _():
                pltpu.make_async_copy(xs_ref.at[pl.ds(0, rows)], xs_ref.at[pl.ds(0, rows)],
                                      zsem.at[s]).wait()
        wait_rows(nzero_ref[0] * ROW_ALIGN, 0)
        wait_rows((max_blocks - nblk_ref[0]) * EXPERT_ROWS, 1)


def _dispatch(hn2, idxt, tab, n_used, ztab, n_zero, n_blocks, n_exp, loc_rows, max_blocks):
    n_tok, d = hn2.shape
    n_tiles = n_tok // MOE_TILE
    return pl.pallas_call(
        functools.partial(_dispatch_kernel, n_exp),
        out_shape=jax.ShapeDtypeStruct((max_blocks * EXPERT_ROWS, d), F32),
        grid_spec=pltpu.PrefetchScalarGridSpec(
            num_scalar_prefetch=5,
            grid=(n_tiles,),
            in_specs=[
                pl.BlockSpec((MOE_TILE, d), lambda i, *_: (i, 0)),
                pl.BlockSpec((TOP_K, MOE_TILE), lambda i, *_: (0, i)),
            ],
            out_specs=pl.BlockSpec(memory_space=pl.ANY),
            scratch_shapes=[
                pltpu.VMEM((2, loc_rows, d), F32),
                pltpu.VMEM((EXPERT_ROWS, d), F32),
                pltpu.SemaphoreType.DMA((2,)),
                pltpu.SemaphoreType.DMA((2,)),
            ],
        ),
        compiler_params=_cparams(("arbitrary",)),
        name="moe_dispatch",
    )(tab, n_used, ztab, n_zero, n_blocks, hn2, idxt)


def _expert_kernel(be_ref, nb_ref, x_ref, wgu_ref, bgu_ref, wdn_ref, bdn_ref, y_ref,
                   wgu_bf, wdn_bf):
    de = wdn_ref.shape[0]
    b = pl.program_id(0)

    @pl.when((b == 0) | (be_ref[b] != be_ref[jnp.maximum(b - 1, 0)]))
    def _():
        cw = 512
        for c0 in range(0, wgu_ref.shape[1], cw):
            wgu_bf[:, c0:c0 + cw] = wgu_ref[:, c0:c0 + cw].astype(BF16)
        for c0 in range(0, wdn_ref.shape[1], cw):
            wdn_bf[:, c0:c0 + cw] = wdn_ref[:, c0:c0 + cw].astype(BF16)

    @pl.when(b < nb_ref[0])
    def _():
        xb = x_ref[...].astype(BF16)
        g = jnp.dot(xb, wgu_bf[:, :de], preferred_element_type=F32) + bgu_ref[:, :de]
        u = jnp.dot(xb, wgu_bf[:, de:], preferred_element_type=F32) + bgu_ref[:, de:]
        g = jnp.minimum(g, SWIGLU_LIMIT)
        u = jnp.clip(u, -SWIGLU_LIMIT, SWIGLU_LIMIT)
        a = (u + 1.0) * g * jax.nn.sigmoid(SWIGLU_ALPHA * g)
        y_ref[...] = jnp.dot(a.astype(BF16), wdn_bf[...], preferred_element_type=F32) + bdn_ref[...]

    @pl.when(pl.program_id(0) >= nb_ref[0])
    def _():
        y_ref[...] = jnp.zeros(y_ref.shape, y_ref.dtype)


def _experts(xs, blk_exp, n_blocks, layer, wgu, bgu, wdn, bdn):
    rows, d = xs.shape
    depth, n_exp, _, de2 = wgu.shape
    de = de2 // 2
    max_blocks = rows // EXPERT_ROWS
    row_map = lambda b, be, nb: (jnp.minimum(b, nb[0] - 1), 0)
    exp_map = lambda b, be, nb: (layer, be[b], 0, 0)
    return pl.pallas_call(
        _expert_kernel,
        out_shape=jax.ShapeDtypeStruct((rows, d), F32),
        grid_spec=pltpu.PrefetchScalarGridSpec(
            num_scalar_prefetch=2,
            grid=(max_blocks,),
            in_specs=[
                pl.BlockSpec((EXPERT_ROWS, d), row_map),
                pl.BlockSpec((None, None, d, de2), exp_map),
                pl.BlockSpec((None, None, 1, de2), exp_map),
                pl.BlockSpec((None, None, de, d), exp_map),
                pl.BlockSpec((None, None, 1, d), exp_map),
            ],
            out_specs=pl.BlockSpec((EXPERT_ROWS, d), lambda b, be, nb: (b, 0)),
            scratch_shapes=[pltpu.VMEM((d, de2), BF16), pltpu.VMEM((de, d), BF16)],
        ),
        compiler_params=_cparams(("arbitrary",)),
        name="moe_experts",
    )(blk_exp, n_blocks, xs, wgu, bgu.reshape(depth, n_exp, 1, de2), wdn,
      bdn.reshape(depth, n_exp, 1, d))


def _combine_kernel(final_norm, n_exp, tab_ref, nused_ref,
                    route_ref, x1_ref, g2_ref, fg_ref, ys_ref, out_ref, loc_ref, sem):
    i = pl.program_id(0)
    n = pl.num_programs(0)
    slot = i % 2
    tile = x1_ref.shape[0]
    loc_rows = loc_ref.shape[1]
    n_chunks = loc_rows // ROW_ALIGN

    def chunk_copy(sl, c, src):
        return pltpu.make_async_copy(
            ys_ref.at[pl.ds(pl.multiple_of(src * ROW_ALIGN, ROW_ALIGN), ROW_ALIGN)],
            loc_ref.at[sl, pl.ds(pl.multiple_of(c * ROW_ALIGN, ROW_ALIGN), ROW_ALIGN)],
            sem.at[sl])

    def issue_tile(t, sl):
        def body(c, carry):
            chunk_copy(sl, c, tab_ref[t * n_chunks + c]).start()
            return carry
        lax.fori_loop(0, nused_ref[t], body, 0)

    @pl.when(i == 0)
    def _():
        issue_tile(0, 0)

    @pl.when(i + 1 < n)
    def _():
        issue_tile(i + 1, 1 - slot)

    route = route_ref[...]
    e_iota = lax.broadcasted_iota(I32, (tile, n_exp), 1)
    hits = [route[:, k:k + 1].astype(I32) == e_iota for k in range(TOP_K)]
    gates = [route[:, TOP_K + k:TOP_K + k + 1] for k in range(TOP_K)]
    onehot = jnp.zeros((tile, n_exp), F32)
    for h in hits:
        onehot = onehot + h.astype(F32)
    lower = (lax.broadcasted_iota(I32, (tile, tile), 1)
             < lax.broadcasted_iota(I32, (tile, tile), 0)).astype(BF16)
    pre = jnp.dot(lower, onehot.astype(BF16), preferred_element_type=F32)
    cnt = jnp.sum(onehot, axis=0, keepdims=True)
    cnt8 = jnp.ceil(cnt / ROW_ALIGN) * ROW_ALIGN
    upper = (lax.broadcasted_iota(I32, (n_exp, n_exp), 0)
             < lax.broadcasted_iota(I32, (n_exp, n_exp), 1)).astype(BF16)
    off = jnp.dot(jnp.broadcast_to(cnt8, (SUBLANES, n_exp)).astype(BF16), upper,
                  preferred_element_type=F32)[0:1, :]
    base = off + pre
    r_iota = lax.broadcasted_iota(I32, (tile, loc_rows), 1)
    comb = jnp.zeros((tile, loc_rows), F32)
    for h, gk in zip(hits, gates):
        dest = jnp.sum(jnp.where(h, base, 0.0), axis=1, keepdims=True).astype(I32)
        comb = comb + jnp.where(r_iota == dest, gk, 0.0)

    @pl.when(nused_ref[i] > 0)
    def _():
        rows = nused_ref[i] * ROW_ALIGN
        pltpu.make_async_copy(ys_ref.at[pl.ds(0, rows)], loc_ref.at[slot, pl.ds(0, rows)],
                              sem.at[slot]).wait()

    used_rows = nused_ref[i] * ROW_ALIGN
    row_ok = lax.broadcasted_iota(I32, (loc_rows, 1), 0) < used_rows
    yl = jnp.where(row_ok, loc_ref[slot], 0.0).astype(BF16)
    moe = jnp.dot(comb.astype(BF16), yl, preferred_element_type=F32)
    out = x1_ref[...] + g2_ref[...] * moe
    if final_norm:
        out = _rms(out, fg_ref[...])
    out_ref[...] = out


def _combine(ys, route, x1, gate2, final_g, tab, n_used, n_exp, loc_rows, final_norm):
    b, s, d = x1.shape
    n_tok = b * s
    n_tiles = n_tok // MOE_TILE
    tiles_per_seq = s // MOE_TILE
    out = pl.pallas_call(
        functools.partial(_combine_kernel, final_norm, n_exp),
        out_shape=jax.ShapeDtypeStruct((n_tok, d), F32),
        grid_spec=pltpu.PrefetchScalarGridSpec(
            num_scalar_prefetch=2,
            grid=(n_tiles,),
            in_specs=[
                pl.BlockSpec((MOE_TILE, LANES), lambda i, *_: (i, 0)),
                pl.BlockSpec((MOE_TILE, d), lambda i, *_: (i, 0)),
                pl.BlockSpec((None, 1, d), lambda i, *_: (i // tiles_per_seq, 0, 0)),
                pl.BlockSpec((1, d), lambda i, *_: (0, 0)),
                pl.BlockSpec(memory_space=pl.ANY),
            ],
            out_specs=pl.BlockSpec((MOE_TILE, d), lambda i, *_: (i, 0)),
            scratch_shapes=[
                pltpu.VMEM((2, loc_rows, d), F32),
                pltpu.SemaphoreType.DMA((2,)),
            ],
        ),
        compiler_params=_cparams(("arbitrary",)),
        name="moe_combine",
    )(tab, n_used, route, x1.reshape(n_tok, d), gate2, final_g, ys)
    return out.reshape(b, s, d)


def _moe(x1, hn2, idxt, route, cnt, gate2, final_g, layer, wgu, bgu, wdn, bdn, final_norm):
    b, s, d = x1.shape
    n_tok = b * s
    n_exp = wgu.shape[1]
    n_tiles, loc_rows, n_chunks, max_blocks = _moe_dims(n_tok, n_exp)
    tab, n_used, ztab, n_zero, blk_exp, n_blocks = _routing_tables(cnt, n_chunks, max_blocks)
    xs = _dispatch(hn2.reshape(n_tok, d), idxt, tab, n_used, ztab, n_zero, n_blocks,
                   n_exp, loc_rows, max_blocks)
    ys = _experts(xs, blk_exp, n_blocks, layer, wgu, bgu, wdn, bdn)
    return _combine(ys, route, x1, gate2, final_g, tab, n_used, n_exp, loc_rows, final_norm)


def _qkv_kernel(n_heads, scale, x_ref, cc_ref, ss_ref, kvg_ref, wdkv_ref, ckvg_ref, wuk_ref, wuv_ref,
                n1g_ref, sh1_ref, sc1_ref, wdq_ref, cqg_ref, wuq_ref,
                q_ref, k_ref, v_ref):
    x = x_ref[...]
    cc = cc_ref[...]
    ss = ss_ref[...]
    r_kv = ckvg_ref.shape[1]
    hk = _rms(x, kvg_ref[...]).astype(BF16)
    lat = jnp.dot(hk, wdkv_ref[...], preferred_element_type=F32)
    ckv = _rms(lat[:, :r_kv], ckvg_ref[...]).astype(BF16)
    krot = lat[:, r_kv:r_kv + LANES] * cc + lat[:, r_kv + LANES:r_kv + 2 * LANES] * ss
    kn = jnp.dot(ckv, wuk_ref[...], preferred_element_type=F32)
    vt = lax.dot_general(wuv_ref[...], ckv, (((1,), (1,)), ((), ())), preferred_element_type=F32)
    hq = (_rms(x, n1g_ref[...]) * (1.0 + sc1_ref[...]) + sh1_ref[...]).astype(BF16)
    cq = _rms(jnp.dot(hq, wdq_ref[...], preferred_element_type=F32), cqg_ref[...]).astype(BF16)
    qq = jnp.dot(cq, wuq_ref[...], preferred_element_type=F32) * scale
    hd = n_heads * LANES
    ts = x.shape[0]
    ones_rows = (lax.broadcasted_iota(I32, (V_ONES_ROWS, ts), 0) == 0).astype(BF16)
    for h in range(n_heads):
        sl = slice(h * LANES, (h + 1) * LANES)
        k_ref[h, :, 0:LANES] = kn[:, sl].astype(BF16)
        k_ref[h, :, LANES:2 * LANES] = krot.astype(BF16)
        v_ref[h, 0:LANES, :] = vt[h * LANES:(h + 1) * LANES, :].astype(BF16)
        v_ref[h, LANES:LANES + V_ONES_ROWS, :] = ones_rows
        q_ref[h, :, 0:LANES] = qq[:, sl].astype(BF16)
        qrot = qq[:, hd + h * LANES:hd + (h + 1) * LANES] * cc \
            + qq[:, 2 * hd + h * LANES:2 * hd + (h + 1) * LANES] * ss
        q_ref[h, :, LANES:2 * LANES] = qrot.astype(BF16)


def _qkv(x, cc, ss, kvg, wdkv, ckvg, wuk, wuv, n1g, sh1, sc1, wdq, cqg, wuq, n_heads, scale, ts):
    b, s, d = x.shape
    full = lambda a: pl.BlockSpec(a.shape, lambda i, j: (0,) * a.ndim)
    hspec = lambda w: pl.BlockSpec((None, n_heads, ts, w), lambda i, j: (i, 0, j, 0))
    return pl.pallas_call(
        functools.partial(_qkv_kernel, n_heads, scale),
        out_shape=[
            jax.ShapeDtypeStruct((b, n_heads, s, 2 * LANES), BF16),
            jax.ShapeDtypeStruct((b, n_heads, s, 2 * LANES), BF16),
            jax.ShapeDtypeStruct((b, n_heads, s // ts, LANES + V_ONES_ROWS, ts), BF16),
        ],
        grid=(b, s // ts),
        in_specs=[
            pl.BlockSpec((None, ts, d), lambda i, j: (i, j, 0)),
            pl.BlockSpec((None, ts, LANES), lambda i, j: (i, j, 0)),
            pl.BlockSpec((None, ts, LANES), lambda i, j: (i, j, 0)),
            full(kvg), full(wdkv), full(ckvg), full(wuk), full(wuv),
            full(n1g), _bvec_spec(d), _bvec_spec(d), full(wdq), full(cqg), full(wuq),
        ],
        out_specs=[hspec(2 * LANES), hspec(2 * LANES),
                   pl.BlockSpec((None, n_heads, None, LANES + V_ONES_ROWS, ts),
                                lambda i, j: (i, 0, j, 0, 0))],
        compiler_params=_cparams(("arbitrary", "arbitrary")),
        name="mla_qkv",
    )(x, cc, ss, kvg, wdkv, ckvg, wuk, wuv, n1g, sh1, sc1, wdq, cqg, wuq)


def _attn_kernel(q_ref, k_ref, vt_ref, o_ref, m_ref, acc_ref):
    tq = q_ref.shape[0]
    tk = vt_ref.shape[2]
    ratio = tq // tk
    dv = o_ref.shape[1]
    qi = pl.program_id(2)
    q = q_ref[...]
    m_ref[...] = jnp.full(m_ref.shape, -jnp.inf, F32)
    acc_ref[...] = jnp.zeros(acc_ref.shape, F32)

    def scores(ki, diag):
        start = pl.multiple_of(ki * tk, tk)
        k = k_ref[pl.ds(start, tk), :]
        st = lax.dot_general(k, q, (((1,), (1,)), ((), ())), preferred_element_type=F32)
        if diag is not None:
            kc = (lax.broadcasted_iota(I32, (tk, tq), 0) + diag * tk) // CHUNK
            qc = lax.broadcasted_iota(I32, (tk, tq), 1) // CHUNK
            st = jnp.where(kc <= qc, st, -jnp.inf)
        return st

    def update(ki, st, m_old, acc_old):
        m_new = jnp.maximum(m_old, jnp.max(st, axis=0, keepdims=True))
        p = jnp.exp2(st - m_new).astype(BF16)
        alpha = jnp.exp2(m_old - m_new)
        acc_new = alpha * acc_old + jnp.dot(vt_ref[ki], p, preferred_element_type=F32)
        return m_new, acc_new

    def step(tiles):
        sts = [scores(ki, diag) for ki, diag in tiles]
        state = (m_ref[...], acc_ref[...])
        for (ki, _), st in zip(tiles, sts):
            state = update(ki, st, *state)
        m_ref[...], acc_ref[...] = state

    n_below = qi * ratio
    step([(n_below + t, t) for t in range(ratio)])

    def fast_step(first, count):
        m = m_ref[...]
        pv = None
        top = None
        for t in range(count):
            st = scores(first + t, None)
            tmax = jnp.max(st, axis=0, keepdims=True)
            top = tmax if top is None else jnp.maximum(top, tmax)
            p = jnp.exp2(st - m).astype(BF16)
            part = jnp.dot(vt_ref[first + t], p, preferred_element_type=F32)
            pv = part if pv is None else pv + part
        safe = jnp.max(top - m) <= ATTN_MAX_SLACK

        @pl.when(safe)
        def _():
            acc_ref[...] += pv

        @pl.when(jnp.logical_not(safe))
        def _():
            def redo(t, carry):
                step([(first + t, None)])
                return carry
            lax.fori_loop(0, count, redo, 0)

    def body(j, carry):
        fast_step(ATTN_UNROLL * j, ATTN_UNROLL)
        return carry
    trips = n_below // ATTN_UNROLL
    lax.fori_loop(0, trips, body, 0)

    for rem in range(math.gcd(ratio, ATTN_UNROLL), ATTN_UNROLL, math.gcd(ratio, ATTN_UNROLL)):
        @pl.when(n_below % ATTN_UNROLL == rem)
        def _(rem=rem):
            fast_step(trips * ATTN_UNROLL, rem)

    o_ref[...] = (acc_ref[0:dv, :] / acc_ref[dv:dv + 1, :]).T.astype(o_ref.dtype)


def _attention(q, k, vt, tq):
    b, h, s, dk = q.shape
    nk, dv_ext, tk = vt.shape[2:]
    dv = dv_ext - V_ONES_ROWS
    assert tk % CHUNK == 0 and s % tq == 0 and tq % tk == 0
    return pl.pallas_call(
        _attn_kernel,
        out_shape=jax.ShapeDtypeStruct((b, s, h * dv), BF16),
        grid=(b, h, s // tq),
        in_specs=[
            pl.BlockSpec((None, None, tq, dk), lambda i, j, t: (i, j, t, 0)),
            pl.BlockSpec((None, None, s, dk), lambda i, j, t: (i, j, 0, 0)),
            pl.BlockSpec((None, None, nk, dv_ext, tk), lambda i, j, t: (i, j, 0, 0, 0)),
        ],
        out_specs=pl.BlockSpec((None, tq, dv), lambda i, j, t: (i, t, j)),
        scratch_shapes=[pltpu.VMEM((1, tq), F32), pltpu.VMEM((dv_ext, tq), F32)],
        compiler_params=_cparams(("arbitrary", "arbitrary", "arbitrary")),
        name="mla_attention",
    )(q, k, vt)


def _swap_halves(w):
    half = w.shape[-1] // 2
    return jnp.concatenate([w[..., half:], w[..., :half]], axis=-1)


def _pad_lanes(w):
    pad = LANES - w.shape[-1]
    return jnp.concatenate([w, jnp.zeros(w.shape[:-1] + (pad,), w.dtype)], axis=-1)


def kernel(x, c, positions, mod_w, mod_b, norm1_g, norm2_g, conv_pw1_w, conv_pw1_b, conv_dw_w, conv_dw_b, conv_ln_g, conv_ln_b, conv_pw2_w, conv_pw2_b, kv_norm_g, w_dkv, ckv_norm_g, w_uk, w_uv, w_dq, cq_norm_g, w_uq, w_o, router_w, router_b, exp_w_gu, exp_b_gu, exp_w_dn, exp_b_dn, final_g):
    b, s, d = x.shape
    n_heads, nope = w_uk.shape[1], w_uk.shape[2]
    r_kv = ckv_norm_g.shape[0]
    rope = w_dkv.shape[1] - r_kv
    vdim = w_uv.shape[2]
    n_exp = router_w.shape[2]
    assert nope == LANES and vdim == LANES and rope <= LANES and d % LANES == 0
    ts = min(512, s)
    tq = min(1024, s)

    mod = _modulation(c, mod_w, mod_b)
    mods = [[m.reshape(b, 1, d) for m in jnp.split(mod[l], 6, axis=-1)] for l in range(2)]
    row = lambda v: v.reshape(1, -1)

    half = rope // 2
    inv = jnp.exp(-(2.0 * math.log(ROPE_THETA) / rope) * jnp.arange(half, dtype=F32))
    ang = positions.astype(F32)[..., None] * inv
    cos, sin = jnp.cos(ang), jnp.sin(ang)
    cc = _pad_lanes(jnp.concatenate([cos, cos], axis=-1))
    ss = _pad_lanes(jnp.concatenate([-sin, sin], axis=-1))

    sh1, sc1, gt1, sh2, sc2, gt2 = mods[0]
    x1, hn2, idxt, route, cnt = _mix0(
        x, row(norm1_g[0]), sh1, sc1, gt1,
        conv_pw1_w[0].astype(BF16), row(conv_pw1_b[0]), conv_dw_w[0], row(conv_dw_b[0]),
        row(conv_ln_g[0]), row(conv_ln_b[0]), conv_pw2_w[0].astype(BF16), row(conv_pw2_b[0]),
        row(norm2_g[0]), sh2, sc2, router_w[0].T, router_b[0].reshape(n_exp, 1), ts)
    cnt = cnt[:, :ts // MOE_TILE].reshape(-1, n_exp)
    x2 = _moe(x1, hn2, idxt, route, cnt, gt2, row(final_g),
              0, exp_w_gu, exp_b_gu, exp_w_dn, exp_b_dn, False)

    sh1, sc1, gt1, sh2, sc2, gt2 = mods[1]
    wdkv_rope = w_dkv[:, r_kv:]
    wdkv_ext = jnp.concatenate(
        [w_dkv[:, :r_kv], _pad_lanes(wdkv_rope), _pad_lanes(_swap_halves(wdkv_rope))], axis=-1)
    wuq = w_uq[0]
    r_q = wuq.shape[0]
    wuq_rope = wuq[:, :, nope:]
    wuq_ext = jnp.concatenate([
        wuq[:, :, :nope].reshape(r_q, n_heads * LANES),
        _pad_lanes(wuq_rope).reshape(r_q, n_heads * LANES),
        _pad_lanes(_swap_halves(wuq_rope)).reshape(r_q, n_heads * LANES)], axis=-1)
    scale = float((nope + rope) ** -0.5 * math.log2(math.e))
    q, k, v = _qkv(
        x2, cc, ss, row(kv_norm_g), wdkv_ext.astype(BF16), row(ckv_norm_g),
        w_uk.reshape(r_kv, n_heads * nope).astype(BF16), w_uv.reshape(r_kv, n_heads * vdim).T.astype(BF16),
        row(norm1_g[1]), sh1, sc1, w_dq[0].astype(BF16), row(cq_norm_g[0]), wuq_ext.astype(BF16),
        n_heads, scale, ts)
    o = _attention(q, k, v, tq)

    x3, hn2, idxt, route, cnt = _mix1(
        x2, o, gt1, w_o[0].astype(BF16), row(norm2_g[1]), sh2, sc2,
        router_w[1].T, router_b[1].reshape(n_exp, 1), ts)
    cnt = cnt[:, :ts // MOE_TILE].reshape(-1, n_exp)
    return _moe(x3, hn2, idxt, route, cnt, gt2, row(final_g),
                1, exp_w_gu, exp_b_gu, exp_w_dn, exp_b_dn, True)
```

```python
import functools
import math

import jax
import jax.numpy as jnp
from jax import lax
from jax.experimental import pallas as pl
from jax.experimental.pallas import tpu as pltpu

CHUNK = 64
TOP_K = 4
ROPE_THETA = 10000.0
SWIGLU_ALPHA = 1.702
SWIGLU_LIMIT = 7.0
EPS = 1e-6

LANES = 128
SUBLANES = 8
VMEM_LIMIT_BYTES = 56 * 1024 * 1024

ROW_ALIGN = SUBLANES
MOE_TILE = 256
EXPERT_ROWS = 512
CONV_HALO = 32
V_ONES_ROWS = 2 * SUBLANES
ATTN_UNROLL = 4
ATTN_MAX_SLACK = 60.0

F32 = jnp.float32
BF16 = jnp.bfloat16
I32 = jnp.int32


def _cparams(sem):
    return pltpu.CompilerParams(dimension_semantics=sem, vmem_limit_bytes=VMEM_LIMIT_BYTES)


def _rms(x, g):
    return x * lax.rsqrt(jnp.mean(x * x, axis=-1, keepdims=True) + EPS) * g


def _round_up(a, m):
    return (a + m - 1) // m * m


def _mod_kernel(c_ref, w_ref, b_ref, o_ref):
    c = c_ref[...]
    ca = c * jax.nn.sigmoid(c)
    o_ref[...] = jnp.dot(ca, w_ref[...], preferred_element_type=F32) + b_ref[...]


def _modulation(c, mod_w, mod_b):
    depth, d, d6 = mod_w.shape
    b = c.shape[0]
    bp = _round_up(b, SUBLANES)
    cp = jnp.zeros((bp, d), F32).at[:b].set(c)
    tn = d6 // 4
    out = pl.pallas_call(
        _mod_kernel,
        out_shape=jax.ShapeDtypeStruct((depth, bp, d6), F32),
        grid=(depth, d6 // tn),
        in_specs=[
            pl.BlockSpec((bp, d), lambda l, j: (0, 0)),
            pl.BlockSpec((None, d, tn), lambda l, j: (l, 0, j)),
            pl.BlockSpec((None, 1, tn), lambda l, j: (l, 0, j)),
        ],
        out_specs=pl.BlockSpec((None, bp, tn), lambda l, j: (l, 0, j)),
        compiler_params=_cparams(("arbitrary", "arbitrary")),
        name="adaln_modulation",
    )(cp, mod_w, mod_b.reshape(depth, 1, d6))
    return out[:, :b]


def _pre_moe(x1, g2, sh2, sc2, rwt, rb, hn_ref, idxt_ref, route_ref, cnt_ref):
    ts = x1.shape[0]
    n_exp = rwt.shape[0]
    hn = _rms(x1, g2) * (1.0 + sc2) + sh2
    hn_ref[...] = hn.astype(BF16)
    logits = lax.dot_general(rwt, hn, (((1,), (1,)), ((), ())),
                             precision=lax.Precision.HIGHEST,
                             preferred_element_type=F32) + rb
    e_iota = lax.broadcasted_iota(I32, (n_exp, ts), 0)
    vals, idxs = [], []
    cur = logits
    for _ in range(TOP_K):
        m = jnp.max(cur, axis=0, keepdims=True)
        i = jnp.min(jnp.where(cur == m, e_iota, n_exp), axis=0, keepdims=True)
        vals.append(m)
        idxs.append(i)
        cur = jnp.where(e_iota == i, -jnp.inf, cur)
    exps = [jnp.exp(v - vals[0]) for v in vals]
    den = exps[0]
    for e in exps[1:]:
        den = den + e
    gates = [e / den for e in exps]
    idxt_ref[...] = jnp.concatenate(idxs, axis=0)
    rows = jnp.concatenate([i.astype(F32) for i in idxs] + gates
                           + [jnp.zeros((LANES - 2 * TOP_K, ts), F32)], axis=0)
    route_ref[...] = rows.T
    onehot = jnp.zeros((n_exp, ts), F32)
    for i in idxs:
        onehot = onehot + (e_iota == i).astype(F32)
    sel = (lax.broadcasted_iota(I32, (SUBLANES, ts), 1) // MOE_TILE
           == lax.broadcasted_iota(I32, (SUBLANES, ts), 0)).astype(BF16)
    cnt = lax.dot_general(sel, onehot.astype(BF16), (((1,), (1,)), ((), ())),
                          preferred_element_type=F32)
    cnt_ref[...] = cnt.astype(I32)


def _pre_moe_specs(b, s, d, ts, n_exp):
    ns = s // ts
    out_shape = [
        jax.ShapeDtypeStruct((b, s, d), F32),
        jax.ShapeDtypeStruct((b, s, d), BF16),
        jax.ShapeDtypeStruct((TOP_K, b * s), I32),
        jax.ShapeDtypeStruct((b * s, LANES), F32),
        jax.ShapeDtypeStruct((b * ns, SUBLANES, n_exp), I32),
    ]
    out_specs = [
        pl.BlockSpec((None, ts, d), lambda i, j: (i, j, 0)),
        pl.BlockSpec((None, ts, d), lambda i, j: (i, j, 0)),
        pl.BlockSpec((TOP_K, ts), lambda i, j: (0, i * ns + j)),
        pl.BlockSpec((ts, LANES), lambda i, j: (i * ns + j, 0)),
        pl.BlockSpec((None, SUBLANES, n_exp), lambda i, j: (i * ns + j, 0, 0)),
    ]
    return out_shape, out_specs


def _vec_spec(d):
    return pl.BlockSpec((1, d), lambda i, j: (0, 0))


def _bvec_spec(d):
    return pl.BlockSpec((None, 1, d), lambda i, j: (i, 0, 0))


def _mix0_kernel(x_ref, n1g_ref, sh1_ref, sc1_ref, gt1_ref,
                 pw1w_ref, pw1b_ref, dww_ref, dwb_ref, lng_ref, lnb_ref,
                 pw2w_ref, pw2b_ref, n2g_ref, sh2_ref, sc2_ref, rwt_ref, rb_ref,
                 x1_ref, hn_ref, idxt_ref, route_ref, cnt_ref, buf_ref, conv_ref, shift_ref):
    ts, d = x_ref.shape
    width = dww_ref.shape[0]
    x = x_ref[...]
    hn = _rms(x, n1g_ref[...]) * (1.0 + sc1_ref[...]) + sh1_ref[...]
    hb = hn.astype(BF16)
    a = jnp.dot(hb, pw1w_ref[:, :d], preferred_element_type=F32) + pw1b_ref[:, :d]
    g = jnp.dot(hb, pw1w_ref[:, d:], preferred_element_type=F32) + pw1b_ref[:, d:]
    glu = a * jax.nn.sigmoid(g)

    @pl.when(pl.program_id(1) == 0)
    def _():
        buf_ref[0:CONV_HALO, :] = jnp.zeros((CONV_HALO, d), F32)

    buf_ref[CONV_HALO:, :] = glu
    base = CONV_HALO - (width - 1)
    rc = 32
    lc = min(512, d)
    sh_rows = shift_ref.shape[1]
    for c0 in range(0, d, lc):
        for r in range(1, SUBLANES):
            shift_ref[r - 1] = buf_ref[r:r + sh_rows, c0:c0 + lc]
        for r0 in range(0, ts, rc):
            acc = jnp.zeros((rc, lc), F32)
            for k in range(width):
                q8, r = divmod(base + k, SUBLANES)
                lo = q8 * SUBLANES + r0
                if r == 0:
                    win = buf_ref[lo:lo + rc, c0:c0 + lc]
                else:
                    win = shift_ref[r - 1, lo:lo + rc, :]
                acc = acc + dww_ref[k:k + 1, c0:c0 + lc] * win
            conv_ref[r0:r0 + rc, c0:c0 + lc] = acc
    buf_ref[0:CONV_HALO, :] = buf_ref[ts:ts + CONV_HALO, :]
    u = conv_ref[...] + dwb_ref[...]
    mu = jnp.mean(u, axis=-1, keepdims=True)
    dlt = u - mu
    var = jnp.mean(dlt * dlt, axis=-1, keepdims=True)
    u = dlt * lax.rsqrt(var + EPS) * lng_ref[...] + lnb_ref[...]
    u = u * jax.nn.sigmoid(u)
    y = jnp.dot(u.astype(BF16), pw2w_ref[...], preferred_element_type=F32) + pw2b_ref[...]
    x1 = x + gt1_ref[...] * y
    x1_ref[...] = x1
    _pre_moe(x1, n2g_ref[...], sh2_ref[...], sc2_ref[...], rwt_ref[...], rb_ref[...],
             hn_ref, idxt_ref, route_ref, cnt_ref)


def _mix0(x, n1g, sh1, sc1, gt1, pw1w, pw1b, dww, dwb, lng, lnb, pw2w, pw2b,
          n2g, sh2, sc2, rwt, rb, ts):
    b, s, d = x.shape
    n_exp = rwt.shape[0]
    width = dww.shape[0]
    assert width - 1 <= CONV_HALO and ts % MOE_TILE == 0 and ts // MOE_TILE <= SUBLANES
    out_shape, out_specs = _pre_moe_specs(b, s, d, ts, n_exp)
    full = lambda shp: pl.BlockSpec(shp, lambda i, j: (0,) * len(shp))
    return pl.pallas_call(
        _mix0_kernel,
        out_shape=out_shape,
        grid=(b, s // ts),
        in_specs=[
            pl.BlockSpec((None, ts, d), lambda i, j: (i, j, 0)),
            _vec_spec(d), _bvec_spec(d), _bvec_spec(d), _bvec_spec(d),
            full((d, 2 * d)), full((1, 2 * d)), full((width, d)), full((1, d)),
            full((1, d)), full((1, d)), full((d, d)), full((1, d)),
            _vec_spec(d), _bvec_spec(d), _bvec_spec(d),
            full((n_exp, d)), full((n_exp, 1)),
        ],
        out_specs=out_specs,
        scratch_shapes=[
            pltpu.VMEM((CONV_HALO + ts, d), F32),
            pltpu.VMEM((ts, d), F32),
            pltpu.VMEM((SUBLANES - 1, CONV_HALO + ts - SUBLANES, min(512, d)), F32),
        ],
        compiler_params=_cparams(("arbitrary", "arbitrary")),
        name="conformer_mixer",
    )(x, n1g, sh1, sc1, gt1, pw1w, pw1b, dww, dwb, lng, lnb, pw2w, pw2b,
      n2g, sh2, sc2, rwt, rb)


def _mix1_kernel(x_ref, o_ref, gt1_ref, wo_ref, n2g_ref, sh2_ref, sc2_ref, rwt_ref, rb_ref,
                 x1_ref, hn_ref, idxt_ref, route_ref, cnt_ref):
    y = jnp.dot(o_ref[...], wo_ref[...], preferred_element_type=F32)
    x1 = x_ref[...] + gt1_ref[...] * y
    x1_ref[...] = x1
    _pre_moe(x1, n2g_ref[...], sh2_ref[...], sc2_ref[...], rwt_ref[...], rb_ref[...],
             hn_ref, idxt_ref, route_ref, cnt_ref)


def _mix1(x, o, gt1, wo, n2g, sh2, sc2, rwt, rb, ts):
    b, s, d = x.shape
    n_exp = rwt.shape[0]
    do = o.shape[-1]
    assert ts % MOE_TILE == 0 and ts // MOE_TILE <= SUBLANES
    out_shape, out_specs = _pre_moe_specs(b, s, d, ts, n_exp)
    full = lambda shp: pl.BlockSpec(shp, lambda i, j: (0,) * len(shp))
    return pl.pallas_call(
        _mix1_kernel,
        out_shape=out_shape,
        grid=(b, s // ts),
        in_specs=[
            pl.BlockSpec((None, ts, d), lambda i, j: (i, j, 0)),
            pl.BlockSpec((None, ts, do), lambda i, j: (i, j, 0)),
            _bvec_spec(d), full((do, d)),
            _vec_spec(d), _bvec_spec(d), _bvec_spec(d),
            full((n_exp, d)), full((n_exp, 1)),
        ],
        out_specs=out_specs,
        compiler_params=_cparams(("arbitrary", "arbitrary")),
        name="attn_out_mixer",
    )(x, o, gt1, wo, n2g, sh2, sc2, rwt, rb)


def _moe_dims(n_tok, n_exp):
    n_tiles = n_tok // MOE_TILE
    loc_rows = _round_up(TOP_K * MOE_TILE + n_exp * (ROW_ALIGN - 1), LANES)
    n_chunks = loc_rows // ROW_ALIGN
    max_rows = TOP_K * n_tok + n_tiles * n_exp * (ROW_ALIGN - 1) + n_exp * (EXPERT_ROWS - ROW_ALIGN)
    max_blocks = -(-max_rows // EXPERT_ROWS)
    return n_tiles, loc_rows, n_chunks, max_blocks


def _routing_tables(cnt, n_chunks, max_blocks):
    n_tiles, n_exp = cnt.shape
    cnt8 = _round_up(cnt, ROW_ALIGN)
    off = jnp.cumsum(cnt8, axis=1) - cnt8
    seg_len = cnt8.sum(axis=0)
    seg_pad = _round_up(seg_len, EXPERT_ROWS)
    seg_end = jnp.cumsum(seg_pad)
    seg_start = seg_end - seg_pad
    run_start = seg_start[None, :] + jnp.cumsum(cnt8, axis=0) - cnt8
    n_used = (cnt8.sum(axis=1) // ROW_ALIGN).astype(I32)
    c_row = jnp.arange(n_chunks, dtype=I32) * ROW_ALIGN
    local_end = off + cnt8
    in_run = ((c_row[None, :, None] >= off[:, None, :])
              & (c_row[None, :, None] < local_end[:, None, :]))
    delta = jnp.sum(jnp.where(in_run, (run_start - off)[:, None, :], 0), axis=-1)
    tab = ((delta + c_row[None, :]) // ROW_ALIGN).astype(I32)
    n_blocks = (seg_end[-1] // EXPERT_ROWS).astype(I32)
    blk_row = jnp.arange(max_blocks, dtype=I32) * EXPERT_ROWS
    blk_row = jnp.minimum(blk_row, seg_end[-1] - EXPERT_ROWS)
    blk_exp = jnp.minimum((blk_row[:, None] >= seg_end[None, :]).sum(axis=-1), n_exp - 1).astype(I32)
    per = EXPERT_ROWS // ROW_ALIGN
    n_tail = (seg_pad - seg_len) // ROW_ALIGN
    z_end = jnp.cumsum(n_tail)
    z_start = z_end - n_tail
    pos = jnp.arange(n_exp * per, dtype=I32)
    in_tail = (pos[:, None] >= z_start[None, :]) & (pos[:, None] < z_end[None, :])
    first = (seg_start + seg_len) // ROW_ALIGN - z_start
    ztab = (jnp.sum(jnp.where(in_tail, first[None, :], 0), axis=-1) + pos).astype(I32)
    ztab = jnp.where(pos < z_end[-1], ztab, 0)
    n_zero = z_end[-1].astype(I32)
    return tab.reshape(-1), n_used, ztab, n_zero.reshape(1), blk_exp, n_blocks.reshape(1)


def _dispatch_kernel(n_exp, tab_ref, nused_ref, ztab_ref, nzero_ref, nblk_ref,
                     x_ref, idxt_ref, xs_ref, loc_ref, zero_ref, sem, zsem):
    i = pl.program_id(0)
    n = pl.num_programs(0)
    slot = i % 2
    tile = x_ref.shape[0]
    loc_rows = loc_ref.shape[1]
    n_chunks = loc_rows // ROW_ALIGN

    def chunk_copy(sl, c, dst):
        return pltpu.make_async_copy(
            loc_ref.at[sl, pl.ds(pl.multiple_of(c * ROW_ALIGN, ROW_ALIGN), ROW_ALIGN)],
            xs_ref.at[pl.ds(pl.multiple_of(dst * ROW_ALIGN, ROW_ALIGN), ROW_ALIGN)],
            sem.at[sl])

    def wait_slot(sl, count):
        @pl.when(count > 0)
        def _():
            rows = count * ROW_ALIGN
            pltpu.make_async_copy(loc_ref.at[sl, pl.ds(0, rows)], xs_ref.at[pl.ds(0, rows)],
                                  sem.at[sl]).wait()

    @pl.when(i >= 2)
    def _():
        wait_slot(slot, nused_ref[i - 2])

    idx = idxt_ref[...]
    e_iota = lax.broadcasted_iota(I32, (n_exp, tile), 0)
    hits = [idx[k:k + 1, :] == e_iota for k in range(TOP_K)]
    onehot = jnp.zeros((n_exp, tile), F32)
    for h in hits:
        onehot = onehot + h.astype(F32)
    upper = (lax.broadcasted_iota(I32, (tile, tile), 0)
             < lax.broadcasted_iota(I32, (tile, tile), 1)).astype(BF16)
    pre = jnp.dot(onehot.astype(BF16), upper, preferred_element_type=F32)
    cnt = jnp.sum(onehot, axis=1, keepdims=True)
    cnt8 = jnp.ceil(cnt / ROW_ALIGN) * ROW_ALIGN
    lower = (lax.broadcasted_iota(I32, (n_exp, n_exp), 1)
             < lax.broadcasted_iota(I32, (n_exp, n_exp), 0)).astype(BF16)
    off = jnp.dot(lower, jnp.broadcast_to(cnt8, (n_exp, LANES)).astype(BF16),
                  preferred_element_type=F32)[:, 0:1]
    base = off + pre
    r_iota = lax.broadcasted_iota(I32, (loc_rows, tile), 0)
    perm = jnp.zeros((loc_rows, tile), F32)
    for h in hits:
        dest = jnp.sum(jnp.where(h, base, 0.0), axis=0, keepdims=True).astype(I32)
        perm = perm + (r_iota == dest).astype(F32)
    loc_ref[slot] = jnp.dot(perm.astype(BF16), x_ref[...], preferred_element_type=F32)

    def issue(c, carry):
        chunk_copy(slot, c, tab_ref[i * n_chunks + c]).start()
        return carry
    lax.fori_loop(0, nused_ref[i], issue, 0)

    @pl.when(i == n - 1)
    def _():
        zero_ref[...] = jnp.zeros(zero_ref.shape, F32)
        max_blocks = xs_ref.shape[0] // EXPERT_ROWS

        def zcopy(dst):
            return pltpu.make_async_copy(
                zero_ref.at[pl.ds(0, ROW_ALIGN)],
                xs_ref.at[pl.ds(pl.multiple_of(dst * ROW_ALIGN, ROW_ALIGN), ROW_ALIGN)],
                zsem.at[0])

        def bcopy(blk):
            return pltpu.make_async_copy(
                zero_ref,
                xs_ref.at[pl.ds(pl.multiple_of(blk * EXPERT_ROWS, EXPERT_ROWS), EXPERT_ROWS)],
                zsem.at[1])

        def zissue(c, carry):
            zcopy(ztab_ref[c]).start()
            return carry
        lax.fori_loop(0, nzero_ref[0], zissue, 0)

        def bissue(blk, carry):
            bcopy(blk).start()
            return carry
        lax.fori_loop(nblk_ref[0], max_blocks, bissue, 0)

        @pl.when(i >= 1)
        def _():
            wait_slot(1 - slot, nused_ref[i - 1])
        wait_slot(slot, nused_ref[i])

        def wait_rows(rows, s):
            @pl.when(rows > 0)
            def _():
                pltpu.make_async_copy(xs_ref.at[pl.ds(0, rows)], xs_ref.at[pl.ds(0, rows)],
                                      zsem.at[s]).wait()
        wait_rows(nzero_ref[0] * ROW_ALIGN, 0)
        wait_rows((max_blocks - nblk_ref[0]) * EXPERT_ROWS, 1)


def _dispatch(hn2, idxt, tab, n_used, ztab, n_zero, n_blocks, n_exp, loc_rows, max_blocks):
    n_tok, d = hn2.shape
    n_tiles = n_tok // MOE_TILE
    return pl.pallas_call(
        functools.partial(_dispatch_kernel, n_exp),
        out_shape=jax.ShapeDtypeStruct((max_blocks * EXPERT_ROWS, d), F32),
        grid_spec=pltpu.PrefetchScalarGridSpec(
            num_scalar_prefetch=5,
            grid=(n_tiles,),
            in_specs=[
                pl.BlockSpec((MOE_TILE, d), lambda i, *_: (i, 0)),
                pl.BlockSpec((TOP_K, MOE_TILE), lambda i, *_: (0, i)),
            ],
            out_specs=pl.BlockSpec(memory_space=pl.ANY),
            scratch_shapes=[
                pltpu.VMEM((2, loc_rows, d), F32),
                pltpu.VMEM((EXPERT_ROWS, d), F32),
                pltpu.SemaphoreType.DMA((2,)),
                pltpu.SemaphoreType.DMA((2,)),
            ],
        ),
        compiler_params=_cparams(("arbitrary",)),
        name="moe_dispatch",
    )(tab, n_used, ztab, n_zero, n_blocks, hn2, idxt)


def _expert_kernel(be_ref, nb_ref, x_ref, wgu_ref, bgu_ref, wdn_ref, bdn_ref, y_ref,
                   wgu_bf, wdn_bf):
    de = wdn_ref.shape[0]
    b = pl.program_id(0)

    @pl.when((b == 0) | (be_ref[b] != be_ref[jnp.maximum(b - 1, 0)]))
    def _():
        cw = 512
        for c0 in range(0, wgu_ref.shape[1], cw):
            wgu_bf[:, c0:c0 + cw] = wgu_ref[:, c0:c0 + cw].astype(BF16)
        for c0 in range(0, wdn_ref.shape[1], cw):
            wdn_bf[:, c0:c0 + cw] = wdn_ref[:, c0:c0 + cw].astype(BF16)

    @pl.when(b < nb_ref[0])
    def _():
        xb = x_ref[...].astype(BF16)
        g = jnp.dot(xb, wgu_bf[:, :de], preferred_element_type=F32) + bgu_ref[:, :de]
        u = jnp.dot(xb, wgu_bf[:, de:], preferred_element_type=F32) + bgu_ref[:, de:]
        g = jnp.minimum(g, SWIGLU_LIMIT)
        u = jnp.clip(u, -SWIGLU_LIMIT, SWIGLU_LIMIT)
        a = (u + 1.0) * g * jax.nn.sigmoid(SWIGLU_ALPHA * g)
        y_ref[...] = jnp.dot(a.astype(BF16), wdn_bf[...], preferred_element_type=F32) + bdn_ref[...]

    @pl.when(pl.program_id(0) >= nb_ref[0])
    def _():
        y_ref[...] = jnp.zeros(y_ref.shape, y_ref.dtype)


def _experts(xs, blk_exp, n_blocks, layer, wgu, bgu, wdn, bdn):
    rows, d = xs.shape
    depth, n_exp, _, de2 = wgu.shape
    de = de2 // 2
    max_blocks = rows // EXPERT_ROWS
    row_map = lambda b, be, nb: (jnp.minimum(b, nb[0] - 1), 0)
    exp_map = lambda b, be, nb: (layer, be[b], 0, 0)
    return pl.pallas_call(
        _expert_kernel,
        out_shape=jax.ShapeDtypeStruct((rows, d), F32),
        grid_spec=pltpu.PrefetchScalarGridSpec(
            num_scalar_prefetch=2,
            grid=(max_blocks,),
            in_specs=[
                pl.BlockSpec((EXPERT_ROWS, d), row_map),
                pl.BlockSpec((None, None, d, de2), exp_map),
                pl.BlockSpec((None, None, 1, de2), exp_map),
                pl.BlockSpec((None, None, de, d), exp_map),
                pl.BlockSpec((None, None, 1, d), exp_map),
            ],
            out_specs=pl.BlockSpec((EXPERT_ROWS, d), lambda b, be, nb: (b, 0)),
            scratch_shapes=[pltpu.VMEM((d, de2), BF16), pltpu.VMEM((de, d), BF16)],
        ),
        compiler_params=_cparams(("arbitrary",)),
        name="moe_experts",
    )(blk_exp, n_blocks, xs, wgu, bgu.reshape(depth, n_exp, 1, de2), wdn,
      bdn.reshape(depth, n_exp, 1, d))


def _combine_kernel(final_norm, n_exp, tab_ref, nused_ref,
                    route_ref, x1_ref, g2_ref, fg_ref, ys_ref, out_ref, loc_ref, sem):
    i = pl.program_id(0)
    n = pl.num_programs(0)
    slot = i % 2
    tile = x1_ref.shape[0]
    loc_rows = loc_ref.shape[1]
    n_chunks = loc_rows // ROW_ALIGN

    def chunk_copy(sl, c, src):
        return pltpu.make_async_copy(
            ys_ref.at[pl.ds(pl.multiple_of(src * ROW_ALIGN, ROW_ALIGN), ROW_ALIGN)],
            loc_ref.at[sl, pl.ds(pl.multiple_of(c * ROW_ALIGN, ROW_ALIGN), ROW_ALIGN)],
            sem.at[sl])

    def issue_tile(t, sl):
        def body(c, carry):
            chunk_copy(sl, c, tab_ref[t * n_chunks + c]).start()
            return carry
        lax.fori_loop(0, nused_ref[t], body, 0)

    @pl.when(i == 0)
    def _():
        issue_tile(0, 0)

    @pl.when(i + 1 < n)
    def _():
        issue_tile(i + 1, 1 - slot)

    route = route_ref[...]
    e_iota = lax.broadcasted_iota(I32, (tile, n_exp), 1)
    hits = [route[:, k:k + 1].astype(I32) == e_iota for k in range(TOP_K)]
    gates = [route[:, TOP_K + k:TOP_K + k + 1] for k in range(TOP_K)]
    onehot = jnp.zeros((tile, n_exp), F32)
    for h in hits:
        onehot = onehot + h.astype(F32)
    lower = (lax.broadcasted_iota(I32, (tile, tile), 1)
             < lax.broadcasted_iota(I32, (tile, tile), 0)).astype(BF16)
    pre = jnp.dot(lower, onehot.astype(BF16), preferred_element_type=F32)
    cnt = jnp.sum(onehot, axis=0, keepdims=True)
    cnt8 = jnp.ceil(cnt / ROW_ALIGN) * ROW_ALIGN
    upper = (lax.broadcasted_iota(I32, (n_exp, n_exp), 0)
             < lax.broadcasted_iota(I32, (n_exp, n_exp), 1)).astype(BF16)
    off = jnp.dot(jnp.broadcast_to(cnt8, (SUBLANES, n_exp)).astype(BF16), upper,
                  preferred_element_type=F32)[0:1, :]
    base = off + pre
    r_iota = lax.broadcasted_iota(I32, (tile, loc_rows), 1)
    comb = jnp.zeros((tile, loc_rows), F32)
    for h, gk in zip(hits, gates):
        dest = jnp.sum(jnp.where(h, base, 0.0), axis=1, keepdims=True).astype(I32)
        comb = comb + jnp.where(r_iota == dest, gk, 0.0)

    @pl.when(nused_ref[i] > 0)
    def _():
        rows = nused_ref[i] * ROW_ALIGN
        pltpu.make_async_copy(ys_ref.at[pl.ds(0, rows)], loc_ref.at[slot, pl.ds(0, rows)],
                              sem.at[slot]).wait()

    used_rows = nused_ref[i] * ROW_ALIGN
    row_ok = lax.broadcasted_iota(I32, (loc_rows, 1), 0) < used_rows
    yl = jnp.where(row_ok, loc_ref[slot], 0.0).astype(BF16)
    moe = jnp.dot(comb.astype(BF16), yl, preferred_element_type=F32)
    out = x1_ref[...] + g2_ref[...] * moe
    if final_norm:
        out = _rms(out, fg_ref[...])
    out_ref[...] = out


def _combine(ys, route, x1, gate2, final_g, tab, n_used, n_exp, loc_rows, final_norm):
    b, s, d = x1.shape
    n_tok = b * s
    n_tiles = n_tok // MOE_TILE
    tiles_per_seq = s // MOE_TILE
    out = pl.pallas_call(
        functools.partial(_combine_kernel, final_norm, n_exp),
        out_shape=jax.ShapeDtypeStruct((n_tok, d), F32),
        grid_spec=pltpu.PrefetchScalarGridSpec(
            num_scalar_prefetch=2,
            grid=(n_tiles,),
            in_specs=[
                pl.BlockSpec((MOE_TILE, LANES), lambda i, *_: (i, 0)),
                pl.BlockSpec((MOE_TILE, d), lambda i, *_: (i, 0)),
                pl.BlockSpec((None, 1, d), lambda i, *_: (i // tiles_per_seq, 0, 0)),
                pl.BlockSpec((1, d), lambda i, *_: (0, 0)),
                pl.BlockSpec(memory_space=pl.ANY),
            ],
            out_specs=pl.BlockSpec((MOE_TILE, d), lambda i, *_: (i, 0)),
            scratch_shapes=[
                pltpu.VMEM((2, loc_rows, d), F32),
                pltpu.SemaphoreType.DMA((2,)),
            ],
        ),
        compiler_params=_cparams(("arbitrary",)),
        name="moe_combine",
    )(tab, n_used, route, x1.reshape(n_tok, d), gate2, final_g, ys)
    return out.reshape(b, s, d)


def _moe(x1, hn2, idxt, route, cnt, gate2, final_g, layer, wgu, bgu, wdn, bdn, final_norm):
    b, s, d = x1.shape
    n_tok = b * s
    n_exp = wgu.shape[1]
    n_tiles, loc_rows, n_chunks, max_blocks = _moe_dims(n_tok, n_exp)
    tab, n_used, ztab, n_zero, blk_exp, n_blocks = _routing_tables(cnt, n_chunks, max_blocks)
    xs = _dispatch(hn2.reshape(n_tok, d), idxt, tab, n_used, ztab, n_zero, n_blocks,
                   n_exp, loc_rows, max_blocks)
    ys = _experts(xs, blk_exp, n_blocks, layer, wgu, bgu, wdn, bdn)
    return _combine(ys, route, x1, gate2, final_g, tab, n_used, n_exp, loc_rows, final_norm)


def _qkv_kernel(n_heads, scale, x_ref, cc_ref, ss_ref, kvg_ref, wdkv_ref, ckvg_ref, wuk_ref, wuv_ref,
                n1g_ref, sh1_ref, sc1_ref, wdq_ref, cqg_ref, wuq_ref,
                q_ref, k_ref, v_ref):
    x = x_ref[...]
    cc = cc_ref[...]
    ss = ss_ref[...]
    r_kv = ckvg_ref.shape[1]
    hk = _rms(x, kvg_ref[...]).astype(BF16)
    lat = jnp.dot(hk, wdkv_ref[...], preferred_element_type=F32)
    ckv = _rms(lat[:, :r_kv], ckvg_ref[...]).astype(BF16)
    krot = lat[:, r_kv:r_kv + LANES] * cc + lat[:, r_kv + LANES:r_kv + 2 * LANES] * ss
    kn = jnp.dot(ckv, wuk_ref[...], preferred_element_type=F32)
    vt = lax.dot_general(wuv_ref[...], ckv, (((1,), (1,)), ((), ())), preferred_element_type=F32)
    hq = (_rms(x, n1g_ref[...]) * (1.0 + sc1_ref[...]) + sh1_ref[...]).astype(BF16)
    cq = _rms(jnp.dot(hq, wdq_ref[...], preferred_element_type=F32), cqg_ref[...]).astype(BF16)
    qq = jnp.dot(cq, wuq_ref[...], preferred_element_type=F32) * scale
    hd = n_heads * LANES
    ts = x.shape[0]
    ones_rows = (lax.broadcasted_iota(I32, (V_ONES_ROWS, ts), 0) == 0).astype(BF16)
    for h in range(n_heads):
        sl = slice(h * LANES, (h + 1) * LANES)
        k_ref[h, :, 0:LANES] = kn[:, sl].astype(BF16)
        k_ref[h, :, LANES:2 * LANES] = krot.astype(BF16)
        v_ref[h, 0:LANES, :] = vt[h * LANES:(h + 1) * LANES, :].astype(BF16)
        v_ref[h, LANES:LANES + V_ONES_ROWS, :] = ones_rows
        q_ref[h, :, 0:LANES] = qq[:, sl].astype(BF16)
        qrot = qq[:, hd + h * LANES:hd + (h + 1) * LANES] * cc \
            + qq[:, 2 * hd + h * LANES:2 * hd + (h + 1) * LANES] * ss
        q_ref[h, :, LANES:2 * LANES] = qrot.astype(BF16)


def _qkv(x, cc, ss, kvg, wdkv, ckvg, wuk, wuv, n1g, sh1, sc1, wdq, cqg, wuq, n_heads, scale, ts):
    b, s, d = x.shape
    full = lambda a: pl.BlockSpec(a.shape, lambda i, j: (0,) * a.ndim)
    hspec = lambda w: pl.BlockSpec((None, n_heads, ts, w), lambda i, j: (i, 0, j, 0))
    return pl.pallas_call(
        functools.partial(_qkv_kernel, n_heads, scale),
        out_shape=[
            jax.ShapeDtypeStruct((b, n_heads, s, 2 * LANES), BF16),
            jax.ShapeDtypeStruct((b, n_heads, s, 2 * LANES), BF16),
            jax.ShapeDtypeStruct((b, n_heads, s // ts, LANES + V_ONES_ROWS, ts), BF16),
        ],
        grid=(b, s // ts),
        in_specs=[
            pl.BlockSpec((None, ts, d), lambda i, j: (i, j, 0)),
            pl.BlockSpec((None, ts, LANES), lambda i, j: (i, j, 0)),
            pl.BlockSpec((None, ts, LANES), lambda i, j: (i, j, 0)),
            full(kvg), full(wdkv), full(ckvg), full(wuk), full(wuv),
            full(n1g), _bvec_spec(d), _bvec_spec(d), full(wdq), full(cqg), full(wuq),
        ],
        out_specs=[hspec(2 * LANES), hspec(2 * LANES),
                   pl.BlockSpec((None, n_heads, None, LANES + V_ONES_ROWS, ts),
                                lambda i, j: (i, 0, j, 0, 0))],
        compiler_params=_cparams(("arbitrary", "arbitrary")),
        name="mla_qkv",
    )(x, cc, ss, kvg, wdkv, ckvg, wuk, wuv, n1g, sh1, sc1, wdq, cqg, wuq)


def _attn_kernel(q_ref, k_ref, vt_ref, o_ref, m_ref, acc_ref):
    tq = q_ref.shape[0]
    tk = vt_ref.shape[2]
    ratio = tq // tk
    dv = o_ref.shape[1]
    qi = pl.program_id(2)
    m_ref[...] = jnp.full(m_ref.shape, -jnp.inf, F32)
    acc_ref[...] = jnp.zeros(acc_ref.shape, F32)

    def scores(ki, cols, on_diagonal):
        start = pl.multiple_of(ki * tk, tk)
        k = k_ref[pl.ds(start, tk), :]
        st = lax.dot_general(k, q_ref[cols, :], (((1,), (1,)), ((), ())),
                             preferred_element_type=F32)
        if on_diagonal:
            kc = lax.broadcasted_iota(I32, st.shape, 0) // CHUNK
            qc = lax.broadcasted_iota(I32, st.shape, 1) // CHUNK
            st = jnp.where(kc <= qc, st, -jnp.inf)
        return st

    def update(ki, st, m_old, acc_old):
        m_new = jnp.maximum(m_old, jnp.max(st, axis=0, keepdims=True))
        p = jnp.exp2(st - m_new).astype(BF16)
        alpha = jnp.exp2(m_old - m_new)
        acc_new = alpha * acc_old + jnp.dot(vt_ref[ki], p, preferred_element_type=F32)
        return m_new, acc_new

    def step(groups):
        sts = [[scores(ki, cols, on_diagonal) for ki, on_diagonal in tiles]
               for cols, tiles in groups]
        for (cols, tiles), group_sts in zip(groups, sts):
            state = (m_ref[:, cols], acc_ref[:, cols])
            for (ki, _), st in zip(tiles, group_sts):
                state = update(ki, st, *state)
            m_ref[:, cols], acc_ref[:, cols] = state

    n_below = qi * ratio
    everything = slice(0, tq)
    step([(slice(g * tk, (g + 1) * tk), [(n_below + a, a == g) for a in range(g + 1)])
          for g in range(ratio)])

    def fast_step(first, count):
        m = m_ref[...]
        pv = None
        top = None
        for t in range(count):
            st = scores(first + t, everything, False)
            tmax = jnp.max(st, axis=0, keepdims=True)
            top = tmax if top is None else jnp.maximum(top, tmax)
            p = jnp.exp2(st - m).astype(BF16)
            part = jnp.dot(vt_ref[first + t], p, preferred_element_type=F32)
            pv = part if pv is None else pv + part
        safe = jnp.max(top - m) <= ATTN_MAX_SLACK

        @pl.when(safe)
        def _():
            acc_ref[...] += pv

        @pl.when(jnp.logical_not(safe))
        def _():
            def redo(t, carry):
                step([(everything, [(first + t, False)])])
                return carry
            lax.fori_loop(0, count, redo, 0)

    def body(j, carry):
        fast_step(ATTN_UNROLL * j, ATTN_UNROLL)
        return carry
    trips = n_below // ATTN_UNROLL
    lax.fori_loop(0, trips, body, 0)

    for rem in range(math.gcd(ratio, ATTN_UNROLL), ATTN_UNROLL, math.gcd(ratio, ATTN_UNROLL)):
        @pl.when(n_below % ATTN_UNROLL == rem)
        def _(rem=rem):
            fast_step(trips * ATTN_UNROLL, rem)

    o_ref[...] = (acc_ref[0:dv, :] / acc_ref[dv:dv + 1, :]).T.astype(o_ref.dtype)


def _attention(q, k, vt, tq):
    b, h, s, dk = q.shape
    nk, dv_ext, tk = vt.shape[2:]
    dv = dv_ext - V_ONES_ROWS
    assert tk % CHUNK == 0 and s % tq == 0 and tq % tk == 0
    return pl.pallas_call(
        _attn_kernel,
        out_shape=jax.ShapeDtypeStruct((b, s, h * dv), BF16),
        grid=(b, h, s // tq),
        in_specs=[
            pl.BlockSpec((None, None, tq, dk), lambda i, j, t: (i, j, t, 0)),
            pl.BlockSpec((None, None, s, dk), lambda i, j, t: (i, j, 0, 0)),
            pl.BlockSpec((None, None, nk, dv_ext, tk), lambda i, j, t: (i, j, 0, 0, 0)),
        ],
        out_specs=pl.BlockSpec((None, tq, dv), lambda i, j, t: (i, t, j)),
        scratch_shapes=[pltpu.VMEM((1, tq), F32), pltpu.VMEM((dv_ext, tq), F32)],
        compiler_params=_cparams(("arbitrary", "arbitrary", "arbitrary")),
        name="mla_attention",
    )(q, k, vt)


def _swap_halves(w):
    half = w.shape[-1] // 2
    return jnp.concatenate([w[..., half:], w[..., :half]], axis=-1)


def _pad_lanes(w):
    pad = LANES - w.shape[-1]
    return jnp.concatenate([w, jnp.zeros(w.shape[:-1] + (pad,), w.dtype)], axis=-1)


def kernel(x, c, positions, mod_w, mod_b, norm1_g, norm2_g, conv_pw1_w, conv_pw1_b, conv_dw_w, conv_dw_b, conv_ln_g, conv_ln_b, conv_pw2_w, conv_pw2_b, kv_norm_g, w_dkv, ckv_norm_g, w_uk, w_uv, w_dq, cq_norm_g, w_uq, w_o, router_w, router_b, exp_w_gu, exp_b_gu, exp_w_dn, exp_b_dn, final_g):
    b, s, d = x.shape
    n_heads, nope = w_uk.shape[1], w_uk.shape[2]
    r_kv = ckv_norm_g.shape[0]
    rope = w_dkv.shape[1] - r_kv
    vdim = w_uv.shape[2]
    n_exp = router_w.shape[2]
    assert nope == LANES and vdim == LANES and rope <= LANES and d % LANES == 0
    ts = min(512, s)
    tq = min(1024, s)

    mod = _modulation(c, mod_w, mod_b)
    mods = [[m.reshape(b, 1, d) for m in jnp.split(mod[l], 6, axis=-1)] for l in range(2)]
    row = lambda v: v.reshape(1, -1)

    half = rope // 2
    inv = jnp.exp(-(2.0 * math.log(ROPE_THETA) / rope) * jnp.arange(half, dtype=F32))
    ang = positions.astype(F32)[..., None] * inv
    cos, sin = jnp.cos(ang), jnp.sin(ang)
    cc = _pad_lanes(jnp.concatenate([cos, cos], axis=-1))
    ss = _pad_lanes(jnp.concatenate([-sin, sin], axis=-1))

    sh1, sc1, gt1, sh2, sc2, gt2 = mods[0]
    x1, hn2, idxt, route, cnt = _mix0(
        x, row(norm1_g[0]), sh1, sc1, gt1,
        conv_pw1_w[0].astype(BF16), row(conv_pw1_b[0]), conv_dw_w[0], row(conv_dw_b[0]),
        row(conv_ln_g[0]), row(conv_ln_b[0]), conv_pw2_w[0].astype(BF16), row(conv_pw2_b[0]),
        row(norm2_g[0]), sh2, sc2, router_w[0].T, router_b[0].reshape(n_exp, 1), ts)
    cnt = cnt[:, :ts // MOE_TILE].reshape(-1, n_exp)
    x2 = _moe(x1, hn2, idxt, route, cnt, gt2, row(final_g),
              0, exp_w_gu, exp_b_gu, exp_w_dn, exp_b_dn, False)

    sh1, sc1, gt1, sh2, sc2, gt2 = mods[1]
    wdkv_rope = w_dkv[:, r_kv:]
    wdkv_ext = jnp.concatenate(
        [w_dkv[:, :r_kv], _pad_lanes(wdkv_rope), _pad_lanes(_swap_halves(wdkv_rope))], axis=-1)
    wuq = w_uq[0]
    r_q = wuq.shape[0]
    wuq_rope = wuq[:, :, nope:]
    wuq_ext = jnp.concatenate([
        wuq[:, :, :nope].reshape(r_q, n_heads * LANES),
        _pad_lanes(wuq_rope).reshape(r_q, n_heads * LANES),
        _pad_lanes(_swap_halves(wuq_rope)).reshape(r_q, n_heads * LANES)], axis=-1)
    scale = float((nope + rope) ** -0.5 * math.log2(math.e))
    q, k, v = _qkv(
        x2, cc, ss, row(kv_norm_g), wdkv_ext.astype(BF16), row(ckv_norm_g),
        w_uk.reshape(r_kv, n_heads * nope).astype(BF16), w_uv.reshape(r_kv, n_heads * vdim).T.astype(BF16),
        row(norm1_g[1]), sh1, sc1, w_dq[0].astype(BF16), row(cq_norm_g[0]), wuq_ext.astype(BF16),
        n_heads, scale, ts)
    o = _attention(q, k, v, tq)

    x3, hn2, idxt, route, cnt = _mix1(
        x2, o, gt1, w_o[0].astype(BF16), row(norm2_g[1]), sh2, sc2,
        router_w[1].T, router_b[1].reshape(n_exp, 1), ts)
    cnt = cnt[:, :ts // MOE_TILE].reshape(-1, n_exp)
    return _moe(x3, hn2, idxt, route, cnt, gt2, row(final_g),
                1, exp_w_gu, exp_b_gu, exp_w_dn, exp_b_dn, True)
```

```python
import functools
import math

import jax
import jax.numpy as jnp
from jax import lax
from jax.experimental import pallas as pl
from jax.experimental.pallas import tpu as pltpu

CHUNK = 64
TOP_K = 4
ROPE_THETA = 10000.0
SWIGLU_ALPHA = 1.702
SWIGLU_LIMIT = 7.0
EPS = 1e-6

LANES = 128
SUBLANES = 8
VMEM_LIMIT_BYTES = 56 * 1024 * 1024

ROW_ALIGN = SUBLANES
MOE_TILE = 256
EXPERT_ROWS = 512
CONV_HALO = 32
V_ONES_ROWS = 2 * SUBLANES
ATTN_UNROLL = 4
ATTN_MAX_SLACK = 60.0

F32 = jnp.float32
BF16 = jnp.bfloat16
I32 = jnp.int32


def _cparams(sem):
    return pltpu.CompilerParams(dimension_semantics=sem, vmem_limit_bytes=VMEM_LIMIT_BYTES)


def _rms(x, g):
    return x * lax.rsqrt(jnp.mean(x * x, axis=-1, keepdims=True) + EPS) * g


def _round_up(a, m):
    return (a + m - 1) // m * m


def _mod_kernel(c_ref, w_ref, b_ref, o_ref):
    c = c_ref[...]
    ca = c * jax.nn.sigmoid(c)
    o_ref[...] = jnp.dot(ca, w_ref[...], preferred_element_type=F32) + b_ref[...]


def _modulation(c, mod_w, mod_b):
    depth, d, d6 = mod_w.shape
    b = c.shape[0]
    bp = _round_up(b, SUBLANES)
    cp = jnp.zeros((bp, d), F32).at[:b].set(c)
    tn = d6 // 4
    out = pl.pallas_call(
        _mod_kernel,
        out_shape=jax.ShapeDtypeStruct((depth, bp, d6), F32),
        grid=(depth, d6 // tn),
        in_specs=[
            pl.BlockSpec((bp, d), lambda l, j: (0, 0)),
            pl.BlockSpec((None, d, tn), lambda l, j: (l, 0, j)),
            pl.BlockSpec((None, 1, tn), lambda l, j: (l, 0, j)),
        ],
        out_specs=pl.BlockSpec((None, bp, tn), lambda l, j: (l, 0, j)),
        compiler_params=_cparams(("arbitrary", "arbitrary")),
        name="adaln_modulation",
    )(cp, mod_w, mod_b.reshape(depth, 1, d6))
    return out[:, :b]


def _pre_moe(x1, g2, sh2, sc2, rwt, rb, hn_ref, idxt_ref, route_ref, cnt_ref):
    ts = x1.shape[0]
    n_exp = rwt.shape[0]
    hn = _rms(x1, g2) * (1.0 + sc2) + sh2
    hn_ref[...] = hn.astype(BF16)
    logits = lax.dot_general(rwt, hn, (((1,), (1,)), ((), ())),
                             precision=lax.Precision.HIGHEST,
                             preferred_element_type=F32) + rb
    e_iota = lax.broadcasted_iota(I32, (n_exp, ts), 0)
    vals, idxs = [], []
    cur = logits
    for _ in range(TOP_K):
        m = jnp.max(cur, axis=0, keepdims=True)
        i = jnp.min(jnp.where(cur == m, e_iota, n_exp), axis=0, keepdims=True)
        vals.append(m)
        idxs.append(i)
        cur = jnp.where(e_iota == i, -jnp.inf, cur)
    exps = [jnp.exp(v - vals[0]) for v in vals]
    den = exps[0]
    for e in exps[1:]:
        den = den + e
    gates = [e / den for e in exps]
    idxt_ref[...] = jnp.concatenate(idxs, axis=0)
    rows = jnp.concatenate([i.astype(F32) for i in idxs] + gates
                           + [jnp.zeros((LANES - 2 * TOP_K, ts), F32)], axis=0)
    route_ref[...] = rows.T
    onehot = jnp.zeros((n_exp, ts), F32)
    for i in idxs:
        onehot = onehot + (e_iota == i).astype(F32)
    sel = (lax.broadcasted_iota(I32, (SUBLANES, ts), 1) // MOE_TILE
           == lax.broadcasted_iota(I32, (SUBLANES, ts), 0)).astype(BF16)
    cnt = lax.dot_general(sel, onehot.astype(BF16), (((1,), (1,)), ((), ())),
                          preferred_element_type=F32)
    cnt_ref[...] = cnt.astype(I32)


def _pre_moe_specs(b, s, d, ts, n_exp):
    ns = s // ts
    out_shape = [
        jax.ShapeDtypeStruct((b, s, d), F32),
        jax.ShapeDtypeStruct((b, s, d), BF16),
        jax.ShapeDtypeStruct((TOP_K, b * s), I32),
        jax.ShapeDtypeStruct((b * s, LANES), F32),
        jax.ShapeDtypeStruct((b * ns, SUBLANES, n_exp), I32),
    ]
    out_specs = [
        pl.BlockSpec((None, ts, d), lambda i, j: (i, j, 0)),
        pl.BlockSpec((None, ts, d), lambda i, j: (i, j, 0)),
        pl.BlockSpec((TOP_K, ts), lambda i, j: (0, i * ns + j)),
        pl.BlockSpec((ts, LANES), lambda i, j: (i * ns + j, 0)),
        pl.BlockSpec((None, SUBLANES, n_exp), lambda i, j: (i * ns + j, 0, 0)),
    ]
    return out_shape, out_specs


def _vec_spec(d):
    return pl.BlockSpec((1, d), lambda i, j: (0, 0))


def _bvec_spec(d):
    return pl.BlockSpec((None, 1, d), lambda i, j: (i, 0, 0))


def _mix0_kernel(x_ref, n1g_ref, sh1_ref, sc1_ref, gt1_ref,
                 pw1w_ref, pw1b_ref, dww_ref, dwb_ref, lng_ref, lnb_ref,
                 pw2w_ref, pw2b_ref, n2g_ref, sh2_ref, sc2_ref, rwt_ref, rb_ref,
                 x1_ref, hn_ref, idxt_ref, route_ref, cnt_ref, buf_ref, conv_ref, shift_ref):
    ts, d = x_ref.shape
    width = dww_ref.shape[0]
    x = x_ref[...]
    hn = _rms(x, n1g_ref[...]) * (1.0 + sc1_ref[...]) + sh1_ref[...]
    hb = hn.astype(BF16)
    a = jnp.dot(hb, pw1w_ref[:, :d], preferred_element_type=F32) + pw1b_ref[:, :d]
    g = jnp.dot(hb, pw1w_ref[:, d:], preferred_element_type=F32) + pw1b_ref[:, d:]
    glu = a * jax.nn.sigmoid(g)

    @pl.when(pl.program_id(1) == 0)
    def _():
        buf_ref[0:CONV_HALO, :] = jnp.zeros((CONV_HALO, d), F32)

    buf_ref[CONV_HALO:, :] = glu
    base = CONV_HALO - (width - 1)
    rc = 32
    lc = min(512, d)
    sh_rows = shift_ref.shape[1]
    for c0 in range(0, d, lc):
        for r in range(1, SUBLANES):
            shift_ref[r - 1] = buf_ref[r:r + sh_rows, c0:c0 + lc]
        for r0 in range(0, ts, rc):
            acc = jnp.zeros((rc, lc), F32)
            for k in range(width):
                q8, r = divmod(base + k, SUBLANES)
                lo = q8 * SUBLANES + r0
                if r == 0:
                    win = buf_ref[lo:lo + rc, c0:c0 + lc]
                else:
                    win = shift_ref[r - 1, lo:lo + rc, :]
                acc = acc + dww_ref[k:k + 1, c0:c0 + lc] * win
            conv_ref[r0:r0 + rc, c0:c0 + lc] = acc
    buf_ref[0:CONV_HALO, :] = buf_ref[ts:ts + CONV_HALO, :]
    u = conv_ref[...] + dwb_ref[...]
    mu = jnp.mean(u, axis=-1, keepdims=True)
    dlt = u - mu
    var = jnp.mean(dlt * dlt, axis=-1, keepdims=True)
    u = dlt * lax.rsqrt(var + EPS) * lng_ref[...] + lnb_ref[...]
    u = u * jax.nn.sigmoid(u)
    y = jnp.dot(u.astype(BF16), pw2w_ref[...], preferred_element_type=F32) + pw2b_ref[...]
    x1 = x + gt1_ref[...] * y
    x1_ref[...] = x1
    _pre_moe(x1, n2g_ref[...], sh2_ref[...], sc2_ref[...], rwt_ref[...], rb_ref[...],
             hn_ref, idxt_ref, route_ref, cnt_ref)


def _mix0(x, n1g, sh1, sc1, gt1, pw1w, pw1b, dww, dwb, lng, lnb, pw2w, pw2b,
          n2g, sh2, sc2, rwt, rb, ts):
    b, s, d = x.shape
    n_exp = rwt.shape[0]
    width = dww.shape[0]
    assert width - 1 <= CONV_HALO and ts % MOE_TILE == 0 and ts // MOE_TILE <= SUBLANES
    out_shape, out_specs = _pre_moe_specs(b, s, d, ts, n_exp)
    full = lambda shp: pl.BlockSpec(shp, lambda i, j: (0,) * len(shp))
    return pl.pallas_call(
        _mix0_kernel,
        out_shape=out_shape,
        grid=(b, s // ts),
        in_specs=[
            pl.BlockSpec((None, ts, d), lambda i, j: (i, j, 0)),
            _vec_spec(d), _bvec_spec(d), _bvec_spec(d), _bvec_spec(d),
            full((d, 2 * d)), full((1, 2 * d)), full((width, d)), full((1, d)),
            full((1, d)), full((1, d)), full((d, d)), full((1, d)),
            _vec_spec(d), _bvec_spec(d), _bvec_spec(d),
            full((n_exp, d)), full((n_exp, 1)),
        ],
        out_specs=out_specs,
        scratch_shapes=[
            pltpu.VMEM((CONV_HALO + ts, d), F32),
            pltpu.VMEM((ts, d), F32),
            pltpu.VMEM((SUBLANES - 1, CONV_HALO + ts - SUBLANES, min(512, d)), F32),
        ],
        compiler_params=_cparams(("arbitrary", "arbitrary")),
        name="conformer_mixer",
    )(x, n1g, sh1, sc1, gt1, pw1w, pw1b, dww, dwb, lng, lnb, pw2w, pw2b,
      n2g, sh2, sc2, rwt, rb)


def _mix1_kernel(x_ref, o_ref, gt1_ref, wo_ref, n2g_ref, sh2_ref, sc2_ref, rwt_ref, rb_ref,
                 x1_ref, hn_ref, idxt_ref, route_ref, cnt_ref):
    y = jnp.dot(o_ref[...], wo_ref[...], preferred_element_type=F32)
    x1 = x_ref[...] + gt1_ref[...] * y
    x1_ref[...] = x1
    _pre_moe(x1, n2g_ref[...], sh2_ref[...], sc2_ref[...], rwt_ref[...], rb_ref[...],
             hn_ref, idxt_ref, route_ref, cnt_ref)


def _mix1(x, o, gt1, wo, n2g, sh2, sc2, rwt, rb, ts):
    b, s, d = x.shape
    n_exp = rwt.shape[0]
    do = o.shape[-1]
    assert ts % MOE_TILE == 0 and ts // MOE_TILE <= SUBLANES
    out_shape, out_specs = _pre_moe_specs(b, s, d, ts, n_exp)
    full = lambda shp: pl.BlockSpec(shp, lambda i, j: (0,) * len(shp))
    return pl.pallas_call(
        _mix1_kernel,
        out_shape=out_shape,
        grid=(b, s // ts),
        in_specs=[
            pl.BlockSpec((None, ts, d), lambda i, j: (i, j, 0)),
            pl.BlockSpec((None, ts, do), lambda i, j: (i, j, 0)),
            _bvec_spec(d), full((do, d)),
            _vec_spec(d), _bvec_spec(d), _bvec_spec(d),
            full((n_exp, d)), full((n_exp, 1)),
        ],
        out_specs=out_specs,
        compiler_params=_cparams(("arbitrary", "arbitrary")),
        name="attn_out_mixer",
    )(x, o, gt1, wo, n2g, sh2, sc2, rwt, rb)


def _moe_dims(n_tok, n_exp):
    n_tiles = n_tok // MOE_TILE
    loc_rows = _round_up(TOP_K * MOE_TILE + n_exp * (ROW_ALIGN - 1), LANES)
    max_rows = TOP_K * n_tok + n_tiles * n_exp * (ROW_ALIGN - 1) + n_exp * (EXPERT_ROWS - ROW_ALIGN)
    max_blocks = -(-max_rows // EXPERT_ROWS)
    return loc_rows, max_blocks


def _routing_tables(cnt, max_blocks):
    n_tiles, n_exp = cnt.shape
    cnt8 = _round_up(cnt, ROW_ALIGN)
    off = jnp.cumsum(cnt8, axis=1) - cnt8
    seg_len = cnt8.sum(axis=0)
    seg_pad = _round_up(seg_len, EXPERT_ROWS)
    seg_end = jnp.cumsum(seg_pad)
    seg_start = seg_end - seg_pad
    run_start = seg_start[None, :] + jnp.cumsum(cnt8, axis=0) - cnt8
    n_blocks = (seg_end[-1] // EXPERT_ROWS).astype(I32)
    blk_row = jnp.arange(max_blocks, dtype=I32) * EXPERT_ROWS
    blk_row = jnp.minimum(blk_row, seg_end[-1] - EXPERT_ROWS)
    blk_exp = jnp.minimum((blk_row[:, None] >= seg_end[None, :]).sum(axis=-1), n_exp - 1).astype(I32)
    flat = lambda a: a.reshape(-1).astype(I32)
    return dict(
        run_loc=flat(off),
        run_glb=flat(run_start),
        run_len=flat(cnt8),
        rows_used=flat(cnt8.sum(axis=1)),
        tail_glb=flat(seg_start + seg_len),
        tail_len=flat(seg_pad - seg_len),
        blk_exp=blk_exp, n_blocks=n_blocks.reshape(1))


def _aligned(v):
    return pl.multiple_of(v, ROW_ALIGN)


def _dispatch_kernel(n_exp, loc0_ref, glb0_ref, len_ref, used_ref, tail0_ref, taillen_ref, nblk_ref,
                     x_ref, idxt_ref, xs_ref, loc_ref, zero_ref, sem, zsem):
    i = pl.program_id(0)
    n = pl.num_programs(0)
    slot = i % 2
    tile = x_ref.shape[0]
    loc_rows = loc_ref.shape[1]

    def wait_slot(sl, rows):
        rows = _aligned(rows)

        @pl.when(rows > 0)
        def _():
            pltpu.make_async_copy(loc_ref.at[sl, pl.ds(0, rows)], xs_ref.at[pl.ds(0, rows)],
                                  sem.at[sl]).wait()

    @pl.when(i >= 2)
    def _():
        wait_slot(slot, used_ref[i - 2])

    idx = idxt_ref[...]
    e_iota = lax.broadcasted_iota(I32, (n_exp, tile), 0)
    hits = [idx[k:k + 1, :] == e_iota for k in range(TOP_K)]
    onehot = jnp.zeros((n_exp, tile), F32)
    for h in hits:
        onehot = onehot + h.astype(F32)
    upper = (lax.broadcasted_iota(I32, (tile, tile), 0)
             < lax.broadcasted_iota(I32, (tile, tile), 1)).astype(BF16)
    pre = jnp.dot(onehot.astype(BF16), upper, preferred_element_type=F32)
    cnt = jnp.sum(onehot, axis=1, keepdims=True)
    cnt8 = jnp.ceil(cnt / ROW_ALIGN) * ROW_ALIGN
    lower = (lax.broadcasted_iota(I32, (n_exp, n_exp), 1)
             < lax.broadcasted_iota(I32, (n_exp, n_exp), 0)).astype(BF16)
    off = jnp.dot(lower, jnp.broadcast_to(cnt8, (n_exp, LANES)).astype(BF16),
                  preferred_element_type=F32)[:, 0:1]
    base = off + pre
    r_iota = lax.broadcasted_iota(I32, (loc_rows, tile), 0)
    perm = jnp.zeros((loc_rows, tile), F32)
    for h in hits:
        dest = jnp.sum(jnp.where(h, base, 0.0), axis=0, keepdims=True).astype(I32)
        perm = perm + (r_iota == dest).astype(F32)
    loc_ref[slot] = jnp.dot(perm.astype(BF16), x_ref[...], preferred_element_type=F32)

    def issue_run(e, carry):
        rows = _aligned(len_ref[i * n_exp + e])

        @pl.when(rows > 0)
        def _():
            pltpu.make_async_copy(
                loc_ref.at[slot, pl.ds(_aligned(loc0_ref[i * n_exp + e]), rows)],
                xs_ref.at[pl.ds(_aligned(glb0_ref[i * n_exp + e]), rows)],
                sem.at[slot]).start()
        return carry
    lax.fori_loop(0, n_exp, issue_run, 0)

    @pl.when(i == n - 1)
    def _():
        zero_ref[...] = jnp.zeros(zero_ref.shape, F32)
        max_blocks = xs_ref.shape[0] // EXPERT_ROWS

        def zero_tail(e, total):
            rows = _aligned(taillen_ref[e])

            @pl.when(rows > 0)
            def _():
                pltpu.make_async_copy(zero_ref.at[pl.ds(0, rows)],
                                      xs_ref.at[pl.ds(_aligned(tail0_ref[e]), rows)],
                                      zsem.at[0]).start()
            return total + rows
        tail_rows = lax.fori_loop(0, n_exp, zero_tail, jnp.int32(0))

        def zero_block(blk, carry):
            pltpu.make_async_copy(
                zero_ref,
                xs_ref.at[pl.ds(pl.multiple_of(blk * EXPERT_ROWS, EXPERT_ROWS), EXPERT_ROWS)],
                zsem.at[1]).start()
            return carry
        lax.fori_loop(nblk_ref[0], max_blocks, zero_block, 0)

        @pl.when(i >= 1)
        def _():
            wait_slot(1 - slot, used_ref[i - 1])
        wait_slot(slot, used_ref[i])

        def wait_rows(rows, s):
            rows = _aligned(rows)

            @pl.when(rows > 0)
            def _():
                pltpu.make_async_copy(xs_ref.at[pl.ds(0, rows)], xs_ref.at[pl.ds(0, rows)],
                                      zsem.at[s]).wait()
        wait_rows(tail_rows, 0)
        wait_rows((max_blocks - nblk_ref[0]) * EXPERT_ROWS, 1)


def _dispatch(hn2, idxt, tabs, n_exp, loc_rows, max_blocks):
    n_tok, d = hn2.shape
    n_tiles = n_tok // MOE_TILE
    return pl.pallas_call(
        functools.partial(_dispatch_kernel, n_exp),
        out_shape=jax.ShapeDtypeStruct((max_blocks * EXPERT_ROWS, d), F32),
        grid_spec=pltpu.PrefetchScalarGridSpec(
            num_scalar_prefetch=7,
            grid=(n_tiles,),
            in_specs=[
                pl.BlockSpec((MOE_TILE, d), lambda i, *_: (i, 0)),
                pl.BlockSpec((TOP_K, MOE_TILE), lambda i, *_: (0, i)),
            ],
            out_specs=pl.BlockSpec(memory_space=pl.ANY),
            scratch_shapes=[
                pltpu.VMEM((2, loc_rows, d), F32),
                pltpu.VMEM((EXPERT_ROWS, d), F32),
                pltpu.SemaphoreType.DMA((2,)),
                pltpu.SemaphoreType.DMA((2,)),
            ],
        ),
        compiler_params=_cparams(("arbitrary",)),
        name="moe_dispatch",
    )(tabs["run_loc"], tabs["run_glb"], tabs["run_len"], tabs["rows_used"],
      tabs["tail_glb"], tabs["tail_len"], tabs["n_blocks"], hn2, idxt)


def _expert_kernel(be_ref, nb_ref, x_ref, wgu_ref, bgu_ref, wdn_ref, bdn_ref, y_ref,
                   wgu_bf, wdn_bf):
    de = wdn_ref.shape[0]
    b = pl.program_id(0)

    @pl.when((b == 0) | (be_ref[b] != be_ref[jnp.maximum(b - 1, 0)]))
    def _():
        cw = 512
        for c0 in range(0, wgu_ref.shape[1], cw):
            wgu_bf[:, c0:c0 + cw] = wgu_ref[:, c0:c0 + cw].astype(BF16)
        for c0 in range(0, wdn_ref.shape[1], cw):
            wdn_bf[:, c0:c0 + cw] = wdn_ref[:, c0:c0 + cw].astype(BF16)

    @pl.when(b < nb_ref[0])
    def _():
        xb = x_ref[...].astype(BF16)
        g = jnp.dot(xb, wgu_bf[:, :de], preferred_element_type=F32) + bgu_ref[:, :de]
        u = jnp.dot(xb, wgu_bf[:, de:], preferred_element_type=F32) + bgu_ref[:, de:]
        g = jnp.minimum(g, SWIGLU_LIMIT)
        u = jnp.clip(u, -SWIGLU_LIMIT, SWIGLU_LIMIT)
        a = (u + 1.0) * g * jax.nn.sigmoid(SWIGLU_ALPHA * g)
        y_ref[...] = jnp.dot(a.astype(BF16), wdn_bf[...], preferred_element_type=F32) + bdn_ref[...]

    @pl.when(pl.program_id(0) >= nb_ref[0])
    def _():
        y_ref[...] = jnp.zeros(y_ref.shape, y_ref.dtype)


def _experts(xs, blk_exp, n_blocks, layer, wgu, bgu, wdn, bdn):
    rows, d = xs.shape
    depth, n_exp, _, de2 = wgu.shape
    de = de2 // 2
    max_blocks = rows // EXPERT_ROWS
    row_map = lambda b, be, nb: (jnp.minimum(b, nb[0] - 1), 0)
    exp_map = lambda b, be, nb: (layer, be[b], 0, 0)
    return pl.pallas_call(
        _expert_kernel,
        out_shape=jax.ShapeDtypeStruct((rows, d), F32),
        grid_spec=pltpu.PrefetchScalarGridSpec(
            num_scalar_prefetch=2,
            grid=(max_blocks,),
            in_specs=[
                pl.BlockSpec((EXPERT_ROWS, d), row_map),
                pl.BlockSpec((None, None, d, de2), exp_map),
                pl.BlockSpec((None, None, 1, de2), exp_map),
                pl.BlockSpec((None, None, de, d), exp_map),
                pl.BlockSpec((None, None, 1, d), exp_map),
            ],
            out_specs=pl.BlockSpec((EXPERT_ROWS, d), lambda b, be, nb: (b, 0)),
            scratch_shapes=[pltpu.VMEM((d, de2), BF16), pltpu.VMEM((de, d), BF16)],
        ),
        compiler_params=_cparams(("arbitrary",)),
        name="moe_experts",
    )(blk_exp, n_blocks, xs, wgu, bgu.reshape(depth, n_exp, 1, de2), wdn,
      bdn.reshape(depth, n_exp, 1, d))


def _combine_kernel(final_norm, n_exp, loc0_ref, glb0_ref, len_ref, used_ref,
                    route_ref, x1_ref, g2_ref, fg_ref, ys_ref, out_ref, loc_ref, sem):
    i = pl.program_id(0)
    n = pl.num_programs(0)
    slot = i % 2
    tile = x1_ref.shape[0]
    loc_rows = loc_ref.shape[1]

    def issue_tile(t, sl):
        def issue_run(e, carry):
            rows = _aligned(len_ref[t * n_exp + e])

            @pl.when(rows > 0)
            def _():
                pltpu.make_async_copy(
                    ys_ref.at[pl.ds(_aligned(glb0_ref[t * n_exp + e]), rows)],
                    loc_ref.at[sl, pl.ds(_aligned(loc0_ref[t * n_exp + e]), rows)],
                    sem.at[sl]).start()
            return carry
        lax.fori_loop(0, n_exp, issue_run, 0)

    @pl.when(i == 0)
    def _():
        issue_tile(0, 0)

    @pl.when(i + 1 < n)
    def _():
        issue_tile(i + 1, 1 - slot)

    route = route_ref[...]
    e_iota = lax.broadcasted_iota(I32, (tile, n_exp), 1)
    hits = [route[:, k:k + 1].astype(I32) == e_iota for k in range(TOP_K)]
    gates = [route[:, TOP_K + k:TOP_K + k + 1] for k in range(TOP_K)]
    onehot = jnp.zeros((tile, n_exp), F32)
    for h in hits:
        onehot = onehot + h.astype(F32)
    lower = (lax.broadcasted_iota(I32, (tile, tile), 1)
             < lax.broadcasted_iota(I32, (tile, tile), 0)).astype(BF16)
    pre = jnp.dot(lower, onehot.astype(BF16), preferred_element_type=F32)
    cnt = jnp.sum(onehot, axis=0, keepdims=True)
    cnt8 = jnp.ceil(cnt / ROW_ALIGN) * ROW_ALIGN
    upper = (lax.broadcasted_iota(I32, (n_exp, n_exp), 0)
             < lax.broadcasted_iota(I32, (n_exp, n_exp), 1)).astype(BF16)
    off = jnp.dot(jnp.broadcast_to(cnt8, (SUBLANES, n_exp)).astype(BF16), upper,
                  preferred_element_type=F32)[0:1, :]
    base = off + pre
    r_iota = lax.broadcasted_iota(I32, (tile, loc_rows), 1)
    comb = jnp.zeros((tile, loc_rows), F32)
    for h, gk in zip(hits, gates):
        dest = jnp.sum(jnp.where(h, base, 0.0), axis=1, keepdims=True).astype(I32)
        comb = comb + jnp.where(r_iota == dest, gk, 0.0)

    used_rows = _aligned(used_ref[i])

    @pl.when(used_rows > 0)
    def _():
        pltpu.make_async_copy(ys_ref.at[pl.ds(0, used_rows)],
                              loc_ref.at[slot, pl.ds(0, used_rows)], sem.at[slot]).wait()

    row_ok = lax.broadcasted_iota(I32, (loc_rows, 1), 0) < used_rows
    yl = jnp.where(row_ok, loc_ref[slot], 0.0).astype(BF16)
    moe = jnp.dot(comb.astype(BF16), yl, preferred_element_type=F32)
    out = x1_ref[...] + g2_ref[...] * moe
    if final_norm:
        out = _rms(out, fg_ref[...])
    out_ref[...] = out


def _combine(ys, route, x1, gate2, final_g, tabs, n_exp, loc_rows, final_norm):
    b, s, d = x1.shape
    n_tok = b * s
    n_tiles = n_tok // MOE_TILE
    tiles_per_seq = s // MOE_TILE
    out = pl.pallas_call(
        functools.partial(_combine_kernel, final_norm, n_exp),
        out_shape=jax.ShapeDtypeStruct((n_tok, d), F32),
        grid_spec=pltpu.PrefetchScalarGridSpec(
            num_scalar_prefetch=4,
            grid=(n_tiles,),
            in_specs=[
                pl.BlockSpec((MOE_TILE, LANES), lambda i, *_: (i, 0)),
                pl.BlockSpec((MOE_TILE, d), lambda i, *_: (i, 0)),
                pl.BlockSpec((None, 1, d), lambda i, *_: (i // tiles_per_seq, 0, 0)),
                pl.BlockSpec((1, d), lambda i, *_: (0, 0)),
                pl.BlockSpec(memory_space=pl.ANY),
            ],
            out_specs=pl.BlockSpec((MOE_TILE, d), lambda i, *_: (i, 0)),
            scratch_shapes=[
                pltpu.VMEM((2, loc_rows, d), F32),
                pltpu.SemaphoreType.DMA((2,)),
            ],
        ),
        compiler_params=_cparams(("arbitrary",)),
        name="moe_combine",
    )(tabs["run_loc"], tabs["run_glb"], tabs["run_len"], tabs["rows_used"],
      route, x1.reshape(n_tok, d), gate2, final_g, ys)
    return out.reshape(b, s, d)


def _moe(x1, hn2, idxt, route, cnt, gate2, final_g, layer, wgu, bgu, wdn, bdn, final_norm):
    b, s, d = x1.shape
    n_tok = b * s
    n_exp = wgu.shape[1]
    loc_rows, max_blocks = _moe_dims(n_tok, n_exp)
    tabs = _routing_tables(cnt, max_blocks)
    xs = _dispatch(hn2.reshape(n_tok, d), idxt, tabs, n_exp, loc_rows, max_blocks)
    ys = _experts(xs, tabs["blk_exp"], tabs["n_blocks"], layer, wgu, bgu, wdn, bdn)
    return _combine(ys, route, x1, gate2, final_g, tabs, n_exp, loc_rows, final_norm)


def _qkv_kernel(n_heads, scale, x_ref, cc_ref, ss_ref, kvg_ref, wdkv_ref, ckvg_ref, wuk_ref, wuv_ref,
                n1g_ref, sh1_ref, sc1_ref, wdq_ref, cqg_ref, wuq_ref,
                q_ref, k_ref, v_ref):
    x = x_ref[...]
    cc = cc_ref[...]
    ss = ss_ref[...]
    r_kv = ckvg_ref.shape[1]
    hk = _rms(x, kvg_ref[...]).astype(BF16)
    lat = jnp.dot(hk, wdkv_ref[...], preferred_element_type=F32)
    ckv = _rms(lat[:, :r_kv], ckvg_ref[...]).astype(BF16)
    krot = lat[:, r_kv:r_kv + LANES] * cc + lat[:, r_kv + LANES:r_kv + 2 * LANES] * ss
    kn = jnp.dot(ckv, wuk_ref[...], preferred_element_type=F32)
    vt = lax.dot_general(wuv_ref[...], ckv, (((1,), (1,)), ((), ())), preferred_element_type=F32)
    hq = (_rms(x, n1g_ref[...]) * (1.0 + sc1_ref[...]) + sh1_ref[...]).astype(BF16)
    cq = _rms(jnp.dot(hq, wdq_ref[...], preferred_element_type=F32), cqg_ref[...]).astype(BF16)
    qq = jnp.dot(cq, wuq_ref[...], preferred_element_type=F32) * scale
    hd = n_heads * LANES
    ts = x.shape[0]
    ones_rows = (lax.broadcasted_iota(I32, (V_ONES_ROWS, ts), 0) == 0).astype(BF16)
    for h in range(n_heads):
        sl = slice(h * LANES, (h + 1) * LANES)
        k_ref[h, :, 0:LANES] = kn[:, sl].astype(BF16)
        k_ref[h, :, LANES:2 * LANES] = krot.astype(BF16)
        v_ref[h, 0:LANES, :] = vt[h * LANES:(h + 1) * LANES, :].astype(BF16)
        v_ref[h, LANES:LANES + V_ONES_ROWS, :] = ones_rows
        q_ref[h, :, 0:LANES] = qq[:, sl].astype(BF16)
        qrot = qq[:, hd + h * LANES:hd + (h + 1) * LANES] * cc \
            + qq[:, 2 * hd + h * LANES:2 * hd + (h + 1) * LANES] * ss
        q_ref[h, :, LANES:2 * LANES] = qrot.astype(BF16)


def _qkv(x, cc, ss, kvg, wdkv, ckvg, wuk, wuv, n1g, sh1, sc1, wdq, cqg, wuq, n_heads, scale, ts):
    b, s, d = x.shape
    full = lambda a: pl.BlockSpec(a.shape, lambda i, j: (0,) * a.ndim)
    hspec = lambda w: pl.BlockSpec((None, n_heads, ts, w), lambda i, j: (i, 0, j, 0))
    return pl.pallas_call(
        functools.partial(_qkv_kernel, n_heads, scale),
        out_shape=[
            jax.ShapeDtypeStruct((b, n_heads, s, 2 * LANES), BF16),
            jax.ShapeDtypeStruct((b, n_heads, s, 2 * LANES), BF16),
            jax.ShapeDtypeStruct((b, n_heads, s // ts, LANES + V_ONES_ROWS, ts), BF16),
        ],
        grid=(b, s // ts),
        in_specs=[
            pl.BlockSpec((None, ts, d), lambda i, j: (i, j, 0)),
            pl.BlockSpec((None, ts, LANES), lambda i, j: (i, j, 0)),
            pl.BlockSpec((None, ts, LANES), lambda i, j: (i, j, 0)),
            full(kvg), full(wdkv), full(ckvg), full(wuk), full(wuv),
            full(n1g), _bvec_spec(d), _bvec_spec(d), full(wdq), full(cqg), full(wuq),
        ],
        out_specs=[hspec(2 * LANES), hspec(2 * LANES),
                   pl.BlockSpec((None, n_heads, None, LANES + V_ONES_ROWS, ts),
                                lambda i, j: (i, 0, j, 0, 0))],
        compiler_params=_cparams(("arbitrary", "arbitrary")),
        name="mla_qkv",
    )(x, cc, ss, kvg, wdkv, ckvg, wuk, wuv, n1g, sh1, sc1, wdq, cqg, wuq)


def _attn_kernel(q_ref, k_ref, vt_ref, o_ref, m_ref, acc_ref):
    tq = q_ref.shape[0]
    tk = vt_ref.shape[2]
    ratio = tq // tk
    dv = o_ref.shape[1]
    qi = pl.program_id(2)
    m_ref[...] = jnp.full(m_ref.shape, -jnp.inf, F32)
    acc_ref[...] = jnp.zeros(acc_ref.shape, F32)

    def scores(ki, cols, on_diagonal):
        start = pl.multiple_of(ki * tk, tk)
        k = k_ref[pl.ds(start, tk), :]
        st = lax.dot_general(k, q_ref[cols, :], (((1,), (1,)), ((), ())),
                             preferred_element_type=F32)
        if on_diagonal:
            kc = lax.broadcasted_iota(I32, st.shape, 0) // CHUNK
            qc = lax.broadcasted_iota(I32, st.shape, 1) // CHUNK
            st = jnp.where(kc <= qc, st, -jnp.inf)
        return st

    def update(ki, st, m_old, acc_old):
        m_new = jnp.maximum(m_old, jnp.max(st, axis=0, keepdims=True))
        p = jnp.exp2(st - m_new).astype(BF16)
        alpha = jnp.exp2(m_old - m_new)
        acc_new = alpha * acc_old + jnp.dot(vt_ref[ki], p, preferred_element_type=F32)
        return m_new, acc_new

    def step(groups):
        sts = [[scores(ki, cols, on_diagonal) for ki, on_diagonal in tiles]
               for cols, tiles in groups]
        for (cols, tiles), group_sts in zip(groups, sts):
            state = (m_ref[:, cols], acc_ref[:, cols])
            for (ki, _), st in zip(tiles, group_sts):
                state = update(ki, st, *state)
            m_ref[:, cols], acc_ref[:, cols] = state

    n_below = qi * ratio
    everything = slice(0, tq)
    step([(slice(g * tk, (g + 1) * tk), [(n_below + a, a == g) for a in range(g + 1)])
          for g in range(ratio)])

    def fast_step(first, count):
        m = m_ref[...]
        pv = None
        top = None
        for t in range(count):
            st = scores(first + t, everything, False)
            tmax = jnp.max(st, axis=0, keepdims=True)
            top = tmax if top is None else jnp.maximum(top, tmax)
            p = jnp.exp2(st - m).astype(BF16)
            part = jnp.dot(vt_ref[first + t], p, preferred_element_type=F32)
            pv = part if pv is None else pv + part
        safe = jnp.max(top - m) <= ATTN_MAX_SLACK

        @pl.when(safe)
        def _():
            acc_ref[...] += pv

        @pl.when(jnp.logical_not(safe))
        def _():
            def redo(t, carry):
                step([(everything, [(first + t, False)])])
                return carry
            lax.fori_loop(0, count, redo, 0)

    def body(j, carry):
        fast_step(ATTN_UNROLL * j, ATTN_UNROLL)
        return carry
    trips = n_below // ATTN_UNROLL
    lax.fori_loop(0, trips, body, 0)

    for rem in range(math.gcd(ratio, ATTN_UNROLL), ATTN_UNROLL, math.gcd(ratio, ATTN_UNROLL)):
        @pl.when(n_below % ATTN_UNROLL == rem)
        def _(rem=rem):
            fast_step(trips * ATTN_UNROLL, rem)

    o_ref[...] = (acc_ref[0:dv, :] / acc_ref[dv:dv + 1, :]).T.astype(o_ref.dtype)


def _attention(q, k, vt, tq):
    b, h, s, dk = q.shape
    nk, dv_ext, tk = vt.shape[2:]
    dv = dv_ext - V_ONES_ROWS
    assert tk % CHUNK == 0 and s % tq == 0 and tq % tk == 0
    return pl.pallas_call(
        _attn_kernel,
        out_shape=jax.ShapeDtypeStruct((b, s, h * dv), BF16),
        grid=(b, h, s // tq),
        in_specs=[
            pl.BlockSpec((None, None, tq, dk), lambda i, j, t: (i, j, t, 0)),
            pl.BlockSpec((None, None, s, dk), lambda i, j, t: (i, j, 0, 0)),
            pl.BlockSpec((None, None, nk, dv_ext, tk), lambda i, j, t: (i, j, 0, 0, 0)),
        ],
        out_specs=pl.BlockSpec((None, tq, dv), lambda i, j, t: (i, t, j)),
        scratch_shapes=[pltpu.VMEM((1, tq), F32), pltpu.VMEM((dv_ext, tq), F32)],
        compiler_params=_cparams(("arbitrary", "arbitrary", "arbitrary")),
        name="mla_attention",
    )(q, k, vt)


def _swap_halves(w):
    half = w.shape[-1] // 2
    return jnp.concatenate([w[..., half:], w[..., :half]], axis=-1)


def _pad_lanes(w):
    pad = LANES - w.shape[-1]
    return jnp.concatenate([w, jnp.zeros(w.shape[:-1] + (pad,), w.dtype)], axis=-1)


def kernel(x, c, positions, mod_w, mod_b, norm1_g, norm2_g, conv_pw1_w, conv_pw1_b, conv_dw_w, conv_dw_b, conv_ln_g, conv_ln_b, conv_pw2_w, conv_pw2_b, kv_norm_g, w_dkv, ckv_norm_g, w_uk, w_uv, w_dq, cq_norm_g, w_uq, w_o, router_w, router_b, exp_w_gu, exp_b_gu, exp_w_dn, exp_b_dn, final_g):
    b, s, d = x.shape
    n_heads, nope = w_uk.shape[1], w_uk.shape[2]
    r_kv = ckv_norm_g.shape[0]
    rope = w_dkv.shape[1] - r_kv
    vdim = w_uv.shape[2]
    n_exp = router_w.shape[2]
    assert nope == LANES and vdim == LANES and rope <= LANES and d % LANES == 0
    ts = min(512, s)
    tq = min(1024, s)

    mod = _modulation(c, mod_w, mod_b)
    mods = [[m.reshape(b, 1, d) for m in jnp.split(mod[l], 6, axis=-1)] for l in range(2)]
    row = lambda v: v.reshape(1, -1)

    half = rope // 2
    inv = jnp.exp(-(2.0 * math.log(ROPE_THETA) / rope) * jnp.arange(half, dtype=F32))
    ang = positions.astype(F32)[..., None] * inv
    cos, sin = jnp.cos(ang), jnp.sin(ang)
    cc = _pad_lanes(jnp.concatenate([cos, cos], axis=-1))
    ss = _pad_lanes(jnp.concatenate([-sin, sin], axis=-1))

    sh1, sc1, gt1, sh2, sc2, gt2 = mods[0]
    x1, hn2, idxt, route, cnt = _mix0(
        x, row(norm1_g[0]), sh1, sc1, gt1,
        conv_pw1_w[0].astype(BF16), row(conv_pw1_b[0]), conv_dw_w[0], row(conv_dw_b[0]),
        row(conv_ln_g[0]), row(conv_ln_b[0]), conv_pw2_w[0].astype(BF16), row(conv_pw2_b[0]),
        row(norm2_g[0]), sh2, sc2, router_w[0].T, router_b[0].reshape(n_exp, 1), ts)
    cnt = cnt[:, :ts // MOE_TILE].reshape(-1, n_exp)
    x2 = _moe(x1, hn2, idxt, route, cnt, gt2, row(final_g),
              0, exp_w_gu, exp_b_gu, exp_w_dn, exp_b_dn, False)

    sh1, sc1, gt1, sh2, sc2, gt2 = mods[1]
    wdkv_rope = w_dkv[:, r_kv:]
    wdkv_ext = jnp.concatenate(
        [w_dkv[:, :r_kv], _pad_lanes(wdkv_rope), _pad_lanes(_swap_halves(wdkv_rope))], axis=-1)
    wuq = w_uq[0]
    r_q = wuq.shape[0]
    wuq_rope = wuq[:, :, nope:]
    wuq_ext = jnp.concatenate([
        wuq[:, :, :nope].reshape(r_q, n_heads * LANES),
        _pad_lanes(wuq_rope).reshape(r_q, n_heads * LANES),
        _pad_lanes(_swap_halves(wuq_rope)).reshape(r_q, n_heads * LANES)], axis=-1)
    scale = float((nope + rope) ** -0.5 * math.log2(math.e))
    q, k, v = _qkv(
        x2, cc, ss, row(kv_norm_g), wdkv_ext.astype(BF16), row(ckv_norm_g),
        w_uk.reshape(r_kv, n_heads * nope).astype(BF16), w_uv.reshape(r_kv, n_heads * vdim).T.astype(BF16),
        row(norm1_g[1]), sh1, sc1, w_dq[0].astype(BF16), row(cq_norm_g[0]), wuq_ext.astype(BF16),
        n_heads, scale, ts)
    o = _attention(q, k, v, tq)

    x3, hn2, idxt, route, cnt = _mix1(
        x2, o, gt1, w_o[0].astype(BF16), row(norm2_g[1]), sh2, sc2,
        router_w[1].T, router_b[1].reshape(n_exp, 1), ts)
    cnt = cnt[:, :ts // MOE_TILE].reshape(-1, n_exp)
    return _moe(x3, hn2, idxt, route, cnt, gt2, row(final_g),
                1, exp_w_gu, exp_b_gu, exp_w_dn, exp_b_dn, True)
```

```python
import functools
import math

import jax
import jax.numpy as jnp
from jax import lax
from jax.experimental import pallas as pl
from jax.experimental.pallas import tpu as pltpu

CHUNK = 64
TOP_K = 4
ROPE_THETA = 10000.0
SWIGLU_ALPHA = 1.702
SWIGLU_LIMIT = 7.0
EPS = 1e-6

LANES = 128
SUBLANES = 8
VMEM_LIMIT_BYTES = 56 * 1024 * 1024

ROW_ALIGN = SUBLANES
MOE_TILE = 256
EXPERT_ROWS = 512
CONV_HALO = 32
V_ONES_ROWS = 2 * SUBLANES
ATTN_UNROLL = 4
ATTN_MAX_SLACK = 60.0

F32 = jnp.float32
BF16 = jnp.bfloat16
I32 = jnp.int32


def _cparams(sem):
    return pltpu.CompilerParams(dimension_semantics=sem, vmem_limit_bytes=VMEM_LIMIT_BYTES)


def _rms(x, g):
    return x * lax.rsqrt(jnp.mean(x * x, axis=-1, keepdims=True) + EPS) * g


def _round_up(a, m):
    return (a + m - 1) // m * m


def _mod_kernel(c_ref, w_ref, b_ref, o_ref):
    c = c_ref[...]
    ca = c * jax.nn.sigmoid(c)
    o_ref[...] = jnp.dot(ca, w_ref[...], preferred_element_type=F32) + b_ref[...]


def _modulation(c, mod_w, mod_b):
    depth, d, d6 = mod_w.shape
    b = c.shape[0]
    bp = _round_up(b, SUBLANES)
    cp = jnp.zeros((bp, d), F32).at[:b].set(c)
    tn = d6 // 4
    out = pl.pallas_call(
        _mod_kernel,
        out_shape=jax.ShapeDtypeStruct((depth, bp, d6), F32),
        grid=(depth, d6 // tn),
        in_specs=[
            pl.BlockSpec((bp, d), lambda l, j: (0, 0)),
            pl.BlockSpec((None, d, tn), lambda l, j: (l, 0, j)),
            pl.BlockSpec((None, 1, tn), lambda l, j: (l, 0, j)),
        ],
        out_specs=pl.BlockSpec((None, bp, tn), lambda l, j: (l, 0, j)),
        compiler_params=_cparams(("arbitrary", "arbitrary")),
        name="adaln_modulation",
    )(cp, mod_w, mod_b.reshape(depth, 1, d6))
    return out[:, :b]


def _pre_moe(x1, g2, sh2, sc2, rwt, rb, hn_ref, idxt_ref, route_ref, cnt_ref):
    ts = x1.shape[0]
    n_exp = rwt.shape[0]
    hn = _rms(x1, g2) * (1.0 + sc2) + sh2
    hn_ref[...] = hn.astype(BF16)
    logits = lax.dot_general(rwt, hn, (((1,), (1,)), ((), ())),
                             precision=lax.Precision.HIGHEST,
                             preferred_element_type=F32) + rb
    e_iota = lax.broadcasted_iota(I32, (n_exp, ts), 0)
    vals, idxs = [], []
    cur = logits
    for _ in range(TOP_K):
        m = jnp.max(cur, axis=0, keepdims=True)
        i = jnp.min(jnp.where(cur == m, e_iota, n_exp), axis=0, keepdims=True)
        vals.append(m)
        idxs.append(i)
        cur = jnp.where(e_iota == i, -jnp.inf, cur)
    exps = [jnp.exp(v - vals[0]) for v in vals]
    den = exps[0]
    for e in exps[1:]:
        den = den + e
    gates = [e / den for e in exps]
    idxt_ref[...] = jnp.concatenate(idxs, axis=0)
    rows = jnp.concatenate([i.astype(F32) for i in idxs] + gates
                           + [jnp.zeros((LANES - 2 * TOP_K, ts), F32)], axis=0)
    route_ref[...] = rows.T
    onehot = jnp.zeros((n_exp, ts), F32)
    for i in idxs:
        onehot = onehot + (e_iota == i).astype(F32)
    sel = (lax.broadcasted_iota(I32, (SUBLANES, ts), 1) // MOE_TILE
           == lax.broadcasted_iota(I32, (SUBLANES, ts), 0)).astype(BF16)
    cnt = lax.dot_general(sel, onehot.astype(BF16), (((1,), (1,)), ((), ())),
                          preferred_element_type=F32)
    cnt_ref[...] = cnt.astype(I32)


def _pre_moe_specs(b, s, d, ts, n_exp):
    ns = s // ts
    out_shape = [
        jax.ShapeDtypeStruct((b, s, d), F32),
        jax.ShapeDtypeStruct((b, s, d), BF16),
        jax.ShapeDtypeStruct((TOP_K, b * s), I32),
        jax.ShapeDtypeStruct((b * s, LANES), F32),
        jax.ShapeDtypeStruct((b * ns, SUBLANES, n_exp), I32),
    ]
    out_specs = [
        pl.BlockSpec((None, ts, d), lambda i, j: (i, j, 0)),
        pl.BlockSpec((None, ts, d), lambda i, j: (i, j, 0)),
        pl.BlockSpec((TOP_K, ts), lambda i, j: (0, i * ns + j)),
        pl.BlockSpec((ts, LANES), lambda i, j: (i * ns + j, 0)),
        pl.BlockSpec((None, SUBLANES, n_exp), lambda i, j: (i * ns + j, 0, 0)),
    ]
    return out_shape, out_specs


def _vec_spec(d):
    return pl.BlockSpec((1, d), lambda i, j: (0, 0))


def _bvec_spec(d):
    return pl.BlockSpec((None, 1, d), lambda i, j: (i, 0, 0))


def _mix0_kernel(x_ref, n1g_ref, sh1_ref, sc1_ref, gt1_ref,
                 pw1w_ref, pw1b_ref, dww_ref, dwb_ref, lng_ref, lnb_ref,
                 pw2w_ref, pw2b_ref, n2g_ref, sh2_ref, sc2_ref, rwt_ref, rb_ref,
                 x1_ref, hn_ref, idxt_ref, route_ref, cnt_ref, buf_ref, conv_ref, shift_ref):
    ts, d = x_ref.shape
    width = dww_ref.shape[0]
    x = x_ref[...]
    hn = _rms(x, n1g_ref[...]) * (1.0 + sc1_ref[...]) + sh1_ref[...]
    hb = hn.astype(BF16)
    a = jnp.dot(hb, pw1w_ref[:, :d], preferred_element_type=F32) + pw1b_ref[:, :d]
    g = jnp.dot(hb, pw1w_ref[:, d:], preferred_element_type=F32) + pw1b_ref[:, d:]
    glu = a * jax.nn.sigmoid(g)

    @pl.when(pl.program_id(1) == 0)
    def _():
        buf_ref[0:CONV_HALO, :] = jnp.zeros((CONV_HALO, d), F32)

    buf_ref[CONV_HALO:, :] = glu
    base = CONV_HALO - (width - 1)
    rc = 32
    lc = min(512, d)
    sh_rows = shift_ref.shape[1]
    for c0 in range(0, d, lc):
        for r in range(1, SUBLANES):
            shift_ref[r - 1] = buf_ref[r:r + sh_rows, c0:c0 + lc]
        for r0 in range(0, ts, rc):
            acc = jnp.zeros((rc, lc), F32)
            for k in range(width):
                q8, r = divmod(base + k, SUBLANES)
                lo = q8 * SUBLANES + r0
                if r == 0:
                    win = buf_ref[lo:lo + rc, c0:c0 + lc]
                else:
                    win = shift_ref[r - 1, lo:lo + rc, :]
                acc = acc + dww_ref[k:k + 1, c0:c0 + lc] * win
            conv_ref[r0:r0 + rc, c0:c0 + lc] = acc
    buf_ref[0:CONV_HALO, :] = buf_ref[ts:ts + CONV_HALO, :]
    u = conv_ref[...] + dwb_ref[...]
    mu = jnp.mean(u, axis=-1, keepdims=True)
    dlt = u - mu
    var = jnp.mean(dlt * dlt, axis=-1, keepdims=True)
    u = dlt * lax.rsqrt(var + EPS) * lng_ref[...] + lnb_ref[...]
    u = u * jax.nn.sigmoid(u)
    y = jnp.dot(u.astype(BF16), pw2w_ref[...], preferred_element_type=F32) + pw2b_ref[...]
    x1 = x + gt1_ref[...] * y
    x1_ref[...] = x1
    _pre_moe(x1, n2g_ref[...], sh2_ref[...], sc2_ref[...], rwt_ref[...], rb_ref[...],
             hn_ref, idxt_ref, route_ref, cnt_ref)


def _mix0(x, n1g, sh1, sc1, gt1, pw1w, pw1b, dww, dwb, lng, lnb, pw2w, pw2b,
          n2g, sh2, sc2, rwt, rb, ts):
    b, s, d = x.shape
    n_exp = rwt.shape[0]
    width = dww.shape[0]
    assert width - 1 <= CONV_HALO and ts % MOE_TILE == 0 and ts // MOE_TILE <= SUBLANES
    out_shape, out_specs = _pre_moe_specs(b, s, d, ts, n_exp)
    full = lambda shp: pl.BlockSpec(shp, lambda i, j: (0,) * len(shp))
    return pl.pallas_call(
        _mix0_kernel,
        out_shape=out_shape,
        grid=(b, s // ts),
        in_specs=[
            pl.BlockSpec((None, ts, d), lambda i, j: (i, j, 0)),
            _vec_spec(d), _bvec_spec(d), _bvec_spec(d), _bvec_spec(d),
            full((d, 2 * d)), full((1, 2 * d)), full((width, d)), full((1, d)),
            full((1, d)), full((1, d)), full((d, d)), full((1, d)),
            _vec_spec(d), _bvec_spec(d), _bvec_spec(d),
            full((n_exp, d)), full((n_exp, 1)),
        ],
        out_specs=out_specs,
        scratch_shapes=[
            pltpu.VMEM((CONV_HALO + ts, d), F32),
            pltpu.VMEM((ts, d), F32),
            pltpu.VMEM((SUBLANES - 1, CONV_HALO + ts - SUBLANES, min(512, d)), F32),
        ],
        compiler_params=_cparams(("arbitrary", "arbitrary")),
        name="conformer_mixer",
    )(x, n1g, sh1, sc1, gt1, pw1w, pw1b, dww, dwb, lng, lnb, pw2w, pw2b,
      n2g, sh2, sc2, rwt, rb)


def _mix1_kernel(x_ref, o_ref, gt1_ref, wo_ref, n2g_ref, sh2_ref, sc2_ref, rwt_ref, rb_ref,
                 x1_ref, hn_ref, idxt_ref, route_ref, cnt_ref):
    y = jnp.dot(o_ref[...], wo_ref[...], preferred_element_type=F32)
    x1 = x_ref[...] + gt1_ref[...] * y
    x1_ref[...] = x1
    _pre_moe(x1, n2g_ref[...], sh2_ref[...], sc2_ref[...], rwt_ref[...], rb_ref[...],
             hn_ref, idxt_ref, route_ref, cnt_ref)


def _mix1(x, o, gt1, wo, n2g, sh2, sc2, rwt, rb, ts):
    b, s, d = x.shape
    n_exp = rwt.shape[0]
    do = o.shape[-1]
    assert ts % MOE_TILE == 0 and ts // MOE_TILE <= SUBLANES
    out_shape, out_specs = _pre_moe_specs(b, s, d, ts, n_exp)
    full = lambda shp: pl.BlockSpec(shp, lambda i, j: (0,) * len(shp))
    return pl.pallas_call(
        _mix1_kernel,
        out_shape=out_shape,
        grid=(b, s // ts),
        in_specs=[
            pl.BlockSpec((None, ts, d), lambda i, j: (i, j, 0)),
            pl.BlockSpec((None, ts, do), lambda i, j: (i, j, 0)),
            _bvec_spec(d), full((do, d)),
            _vec_spec(d), _bvec_spec(d), _bvec_spec(d),
            full((n_exp, d)), full((n_exp, 1)),
        ],
        out_specs=out_specs,
        compiler_params=_cparams(("arbitrary", "arbitrary")),
        name="attn_out_mixer",
    )(x, o, gt1, wo, n2g, sh2, sc2, rwt, rb)


def _moe_dims(n_tok, n_exp):
    n_tiles = n_tok // MOE_TILE
    loc_rows = _round_up(TOP_K * MOE_TILE + n_exp * (ROW_ALIGN - 1), LANES)
    max_rows = TOP_K * n_tok + n_tiles * n_exp * (ROW_ALIGN - 1) + n_exp * (EXPERT_ROWS - ROW_ALIGN)
    max_blocks = -(-max_rows // EXPERT_ROWS)
    return loc_rows, max_blocks


def _routing_tables(cnt, max_blocks):
    n_tiles, n_exp = cnt.shape
    cnt8 = _round_up(cnt, ROW_ALIGN)
    off = jnp.cumsum(cnt8, axis=1) - cnt8
    seg_len = cnt8.sum(axis=0)
    seg_pad = _round_up(seg_len, EXPERT_ROWS)
    seg_end = jnp.cumsum(seg_pad)
    seg_start = seg_end - seg_pad
    run_start = seg_start[None, :] + jnp.cumsum(cnt8, axis=0) - cnt8
    n_blocks = (seg_end[-1] // EXPERT_ROWS).astype(I32)
    blk_row = jnp.arange(max_blocks, dtype=I32) * EXPERT_ROWS
    blk_row = jnp.minimum(blk_row, seg_end[-1] - EXPERT_ROWS)
    blk_exp = jnp.minimum((blk_row[:, None] >= seg_end[None, :]).sum(axis=-1), n_exp - 1).astype(I32)
    blk = jnp.arange(max_blocks, dtype=I32)
    prev_exp = jnp.concatenate([jnp.full((1,), -1, I32), blk_exp[:-1]])
    blk_first = ((blk_exp != prev_exp) & (blk < n_blocks)).astype(I32)
    blk_slot = ((jnp.cumsum(blk_first) - 1) % 2).astype(I32)
    e_ids = jnp.arange(n_exp, dtype=I32)
    later = (e_ids[None, :] > e_ids[:, None]) & (seg_pad[None, :] > 0)
    next_exp = jnp.min(jnp.where(later, e_ids[None, :], n_exp), axis=1)
    next_exp = jnp.where(next_exp < n_exp, next_exp, -1)
    blk_next = jnp.sum(jnp.where(blk_exp[:, None] == e_ids[None, :], next_exp[None, :], 0), axis=1)
    flat = lambda a: a.reshape(-1).astype(I32)
    return dict(
        run_loc=flat(off),
        run_glb=flat(run_start),
        run_len=flat(cnt8),
        rows_used=flat(cnt8.sum(axis=1)),
        tail_glb=flat(seg_start + seg_len),
        tail_len=flat(seg_pad - seg_len),
        blk_exp=blk_exp, blk_first=blk_first, blk_slot=blk_slot, blk_next=flat(blk_next),
        n_blocks=n_blocks.reshape(1))


def _aligned(v):
    return pl.multiple_of(v, ROW_ALIGN)


def _dispatch_kernel(n_exp, loc0_ref, glb0_ref, len_ref, used_ref, tail0_ref, taillen_ref, nblk_ref,
                     x_ref, idxt_ref, xs_ref, loc_ref, zero_ref, sem, zsem):
    i = pl.program_id(0)
    n = pl.num_programs(0)
    slot = i % 2
    tile = x_ref.shape[0]
    loc_rows = loc_ref.shape[1]

    def wait_slot(sl, rows):
        rows = _aligned(rows)

        @pl.when(rows > 0)
        def _():
            pltpu.make_async_copy(loc_ref.at[sl, pl.ds(0, rows)], xs_ref.at[pl.ds(0, rows)],
                                  sem.at[sl]).wait()

    @pl.when(i >= 2)
    def _():
        wait_slot(slot, used_ref[i - 2])

    idx = idxt_ref[...]
    e_iota = lax.broadcasted_iota(I32, (n_exp, tile), 0)
    hits = [idx[k:k + 1, :] == e_iota for k in range(TOP_K)]
    onehot = jnp.zeros((n_exp, tile), F32)
    for h in hits:
        onehot = onehot + h.astype(F32)
    upper = (lax.broadcasted_iota(I32, (tile, tile), 0)
             < lax.broadcasted_iota(I32, (tile, tile), 1)).astype(BF16)
    pre = jnp.dot(onehot.astype(BF16), upper, preferred_element_type=F32)
    cnt = jnp.sum(onehot, axis=1, keepdims=True)
    cnt8 = jnp.ceil(cnt / ROW_ALIGN) * ROW_ALIGN
    lower = (lax.broadcasted_iota(I32, (n_exp, n_exp), 1)
             < lax.broadcasted_iota(I32, (n_exp, n_exp), 0)).astype(BF16)
    off = jnp.dot(lower, jnp.broadcast_to(cnt8, (n_exp, LANES)).astype(BF16),
                  preferred_element_type=F32)[:, 0:1]
    base = off + pre
    r_iota = lax.broadcasted_iota(I32, (loc_rows, tile), 0)
    perm = jnp.zeros((loc_rows, tile), F32)
    for h in hits:
        dest = jnp.sum(jnp.where(h, base, 0.0), axis=0, keepdims=True).astype(I32)
        perm = perm + (r_iota == dest).astype(F32)
    loc_ref[slot] = jnp.dot(perm.astype(BF16), x_ref[...], preferred_element_type=F32)

    def issue_run(e, carry):
        rows = _aligned(len_ref[i * n_exp + e])

        @pl.when(rows > 0)
        def _():
            pltpu.make_async_copy(
                loc_ref.at[slot, pl.ds(_aligned(loc0_ref[i * n_exp + e]), rows)],
                xs_ref.at[pl.ds(_aligned(glb0_ref[i * n_exp + e]), rows)],
                sem.at[slot]).start()
        return carry
    lax.fori_loop(0, n_exp, issue_run, 0)

    @pl.when(i == n - 1)
    def _():
        zero_ref[...] = jnp.zeros(zero_ref.shape, F32)
        max_blocks = xs_ref.shape[0] // EXPERT_ROWS

        def zero_tail(e, total):
            rows = _aligned(taillen_ref[e])

            @pl.when(rows > 0)
            def _():
                pltpu.make_async_copy(zero_ref.at[pl.ds(0, rows)],
                                      xs_ref.at[pl.ds(_aligned(tail0_ref[e]), rows)],
                                      zsem.at[0]).start()
            return total + rows
        tail_rows = lax.fori_loop(0, n_exp, zero_tail, jnp.int32(0))

        def zero_block(blk, carry):
            pltpu.make_async_copy(
                zero_ref,
                xs_ref.at[pl.ds(pl.multiple_of(blk * EXPERT_ROWS, EXPERT_ROWS), EXPERT_ROWS)],
                zsem.at[1]).start()
            return carry
        lax.fori_loop(nblk_ref[0], max_blocks, zero_block, 0)

        @pl.when(i >= 1)
        def _():
            wait_slot(1 - slot, used_ref[i - 1])
        wait_slot(slot, used_ref[i])

        def wait_rows(rows, s):
            rows = _aligned(rows)

            @pl.when(rows > 0)
            def _():
                pltpu.make_async_copy(xs_ref.at[pl.ds(0, rows)], xs_ref.at[pl.ds(0, rows)],
                                      zsem.at[s]).wait()
        wait_rows(tail_rows, 0)
        wait_rows((max_blocks - nblk_ref[0]) * EXPERT_ROWS, 1)


def _dispatch(hn2, idxt, tabs, n_exp, loc_rows, max_blocks):
    n_tok, d = hn2.shape
    n_tiles = n_tok // MOE_TILE
    return pl.pallas_call(
        functools.partial(_dispatch_kernel, n_exp),
        out_shape=jax.ShapeDtypeStruct((max_blocks * EXPERT_ROWS, d), F32),
        grid_spec=pltpu.PrefetchScalarGridSpec(
            num_scalar_prefetch=7,
            grid=(n_tiles,),
            in_specs=[
                pl.BlockSpec((MOE_TILE, d), lambda i, *_: (i, 0)),
                pl.BlockSpec((TOP_K, MOE_TILE), lambda i, *_: (0, i)),
            ],
            out_specs=pl.BlockSpec(memory_space=pl.ANY),
            scratch_shapes=[
                pltpu.VMEM((2, loc_rows, d), F32),
                pltpu.VMEM((EXPERT_ROWS, d), F32),
                pltpu.SemaphoreType.DMA((2,)),
                pltpu.SemaphoreType.DMA((2,)),
            ],
        ),
        compiler_params=_cparams(("arbitrary",)),
        name="moe_dispatch",
    )(tabs["run_loc"], tabs["run_glb"], tabs["run_len"], tabs["rows_used"],
      tabs["tail_glb"], tabs["tail_len"], tabs["n_blocks"], hn2, idxt)


def _expert_kernel(layer, be_ref, first_ref, slot_ref, next_ref, nb_ref,
                   x_ref, wgu_hbm, bgu_ref, wdn_hbm, bdn_ref, y_ref,
                   wgu_f32, wdn_f32, wgu_bf, wdn_bf, sem):
    de = wdn_bf.shape[0]
    b = pl.program_id(0)

    def weight_copies(e, s):
        return (pltpu.make_async_copy(wgu_hbm.at[layer, e], wgu_f32.at[s], sem.at[s, 0]),
                pltpu.make_async_copy(wdn_hbm.at[layer, e], wdn_f32.at[s], sem.at[s, 1]))

    @pl.when(b == 0)
    def _():
        for cp in weight_copies(be_ref[0], 0):
            cp.start()

    @pl.when(first_ref[b] == 1)
    def _():
        s = slot_ref[b]
        for cp in weight_copies(be_ref[b], s):
            cp.wait()

        @pl.when(next_ref[b] >= 0)
        def _():
            for cp in weight_copies(next_ref[b], 1 - s):
                cp.start()
        cw = 512
        for c0 in range(0, wgu_bf.shape[1], cw):
            wgu_bf[:, c0:c0 + cw] = wgu_f32[s, :, c0:c0 + cw].astype(BF16)
        for c0 in range(0, wdn_bf.shape[1], cw):
            wdn_bf[:, c0:c0 + cw] = wdn_f32[s, :, c0:c0 + cw].astype(BF16)

    @pl.when(b < nb_ref[0])
    def _():
        xb = x_ref[...].astype(BF16)
        g = jnp.dot(xb, wgu_bf[:, :de], preferred_element_type=F32) + bgu_ref[:, :de]
        u = jnp.dot(xb, wgu_bf[:, de:], preferred_element_type=F32) + bgu_ref[:, de:]
        g = jnp.minimum(g, SWIGLU_LIMIT)
        u = jnp.clip(u, -SWIGLU_LIMIT, SWIGLU_LIMIT)
        a = (u + 1.0) * g * jax.nn.sigmoid(SWIGLU_ALPHA * g)
        y_ref[...] = jnp.dot(a.astype(BF16), wdn_bf[...], preferred_element_type=F32) + bdn_ref[...]

    @pl.when(pl.program_id(0) >= nb_ref[0])
    def _():
        y_ref[...] = jnp.zeros(y_ref.shape, y_ref.dtype)


def _experts(xs, tabs, layer, wgu, bgu, wdn, bdn):
    rows, d = xs.shape
    depth, n_exp, _, de2 = wgu.shape
    de = de2 // 2
    max_blocks = rows // EXPERT_ROWS
    row_map = lambda b, be, fi, sl, nx, nb: (jnp.minimum(b, nb[0] - 1), 0)
    exp_map = lambda b, be, fi, sl, nx, nb: (layer, be[b], 0, 0)
    return pl.pallas_call(
        functools.partial(_expert_kernel, layer),
        out_shape=jax.ShapeDtypeStruct((rows, d), F32),
        grid_spec=pltpu.PrefetchScalarGridSpec(
            num_scalar_prefetch=5,
            grid=(max_blocks,),
            in_specs=[
                pl.BlockSpec((EXPERT_ROWS, d), row_map),
                pl.BlockSpec(memory_space=pl.ANY),
                pl.BlockSpec((None, None, 1, de2), exp_map),
                pl.BlockSpec(memory_space=pl.ANY),
                pl.BlockSpec((None, None, 1, d), exp_map),
            ],
            out_specs=pl.BlockSpec((EXPERT_ROWS, d), lambda b, *_: (b, 0)),
            scratch_shapes=[
                pltpu.VMEM((2, d, de2), F32), pltpu.VMEM((2, de, d), F32),
                pltpu.VMEM((d, de2), BF16), pltpu.VMEM((de, d), BF16),
                pltpu.SemaphoreType.DMA((2, 2)),
            ],
        ),
        compiler_params=_cparams(("arbitrary",)),
        name="moe_experts",
    )(tabs["blk_exp"], tabs["blk_first"], tabs["blk_slot"], tabs["blk_next"], tabs["n_blocks"],
      xs, wgu, bgu.reshape(depth, n_exp, 1, de2), wdn, bdn.reshape(depth, n_exp, 1, d))


def _combine_kernel(final_norm, n_exp, loc0_ref, glb0_ref, len_ref, used_ref,
                    route_ref, x1_ref, g2_ref, fg_ref, ys_ref, out_ref, loc_ref, sem):
    i = pl.program_id(0)
    n = pl.num_programs(0)
    slot = i % 2
    tile = x1_ref.shape[0]
    loc_rows = loc_ref.shape[1]

    def issue_tile(t, sl):
        def issue_run(e, carry):
            rows = _aligned(len_ref[t * n_exp + e])

            @pl.when(rows > 0)
            def _():
                pltpu.make_async_copy(
                    ys_ref.at[pl.ds(_aligned(glb0_ref[t * n_exp + e]), rows)],
                    loc_ref.at[sl, pl.ds(_aligned(loc0_ref[t * n_exp + e]), rows)],
                    sem.at[sl]).start()
            return carry
        lax.fori_loop(0, n_exp, issue_run, 0)

    @pl.when(i == 0)
    def _():
        issue_tile(0, 0)

    @pl.when(i + 1 < n)
    def _():
        issue_tile(i + 1, 1 - slot)

    route = route_ref[...]
    e_iota = lax.broadcasted_iota(I32, (tile, n_exp), 1)
    hits = [route[:, k:k + 1].astype(I32) == e_iota for k in range(TOP_K)]
    gates = [route[:, TOP_K + k:TOP_K + k + 1] for k in range(TOP_K)]
    onehot = jnp.zeros((tile, n_exp), F32)
    for h in hits:
        onehot = onehot + h.astype(F32)
    lower = (lax.broadcasted_iota(I32, (tile, tile), 1)
             < lax.broadcasted_iota(I32, (tile, tile), 0)).astype(BF16)
    pre = jnp.dot(lower, onehot.astype(BF16), preferred_element_type=F32)
    cnt = jnp.sum(onehot, axis=0, keepdims=True)
    cnt8 = jnp.ceil(cnt / ROW_ALIGN) * ROW_ALIGN
    upper = (lax.broadcasted_iota(I32, (n_exp, n_exp), 0)
             < lax.broadcasted_iota(I32, (n_exp, n_exp), 1)).astype(BF16)
    off = jnp.dot(jnp.broadcast_to(cnt8, (SUBLANES, n_exp)).astype(BF16), upper,
                  preferred_element_type=F32)[0:1, :]
    base = off + pre
    r_iota = lax.broadcasted_iota(I32, (tile, loc_rows), 1)
    comb = jnp.zeros((tile, loc_rows), F32)
    for h, gk in zip(hits, gates):
        dest = jnp.sum(jnp.where(h, base, 0.0), axis=1, keepdims=True).astype(I32)
        comb = comb + jnp.where(r_iota == dest, gk, 0.0)

    used_rows = _aligned(used_ref[i])

    @pl.when(used_rows > 0)
    def _():
        pltpu.make_async_copy(ys_ref.at[pl.ds(0, used_rows)],
                              loc_ref.at[slot, pl.ds(0, used_rows)], sem.at[slot]).wait()

    row_ok = lax.broadcasted_iota(I32, (loc_rows, 1), 0) < used_rows
    yl = jnp.where(row_ok, loc_ref[slot], 0.0).astype(BF16)
    moe = jnp.dot(comb.astype(BF16), yl, preferred_element_type=F32)
    out = x1_ref[...] + g2_ref[...] * moe
    if final_norm:
        out = _rms(out, fg_ref[...])
    out_ref[...] = out


def _combine(ys, route, x1, gate2, final_g, tabs, n_exp, loc_rows, final_norm):
    b, s, d = x1.shape
    n_tok = b * s
    n_tiles = n_tok // MOE_TILE
    tiles_per_seq = s // MOE_TILE
    out = pl.pallas_call(
        functools.partial(_combine_kernel, final_norm, n_exp),
        out_shape=jax.ShapeDtypeStruct((n_tok, d), F32),
        grid_spec=pltpu.PrefetchScalarGridSpec(
            num_scalar_prefetch=4,
            grid=(n_tiles,),
            in_specs=[
                pl.BlockSpec((MOE_TILE, LANES), lambda i, *_: (i, 0)),
                pl.BlockSpec((MOE_TILE, d), lambda i, *_: (i, 0)),
                pl.BlockSpec((None, 1, d), lambda i, *_: (i // tiles_per_seq, 0, 0)),
                pl.BlockSpec((1, d), lambda i, *_: (0, 0)),
                pl.BlockSpec(memory_space=pl.ANY),
            ],
            out_specs=pl.BlockSpec((MOE_TILE, d), lambda i, *_: (i, 0)),
            scratch_shapes=[
                pltpu.VMEM((2, loc_rows, d), F32),
                pltpu.SemaphoreType.DMA((2,)),
            ],
        ),
        compiler_params=_cparams(("arbitrary",)),
        name="moe_combine",
    )(tabs["run_loc"], tabs["run_glb"], tabs["run_len"], tabs["rows_used"],
      route, x1.reshape(n_tok, d), gate2, final_g, ys)
    return out.reshape(b, s, d)


def _moe(x1, hn2, idxt, route, cnt, gate2, final_g, layer, wgu, bgu, wdn, bdn, final_norm):
    b, s, d = x1.shape
    n_tok = b * s
    n_exp = wgu.shape[1]
    loc_rows, max_blocks = _moe_dims(n_tok, n_exp)
    tabs = _routing_tables(cnt, max_blocks)
    xs = _dispatch(hn2.reshape(n_tok, d), idxt, tabs, n_exp, loc_rows, max_blocks)
    ys = _experts(xs, tabs, layer, wgu, bgu, wdn, bdn)
    return _combine(ys, route, x1, gate2, final_g, tabs, n_exp, loc_rows, final_norm)


def _qkv_kernel(n_heads, scale, x_ref, cc_ref, ss_ref, kvg_ref, wdkv_ref, ckvg_ref, wuk_ref, wuv_ref,
                n1g_ref, sh1_ref, sc1_ref, wdq_ref, cqg_ref, wuq_ref,
                q_ref, k_ref, v_ref):
    x = x_ref[...]
    cc = cc_ref[...]
    ss = ss_ref[...]
    r_kv = ckvg_ref.shape[1]
    hk = _rms(x, kvg_ref[...]).astype(BF16)
    lat = jnp.dot(hk, wdkv_ref[...], preferred_element_type=F32)
    ckv = _rms(lat[:, :r_kv], ckvg_ref[...]).astype(BF16)
    krot = lat[:, r_kv:r_kv + LANES] * cc + lat[:, r_kv + LANES:r_kv + 2 * LANES] * ss
    kn = jnp.dot(ckv, wuk_ref[...], preferred_element_type=F32)
    vt = lax.dot_general(wuv_ref[...], ckv, (((1,), (1,)), ((), ())), preferred_element_type=F32)
    hq = (_rms(x, n1g_ref[...]) * (1.0 + sc1_ref[...]) + sh1_ref[...]).astype(BF16)
    cq = _rms(jnp.dot(hq, wdq_ref[...], preferred_element_type=F32), cqg_ref[...]).astype(BF16)
    qq = jnp.dot(cq, wuq_ref[...], preferred_element_type=F32) * scale
    hd = n_heads * LANES
    ts = x.shape[0]
    ones_rows = (lax.broadcasted_iota(I32, (V_ONES_ROWS, ts), 0) == 0).astype(BF16)
    for h in range(n_heads):
        sl = slice(h * LANES, (h + 1) * LANES)
        k_ref[h, :, 0:LANES] = kn[:, sl].astype(BF16)
        k_ref[h, :, LANES:2 * LANES] = krot.astype(BF16)
        v_ref[h, 0:LANES, :] = vt[h * LANES:(h + 1) * LANES, :].astype(BF16)
        v_ref[h, LANES:LANES + V_ONES_ROWS, :] = ones_rows
        q_ref[h, :, 0:LANES] = qq[:, sl].astype(BF16)
        qrot = qq[:, hd + h * LANES:hd + (h + 1) * LANES] * cc \
            + qq[:, 2 * hd + h * LANES:2 * hd + (h + 1) * LANES] * ss
        q_ref[h, :, LANES:2 * LANES] = qrot.astype(BF16)


def _qkv(x, cc, ss, kvg, wdkv, ckvg, wuk, wuv, n1g, sh1, sc1, wdq, cqg, wuq, n_heads, scale, ts):
    b, s, d = x.shape
    full = lambda a: pl.BlockSpec(a.shape, lambda i, j: (0,) * a.ndim)
    hspec = lambda w: pl.BlockSpec((None, n_heads, ts, w), lambda i, j: (i, 0, j, 0))
    return pl.pallas_call(
        functools.partial(_qkv_kernel, n_heads, scale),
        out_shape=[
            jax.ShapeDtypeStruct((b, n_heads, s, 2 * LANES), BF16),
            jax.ShapeDtypeStruct((b, n_heads, s, 2 * LANES), BF16),
            jax.ShapeDtypeStruct((b, n_heads, s // ts, LANES + V_ONES_ROWS, ts), BF16),
        ],
        grid=(b, s // ts),
        in_specs=[
            pl.BlockSpec((None, ts, d), lambda i, j: (i, j, 0)),
            pl.BlockSpec((None, ts, LANES), lambda i, j: (i, j, 0)),
            pl.BlockSpec((None, ts, LANES), lambda i, j: (i, j, 0)),
            full(kvg), full(wdkv), full(ckvg), full(wuk), full(wuv),
            full(n1g), _bvec_spec(d), _bvec_spec(d), full(wdq), full(cqg), full(wuq),
        ],
        out_specs=[hspec(2 * LANES), hspec(2 * LANES),
                   pl.BlockSpec((None, n_heads, None, LANES + V_ONES_ROWS, ts),
                                lambda i, j: (i, 0, j, 0, 0))],
        compiler_params=_cparams(("arbitrary", "arbitrary")),
        name="mla_qkv",
    )(x, cc, ss, kvg, wdkv, ckvg, wuk, wuv, n1g, sh1, sc1, wdq, cqg, wuq)


def _attn_kernel(q_ref, k_ref, vt_ref, o_ref, m_ref, acc_ref):
    tq = q_ref.shape[0]
    tk = vt_ref.shape[2]
    ratio = tq // tk
    dv = o_ref.shape[1]
    qi = pl.program_id(2)
    m_ref[...] = jnp.full(m_ref.shape, -jnp.inf, F32)
    acc_ref[...] = jnp.zeros(acc_ref.shape, F32)

    def scores(ki, cols, on_diagonal):
        start = pl.multiple_of(ki * tk, tk)
        k = k_ref[pl.ds(start, tk), :]
        st = lax.dot_general(k, q_ref[cols, :], (((1,), (1,)), ((), ())),
                             preferred_element_type=F32)
        if on_diagonal:
            kc = lax.broadcasted_iota(I32, st.shape, 0) // CHUNK
            qc = lax.broadcasted_iota(I32, st.shape, 1) // CHUNK
            st = jnp.where(kc <= qc, st, -jnp.inf)
        return st

    def update(ki, st, m_old, acc_old):
        m_new = jnp.maximum(m_old, jnp.max(st, axis=0, keepdims=True))
        p = jnp.exp2(st - m_new).astype(BF16)
        alpha = jnp.exp2(m_old - m_new)
        acc_new = alpha * acc_old + jnp.dot(vt_ref[ki], p, preferred_element_type=F32)
        return m_new, acc_new

    def step(groups):
        sts = [[scores(ki, cols, on_diagonal) for ki, on_diagonal in tiles]
               for cols, tiles in groups]
        for (cols, tiles), group_sts in zip(groups, sts):
            state = (m_ref[:, cols], acc_ref[:, cols])
            for (ki, _), st in zip(tiles, group_sts):
                state = update(ki, st, *state)
            m_ref[:, cols], acc_ref[:, cols] = state

    n_below = qi * ratio
    everything = slice(0, tq)
    step([(slice(g * tk, (g + 1) * tk), [(n_below + a, a == g) for a in range(g + 1)])
          for g in range(ratio)])

    def fast_step(first, count):
        m = m_ref[...]
        pv = None
        top = None
        for t in range(count):
            st = scores(first + t, everything, False)
            tmax = jnp.max(st, axis=0, keepdims=True)
            top = tmax if top is None else jnp.maximum(top, tmax)
            p = jnp.exp2(st - m).astype(BF16)
            part = jnp.dot(vt_ref[first + t], p, preferred_element_type=F32)
            pv = part if pv is None else pv + part
        safe = jnp.max(top - m) <= ATTN_MAX_SLACK

        @pl.when(safe)
        def _():
            acc_ref[...] += pv

        @pl.when(jnp.logical_not(safe))
        def _():
            def redo(t, carry):
                step([(everything, [(first + t, False)])])
                return carry
            lax.fori_loop(0, count, redo, 0)

    def body(j, carry):
        fast_step(ATTN_UNROLL * j, ATTN_UNROLL)
        return carry
    trips = n_below // ATTN_UNROLL
    lax.fori_loop(0, trips, body, 0)

    for rem in range(math.gcd(ratio, ATTN_UNROLL), ATTN_UNROLL, math.gcd(ratio, ATTN_UNROLL)):
        @pl.when(n_below % ATTN_UNROLL == rem)
        def _(rem=rem):
            fast_step(trips * ATTN_UNROLL, rem)

    o_ref[...] = (acc_ref[0:dv, :] / acc_ref[dv:dv + 1, :]).T.astype(o_ref.dtype)


def _attention(q, k, vt, tq):
    b, h, s, dk = q.shape
    nk, dv_ext, tk = vt.shape[2:]
    dv = dv_ext - V_ONES_ROWS
    assert tk % CHUNK == 0 and s % tq == 0 and tq % tk == 0
    return pl.pallas_call(
        _attn_kernel,
        out_shape=jax.ShapeDtypeStruct((b, s, h * dv), BF16),
        grid=(b, h, s // tq),
        in_specs=[
            pl.BlockSpec((None, None, tq, dk), lambda i, j, t: (i, j, t, 0)),
            pl.BlockSpec((None, None, s, dk), lambda i, j, t: (i, j, 0, 0)),
            pl.BlockSpec((None, None, nk, dv_ext, tk), lambda i, j, t: (i, j, 0, 0, 0)),
        ],
        out_specs=pl.BlockSpec((None, tq, dv), lambda i, j, t: (i, t, j)),
        scratch_shapes=[pltpu.VMEM((1, tq), F32), pltpu.VMEM((dv_ext, tq), F32)],
        compiler_params=_cparams(("arbitrary", "arbitrary", "arbitrary")),
        name="mla_attention",
    )(q, k, vt)


def _swap_halves(w):
    half = w.shape[-1] // 2
    return jnp.concatenate([w[..., half:], w[..., :half]], axis=-1)


def _pad_lanes(w):
    pad = LANES - w.shape[-1]
    return jnp.concatenate([w, jnp.zeros(w.shape[:-1] + (pad,), w.dtype)], axis=-1)


def kernel(x, c, positions, mod_w, mod_b, norm1_g, norm2_g, conv_pw1_w, conv_pw1_b, conv_dw_w, conv_dw_b, conv_ln_g, conv_ln_b, conv_pw2_w, conv_pw2_b, kv_norm_g, w_dkv, ckv_norm_g, w_uk, w_uv, w_dq, cq_norm_g, w_uq, w_o, router_w, router_b, exp_w_gu, exp_b_gu, exp_w_dn, exp_b_dn, final_g):
    b, s, d = x.shape
    n_heads, nope = w_uk.shape[1], w_uk.shape[2]
    r_kv = ckv_norm_g.shape[0]
    rope = w_dkv.shape[1] - r_kv
    vdim = w_uv.shape[2]
    n_exp = router_w.shape[2]
    assert nope == LANES and vdim == LANES and rope <= LANES and d % LANES == 0
    ts = min(512, s)
    tq = min(1024, s)

    mod = _modulation(c, mod_w, mod_b)
    mods = [[m.reshape(b, 1, d) for m in jnp.split(mod[l], 6, axis=-1)] for l in range(2)]
    row = lambda v: v.reshape(1, -1)

    half = rope // 2
    inv = jnp.exp(-(2.0 * math.log(ROPE_THETA) / rope) * jnp.arange(half, dtype=F32))
    ang = positions.astype(F32)[..., None] * inv
    cos, sin = jnp.cos(ang), jnp.sin(ang)
    cc = _pad_lanes(jnp.concatenate([cos, cos], axis=-1))
    ss = _pad_lanes(jnp.concatenate([-sin, sin], axis=-1))

    sh1, sc1, gt1, sh2, sc2, gt2 = mods[0]
    x1, hn2, idxt, route, cnt = _mix0(
        x, row(norm1_g[0]), sh1, sc1, gt1,
        conv_pw1_w[0].astype(BF16), row(conv_pw1_b[0]), conv_dw_w[0], row(conv_dw_b[0]),
        row(conv_ln_g[0]), row(conv_ln_b[0]), conv_pw2_w[0].astype(BF16), row(conv_pw2_b[0]),
        row(norm2_g[0]), sh2, sc2, router_w[0].T, router_b[0].reshape(n_exp, 1), ts)
    cnt = cnt[:, :ts // MOE_TILE].reshape(-1, n_exp)
    x2 = _moe(x1, hn2, idxt, route, cnt, gt2, row(final_g),
              0, exp_w_gu, exp_b_gu, exp_w_dn, exp_b_dn, False)

    sh1, sc1, gt1, sh2, sc2, gt2 = mods[1]
    wdkv_rope = w_dkv[:, r_kv:]
    wdkv_ext = jnp.concatenate(
        [w_dkv[:, :r_kv], _pad_lanes(wdkv_rope), _pad_lanes(_swap_halves(wdkv_rope))], axis=-1)
    wuq = w_uq[0]
    r_q = wuq.shape[0]
    wuq_rope = wuq[:, :, nope:]
    wuq_ext = jnp.concatenate([
        wuq[:, :, :nope].reshape(r_q, n_heads * LANES),
        _pad_lanes(wuq_rope).reshape(r_q, n_heads * LANES),
        _pad_lanes(_swap_halves(wuq_rope)).reshape(r_q, n_heads * LANES)], axis=-1)
    scale = float((nope + rope) ** -0.5 * math.log2(math.e))
    q, k, v = _qkv(
        x2, cc, ss, row(kv_norm_g), wdkv_ext.astype(BF16), row(ckv_norm_g),
        w_uk.reshape(r_kv, n_heads * nope).astype(BF16), w_uv.reshape(r_kv, n_heads * vdim).T.astype(BF16),
        row(norm1_g[1]), sh1, sc1, w_dq[0].astype(BF16), row(cq_norm_g[0]), wuq_ext.astype(BF16),
        n_heads, scale, ts)
    o = _attention(q, k, v, tq)

    x3, hn2, idxt, route, cnt = _mix1(
        x2, o, gt1, w_o[0].astype(BF16), row(norm2_g[1]), sh2, sc2,
        router_w[1].T, router_b[1].reshape(n_exp, 1), ts)
    cnt = cnt[:, :ts // MOE_TILE].reshape(-1, n_exp)
    return _moe(x3, hn2, idxt, route, cnt, gt2, row(final_g),
                1, exp_w_gu, exp_b_gu, exp_w_dn, exp_b_dn, True)
```

```python
import functools
import math

import jax
import jax.numpy as jnp
from jax import lax
from jax.experimental import pallas as pl
from jax.experimental.pallas import tpu as pltpu

CHUNK = 64
TOP_K = 4
ROPE_THETA = 10000.0
SWIGLU_ALPHA = 1.702
SWIGLU_LIMIT = 7.0
EPS = 1e-6

LANES = 128
SUBLANES = 8
VMEM_LIMIT_BYTES = 56 * 1024 * 1024

ROW_ALIGN = SUBLANES
MOE_TILE = 256
EXPERT_ROWS = 512
CONV_HALO = 32
V_ONES_ROWS = 2 * SUBLANES
ATTN_UNROLL = 4
ATTN_MAX_SLACK = 60.0

F32 = jnp.float32
BF16 = jnp.bfloat16
I32 = jnp.int32


def _cparams(sem):
    return pltpu.CompilerParams(dimension_semantics=sem, vmem_limit_bytes=VMEM_LIMIT_BYTES)


def _rms(x, g):
    return x * lax.rsqrt(jnp.mean(x * x, axis=-1, keepdims=True) + EPS) * g


def _round_up(a, m):
    return (a + m - 1) // m * m


def _mod_kernel(c_ref, w_ref, b_ref, o_ref):
    c = c_ref[...]
    ca = c * jax.nn.sigmoid(c)
    o_ref[...] = jnp.dot(ca, w_ref[...], preferred_element_type=F32) + b_ref[...]


def _modulation(c, mod_w, mod_b):
    depth, d, d6 = mod_w.shape
    b = c.shape[0]
    bp = _round_up(b, SUBLANES)
    cp = jnp.zeros((bp, d), F32).at[:b].set(c)
    tn = d6 // 4
    out = pl.pallas_call(
        _mod_kernel,
        out_shape=jax.ShapeDtypeStruct((depth, bp, d6), F32),
        grid=(depth, d6 // tn),
        in_specs=[
            pl.BlockSpec((bp, d), lambda l, j: (0, 0)),
            pl.BlockSpec((None, d, tn), lambda l, j: (l, 0, j)),
            pl.BlockSpec((None, 1, tn), lambda l, j: (l, 0, j)),
        ],
        out_specs=pl.BlockSpec((None, bp, tn), lambda l, j: (l, 0, j)),
        compiler_params=_cparams(("arbitrary", "arbitrary")),
        name="adaln_modulation",
    )(cp, mod_w, mod_b.reshape(depth, 1, d6))
    return out[:, :b]


def _pre_moe(x1, g2, sh2, sc2, rwt, rb, hn_ref, idxt_ref, route_ref, cnt_ref):
    ts = x1.shape[0]
    n_exp = rwt.shape[0]
    hn = _rms(x1, g2) * (1.0 + sc2) + sh2
    hn_ref[...] = hn.astype(BF16)
    def split(v):
        hi = v.astype(BF16)
        return hi, (v - hi.astype(F32)).astype(BF16)
    nt = lambda a, b: lax.dot_general(a, b, (((1,), (1,)), ((), ())), preferred_element_type=F32)
    w_hi, w_lo = split(rwt)
    h_hi, h_lo = split(hn)
    logits = nt(w_hi, h_hi) + (nt(w_hi, h_lo) + nt(w_lo, h_hi)) + rb
    e_iota = lax.broadcasted_iota(I32, (n_exp, ts), 0)
    vals, idxs = [], []
    cur = logits
    for _ in range(TOP_K):
        m = jnp.max(cur, axis=0, keepdims=True)
        i = jnp.min(jnp.where(cur == m, e_iota, n_exp), axis=0, keepdims=True)
        vals.append(m)
        idxs.append(i)
        cur = jnp.where(e_iota == i, -jnp.inf, cur)
    exps = [jnp.exp(v - vals[0]) for v in vals]
    den = exps[0]
    for e in exps[1:]:
        den = den + e
    gates = [e / den for e in exps]
    idxt_ref[...] = jnp.concatenate(idxs, axis=0)
    rows = jnp.concatenate([i.astype(F32) for i in idxs] + gates
                           + [jnp.zeros((LANES - 2 * TOP_K, ts), F32)], axis=0)
    route_ref[...] = rows.T
    onehot = jnp.zeros((n_exp, ts), F32)
    for i in idxs:
        onehot = onehot + (e_iota == i).astype(F32)
    sel = (lax.broadcasted_iota(I32, (SUBLANES, ts), 1) // MOE_TILE
           == lax.broadcasted_iota(I32, (SUBLANES, ts), 0)).astype(BF16)
    cnt = lax.dot_general(sel, onehot.astype(BF16), (((1,), (1,)), ((), ())),
                          preferred_element_type=F32)
    cnt_ref[...] = cnt.astype(I32)


def _pre_moe_specs(b, s, d, ts, n_exp):
    ns = s // ts
    out_shape = [
        jax.ShapeDtypeStruct((b, s, d), F32),
        jax.ShapeDtypeStruct((b, s, d), BF16),
        jax.ShapeDtypeStruct((TOP_K, b * s), I32),
        jax.ShapeDtypeStruct((b * s, LANES), F32),
        jax.ShapeDtypeStruct((b * ns, SUBLANES, n_exp), I32),
    ]
    out_specs = [
        pl.BlockSpec((None, ts, d), lambda i, j: (i, j, 0)),
        pl.BlockSpec((None, ts, d), lambda i, j: (i, j, 0)),
        pl.BlockSpec((TOP_K, ts), lambda i, j: (0, i * ns + j)),
        pl.BlockSpec((ts, LANES), lambda i, j: (i * ns + j, 0)),
        pl.BlockSpec((None, SUBLANES, n_exp), lambda i, j: (i * ns + j, 0, 0)),
    ]
    return out_shape, out_specs


def _vec_spec(d):
    return pl.BlockSpec((1, d), lambda i, j: (0, 0))


def _bvec_spec(d):
    return pl.BlockSpec((None, 1, d), lambda i, j: (i, 0, 0))


def _mix0_kernel(x_ref, n1g_ref, sh1_ref, sc1_ref, gt1_ref,
                 pw1w_ref, pw1b_ref, dww_ref, dwb_ref, lng_ref, lnb_ref,
                 pw2w_ref, pw2b_ref, n2g_ref, sh2_ref, sc2_ref, rwt_ref, rb_ref,
                 x1_ref, hn_ref, idxt_ref, route_ref, cnt_ref, buf_ref, conv_ref, shift_ref):
    ts, d = x_ref.shape
    width = dww_ref.shape[0]
    x = x_ref[...]
    hn = _rms(x, n1g_ref[...]) * (1.0 + sc1_ref[...]) + sh1_ref[...]
    hb = hn.astype(BF16)
    a = jnp.dot(hb, pw1w_ref[:, :d], preferred_element_type=F32) + pw1b_ref[:, :d]
    g = jnp.dot(hb, pw1w_ref[:, d:], preferred_element_type=F32) + pw1b_ref[:, d:]
    glu = a * jax.nn.sigmoid(g)

    @pl.when(pl.program_id(1) == 0)
    def _():
        buf_ref[0:CONV_HALO, :] = jnp.zeros((CONV_HALO, d), F32)

    buf_ref[CONV_HALO:, :] = glu
    base = CONV_HALO - (width - 1)
    rc = 32
    lc = min(512, d)
    sh_rows = shift_ref.shape[1]
    sub = lax.broadcasted_iota(I32, (SUBLANES, lc), 0)
    for c0 in range(0, d, lc):
        for r in range(1, SUBLANES):
            from_this = sub < SUBLANES - r
            cur = pltpu.roll(buf_ref[0:SUBLANES, c0:c0 + lc], SUBLANES - r, axis=0)
            for j0 in range(0, sh_rows, SUBLANES):
                nxt = pltpu.roll(buf_ref[j0 + SUBLANES:j0 + 2 * SUBLANES, c0:c0 + lc],
                                 SUBLANES - r, axis=0)
                shift_ref[r - 1, j0:j0 + SUBLANES, :] = jnp.where(from_this, cur, nxt)
                cur = nxt
        for r0 in range(0, ts, rc):
            acc = jnp.zeros((rc, lc), F32)
            for k in range(width):
                q8, r = divmod(base + k, SUBLANES)
                lo = q8 * SUBLANES + r0
                if r == 0:
                    win = buf_ref[lo:lo + rc, c0:c0 + lc]
                else:
                    win = shift_ref[r - 1, lo:lo + rc, :]
                acc = acc + dww_ref[k:k + 1, c0:c0 + lc] * win
            conv_ref[r0:r0 + rc, c0:c0 + lc] = acc
    buf_ref[0:CONV_HALO, :] = buf_ref[ts:ts + CONV_HALO, :]
    u = conv_ref[...] + dwb_ref[...]
    mu = jnp.mean(u, axis=-1, keepdims=True)
    dlt = u - mu
    var = jnp.mean(dlt * dlt, axis=-1, keepdims=True)
    u = dlt * lax.rsqrt(var + EPS) * lng_ref[...] + lnb_ref[...]
    u = u * jax.nn.sigmoid(u)
    y = jnp.dot(u.astype(BF16), pw2w_ref[...], preferred_element_type=F32) + pw2b_ref[...]
    x1 = x + gt1_ref[...] * y
    x1_ref[...] = x1
    _pre_moe(x1, n2g_ref[...], sh2_ref[...], sc2_ref[...], rwt_ref[...], rb_ref[...],
             hn_ref, idxt_ref, route_ref, cnt_ref)


def _mix0(x, n1g, sh1, sc1, gt1, pw1w, pw1b, dww, dwb, lng, lnb, pw2w, pw2b,
          n2g, sh2, sc2, rwt, rb, ts):
    b, s, d = x.shape
    n_exp = rwt.shape[0]
    width = dww.shape[0]
    assert width - 1 <= CONV_HALO and ts % MOE_TILE == 0 and ts // MOE_TILE <= SUBLANES
    out_shape, out_specs = _pre_moe_specs(b, s, d, ts, n_exp)
    full = lambda shp: pl.BlockSpec(shp, lambda i, j: (0,) * len(shp))
    return pl.pallas_call(
        _mix0_kernel,
        out_shape=out_shape,
        grid=(b, s // ts),
        in_specs=[
            pl.BlockSpec((None, ts, d), lambda i, j: (i, j, 0)),
            _vec_spec(d), _bvec_spec(d), _bvec_spec(d), _bvec_spec(d),
            full((d, 2 * d)), full((1, 2 * d)), full((width, d)), full((1, d)),
            full((1, d)), full((1, d)), full((d, d)), full((1, d)),
            _vec_spec(d), _bvec_spec(d), _bvec_spec(d),
            full((n_exp, d)), full((n_exp, 1)),
        ],
        out_specs=out_specs,
        scratch_shapes=[
            pltpu.VMEM((CONV_HALO + ts, d), F32),
            pltpu.VMEM((ts, d), F32),
            pltpu.VMEM((SUBLANES - 1, CONV_HALO + ts - SUBLANES, min(512, d)), F32),
        ],
        compiler_params=_cparams(("arbitrary", "arbitrary")),
        name="conformer_mixer",
    )(x, n1g, sh1, sc1, gt1, pw1w, pw1b, dww, dwb, lng, lnb, pw2w, pw2b,
      n2g, sh2, sc2, rwt, rb)


def _mix1_kernel(x_ref, o_ref, gt1_ref, wo_ref, n2g_ref, sh2_ref, sc2_ref, rwt_ref, rb_ref,
                 x1_ref, hn_ref, idxt_ref, route_ref, cnt_ref):
    y = jnp.dot(o_ref[...], wo_ref[...], preferred_element_type=F32)
    x1 = x_ref[...] + gt1_ref[...] * y
    x1_ref[...] = x1
    _pre_moe(x1, n2g_ref[...], sh2_ref[...], sc2_ref[...], rwt_ref[...], rb_ref[...],
             hn_ref, idxt_ref, route_ref, cnt_ref)


def _mix1(x, o, gt1, wo, n2g, sh2, sc2, rwt, rb, ts):
    b, s, d = x.shape
    n_exp = rwt.shape[0]
    do = o.shape[-1]
    assert ts % MOE_TILE == 0 and ts // MOE_TILE <= SUBLANES
    out_shape, out_specs = _pre_moe_specs(b, s, d, ts, n_exp)
    full = lambda shp: pl.BlockSpec(shp, lambda i, j: (0,) * len(shp))
    return pl.pallas_call(
        _mix1_kernel,
        out_shape=out_shape,
        grid=(b, s // ts),
        in_specs=[
            pl.BlockSpec((None, ts, d), lambda i, j: (i, j, 0)),
            pl.BlockSpec((None, ts, do), lambda i, j: (i, j, 0)),
            _bvec_spec(d), full((do, d)),
            _vec_spec(d), _bvec_spec(d), _bvec_spec(d),
            full((n_exp, d)), full((n_exp, 1)),
        ],
        out_specs=out_specs,
        compiler_params=_cparams(("arbitrary", "arbitrary")),
        name="attn_out_mixer",
    )(x, o, gt1, wo, n2g, sh2, sc2, rwt, rb)


def _moe_dims(n_tok, n_exp):
    n_tiles = n_tok // MOE_TILE
    loc_rows = _round_up(TOP_K * MOE_TILE + n_exp * (ROW_ALIGN - 1), LANES)
    max_rows = TOP_K * n_tok + n_tiles * n_exp * (ROW_ALIGN - 1) + n_exp * (EXPERT_ROWS - ROW_ALIGN)
    max_blocks = -(-max_rows // EXPERT_ROWS)
    return loc_rows, max_blocks


def _routing_tables(cnt, max_blocks):
    n_tiles, n_exp = cnt.shape
    cnt8 = _round_up(cnt, ROW_ALIGN)
    off = jnp.cumsum(cnt8, axis=1) - cnt8
    seg_len = cnt8.sum(axis=0)
    seg_pad = _round_up(seg_len, EXPERT_ROWS)
    seg_end = jnp.cumsum(seg_pad)
    seg_start = seg_end - seg_pad
    run_start = seg_start[None, :] + jnp.cumsum(cnt8, axis=0) - cnt8
    n_blocks = (seg_end[-1] // EXPERT_ROWS).astype(I32)
    blk_row = jnp.arange(max_blocks, dtype=I32) * EXPERT_ROWS
    blk_row = jnp.minimum(blk_row, seg_end[-1] - EXPERT_ROWS)
    blk_exp = jnp.minimum((blk_row[:, None] >= seg_end[None, :]).sum(axis=-1), n_exp - 1).astype(I32)
    blk = jnp.arange(max_blocks, dtype=I32)
    prev_exp = jnp.concatenate([jnp.full((1,), -1, I32), blk_exp[:-1]])
    blk_first = ((blk_exp != prev_exp) & (blk < n_blocks)).astype(I32)
    blk_slot = ((jnp.cumsum(blk_first) - 1) % 2).astype(I32)
    e_ids = jnp.arange(n_exp, dtype=I32)
    later = (e_ids[None, :] > e_ids[:, None]) & (seg_pad[None, :] > 0)
    next_exp = jnp.min(jnp.where(later, e_ids[None, :], n_exp), axis=1)
    next_exp = jnp.where(next_exp < n_exp, next_exp, -1)
    blk_next = jnp.sum(jnp.where(blk_exp[:, None] == e_ids[None, :], next_exp[None, :], 0), axis=1)
    flat = lambda a: a.reshape(-1).astype(I32)
    return dict(
        run_loc=flat(off),
        run_glb=flat(run_start),
        run_len=flat(cnt8),
        rows_used=flat(cnt8.sum(axis=1)),
        tail_glb=flat(seg_start + seg_len),
        tail_len=flat(seg_pad - seg_len),
        blk_exp=blk_exp, blk_first=blk_first, blk_slot=blk_slot, blk_next=flat(blk_next),
        n_blocks=n_blocks.reshape(1))


def _aligned(v):
    return pl.multiple_of(v, ROW_ALIGN)


def _dispatch_kernel(n_exp, loc0_ref, glb0_ref, len_ref, used_ref, tail0_ref, taillen_ref, nblk_ref,
                     x_ref, idxt_ref, xs_ref, loc_ref, zero_ref, sem, zsem):
    i = pl.program_id(0)
    n = pl.num_programs(0)
    slot = i % 2
    tile = x_ref.shape[0]
    loc_rows = loc_ref.shape[1]

    def wait_slot(sl, rows):
        rows = _aligned(rows)

        @pl.when(rows > 0)
        def _():
            pltpu.make_async_copy(loc_ref.at[sl, pl.ds(0, rows)], xs_ref.at[pl.ds(0, rows)],
                                  sem.at[sl]).wait()

    @pl.when(i >= 2)
    def _():
        wait_slot(slot, used_ref[i - 2])

    idx = idxt_ref[...]
    e_iota = lax.broadcasted_iota(I32, (n_exp, tile), 0)
    hits = [idx[k:k + 1, :] == e_iota for k in range(TOP_K)]
    onehot = jnp.zeros((n_exp, tile), F32)
    for h in hits:
        onehot = onehot + h.astype(F32)
    upper = (lax.broadcasted_iota(I32, (tile, tile), 0)
             < lax.broadcasted_iota(I32, (tile, tile), 1)).astype(BF16)
    pre = jnp.dot(onehot.astype(BF16), upper, preferred_element_type=F32)
    cnt = jnp.sum(onehot, axis=1, keepdims=True)
    cnt8 = jnp.ceil(cnt / ROW_ALIGN) * ROW_ALIGN
    lower = (lax.broadcasted_iota(I32, (n_exp, n_exp), 1)
             < lax.broadcasted_iota(I32, (n_exp, n_exp), 0)).astype(BF16)
    off = jnp.dot(lower, jnp.broadcast_to(cnt8, (n_exp, LANES)).astype(BF16),
                  preferred_element_type=F32)[:, 0:1]
    base = off + pre
    r_iota = lax.broadcasted_iota(I32, (loc_rows, tile), 0)
    perm = jnp.zeros((loc_rows, tile), F32)
    for h in hits:
        dest = jnp.sum(jnp.where(h, base, 0.0), axis=0, keepdims=True).astype(I32)
        perm = perm + (r_iota == dest).astype(F32)
    loc_ref[slot] = jnp.dot(perm.astype(BF16), x_ref[...], preferred_element_type=F32)

    def issue_run(e, carry):
        rows = _aligned(len_ref[i * n_exp + e])

        @pl.when(rows > 0)
        def _():
            pltpu.make_async_copy(
                loc_ref.at[slot, pl.ds(_aligned(loc0_ref[i * n_exp + e]), rows)],
                xs_ref.at[pl.ds(_aligned(glb0_ref[i * n_exp + e]), rows)],
                sem.at[slot]).start()
        return carry
    lax.fori_loop(0, n_exp, issue_run, 0)

    @pl.when(i == n - 1)
    def _():
        zero_ref[...] = jnp.zeros(zero_ref.shape, F32)
        max_blocks = xs_ref.shape[0] // EXPERT_ROWS

        def zero_tail(e, total):
            rows = _aligned(taillen_ref[e])

            @pl.when(rows > 0)
            def _():
                pltpu.make_async_copy(zero_ref.at[pl.ds(0, rows)],
                                      xs_ref.at[pl.ds(_aligned(tail0_ref[e]), rows)],
                                      zsem.at[0]).start()
            return total + rows
        tail_rows = lax.fori_loop(0, n_exp, zero_tail, jnp.int32(0))

        def zero_block(blk, carry):
            pltpu.make_async_copy(
                zero_ref,
                xs_ref.at[pl.ds(pl.multiple_of(blk * EXPERT_ROWS, EXPERT_ROWS), EXPERT_ROWS)],
                zsem.at[1]).start()
            return carry
        lax.fori_loop(nblk_ref[0], max_blocks, zero_block, 0)

        @pl.when(i >= 1)
        def _():
            wait_slot(1 - slot, used_ref[i - 1])
        wait_slot(slot, used_ref[i])

        def wait_rows(rows, s):
            rows = _aligned(rows)

            @pl.when(rows > 0)
            def _():
                pltpu.make_async_copy(xs_ref.at[pl.ds(0, rows)], xs_ref.at[pl.ds(0, rows)],
                                      zsem.at[s]).wait()
        wait_rows(tail_rows, 0)
        wait_rows((max_blocks - nblk_ref[0]) * EXPERT_ROWS, 1)


def _dispatch(hn2, idxt, tabs, n_exp, loc_rows, max_blocks):
    n_tok, d = hn2.shape
    n_tiles = n_tok // MOE_TILE
    return pl.pallas_call(
        functools.partial(_dispatch_kernel, n_exp),
        out_shape=jax.ShapeDtypeStruct((max_blocks * EXPERT_ROWS, d), F32),
        grid_spec=pltpu.PrefetchScalarGridSpec(
            num_scalar_prefetch=7,
            grid=(n_tiles,),
            in_specs=[
                pl.BlockSpec((MOE_TILE, d), lambda i, *_: (i, 0)),
                pl.BlockSpec((TOP_K, MOE_TILE), lambda i, *_: (0, i)),
            ],
            out_specs=pl.BlockSpec(memory_space=pl.ANY),
            scratch_shapes=[
                pltpu.VMEM((2, loc_rows, d), F32),
                pltpu.VMEM((EXPERT_ROWS, d), F32),
                pltpu.SemaphoreType.DMA((2,)),
                pltpu.SemaphoreType.DMA((2,)),
            ],
        ),
        compiler_params=_cparams(("arbitrary",)),
        name="moe_dispatch",
    )(tabs["run_loc"], tabs["run_glb"], tabs["run_len"], tabs["rows_used"],
      tabs["tail_glb"], tabs["tail_len"], tabs["n_blocks"], hn2, idxt)


def _expert_kernel(layer, be_ref, first_ref, slot_ref, next_ref, nb_ref,
                   x_ref, wgu_hbm, bgu_ref, wdn_hbm, bdn_ref, y_ref,
                   wgu_f32, wdn_f32, wgu_bf, wdn_bf, sem):
    de = wdn_bf.shape[0]
    b = pl.program_id(0)

    def weight_copies(e, s):
        return (pltpu.make_async_copy(wgu_hbm.at[layer, e], wgu_f32.at[s], sem.at[s, 0]),
                pltpu.make_async_copy(wdn_hbm.at[layer, e], wdn_f32.at[s], sem.at[s, 1]))

    @pl.when(b == 0)
    def _():
        for cp in weight_copies(be_ref[0], 0):
            cp.start()

    @pl.when(first_ref[b] == 1)
    def _():
        s = slot_ref[b]
        for cp in weight_copies(be_ref[b], s):
            cp.wait()

        @pl.when(next_ref[b] >= 0)
        def _():
            for cp in weight_copies(next_ref[b], 1 - s):
                cp.start()
        cw = 512
        for c0 in range(0, wgu_bf.shape[1], cw):
            wgu_bf[:, c0:c0 + cw] = wgu_f32[s, :, c0:c0 + cw].astype(BF16)
        for c0 in range(0, wdn_bf.shape[1], cw):
            wdn_bf[:, c0:c0 + cw] = wdn_f32[s, :, c0:c0 + cw].astype(BF16)

    @pl.when(b < nb_ref[0])
    def _():
        xb = x_ref[...].astype(BF16)
        g = jnp.dot(xb, wgu_bf[:, :de], preferred_element_type=F32) + bgu_ref[:, :de]
        u = jnp.dot(xb, wgu_bf[:, de:], preferred_element_type=F32) + bgu_ref[:, de:]
        g = jnp.minimum(g, SWIGLU_LIMIT)
        u = jnp.clip(u, -SWIGLU_LIMIT, SWIGLU_LIMIT)
        a = (u + 1.0) * g * jax.nn.sigmoid(SWIGLU_ALPHA * g)
        y_ref[...] = jnp.dot(a.astype(BF16), wdn_bf[...], preferred_element_type=F32) + bdn_ref[...]

    @pl.when(pl.program_id(0) >= nb_ref[0])
    def _():
        y_ref[...] = jnp.zeros(y_ref.shape, y_ref.dtype)


def _experts(xs, tabs, layer, wgu, bgu, wdn, bdn):
    rows, d = xs.shape
    depth, n_exp, _, de2 = wgu.shape
    de = de2 // 2
    max_blocks = rows // EXPERT_ROWS
    row_map = lambda b, be, fi, sl, nx, nb: (jnp.minimum(b, nb[0] - 1), 0)
    exp_map = lambda b, be, fi, sl, nx, nb: (layer, be[b], 0, 0)
    return pl.pallas_call(
        functools.partial(_expert_kernel, layer),
        out_shape=jax.ShapeDtypeStruct((rows, d), F32),
        grid_spec=pltpu.PrefetchScalarGridSpec(
            num_scalar_prefetch=5,
            grid=(max_blocks,),
            in_specs=[
                pl.BlockSpec((EXPERT_ROWS, d), row_map),
                pl.BlockSpec(memory_space=pl.ANY),
                pl.BlockSpec((None, None, 1, de2), exp_map),
                pl.BlockSpec(memory_space=pl.ANY),
                pl.BlockSpec((None, None, 1, d), exp_map),
            ],
            out_specs=pl.BlockSpec((EXPERT_ROWS, d), lambda b, *_: (b, 0)),
            scratch_shapes=[
                pltpu.VMEM((2, d, de2), F32), pltpu.VMEM((2, de, d), F32),
                pltpu.VMEM((d, de2), BF16), pltpu.VMEM((de, d), BF16),
                pltpu.SemaphoreType.DMA((2, 2)),
            ],
        ),
        compiler_params=_cparams(("arbitrary",)),
        name="moe_experts",
    )(tabs["blk_exp"], tabs["blk_first"], tabs["blk_slot"], tabs["blk_next"], tabs["n_blocks"],
      xs, wgu, bgu.reshape(depth, n_exp, 1, de2), wdn, bdn.reshape(depth, n_exp, 1, d))


def _combine_kernel(final_norm, n_exp, loc0_ref, glb0_ref, len_ref, used_ref,
                    route_ref, x1_ref, g2_ref, fg_ref, ys_ref, out_ref, loc_ref, sem):
    i = pl.program_id(0)
    n = pl.num_programs(0)
    slot = i % 2
    tile = x1_ref.shape[0]
    loc_rows = loc_ref.shape[1]

    def issue_tile(t, sl):
        def issue_run(e, carry):
            rows = _aligned(len_ref[t * n_exp + e])

            @pl.when(rows > 0)
            def _():
                pltpu.make_async_copy(
                    ys_ref.at[pl.ds(_aligned(glb0_ref[t * n_exp + e]), rows)],
                    loc_ref.at[sl, pl.ds(_aligned(loc0_ref[t * n_exp + e]), rows)],
                    sem.at[sl]).start()
            return carry
        lax.fori_loop(0, n_exp, issue_run, 0)

    @pl.when(i == 0)
    def _():
        issue_tile(0, 0)

    @pl.when(i + 1 < n)
    def _():
        issue_tile(i + 1, 1 - slot)

    route = route_ref[...]
    e_iota = lax.broadcasted_iota(I32, (tile, n_exp), 1)
    hits = [route[:, k:k + 1].astype(I32) == e_iota for k in range(TOP_K)]
    gates = [route[:, TOP_K + k:TOP_K + k + 1] for k in range(TOP_K)]
    onehot = jnp.zeros((tile, n_exp), F32)
    for h in hits:
        onehot = onehot + h.astype(F32)
    lower = (lax.broadcasted_iota(I32, (tile, tile), 1)
             < lax.broadcasted_iota(I32, (tile, tile), 0)).astype(BF16)
    pre = jnp.dot(lower, onehot.astype(BF16), preferred_element_type=F32)
    cnt = jnp.sum(onehot, axis=0, keepdims=True)
    cnt8 = jnp.ceil(cnt / ROW_ALIGN) * ROW_ALIGN
    upper = (lax.broadcasted_iota(I32, (n_exp, n_exp), 0)
             < lax.broadcasted_iota(I32, (n_exp, n_exp), 1)).astype(BF16)
    off = jnp.dot(jnp.broadcast_to(cnt8, (SUBLANES, n_exp)).astype(BF16), upper,
                  preferred_element_type=F32)[0:1, :]
    base = off + pre
    r_iota = lax.broadcasted_iota(I32, (tile, loc_rows), 1)
    comb = jnp.zeros((tile, loc_rows), F32)
    for h, gk in zip(hits, gates):
        dest = jnp.sum(jnp.where(h, base, 0.0), axis=1, keepdims=True).astype(I32)
        comb = comb + jnp.where(r_iota == dest, gk, 0.0)

    used_rows = _aligned(used_ref[i])

    @pl.when(used_rows > 0)
    def _():
        pltpu.make_async_copy(ys_ref.at[pl.ds(0, used_rows)],
                              loc_ref.at[slot, pl.ds(0, used_rows)], sem.at[slot]).wait()

    row_ok = lax.broadcasted_iota(I32, (loc_rows, 1), 0) < used_rows
    yl = jnp.where(row_ok, loc_ref[slot], 0.0).astype(BF16)
    moe = jnp.dot(comb.astype(BF16), yl, preferred_element_type=F32)
    out = x1_ref[...] + g2_ref[...] * moe
    if final_norm:
        out = _rms(out, fg_ref[...])
    out_ref[...] = out


def _combine(ys, route, x1, gate2, final_g, tabs, n_exp, loc_rows, final_norm):
    b, s, d = x1.shape
    n_tok = b * s
    n_tiles = n_tok // MOE_TILE
    tiles_per_seq = s // MOE_TILE
    out = pl.pallas_call(
        functools.partial(_combine_kernel, final_norm, n_exp),
        out_shape=jax.ShapeDtypeStruct((n_tok, d), F32),
        grid_spec=pltpu.PrefetchScalarGridSpec(
            num_scalar_prefetch=4,
            grid=(n_tiles,),
            in_specs=[
                pl.BlockSpec((MOE_TILE, LANES), lambda i, *_: (i, 0)),
                pl.BlockSpec((MOE_TILE, d), lambda i, *_: (i, 0)),
                pl.BlockSpec((None, 1, d), lambda i, *_: (i // tiles_per_seq, 0, 0)),
                pl.BlockSpec((1, d), lambda i, *_: (0, 0)),
                pl.BlockSpec(memory_space=pl.ANY),
            ],
            out_specs=pl.BlockSpec((MOE_TILE, d), lambda i, *_: (i, 0)),
            scratch_shapes=[
                pltpu.VMEM((2, loc_rows, d), F32),
                pltpu.SemaphoreType.DMA((2,)),
            ],
        ),
        compiler_params=_cparams(("arbitrary",)),
        name="moe_combine",
    )(tabs["run_loc"], tabs["run_glb"], tabs["run_len"], tabs["rows_used"],
      route, x1.reshape(n_tok, d), gate2, final_g, ys)
    return out.reshape(b, s, d)


def _moe(x1, hn2, idxt, route, cnt, gate2, final_g, layer, wgu, bgu, wdn, bdn, final_norm):
    b, s, d = x1.shape
    n_tok = b * s
    n_exp = wgu.shape[1]
    loc_rows, max_blocks = _moe_dims(n_tok, n_exp)
    tabs = _routing_tables(cnt, max_blocks)
    xs = _dispatch(hn2.reshape(n_tok, d), idxt, tabs, n_exp, loc_rows, max_blocks)
    ys = _experts(xs, tabs, layer, wgu, bgu, wdn, bdn)
    return _combine(ys, route, x1, gate2, final_g, tabs, n_exp, loc_rows, final_norm)


def _qkv_kernel(n_heads, scale, x_ref, cc_ref, ss_ref, kvg_ref, wdkv_ref, ckvg_ref, wuk_ref, wuv_ref,
                n1g_ref, sh1_ref, sc1_ref, wdq_ref, cqg_ref, wuq_ref,
                q_ref, k_ref, v_ref):
    x = x_ref[...]
    cc = cc_ref[...]
    ss = ss_ref[...]
    r_kv = ckvg_ref.shape[1]
    hk = _rms(x, kvg_ref[...]).astype(BF16)
    lat = jnp.dot(hk, wdkv_ref[...], preferred_element_type=F32)
    ckv = _rms(lat[:, :r_kv], ckvg_ref[...]).astype(BF16)
    krot = lat[:, r_kv:r_kv + LANES] * cc + lat[:, r_kv + LANES:r_kv + 2 * LANES] * ss
    kn = jnp.dot(ckv, wuk_ref[...], preferred_element_type=F32)
    vt = lax.dot_general(wuv_ref[...], ckv, (((1,), (1,)), ((), ())), preferred_element_type=F32)
    hq = (_rms(x, n1g_ref[...]) * (1.0 + sc1_ref[...]) + sh1_ref[...]).astype(BF16)
    cq = _rms(jnp.dot(hq, wdq_ref[...], preferred_element_type=F32), cqg_ref[...]).astype(BF16)
    qq = jnp.dot(cq, wuq_ref[...], preferred_element_type=F32) * scale
    hd = n_heads * LANES
    ts = x.shape[0]
    ones_rows = (lax.broadcasted_iota(I32, (V_ONES_ROWS, ts), 0) == 0).astype(BF16)
    for h in range(n_heads):
        sl = slice(h * LANES, (h + 1) * LANES)
        k_ref[h, :, 0:LANES] = kn[:, sl].astype(BF16)
        k_ref[h, :, LANES:2 * LANES] = krot.astype(BF16)
        v_ref[h, 0:LANES, :] = vt[h * LANES:(h + 1) * LANES, :].astype(BF16)
        v_ref[h, LANES:LANES + V_ONES_ROWS, :] = ones_rows
        q_ref[h, :, 0:LANES] = qq[:, sl].astype(BF16)
        qrot = qq[:, hd + h * LANES:hd + (h + 1) * LANES] * cc \
            + qq[:, 2 * hd + h * LANES:2 * hd + (h + 1) * LANES] * ss
        q_ref[h, :, LANES:2 * LANES] = qrot.astype(BF16)


def _qkv(x, cc, ss, kvg, wdkv, ckvg, wuk, wuv, n1g, sh1, sc1, wdq, cqg, wuq, n_heads, scale, ts):
    b, s, d = x.shape
    full = lambda a: pl.BlockSpec(a.shape, lambda i, j: (0,) * a.ndim)
    hspec = lambda w: pl.BlockSpec((None, n_heads, ts, w), lambda i, j: (i, 0, j, 0))
    return pl.pallas_call(
        functools.partial(_qkv_kernel, n_heads, scale),
        out_shape=[
            jax.ShapeDtypeStruct((b, n_heads, s, 2 * LANES), BF16),
            jax.ShapeDtypeStruct((b, n_heads, s, 2 * LANES), BF16),
            jax.ShapeDtypeStruct((b, n_heads, s // ts, LANES + V_ONES_ROWS, ts), BF16),
        ],
        grid=(b, s // ts),
        in_specs=[
            pl.BlockSpec((None, ts, d), lambda i, j: (i, j, 0)),
            pl.BlockSpec((None, ts, LANES), lambda i, j: (i, j, 0)),
            pl.BlockSpec((None, ts, LANES), lambda i, j: (i, j, 0)),
            full(kvg), full(wdkv), full(ckvg), full(wuk), full(wuv),
            full(n1g), _bvec_spec(d), _bvec_spec(d), full(wdq), full(cqg), full(wuq),
        ],
        out_specs=[hspec(2 * LANES), hspec(2 * LANES),
                   pl.BlockSpec((None, n_heads, None, LANES + V_ONES_ROWS, ts),
                                lambda i, j: (i, 0, j, 0, 0))],
        compiler_params=_cparams(("arbitrary", "arbitrary")),
        name="mla_qkv",
    )(x, cc, ss, kvg, wdkv, ckvg, wuk, wuv, n1g, sh1, sc1, wdq, cqg, wuq)


def _attn_kernel(q_ref, k_ref, vt_ref, o_ref, m_ref, acc_ref):
    tq = q_ref.shape[0]
    tk = vt_ref.shape[2]
    ratio = tq // tk
    dv = o_ref.shape[1]
    qi = pl.program_id(2)
    m_ref[...] = jnp.full(m_ref.shape, -jnp.inf, F32)
    acc_ref[...] = jnp.zeros(acc_ref.shape, F32)

    def scores(ki, cols, on_diagonal):
        start = pl.multiple_of(ki * tk, tk)
        k = k_ref[pl.ds(start, tk), :]
        st = lax.dot_general(k, q_ref[cols, :], (((1,), (1,)), ((), ())),
                             preferred_element_type=F32)
        if on_diagonal:
            kc = lax.broadcasted_iota(I32, st.shape, 0) // CHUNK
            qc = lax.broadcasted_iota(I32, st.shape, 1) // CHUNK
            st = jnp.where(kc <= qc, st, -jnp.inf)
        return st

    def update(ki, st, m_old, acc_old):
        m_new = jnp.maximum(m_old, jnp.max(st, axis=0, keepdims=True))
        p = jnp.exp2(st - m_new).astype(BF16)
        alpha = jnp.exp2(m_old - m_new)
        acc_new = alpha * acc_old + jnp.dot(vt_ref[ki], p, preferred_element_type=F32)
        return m_new, acc_new

    def step(groups):
        sts = [[scores(ki, cols, on_diagonal) for ki, on_diagonal in tiles]
               for cols, tiles in groups]
        for (cols, tiles), group_sts in zip(groups, sts):
            state = (m_ref[:, cols], acc_ref[:, cols])
            for (ki, _), st in zip(tiles, group_sts):
                state = update(ki, st, *state)
            m_ref[:, cols], acc_ref[:, cols] = state

    n_below = qi * ratio
    everything = slice(0, tq)
    step([(slice(g * tk, (g + 1) * tk), [(n_below + a, a == g) for a in range(g + 1)])
          for g in range(ratio)])

    def fast_step(first, count):
        m = m_ref[...]
        pv = None
        top = None
        for t in range(count):
            st = scores(first + t, everything, False)
            tmax = jnp.max(st, axis=0, keepdims=True)
            top = tmax if top is None else jnp.maximum(top, tmax)
            p = jnp.exp2(st - m).astype(BF16)
            part = jnp.dot(vt_ref[first + t], p, preferred_element_type=F32)
            pv = part if pv is None else pv + part
        safe = jnp.max(top - m) <= ATTN_MAX_SLACK

        @pl.when(safe)
        def _():
            acc_ref[...] += pv

        @pl.when(jnp.logical_not(safe))
        def _():
            def redo(t, carry):
                step([(everything, [(first + t, False)])])
                return carry
            lax.fori_loop(0, count, redo, 0)

    def body(j, carry):
        fast_step(ATTN_UNROLL * j, ATTN_UNROLL)
        return carry
    trips = n_below // ATTN_UNROLL
    lax.fori_loop(0, trips, body, 0)

    for rem in range(math.gcd(ratio, ATTN_UNROLL), ATTN_UNROLL, math.gcd(ratio, ATTN_UNROLL)):
        @pl.when(n_below % ATTN_UNROLL == rem)
        def _(rem=rem):
            fast_step(trips * ATTN_UNROLL, rem)

    o_ref[...] = (acc_ref[0:dv, :] / acc_ref[dv:dv + 1, :]).T.astype(o_ref.dtype)


def _attention(q, k, vt, tq):
    b, h, s, dk = q.shape
    nk, dv_ext, tk = vt.shape[2:]
    dv = dv_ext - V_ONES_ROWS
    assert tk % CHUNK == 0 and s % tq == 0 and tq % tk == 0
    return pl.pallas_call(
        _attn_kernel,
        out_shape=jax.ShapeDtypeStruct((b, s, h * dv), BF16),
        grid=(b, h, s // tq),
        in_specs=[
            pl.BlockSpec((None, None, tq, dk), lambda i, j, t: (i, j, t, 0)),
            pl.BlockSpec((None, None, s, dk), lambda i, j, t: (i, j, 0, 0)),
            pl.BlockSpec((None, None, nk, dv_ext, tk), lambda i, j, t: (i, j, 0, 0, 0)),
        ],
        out_specs=pl.BlockSpec((None, tq, dv), lambda i, j, t: (i, t, j)),
        scratch_shapes=[pltpu.VMEM((1, tq), F32), pltpu.VMEM((dv_ext, tq), F32)],
        compiler_params=_cparams(("arbitrary", "arbitrary", "arbitrary")),
        name="mla_attention",
    )(q, k, vt)


def _swap_halves(w):
    half = w.shape[-1] // 2
    return jnp.concatenate([w[..., half:], w[..., :half]], axis=-1)


def _pad_lanes(w):
    pad = LANES - w.shape[-1]
    return jnp.concatenate([w, jnp.zeros(w.shape[:-1] + (pad,), w.dtype)], axis=-1)


def kernel(x, c, positions, mod_w, mod_b, norm1_g, norm2_g, conv_pw1_w, conv_pw1_b, conv_dw_w, conv_dw_b, conv_ln_g, conv_ln_b, conv_pw2_w, conv_pw2_b, kv_norm_g, w_dkv, ckv_norm_g, w_uk, w_uv, w_dq, cq_norm_g, w_uq, w_o, router_w, router_b, exp_w_gu, exp_b_gu, exp_w_dn, exp_b_dn, final_g):
    b, s, d = x.shape
    n_heads, nope = w_uk.shape[1], w_uk.shape[2]
    r_kv = ckv_norm_g.shape[0]
    rope = w_dkv.shape[1] - r_kv
    vdim = w_uv.shape[2]
    n_exp = router_w.shape[2]
    assert nope == LANES and vdim == LANES and rope <= LANES and d % LANES == 0
    ts = min(512, s)
    tq = min(1024, s)

    mod = _modulation(c, mod_w, mod_b)
    mods = [[m.reshape(b, 1, d) for m in jnp.split(mod[l], 6, axis=-1)] for l in range(2)]
    row = lambda v: v.reshape(1, -1)

    half = rope // 2
    inv = jnp.exp(-(2.0 * math.log(ROPE_THETA) / rope) * jnp.arange(half, dtype=F32))
    ang = positions.astype(F32)[..., None] * inv
    cos, sin = jnp.cos(ang), jnp.sin(ang)
    cc = _pad_lanes(jnp.concatenate([cos, cos], axis=-1))
    ss = _pad_lanes(jnp.concatenate([-sin, sin], axis=-1))

    sh1, sc1, gt1, sh2, sc2, gt2 = mods[0]
    x1, hn2, idxt, route, cnt = _mix0(
        x, row(norm1_g[0]), sh1, sc1, gt1,
        conv_pw1_w[0].astype(BF16), row(conv_pw1_b[0]), conv_dw_w[0], row(conv_dw_b[0]),
        row(conv_ln_g[0]), row(conv_ln_b[0]), conv_pw2_w[0].astype(BF16), row(conv_pw2_b[0]),
        row(norm2_g[0]), sh2, sc2, router_w[0].T, router_b[0].reshape(n_exp, 1), ts)
    cnt = cnt[:, :ts // MOE_TILE].reshape(-1, n_exp)
    x2 = _moe(x1, hn2, idxt, route, cnt, gt2, row(final_g),
              0, exp_w_gu, exp_b_gu, exp_w_dn, exp_b_dn, False)

    sh1, sc1, gt1, sh2, sc2, gt2 = mods[1]
    wdkv_rope = w_dkv[:, r_kv:]
    wdkv_ext = jnp.concatenate(
        [w_dkv[:, :r_kv], _pad_lanes(wdkv_rope), _pad_lanes(_swap_halves(wdkv_rope))], axis=-1)
    wuq = w_uq[0]
    r_q = wuq.shape[0]
    wuq_rope = wuq[:, :, nope:]
    wuq_ext = jnp.concatenate([
        wuq[:, :, :nope].reshape(r_q, n_heads * LANES),
        _pad_lanes(wuq_rope).reshape(r_q, n_heads * LANES),
        _pad_lanes(_swap_halves(wuq_rope)).reshape(r_q, n_heads * LANES)], axis=-1)
    scale = float((nope + rope) ** -0.5 * math.log2(math.e))
    q, k, v = _qkv(
        x2, cc, ss, row(kv_norm_g), wdkv_ext.astype(BF16), row(ckv_norm_g),
        w_uk.reshape(r_kv, n_heads * nope).astype(BF16), w_uv.reshape(r_kv, n_heads * vdim).T.astype(BF16),
        row(norm1_g[1]), sh1, sc1, w_dq[0].astype(BF16), row(cq_norm_g[0]), wuq_ext.astype(BF16),
        n_heads, scale, ts)
    o = _attention(q, k, v, tq)

    x3, hn2, idxt, route, cnt = _mix1(
        x2, o, gt1, w_o[0].astype(BF16), row(norm2_g[1]), sh2, sc2,
        router_w[1].T, router_b[1].reshape(n_exp, 1), ts)
    cnt = cnt[:, :ts // MOE_TILE].reshape(-1, n_exp)
    return _moe(x3, hn2, idxt, route, cnt, gt2, row(final_g),
                1, exp_w_gu, exp_b_gu, exp_w_dn, exp_b_dn, True)
```

```python
import functools
import math

import jax
import jax.numpy as jnp
from jax import lax
from jax.experimental import pallas as pl
from jax.experimental.pallas import tpu as pltpu

CHUNK = 64
TOP_K = 4
ROPE_THETA = 10000.0
SWIGLU_ALPHA = 1.702
SWIGLU_LIMIT = 7.0
EPS = 1e-6

LANES = 128
SUBLANES = 8
VMEM_LIMIT_BYTES = 56 * 1024 * 1024

ROW_ALIGN = SUBLANES
MOE_TILE = 256
EXPERT_ROWS = 512
CONV_HALO = 32
V_ONES_ROWS = 2 * SUBLANES
ATTN_UNROLL = 4
ATTN_MAX_SLACK = 60.0

F32 = jnp.float32
BF16 = jnp.bfloat16
I32 = jnp.int32


def _cparams(sem):
    return pltpu.CompilerParams(dimension_semantics=sem, vmem_limit_bytes=VMEM_LIMIT_BYTES)


def _rms(x, g):
    return x * lax.rsqrt(jnp.mean(x * x, axis=-1, keepdims=True) + EPS) * g


def _round_up(a, m):
    return (a + m - 1) // m * m


def _mod_kernel(c_ref, w_ref, b_ref, o_ref):
    c = c_ref[...]
    ca = c * jax.nn.sigmoid(c)
    o_ref[...] = jnp.dot(ca, w_ref[...], preferred_element_type=F32) + b_ref[...]


def _modulation(c, mod_w, mod_b):
    depth, d, d6 = mod_w.shape
    b = c.shape[0]
    bp = _round_up(b, SUBLANES)
    cp = jnp.zeros((bp, d), F32).at[:b].set(c)
    tn = d6 // 4
    out = pl.pallas_call(
        _mod_kernel,
        out_shape=jax.ShapeDtypeStruct((depth, bp, d6), F32),
        grid=(depth, d6 // tn),
        in_specs=[
            pl.BlockSpec((bp, d), lambda l, j: (0, 0)),
            pl.BlockSpec((None, d, tn), lambda l, j: (l, 0, j)),
            pl.BlockSpec((None, 1, tn), lambda l, j: (l, 0, j)),
        ],
        out_specs=pl.BlockSpec((None, bp, tn), lambda l, j: (l, 0, j)),
        compiler_params=_cparams(("arbitrary", "arbitrary")),
        name="adaln_modulation",
    )(cp, mod_w, mod_b.reshape(depth, 1, d6))
    return out[:, :b]


def _pre_moe(x1, g2, sh2, sc2, rwt, rb, hn_ref, idxt_ref, route_ref, cnt_ref):
    ts = x1.shape[0]
    n_exp = rwt.shape[0]
    hn = _rms(x1, g2) * (1.0 + sc2) + sh2
    hn_ref[...] = hn.astype(BF16)
    def split(v):
        hi = v.astype(BF16)
        return hi, (v - hi.astype(F32)).astype(BF16)
    nt = lambda a, b: lax.dot_general(a, b, (((1,), (1,)), ((), ())), preferred_element_type=F32)
    w_hi, w_lo = split(rwt)
    h_hi, h_lo = split(hn)
    logits = nt(w_hi, h_hi) + (nt(w_hi, h_lo) + nt(w_lo, h_hi)) + rb
    e_iota = lax.broadcasted_iota(I32, (n_exp, ts), 0)
    vals, idxs = [], []
    cur = logits
    for _ in range(TOP_K):
        m = jnp.max(cur, axis=0, keepdims=True)
        i = jnp.min(jnp.where(cur == m, e_iota, n_exp), axis=0, keepdims=True)
        vals.append(m)
        idxs.append(i)
        cur = jnp.where(e_iota == i, -jnp.inf, cur)
    exps = [jnp.exp(v - vals[0]) for v in vals]
    den = exps[0]
    for e in exps[1:]:
        den = den + e
    gates = [e / den for e in exps]
    idxt_ref[...] = jnp.concatenate(idxs, axis=0)
    rows = jnp.concatenate([i.astype(F32) for i in idxs] + gates
                           + [jnp.zeros((LANES - 2 * TOP_K, ts), F32)], axis=0)
    route_ref[...] = rows.T
    onehot = jnp.zeros((n_exp, ts), F32)
    for i in idxs:
        onehot = onehot + (e_iota == i).astype(F32)
    sel = (lax.broadcasted_iota(I32, (SUBLANES, ts), 1) // MOE_TILE
           == lax.broadcasted_iota(I32, (SUBLANES, ts), 0)).astype(BF16)
    cnt = lax.dot_general(sel, onehot.astype(BF16), (((1,), (1,)), ((), ())),
                          preferred_element_type=F32)
    cnt_ref[...] = cnt.astype(I32)


def _pre_moe_specs(b, s, d, ts, n_exp):
    ns = s // ts
    out_shape = [
        jax.ShapeDtypeStruct((b, s, d), F32),
        jax.ShapeDtypeStruct((b, s, d), BF16),
        jax.ShapeDtypeStruct((TOP_K, b * s), I32),
        jax.ShapeDtypeStruct((b * s, LANES), F32),
        jax.ShapeDtypeStruct((b * ns, SUBLANES, n_exp), I32),
    ]
    out_specs = [
        pl.BlockSpec((None, ts, d), lambda i, j: (i, j, 0)),
        pl.BlockSpec((None, ts, d), lambda i, j: (i, j, 0)),
        pl.BlockSpec((TOP_K, ts), lambda i, j: (0, i * ns + j)),
        pl.BlockSpec((ts, LANES), lambda i, j: (i * ns + j, 0)),
        pl.BlockSpec((None, SUBLANES, n_exp), lambda i, j: (i * ns + j, 0, 0)),
    ]
    return out_shape, out_specs


def _vec_spec(d):
    return pl.BlockSpec((1, d), lambda i, j: (0, 0))


def _bvec_spec(d):
    return pl.BlockSpec((None, 1, d), lambda i, j: (i, 0, 0))


def _mix0_kernel(x_ref, n1g_ref, sh1_ref, sc1_ref, gt1_ref,
                 pw1w_ref, pw1b_ref, dww_ref, dwb_ref, lng_ref, lnb_ref,
                 pw2w_ref, pw2b_ref, n2g_ref, sh2_ref, sc2_ref, rwt_ref, rb_ref,
                 x1_ref, hn_ref, idxt_ref, route_ref, cnt_ref, buf_ref, conv_ref, shift_ref):
    ts, d = x_ref.shape
    width = dww_ref.shape[0]
    x = x_ref[...]
    hn = _rms(x, n1g_ref[...]) * (1.0 + sc1_ref[...]) + sh1_ref[...]
    hb = hn.astype(BF16)
    a = jnp.dot(hb, pw1w_ref[:, :d], preferred_element_type=F32) + pw1b_ref[:, :d]
    g = jnp.dot(hb, pw1w_ref[:, d:], preferred_element_type=F32) + pw1b_ref[:, d:]
    glu = a * jax.nn.sigmoid(g)

    @pl.when(pl.program_id(1) == 0)
    def _():
        buf_ref[0:CONV_HALO, :] = jnp.zeros((CONV_HALO, d), F32)

    buf_ref[CONV_HALO:, :] = glu
    base = CONV_HALO - (width - 1)
    rc = 32
    lc = min(512, d)
    sh_rows = shift_ref.shape[1]
    sub = lax.broadcasted_iota(I32, (SUBLANES, lc), 0)
    for c0 in range(0, d, lc):
        for r in range(1, SUBLANES):
            from_this = sub < SUBLANES - r
            cur = pltpu.roll(buf_ref[0:SUBLANES, c0:c0 + lc], SUBLANES - r, axis=0)
            for j0 in range(0, sh_rows, SUBLANES):
                nxt = pltpu.roll(buf_ref[j0 + SUBLANES:j0 + 2 * SUBLANES, c0:c0 + lc],
                                 SUBLANES - r, axis=0)
                shift_ref[r - 1, j0:j0 + SUBLANES, :] = jnp.where(from_this, cur, nxt)
                cur = nxt
        for r0 in range(0, ts, rc):
            acc = jnp.zeros((rc, lc), F32)
            for k in range(width):
                q8, r = divmod(base + k, SUBLANES)
                lo = q8 * SUBLANES + r0
                if r == 0:
                    win = buf_ref[lo:lo + rc, c0:c0 + lc]
                else:
                    win = shift_ref[r - 1, lo:lo + rc, :]
                acc = acc + dww_ref[k:k + 1, c0:c0 + lc] * win
            conv_ref[r0:r0 + rc, c0:c0 + lc] = acc
    buf_ref[0:CONV_HALO, :] = buf_ref[ts:ts + CONV_HALO, :]
    u = conv_ref[...] + dwb_ref[...]
    mu = jnp.mean(u, axis=-1, keepdims=True)
    dlt = u - mu
    var = jnp.mean(dlt * dlt, axis=-1, keepdims=True)
    u = dlt * lax.rsqrt(var + EPS) * lng_ref[...] + lnb_ref[...]
    u = u * jax.nn.sigmoid(u)
    y = jnp.dot(u.astype(BF16), pw2w_ref[...], preferred_element_type=F32) + pw2b_ref[...]
    x1 = x + gt1_ref[...] * y
    x1_ref[...] = x1
    _pre_moe(x1, n2g_ref[...], sh2_ref[...], sc2_ref[...], rwt_ref[...], rb_ref[...],
             hn_ref, idxt_ref, route_ref, cnt_ref)


def _mix0(x, n1g, sh1, sc1, gt1, pw1w, pw1b, dww, dwb, lng, lnb, pw2w, pw2b,
          n2g, sh2, sc2, rwt, rb, ts):
    b, s, d = x.shape
    n_exp = rwt.shape[0]
    width = dww.shape[0]
    assert width - 1 <= CONV_HALO and ts % MOE_TILE == 0 and ts // MOE_TILE <= SUBLANES
    out_shape, out_specs = _pre_moe_specs(b, s, d, ts, n_exp)
    full = lambda shp: pl.BlockSpec(shp, lambda i, j: (0,) * len(shp))
    return pl.pallas_call(
        _mix0_kernel,
        out_shape=out_shape,
        grid=(b, s // ts),
        in_specs=[
            pl.BlockSpec((None, ts, d), lambda i, j: (i, j, 0)),
            _vec_spec(d), _bvec_spec(d), _bvec_spec(d), _bvec_spec(d),
            full((d, 2 * d)), full((1, 2 * d)), full((width, d)), full((1, d)),
            full((1, d)), full((1, d)), full((d, d)), full((1, d)),
            _vec_spec(d), _bvec_spec(d), _bvec_spec(d),
            full((n_exp, d)), full((n_exp, 1)),
        ],
        out_specs=out_specs,
        scratch_shapes=[
            pltpu.VMEM((CONV_HALO + ts, d), F32),
            pltpu.VMEM((ts, d), F32),
            pltpu.VMEM((SUBLANES - 1, CONV_HALO + ts - SUBLANES, min(512, d)), F32),
        ],
        compiler_params=_cparams(("arbitrary", "arbitrary")),
        name="conformer_mixer",
    )(x, n1g, sh1, sc1, gt1, pw1w, pw1b, dww, dwb, lng, lnb, pw2w, pw2b,
      n2g, sh2, sc2, rwt, rb)


def _mix1_kernel(x_ref, o_ref, gt1_ref, wo_ref, n2g_ref, sh2_ref, sc2_ref, rwt_ref, rb_ref,
                 x1_ref, hn_ref, idxt_ref, route_ref, cnt_ref):
    y = jnp.dot(o_ref[...], wo_ref[...], preferred_element_type=F32)
    x1 = x_ref[...] + gt1_ref[...] * y
    x1_ref[...] = x1
    _pre_moe(x1, n2g_ref[...], sh2_ref[...], sc2_ref[...], rwt_ref[...], rb_ref[...],
             hn_ref, idxt_ref, route_ref, cnt_ref)


def _mix1(x, o, gt1, wo, n2g, sh2, sc2, rwt, rb, ts):
    b, s, d = x.shape
    n_exp = rwt.shape[0]
    do = o.shape[-1]
    assert ts % MOE_TILE == 0 and ts // MOE_TILE <= SUBLANES
    out_shape, out_specs = _pre_moe_specs(b, s, d, ts, n_exp)
    full = lambda shp: pl.BlockSpec(shp, lambda i, j: (0,) * len(shp))
    return pl.pallas_call(
        _mix1_kernel,
        out_shape=out_shape,
        grid=(b, s // ts),
        in_specs=[
            pl.BlockSpec((None, ts, d), lambda i, j: (i, j, 0)),
            pl.BlockSpec((None, ts, do), lambda i, j: (i, j, 0)),
            _bvec_spec(d), full((do, d)),
            _vec_spec(d), _bvec_spec(d), _bvec_spec(d),
            full((n_exp, d)), full((n_exp, 1)),
        ],
        out_specs=out_specs,
        compiler_params=_cparams(("arbitrary", "arbitrary")),
        name="attn_out_mixer",
    )(x, o, gt1, wo, n2g, sh2, sc2, rwt, rb)


def _moe_dims(n_tok, n_exp):
    n_tiles = n_tok // MOE_TILE
    loc_rows = _round_up(TOP_K * MOE_TILE + n_exp * (ROW_ALIGN - 1), LANES)
    max_rows = TOP_K * n_tok + n_tiles * n_exp * (ROW_ALIGN - 1) + n_exp * (EXPERT_ROWS - ROW_ALIGN)
    max_blocks = -(-max_rows // EXPERT_ROWS)
    return loc_rows, max_blocks


def _routing_tables(cnt, max_blocks):
    n_tiles, n_exp = cnt.shape
    cnt8 = _round_up(cnt, ROW_ALIGN)
    off = jnp.cumsum(cnt8, axis=1) - cnt8
    seg_len = cnt8.sum(axis=0)
    seg_pad = _round_up(seg_len, EXPERT_ROWS)
    seg_end = jnp.cumsum(seg_pad)
    seg_start = seg_end - seg_pad
    run_start = seg_start[None, :] + jnp.cumsum(cnt8, axis=0) - cnt8
    n_blocks = (seg_end[-1] // EXPERT_ROWS).astype(I32)
    blk_row = jnp.arange(max_blocks, dtype=I32) * EXPERT_ROWS
    blk_row = jnp.minimum(blk_row, seg_end[-1] - EXPERT_ROWS)
    blk_exp = jnp.minimum((blk_row[:, None] >= seg_end[None, :]).sum(axis=-1), n_exp - 1).astype(I32)
    blk = jnp.arange(max_blocks, dtype=I32)
    prev_exp = jnp.concatenate([jnp.full((1,), -1, I32), blk_exp[:-1]])
    blk_first = ((blk_exp != prev_exp) & (blk < n_blocks)).astype(I32)
    blk_slot = ((jnp.cumsum(blk_first) - 1) % 2).astype(I32)
    e_ids = jnp.arange(n_exp, dtype=I32)
    later = (e_ids[None, :] > e_ids[:, None]) & (seg_pad[None, :] > 0)
    next_exp = jnp.min(jnp.where(later, e_ids[None, :], n_exp), axis=1)
    next_exp = jnp.where(next_exp < n_exp, next_exp, -1)
    blk_next = jnp.sum(jnp.where(blk_exp[:, None] == e_ids[None, :], next_exp[None, :], 0), axis=1)
    flat = lambda a: a.reshape(-1).astype(I32)
    return dict(
        run_loc=flat(off),
        run_glb=flat(run_start),
        run_len=flat(cnt8),
        rows_used=flat(cnt8.sum(axis=1)),
        tail_glb=flat(seg_start + seg_len),
        tail_len=flat(seg_pad - seg_len),
        blk_exp=blk_exp, blk_first=blk_first, blk_slot=blk_slot, blk_next=flat(blk_next),
        n_blocks=n_blocks.reshape(1))


def _aligned(v):
    return pl.multiple_of(v, ROW_ALIGN)


def _dispatch_kernel(n_exp, loc0_ref, glb0_ref, len_ref, used_ref, tail0_ref, taillen_ref, nblk_ref,
                     x_ref, idxt_ref, xs_ref, loc_ref, zero_ref, sem, zsem):
    i = pl.program_id(0)
    n = pl.num_programs(0)
    slot = i % 2
    tile = x_ref.shape[0]
    loc_rows = loc_ref.shape[1]

    def wait_slot(sl, rows):
        rows = _aligned(rows)

        @pl.when(rows > 0)
        def _():
            pltpu.make_async_copy(loc_ref.at[sl, pl.ds(0, rows)], xs_ref.at[pl.ds(0, rows)],
                                  sem.at[sl]).wait()

    @pl.when(i >= 2)
    def _():
        wait_slot(slot, used_ref[i - 2])

    idx = idxt_ref[...]
    e_iota = lax.broadcasted_iota(I32, (n_exp, tile), 0)
    hits = [idx[k:k + 1, :] == e_iota for k in range(TOP_K)]
    onehot = jnp.zeros((n_exp, tile), F32)
    for h in hits:
        onehot = onehot + h.astype(F32)
    upper = (lax.broadcasted_iota(I32, (tile, tile), 0)
             < lax.broadcasted_iota(I32, (tile, tile), 1)).astype(BF16)
    pre = jnp.dot(onehot.astype(BF16), upper, preferred_element_type=F32)
    cnt = jnp.sum(onehot, axis=1, keepdims=True)
    cnt8 = jnp.ceil(cnt / ROW_ALIGN) * ROW_ALIGN
    lower = (lax.broadcasted_iota(I32, (n_exp, n_exp), 1)
             < lax.broadcasted_iota(I32, (n_exp, n_exp), 0)).astype(BF16)
    off = jnp.dot(lower, jnp.broadcast_to(cnt8, (n_exp, LANES)).astype(BF16),
                  preferred_element_type=F32)[:, 0:1]
    base = off + pre
    r_iota = lax.broadcasted_iota(I32, (loc_rows, tile), 0)
    perm = jnp.zeros((loc_rows, tile), F32)
    for h in hits:
        dest = jnp.sum(jnp.where(h, base, 0.0), axis=0, keepdims=True).astype(I32)
        perm = jnp.where(r_iota == dest, 1.0, perm)
    loc_ref[slot] = jnp.dot(perm.astype(BF16), x_ref[...], preferred_element_type=F32)

    def issue_run(e, carry):
        rows = _aligned(len_ref[i * n_exp + e])

        @pl.when(rows > 0)
        def _():
            pltpu.make_async_copy(
                loc_ref.at[slot, pl.ds(_aligned(loc0_ref[i * n_exp + e]), rows)],
                xs_ref.at[pl.ds(_aligned(glb0_ref[i * n_exp + e]), rows)],
                sem.at[slot]).start()
        return carry
    lax.fori_loop(0, n_exp, issue_run, 0)

    @pl.when(i == n - 1)
    def _():
        zero_ref[...] = jnp.zeros(zero_ref.shape, F32)
        max_blocks = xs_ref.shape[0] // EXPERT_ROWS

        def zero_tail(e, total):
            rows = _aligned(taillen_ref[e])

            @pl.when(rows > 0)
            def _():
                pltpu.make_async_copy(zero_ref.at[pl.ds(0, rows)],
                                      xs_ref.at[pl.ds(_aligned(tail0_ref[e]), rows)],
                                      zsem.at[0]).start()
            return total + rows
        tail_rows = lax.fori_loop(0, n_exp, zero_tail, jnp.int32(0))

        def zero_block(blk, carry):
            pltpu.make_async_copy(
                zero_ref,
                xs_ref.at[pl.ds(pl.multiple_of(blk * EXPERT_ROWS, EXPERT_ROWS), EXPERT_ROWS)],
                zsem.at[1]).start()
            return carry
        lax.fori_loop(nblk_ref[0], max_blocks, zero_block, 0)

        @pl.when(i >= 1)
        def _():
            wait_slot(1 - slot, used_ref[i - 1])
        wait_slot(slot, used_ref[i])

        def wait_rows(rows, s):
            rows = _aligned(rows)

            @pl.when(rows > 0)
            def _():
                pltpu.make_async_copy(xs_ref.at[pl.ds(0, rows)], xs_ref.at[pl.ds(0, rows)],
                                      zsem.at[s]).wait()
        wait_rows(tail_rows, 0)
        wait_rows((max_blocks - nblk_ref[0]) * EXPERT_ROWS, 1)


def _dispatch(hn2, idxt, tabs, n_exp, loc_rows, max_blocks):
    n_tok, d = hn2.shape
    n_tiles = n_tok // MOE_TILE
    return pl.pallas_call(
        functools.partial(_dispatch_kernel, n_exp),
        out_shape=jax.ShapeDtypeStruct((max_blocks * EXPERT_ROWS, d), F32),
        grid_spec=pltpu.PrefetchScalarGridSpec(
            num_scalar_prefetch=7,
            grid=(n_tiles,),
            in_specs=[
                pl.BlockSpec((MOE_TILE, d), lambda i, *_: (i, 0)),
                pl.BlockSpec((TOP_K, MOE_TILE), lambda i, *_: (0, i)),
            ],
            out_specs=pl.BlockSpec(memory_space=pl.ANY),
            scratch_shapes=[
                pltpu.VMEM((2, loc_rows, d), F32),
                pltpu.VMEM((EXPERT_ROWS, d), F32),
                pltpu.SemaphoreType.DMA((2,)),
                pltpu.SemaphoreType.DMA((2,)),
            ],
        ),
        compiler_params=_cparams(("arbitrary",)),
        name="moe_dispatch",
    )(tabs["run_loc"], tabs["run_glb"], tabs["run_len"], tabs["rows_used"],
      tabs["tail_glb"], tabs["tail_len"], tabs["n_blocks"], hn2, idxt)


def _expert_kernel(layer, be_ref, first_ref, slot_ref, next_ref, nb_ref,
                   x_ref, wgu_hbm, bgu_ref, wdn_hbm, bdn_ref, y_ref,
                   wgu_f32, wdn_f32, wgu_bf, wdn_bf, sem):
    de = wdn_bf.shape[0]
    b = pl.program_id(0)

    def weight_copies(e, s):
        return (pltpu.make_async_copy(wgu_hbm.at[layer, e], wgu_f32.at[s], sem.at[s, 0]),
                pltpu.make_async_copy(wdn_hbm.at[layer, e], wdn_f32.at[s], sem.at[s, 1]))

    @pl.when(b == 0)
    def _():
        for cp in weight_copies(be_ref[0], 0):
            cp.start()

    @pl.when(first_ref[b] == 1)
    def _():
        s = slot_ref[b]
        for cp in weight_copies(be_ref[b], s):
            cp.wait()

        @pl.when(next_ref[b] >= 0)
        def _():
            for cp in weight_copies(next_ref[b], 1 - s):
                cp.start()
        cw = 512
        for c0 in range(0, wgu_bf.shape[1], cw):
            wgu_bf[:, c0:c0 + cw] = wgu_f32[s, :, c0:c0 + cw].astype(BF16)
        for c0 in range(0, wdn_bf.shape[1], cw):
            wdn_bf[:, c0:c0 + cw] = wdn_f32[s, :, c0:c0 + cw].astype(BF16)

    @pl.when(b < nb_ref[0])
    def _():
        xb = x_ref[...].astype(BF16)
        g = jnp.dot(xb, wgu_bf[:, :de], preferred_element_type=F32) + bgu_ref[:, :de]
        u = jnp.dot(xb, wgu_bf[:, de:], preferred_element_type=F32) + bgu_ref[:, de:]
        g = jnp.minimum(g, SWIGLU_LIMIT)
        u = jnp.clip(u, -SWIGLU_LIMIT, SWIGLU_LIMIT)
        a = (u + 1.0) * g * jax.nn.sigmoid(SWIGLU_ALPHA * g)
        y_ref[...] = jnp.dot(a.astype(BF16), wdn_bf[...], preferred_element_type=F32) + bdn_ref[...]

    @pl.when(pl.program_id(0) >= nb_ref[0])
    def _():
        y_ref[...] = jnp.zeros(y_ref.shape, y_ref.dtype)


def _experts(xs, tabs, layer, wgu, bgu, wdn, bdn):
    rows, d = xs.shape
    depth, n_exp, _, de2 = wgu.shape
    de = de2 // 2
    max_blocks = rows // EXPERT_ROWS
    row_map = lambda b, be, fi, sl, nx, nb: (jnp.minimum(b, nb[0] - 1), 0)
    exp_map = lambda b, be, fi, sl, nx, nb: (layer, be[b], 0, 0)
    return pl.pallas_call(
        functools.partial(_expert_kernel, layer),
        out_shape=jax.ShapeDtypeStruct((rows, d), F32),
        grid_spec=pltpu.PrefetchScalarGridSpec(
            num_scalar_prefetch=5,
            grid=(max_blocks,),
            in_specs=[
                pl.BlockSpec((EXPERT_ROWS, d), row_map),
                pl.BlockSpec(memory_space=pl.ANY),
                pl.BlockSpec((None, None, 1, de2), exp_map),
                pl.BlockSpec(memory_space=pl.ANY),
                pl.BlockSpec((None, None, 1, d), exp_map),
            ],
            out_specs=pl.BlockSpec((EXPERT_ROWS, d), lambda b, *_: (b, 0)),
            scratch_shapes=[
                pltpu.VMEM((2, d, de2), F32), pltpu.VMEM((2, de, d), F32),
                pltpu.VMEM((d, de2), BF16), pltpu.VMEM((de, d), BF16),
                pltpu.SemaphoreType.DMA((2, 2)),
            ],
        ),
        compiler_params=_cparams(("arbitrary",)),
        name="moe_experts",
    )(tabs["blk_exp"], tabs["blk_first"], tabs["blk_slot"], tabs["blk_next"], tabs["n_blocks"],
      xs, wgu, bgu.reshape(depth, n_exp, 1, de2), wdn, bdn.reshape(depth, n_exp, 1, d))


def _combine_kernel(final_norm, n_exp, loc0_ref, glb0_ref, len_ref, used_ref,
                    route_ref, x1_ref, g2_ref, fg_ref, ys_ref, out_ref, loc_ref, sem):
    i = pl.program_id(0)
    n = pl.num_programs(0)
    slot = i % 2
    tile = x1_ref.shape[0]
    loc_rows = loc_ref.shape[1]

    def issue_tile(t, sl):
        def issue_run(e, carry):
            rows = _aligned(len_ref[t * n_exp + e])

            @pl.when(rows > 0)
            def _():
                pltpu.make_async_copy(
                    ys_ref.at[pl.ds(_aligned(glb0_ref[t * n_exp + e]), rows)],
                    loc_ref.at[sl, pl.ds(_aligned(loc0_ref[t * n_exp + e]), rows)],
                    sem.at[sl]).start()
            return carry
        lax.fori_loop(0, n_exp, issue_run, 0)

    @pl.when(i == 0)
    def _():
        issue_tile(0, 0)

    @pl.when(i + 1 < n)
    def _():
        issue_tile(i + 1, 1 - slot)

    route = route_ref[...]
    e_iota = lax.broadcasted_iota(I32, (tile, n_exp), 1)
    hits = [route[:, k:k + 1].astype(I32) == e_iota for k in range(TOP_K)]
    gates = [route[:, TOP_K + k:TOP_K + k + 1] for k in range(TOP_K)]
    onehot = jnp.zeros((tile, n_exp), F32)
    for h in hits:
        onehot = onehot + h.astype(F32)
    lower = (lax.broadcasted_iota(I32, (tile, tile), 1)
             < lax.broadcasted_iota(I32, (tile, tile), 0)).astype(BF16)
    pre = jnp.dot(lower, onehot.astype(BF16), preferred_element_type=F32)
    cnt = jnp.sum(onehot, axis=0, keepdims=True)
    cnt8 = jnp.ceil(cnt / ROW_ALIGN) * ROW_ALIGN
    upper = (lax.broadcasted_iota(I32, (n_exp, n_exp), 0)
             < lax.broadcasted_iota(I32, (n_exp, n_exp), 1)).astype(BF16)
    off = jnp.dot(jnp.broadcast_to(cnt8, (SUBLANES, n_exp)).astype(BF16), upper,
                  preferred_element_type=F32)[0:1, :]
    base = off + pre
    r_iota = lax.broadcasted_iota(I32, (tile, loc_rows), 1)
    comb = jnp.zeros((tile, loc_rows), F32)
    for h, gk in zip(hits, gates):
        dest = jnp.sum(jnp.where(h, base, 0.0), axis=1, keepdims=True).astype(I32)
        comb = jnp.where(r_iota == dest, gk, comb)

    used_rows = _aligned(used_ref[i])

    @pl.when(used_rows > 0)
    def _():
        pltpu.make_async_copy(ys_ref.at[pl.ds(0, used_rows)],
                              loc_ref.at[slot, pl.ds(0, used_rows)], sem.at[slot]).wait()

    row_ok = lax.broadcasted_iota(I32, (loc_rows, 1), 0) < used_rows
    yl = jnp.where(row_ok, loc_ref[slot], 0.0).astype(BF16)
    moe = jnp.dot(comb.astype(BF16), yl, preferred_element_type=F32)
    out = x1_ref[...] + g2_ref[...] * moe
    if final_norm:
        out = _rms(out, fg_ref[...])
    out_ref[...] = out


def _combine(ys, route, x1, gate2, final_g, tabs, n_exp, loc_rows, final_norm):
    b, s, d = x1.shape
    n_tok = b * s
    n_tiles = n_tok // MOE_TILE
    tiles_per_seq = s // MOE_TILE
    out = pl.pallas_call(
        functools.partial(_combine_kernel, final_norm, n_exp),
        out_shape=jax.ShapeDtypeStruct((n_tok, d), F32),
        grid_spec=pltpu.PrefetchScalarGridSpec(
            num_scalar_prefetch=4,
            grid=(n_tiles,),
            in_specs=[
                pl.BlockSpec((MOE_TILE, LANES), lambda i, *_: (i, 0)),
                pl.BlockSpec((MOE_TILE, d), lambda i, *_: (i, 0)),
                pl.BlockSpec((None, 1, d), lambda i, *_: (i // tiles_per_seq, 0, 0)),
                pl.BlockSpec((1, d), lambda i, *_: (0, 0)),
                pl.BlockSpec(memory_space=pl.ANY),
            ],
            out_specs=pl.BlockSpec((MOE_TILE, d), lambda i, *_: (i, 0)),
            scratch_shapes=[
                pltpu.VMEM((2, loc_rows, d), F32),
                pltpu.SemaphoreType.DMA((2,)),
            ],
        ),
        compiler_params=_cparams(("arbitrary",)),
        name="moe_combine",
    )(tabs["run_loc"], tabs["run_glb"], tabs["run_len"], tabs["rows_used"],
      route, x1.reshape(n_tok, d), gate2, final_g, ys)
    return out.reshape(b, s, d)


def _moe(x1, hn2, idxt, route, cnt, gate2, final_g, layer, wgu, bgu, wdn, bdn, final_norm):
    b, s, d = x1.shape
    n_tok = b * s
    n_exp = wgu.shape[1]
    loc_rows, max_blocks = _moe_dims(n_tok, n_exp)
    tabs = _routing_tables(cnt, max_blocks)
    xs = _dispatch(hn2.reshape(n_tok, d), idxt, tabs, n_exp, loc_rows, max_blocks)
    ys = _experts(xs, tabs, layer, wgu, bgu, wdn, bdn)
    return _combine(ys, route, x1, gate2, final_g, tabs, n_exp, loc_rows, final_norm)


def _qkv_kernel(n_heads, scale, x_ref, cc_ref, ss_ref, kvg_ref, wdkv_ref, ckvg_ref, wuk_ref, wuv_ref,
                n1g_ref, sh1_ref, sc1_ref, wdq_ref, cqg_ref, wuq_ref,
                q_ref, k_ref, v_ref):
    x = x_ref[...]
    cc = cc_ref[...]
    ss = ss_ref[...]
    r_kv = ckvg_ref.shape[1]
    hk = _rms(x, kvg_ref[...]).astype(BF16)
    lat = jnp.dot(hk, wdkv_ref[...], preferred_element_type=F32)
    ckv = _rms(lat[:, :r_kv], ckvg_ref[...]).astype(BF16)
    krot = lat[:, r_kv:r_kv + LANES] * cc + lat[:, r_kv + LANES:r_kv + 2 * LANES] * ss
    kn = jnp.dot(ckv, wuk_ref[...], preferred_element_type=F32)
    vt = lax.dot_general(wuv_ref[...], ckv, (((1,), (1,)), ((), ())), preferred_element_type=F32)
    hq = (_rms(x, n1g_ref[...]) * (1.0 + sc1_ref[...]) + sh1_ref[...]).astype(BF16)
    cq = _rms(jnp.dot(hq, wdq_ref[...], preferred_element_type=F32), cqg_ref[...]).astype(BF16)
    qq = jnp.dot(cq, wuq_ref[...], preferred_element_type=F32) * scale
    hd = n_heads * LANES
    ts = x.shape[0]
    ones_rows = (lax.broadcasted_iota(I32, (V_ONES_ROWS, ts), 0) == 0).astype(BF16)
    for h in range(n_heads):
        sl = slice(h * LANES, (h + 1) * LANES)
        k_ref[h, :, 0:LANES] = kn[:, sl].astype(BF16)
        k_ref[h, :, LANES:2 * LANES] = krot.astype(BF16)
        v_ref[h, 0:LANES, :] = vt[h * LANES:(h + 1) * LANES, :].astype(BF16)
        v_ref[h, LANES:LANES + V_ONES_ROWS, :] = ones_rows
        q_ref[h, :, 0:LANES] = qq[:, sl].astype(BF16)
        qrot = qq[:, hd + h * LANES:hd + (h + 1) * LANES] * cc \
            + qq[:, 2 * hd + h * LANES:2 * hd + (h + 1) * LANES] * ss
        q_ref[h, :, LANES:2 * LANES] = qrot.astype(BF16)


def _qkv(x, cc, ss, kvg, wdkv, ckvg, wuk, wuv, n1g, sh1, sc1, wdq, cqg, wuq, n_heads, scale, ts):
    b, s, d = x.shape
    full = lambda a: pl.BlockSpec(a.shape, lambda i, j: (0,) * a.ndim)
    hspec = lambda w: pl.BlockSpec((None, n_heads, ts, w), lambda i, j: (i, 0, j, 0))
    return pl.pallas_call(
        functools.partial(_qkv_kernel, n_heads, scale),
        out_shape=[
            jax.ShapeDtypeStruct((b, n_heads, s, 2 * LANES), BF16),
            jax.ShapeDtypeStruct((b, n_heads, s, 2 * LANES), BF16),
            jax.ShapeDtypeStruct((b, n_heads, s // ts, LANES + V_ONES_ROWS, ts), BF16),
        ],
        grid=(b, s // ts),
        in_specs=[
            pl.BlockSpec((None, ts, d), lambda i, j: (i, j, 0)),
            pl.BlockSpec((None, ts, LANES), lambda i, j: (i, j, 0)),
            pl.BlockSpec((None, ts, LANES), lambda i, j: (i, j, 0)),
            full(kvg), full(wdkv), full(ckvg), full(wuk), full(wuv),
            full(n1g), _bvec_spec(d), _bvec_spec(d), full(wdq), full(cqg), full(wuq),
        ],
        out_specs=[hspec(2 * LANES), hspec(2 * LANES),
                   pl.BlockSpec((None, n_heads, None, LANES + V_ONES_ROWS, ts),
                                lambda i, j: (i, 0, j, 0, 0))],
        compiler_params=_cparams(("arbitrary", "arbitrary")),
        name="mla_qkv",
    )(x, cc, ss, kvg, wdkv, ckvg, wuk, wuv, n1g, sh1, sc1, wdq, cqg, wuq)


def _attn_kernel(q_ref, k_ref, vt_ref, o_ref, m_ref, acc_ref):
    tq = q_ref.shape[0]
    tk = vt_ref.shape[2]
    ratio = tq // tk
    dv = o_ref.shape[1]
    qi = pl.program_id(2)
    m_ref[...] = jnp.full(m_ref.shape, -jnp.inf, F32)
    acc_ref[...] = jnp.zeros(acc_ref.shape, F32)

    def scores(ki, cols, on_diagonal):
        start = pl.multiple_of(ki * tk, tk)
        k = k_ref[pl.ds(start, tk), :]
        st = lax.dot_general(k, q_ref[cols, :], (((1,), (1,)), ((), ())),
                             preferred_element_type=F32)
        if on_diagonal:
            kc = lax.broadcasted_iota(I32, st.shape, 0) // CHUNK
            qc = lax.broadcasted_iota(I32, st.shape, 1) // CHUNK
            st = jnp.where(kc <= qc, st, -jnp.inf)
        return st

    def update(ki, st, m_old, acc_old):
        m_new = jnp.maximum(m_old, jnp.max(st, axis=0, keepdims=True))
        p = jnp.exp2(st - m_new).astype(BF16)
        alpha = jnp.exp2(m_old - m_new)
        acc_new = alpha * acc_old + jnp.dot(vt_ref[ki], p, preferred_element_type=F32)
        return m_new, acc_new

    def step(groups):
        sts = [[scores(ki, cols, on_diagonal) for ki, on_diagonal in tiles]
               for cols, tiles in groups]
        for (cols, tiles), group_sts in zip(groups, sts):
            state = (m_ref[:, cols], acc_ref[:, cols])
            for (ki, _), st in zip(tiles, group_sts):
                state = update(ki, st, *state)
            m_ref[:, cols], acc_ref[:, cols] = state

    n_below = qi * ratio
    everything = slice(0, tq)
    step([(slice(g * tk, (g + 1) * tk), [(n_below + a, a == g) for a in range(g + 1)])
          for g in range(ratio)])

    def fast_step(first, count):
        m = m_ref[...]
        pv = None
        top = None
        for t in range(count):
            st = scores(first + t, everything, False)
            tmax = jnp.max(st, axis=0, keepdims=True)
            top = tmax if top is None else jnp.maximum(top, tmax)
            p = jnp.exp2(st - m).astype(BF16)
            part = jnp.dot(vt_ref[first + t], p, preferred_element_type=F32)
            pv = part if pv is None else pv + part
        safe = jnp.max(top - m) <= ATTN_MAX_SLACK

        @pl.when(safe)
        def _():
            acc_ref[...] += pv

        @pl.when(jnp.logical_not(safe))
        def _():
            def redo(t, carry):
                step([(everything, [(first + t, False)])])
                return carry
            lax.fori_loop(0, count, redo, 0)

    def body(j, carry):
        fast_step(ATTN_UNROLL * j, ATTN_UNROLL)
        return carry
    trips = n_below // ATTN_UNROLL
    lax.fori_loop(0, trips, body, 0)

    for rem in range(math.gcd(ratio, ATTN_UNROLL), ATTN_UNROLL, math.gcd(ratio, ATTN_UNROLL)):
        @pl.when(n_below % ATTN_UNROLL == rem)
        def _(rem=rem):
            fast_step(trips * ATTN_UNROLL, rem)

    o_ref[...] = (acc_ref[0:dv, :] / acc_ref[dv:dv + 1, :]).T.astype(o_ref.dtype)


def _attention(q, k, vt, tq):
    b, h, s, dk = q.shape
    nk, dv_ext, tk = vt.shape[2:]
    dv = dv_ext - V_ONES_ROWS
    assert tk % CHUNK == 0 and s % tq == 0 and tq % tk == 0
    return pl.pallas_call(
        _attn_kernel,
        out_shape=jax.ShapeDtypeStruct((b, s, h * dv), BF16),
        grid=(b, h, s // tq),
        in_specs=[
            pl.BlockSpec((None, None, tq, dk), lambda i, j, t: (i, j, t, 0)),
            pl.BlockSpec((None, None, s, dk), lambda i, j, t: (i, j, 0, 0)),
            pl.BlockSpec((None, None, nk, dv_ext, tk), lambda i, j, t: (i, j, 0, 0, 0)),
        ],
        out_specs=pl.BlockSpec((None, tq, dv), lambda i, j, t: (i, t, j)),
        scratch_shapes=[pltpu.VMEM((1, tq), F32), pltpu.VMEM((dv_ext, tq), F32)],
        compiler_params=_cparams(("arbitrary", "arbitrary", "arbitrary")),
        name="mla_attention",
    )(q, k, vt)


def _swap_halves(w):
    half = w.shape[-1] // 2
    return jnp.concatenate([w[..., half:], w[..., :half]], axis=-1)


def _pad_lanes(w):
    pad = LANES - w.shape[-1]
    return jnp.concatenate([w, jnp.zeros(w.shape[:-1] + (pad,), w.dtype)], axis=-1)


def kernel(x, c, positions, mod_w, mod_b, norm1_g, norm2_g, conv_pw1_w, conv_pw1_b, conv_dw_w, conv_dw_b, conv_ln_g, conv_ln_b, conv_pw2_w, conv_pw2_b, kv_norm_g, w_dkv, ckv_norm_g, w_uk, w_uv, w_dq, cq_norm_g, w_uq, w_o, router_w, router_b, exp_w_gu, exp_b_gu, exp_w_dn, exp_b_dn, final_g):
    b, s, d = x.shape
    n_heads, nope = w_uk.shape[1], w_uk.shape[2]
    r_kv = ckv_norm_g.shape[0]
    rope = w_dkv.shape[1] - r_kv
    vdim = w_uv.shape[2]
    n_exp = router_w.shape[2]
    assert nope == LANES and vdim == LANES and rope <= LANES and d % LANES == 0
    ts = min(512, s)
    tq = min(1024, s)

    mod = _modulation(c, mod_w, mod_b)
    mods = [[m.reshape(b, 1, d) for m in jnp.split(mod[l], 6, axis=-1)] for l in range(2)]
    row = lambda v: v.reshape(1, -1)

    half = rope // 2
    inv = jnp.exp(-(2.0 * math.log(ROPE_THETA) / rope) * jnp.arange(half, dtype=F32))
    ang = positions.astype(F32)[..., None] * inv
    cos, sin = jnp.cos(ang), jnp.sin(ang)
    cc = _pad_lanes(jnp.concatenate([cos, cos], axis=-1))
    ss = _pad_lanes(jnp.concatenate([-sin, sin], axis=-1))

    sh1, sc1, gt1, sh2, sc2, gt2 = mods[0]
    x1, hn2, idxt, route, cnt = _mix0(
        x, row(norm1_g[0]), sh1, sc1, gt1,
        conv_pw1_w[0].astype(BF16), row(conv_pw1_b[0]), conv_dw_w[0], row(conv_dw_b[0]),
        row(conv_ln_g[0]), row(conv_ln_b[0]), conv_pw2_w[0].astype(BF16), row(conv_pw2_b[0]),
        row(norm2_g[0]), sh2, sc2, router_w[0].T, router_b[0].reshape(n_exp, 1), ts)
    cnt = cnt[:, :ts // MOE_TILE].reshape(-1, n_exp)
    x2 = _moe(x1, hn2, idxt, route, cnt, gt2, row(final_g),
              0, exp_w_gu, exp_b_gu, exp_w_dn, exp_b_dn, False)

    sh1, sc1, gt1, sh2, sc2, gt2 = mods[1]
    wdkv_rope = w_dkv[:, r_kv:]
    wdkv_ext = jnp.concatenate(
        [w_dkv[:, :r_kv], _pad_lanes(wdkv_rope), _pad_lanes(_swap_halves(wdkv_rope))], axis=-1)
    wuq = w_uq[0]
    r_q = wuq.shape[0]
    wuq_rope = wuq[:, :, nope:]
    wuq_ext = jnp.concatenate([
        wuq[:, :, :nope].reshape(r_q, n_heads * LANES),
        _pad_lanes(wuq_rope).reshape(r_q, n_heads * LANES),
        _pad_lanes(_swap_halves(wuq_rope)).reshape(r_q, n_heads * LANES)], axis=-1)
    scale = float((nope + rope) ** -0.5 * math.log2(math.e))
    q, k, v = _qkv(
        x2, cc, ss, row(kv_norm_g), wdkv_ext.astype(BF16), row(ckv_norm_g),
        w_uk.reshape(r_kv, n_heads * nope).astype(BF16), w_uv.reshape(r_kv, n_heads * vdim).T.astype(BF16),
        row(norm1_g[1]), sh1, sc1, w_dq[0].astype(BF16), row(cq_norm_g[0]), wuq_ext.astype(BF16),
        n_heads, scale, ts)
    o = _attention(q, k, v, tq)

    x3, hn2, idxt, route, cnt = _mix1(
        x2, o, gt1, w_o[0].astype(BF16), row(norm2_g[1]), sh2, sc2,
        router_w[1].T, router_b[1].reshape(n_exp, 1), ts)
    cnt = cnt[:, :ts // MOE_TILE].reshape(-1, n_exp)
    return _moe(x3, hn2, idxt, route, cnt, gt2, row(final_g),
                1, exp_w_gu, exp_b_gu, exp_w_dn, exp_b_dn, True)
```

```python
import functools
import math

import jax
import jax.numpy as jnp
from jax import lax
from jax.experimental import pallas as pl
from jax.experimental.pallas import tpu as pltpu

CHUNK = 64
TOP_K = 4
ROPE_THETA = 10000.0
SWIGLU_ALPHA = 1.702
SWIGLU_LIMIT = 7.0
EPS = 1e-6

LANES = 128
SUBLANES = 8
VMEM_LIMIT_BYTES = 56 * 1024 * 1024

ROW_ALIGN = SUBLANES
MOE_TILE = 256
EXPERT_ROWS = 512
CONV_HALO = 32
V_ONES_ROWS = 2 * SUBLANES
ATTN_UNROLL = 4
ATTN_MAX_SLACK = 60.0

F32 = jnp.float32
BF16 = jnp.bfloat16
I32 = jnp.int32


def _cparams(sem):
    return pltpu.CompilerParams(dimension_semantics=sem, vmem_limit_bytes=VMEM_LIMIT_BYTES)


def _rms(x, g):
    return x * lax.rsqrt(jnp.mean(x * x, axis=-1, keepdims=True) + EPS) * g


def _round_up(a, m):
    return (a + m - 1) // m * m


def _mod_kernel(c_ref, w_ref, b_ref, o_ref):
    c = c_ref[...]
    ca = c * jax.nn.sigmoid(c)
    o_ref[...] = jnp.dot(ca, w_ref[...], preferred_element_type=F32) + b_ref[...]


def _modulation(c, mod_w, mod_b):
    depth, d, d6 = mod_w.shape
    b = c.shape[0]
    bp = _round_up(b, SUBLANES)
    cp = jnp.zeros((bp, d), F32).at[:b].set(c)
    tn = d6 // 4
    out = pl.pallas_call(
        _mod_kernel,
        out_shape=jax.ShapeDtypeStruct((depth, bp, d6), F32),
        grid=(depth, d6 // tn),
        in_specs=[
            pl.BlockSpec((bp, d), lambda l, j: (0, 0)),
            pl.BlockSpec((None, d, tn), lambda l, j: (l, 0, j)),
            pl.BlockSpec((None, 1, tn), lambda l, j: (l, 0, j)),
        ],
        out_specs=pl.BlockSpec((None, bp, tn), lambda l, j: (l, 0, j)),
        compiler_params=_cparams(("arbitrary", "arbitrary")),
        name="adaln_modulation",
    )(cp, mod_w, mod_b.reshape(depth, 1, d6))
    return out[:, :b]


def _pre_moe(x1, g2, sh2, sc2, rwt, rb, hn_ref, idxt_ref, route_ref, cnt_ref):
    ts = x1.shape[0]
    n_exp = rwt.shape[0]
    hn = _rms(x1, g2) * (1.0 + sc2) + sh2
    hn_ref[...] = hn.astype(BF16)
    def split(v):
        hi = v.astype(BF16)
        return hi, (v - hi.astype(F32)).astype(BF16)
    nt = lambda a, b: lax.dot_general(a, b, (((1,), (1,)), ((), ())), preferred_element_type=F32)
    w_hi, w_lo = split(rwt)
    h_hi, h_lo = split(hn)
    logits = nt(w_hi, h_hi) + (nt(w_hi, h_lo) + nt(w_lo, h_hi)) + rb
    e_iota = lax.broadcasted_iota(I32, (n_exp, ts), 0)
    vals, idxs = [], []
    cur = logits
    for _ in range(TOP_K):
        m = jnp.max(cur, axis=0, keepdims=True)
        i = jnp.min(jnp.where(cur == m, e_iota, n_exp), axis=0, keepdims=True)
        vals.append(m)
        idxs.append(i)
        cur = jnp.where(e_iota == i, -jnp.inf, cur)
    exps = [jnp.exp(v - vals[0]) for v in vals]
    den = exps[0]
    for e in exps[1:]:
        den = den + e
    gates = [e / den for e in exps]
    idxt_ref[...] = jnp.concatenate(idxs, axis=0)
    rows = jnp.concatenate([i.astype(F32) for i in idxs] + gates
                           + [jnp.zeros((LANES - 2 * TOP_K, ts), F32)], axis=0)
    route_ref[...] = rows.T
    onehot = jnp.zeros((n_exp, ts), F32)
    for i in idxs:
        onehot = onehot + (e_iota == i).astype(F32)
    sel = (lax.broadcasted_iota(I32, (SUBLANES, ts), 1) // MOE_TILE
           == lax.broadcasted_iota(I32, (SUBLANES, ts), 0)).astype(BF16)
    cnt = lax.dot_general(sel, onehot.astype(BF16), (((1,), (1,)), ((), ())),
                          preferred_element_type=F32)
    cnt_ref[...] = cnt.astype(I32)


def _pre_moe_specs(b, s, d, ts, n_exp):
    ns = s // ts
    out_shape = [
        jax.ShapeDtypeStruct((b, s, d), F32),
        jax.ShapeDtypeStruct((b, s, d), BF16),
        jax.ShapeDtypeStruct((TOP_K, b * s), I32),
        jax.ShapeDtypeStruct((b * s, LANES), F32),
        jax.ShapeDtypeStruct((b * ns, SUBLANES, n_exp), I32),
    ]
    out_specs = [
        pl.BlockSpec((None, ts, d), lambda i, j: (i, j, 0)),
        pl.BlockSpec((None, ts, d), lambda i, j: (i, j, 0)),
        pl.BlockSpec((TOP_K, ts), lambda i, j: (0, i * ns + j)),
        pl.BlockSpec((ts, LANES), lambda i, j: (i * ns + j, 0)),
        pl.BlockSpec((None, SUBLANES, n_exp), lambda i, j: (i * ns + j, 0, 0)),
    ]
    return out_shape, out_specs


def _vec_spec(d):
    return pl.BlockSpec((1, d), lambda i, j: (0, 0))


def _bvec_spec(d):
    return pl.BlockSpec((None, 1, d), lambda i, j: (i, 0, 0))


def _mix0_kernel(x_ref, n1g_ref, sh1_ref, sc1_ref, gt1_ref,
                 pw1w_ref, pw1b_ref, dww_ref, dwb_ref, lng_ref, lnb_ref,
                 pw2w_ref, pw2b_ref, n2g_ref, sh2_ref, sc2_ref, rwt_ref, rb_ref,
                 x1_ref, hn_ref, idxt_ref, route_ref, cnt_ref, buf_ref, conv_ref, shift_ref):
    ts, d = x_ref.shape
    width = dww_ref.shape[0]
    x = x_ref[...]
    hn = _rms(x, n1g_ref[...]) * (1.0 + sc1_ref[...]) + sh1_ref[...]
    hb = hn.astype(BF16)
    a = jnp.dot(hb, pw1w_ref[:, :d], preferred_element_type=F32) + pw1b_ref[:, :d]
    g = jnp.dot(hb, pw1w_ref[:, d:], preferred_element_type=F32) + pw1b_ref[:, d:]
    glu = a * jax.nn.sigmoid(g)

    @pl.when(pl.program_id(1) == 0)
    def _():
        buf_ref[0:CONV_HALO, :] = jnp.zeros((CONV_HALO, d), F32)

    buf_ref[CONV_HALO:, :] = glu
    base = CONV_HALO - (width - 1)
    rc = 32
    lc = min(512, d)
    sh_rows = shift_ref.shape[1]
    sub = lax.broadcasted_iota(I32, (SUBLANES, lc), 0)
    for c0 in range(0, d, lc):
        for r in range(1, SUBLANES):
            from_this = sub < SUBLANES - r
            cur = pltpu.roll(buf_ref[0:SUBLANES, c0:c0 + lc], SUBLANES - r, axis=0)
            for j0 in range(0, sh_rows, SUBLANES):
                nxt = pltpu.roll(buf_ref[j0 + SUBLANES:j0 + 2 * SUBLANES, c0:c0 + lc],
                                 SUBLANES - r, axis=0)
                shift_ref[r - 1, j0:j0 + SUBLANES, :] = jnp.where(from_this, cur, nxt)
                cur = nxt
        for r0 in range(0, ts, rc):
            acc = jnp.zeros((rc, lc), F32)
            for k in range(width):
                q8, r = divmod(base + k, SUBLANES)
                lo = q8 * SUBLANES + r0
                if r == 0:
                    win = buf_ref[lo:lo + rc, c0:c0 + lc]
                else:
                    win = shift_ref[r - 1, lo:lo + rc, :]
                acc = acc + dww_ref[k:k + 1, c0:c0 + lc] * win
            conv_ref[r0:r0 + rc, c0:c0 + lc] = acc
    buf_ref[0:CONV_HALO, :] = buf_ref[ts:ts + CONV_HALO, :]
    u = conv_ref[...] + dwb_ref[...]
    mu = jnp.mean(u, axis=-1, keepdims=True)
    dlt = u - mu
    var = jnp.mean(dlt * dlt, axis=-1, keepdims=True)
    u = dlt * lax.rsqrt(var + EPS) * lng_ref[...] + lnb_ref[...]
    u = u * jax.nn.sigmoid(u)
    y = jnp.dot(u.astype(BF16), pw2w_ref[...], preferred_element_type=F32) + pw2b_ref[...]
    x1 = x + gt1_ref[...] * y
    x1_ref[...] = x1
    _pre_moe(x1, n2g_ref[...], sh2_ref[...], sc2_ref[...], rwt_ref[...], rb_ref[...],
             hn_ref, idxt_ref, route_ref, cnt_ref)


def _mix0(x, n1g, sh1, sc1, gt1, pw1w, pw1b, dww, dwb, lng, lnb, pw2w, pw2b,
          n2g, sh2, sc2, rwt, rb, ts):
    b, s, d = x.shape
    n_exp = rwt.shape[0]
    width = dww.shape[0]
    assert width - 1 <= CONV_HALO and ts % MOE_TILE == 0 and ts // MOE_TILE <= SUBLANES
    out_shape, out_specs = _pre_moe_specs(b, s, d, ts, n_exp)
    full = lambda shp: pl.BlockSpec(shp, lambda i, j: (0,) * len(shp))
    return pl.pallas_call(
        _mix0_kernel,
        out_shape=out_shape,
        grid=(b, s // ts),
        in_specs=[
            pl.BlockSpec((None, ts, d), lambda i, j: (i, j, 0)),
            _vec_spec(d), _bvec_spec(d), _bvec_spec(d), _bvec_spec(d),
            full((d, 2 * d)), full((1, 2 * d)), full((width, d)), full((1, d)),
            full((1, d)), full((1, d)), full((d, d)), full((1, d)),
            _vec_spec(d), _bvec_spec(d), _bvec_spec(d),
            full((n_exp, d)), full((n_exp, 1)),
        ],
        out_specs=out_specs,
        scratch_shapes=[
            pltpu.VMEM((CONV_HALO + ts, d), F32),
            pltpu.VMEM((ts, d), F32),
            pltpu.VMEM((SUBLANES - 1, CONV_HALO + ts - SUBLANES, min(512, d)), F32),
        ],
        compiler_params=_cparams(("arbitrary", "arbitrary")),
        name="conformer_mixer",
    )(x, n1g, sh1, sc1, gt1, pw1w, pw1b, dww, dwb, lng, lnb, pw2w, pw2b,
      n2g, sh2, sc2, rwt, rb)


def _mix1_kernel(x_ref, o_ref, gt1_ref, wo_ref, n2g_ref, sh2_ref, sc2_ref, rwt_ref, rb_ref,
                 x1_ref, hn_ref, idxt_ref, route_ref, cnt_ref):
    y = jnp.dot(o_ref[...], wo_ref[...], preferred_element_type=F32)
    x1 = x_ref[...] + gt1_ref[...] * y
    x1_ref[...] = x1
    _pre_moe(x1, n2g_ref[...], sh2_ref[...], sc2_ref[...], rwt_ref[...], rb_ref[...],
             hn_ref, idxt_ref, route_ref, cnt_ref)


def _mix1(x, o, gt1, wo, n2g, sh2, sc2, rwt, rb, ts):
    b, s, d = x.shape
    n_exp = rwt.shape[0]
    do = o.shape[-1]
    assert ts % MOE_TILE == 0 and ts // MOE_TILE <= SUBLANES
    out_shape, out_specs = _pre_moe_specs(b, s, d, ts, n_exp)
    full = lambda shp: pl.BlockSpec(shp, lambda i, j: (0,) * len(shp))
    return pl.pallas_call(
        _mix1_kernel,
        out_shape=out_shape,
        grid=(b, s // ts),
        in_specs=[
            pl.BlockSpec((None, ts, d), lambda i, j: (i, j, 0)),
            pl.BlockSpec((None, ts, do), lambda i, j: (i, j, 0)),
            _bvec_spec(d), full((do, d)),
            _vec_spec(d), _bvec_spec(d), _bvec_spec(d),
            full((n_exp, d)), full((n_exp, 1)),
        ],
        out_specs=out_specs,
        compiler_params=_cparams(("arbitrary", "arbitrary")),
        name="attn_out_mixer",
    )(x, o, gt1, wo, n2g, sh2, sc2, rwt, rb)


def _moe_dims(n_tok, n_exp):
    n_tiles = n_tok // MOE_TILE
    loc_rows = _round_up(TOP_K * MOE_TILE + n_exp * (ROW_ALIGN - 1), LANES)
    max_rows = TOP_K * n_tok + n_tiles * n_exp * (ROW_ALIGN - 1) + n_exp * (EXPERT_ROWS - ROW_ALIGN)
    max_blocks = -(-max_rows // EXPERT_ROWS)
    return loc_rows, max_blocks


def _routing_tables(cnt, max_blocks):
    n_tiles, n_exp = cnt.shape
    cnt8 = _round_up(cnt, ROW_ALIGN)
    off = jnp.cumsum(cnt8, axis=1) - cnt8
    seg_len = cnt8.sum(axis=0)
    seg_pad = _round_up(seg_len, EXPERT_ROWS)
    seg_end = jnp.cumsum(seg_pad)
    seg_start = seg_end - seg_pad
    run_start = seg_start[None, :] + jnp.cumsum(cnt8, axis=0) - cnt8
    n_blocks = (seg_end[-1] // EXPERT_ROWS).astype(I32)
    blk_row = jnp.arange(max_blocks + 1, dtype=I32) * EXPERT_ROWS
    blk_row = jnp.minimum(blk_row, seg_end[-1] - EXPERT_ROWS)
    blk_exp = jnp.minimum((blk_row[:, None] >= seg_end[None, :]).sum(axis=-1), n_exp - 1).astype(I32)
    blk = jnp.arange(max_blocks + 1, dtype=I32)
    prev_exp = jnp.concatenate([jnp.full((1,), -1, I32), blk_exp[:-1]])
    blk_first = ((blk_exp != prev_exp) & (blk < n_blocks)).astype(I32)
    blk_slot = ((jnp.cumsum(blk_first) - 1) % 2).astype(I32)
    e_ids = jnp.arange(n_exp, dtype=I32)
    later = (e_ids[None, :] > e_ids[:, None]) & (seg_pad[None, :] > 0)
    next_exp = jnp.min(jnp.where(later, e_ids[None, :], n_exp), axis=1)
    next_exp = jnp.where(next_exp < n_exp, next_exp, -1)
    blk_next = jnp.sum(jnp.where(blk_exp[:, None] == e_ids[None, :], next_exp[None, :], 0), axis=1)
    flat = lambda a: a.reshape(-1).astype(I32)
    return dict(
        run_loc=flat(off),
        run_glb=flat(run_start),
        run_len=flat(cnt8),
        rows_used=flat(cnt8.sum(axis=1)),
        tail_glb=flat(seg_start + seg_len),
        tail_len=flat(seg_pad - seg_len),
        blk_exp=blk_exp, blk_first=blk_first, blk_slot=blk_slot, blk_next=flat(blk_next),
        n_blocks=n_blocks.reshape(1))


def _aligned(v):
    return pl.multiple_of(v, ROW_ALIGN)


def _dispatch_kernel(n_exp, loc0_ref, glb0_ref, len_ref, used_ref, tail0_ref, taillen_ref, nblk_ref,
                     x_ref, idxt_ref, xs_ref, loc_ref, zero_ref, sem, zsem):
    i = pl.program_id(0)
    n = pl.num_programs(0)
    slot = i % 2
    tile = x_ref.shape[0]
    loc_rows = loc_ref.shape[1]

    def wait_slot(sl, rows):
        rows = _aligned(rows)

        @pl.when(rows > 0)
        def _():
            pltpu.make_async_copy(loc_ref.at[sl, pl.ds(0, rows)], xs_ref.at[pl.ds(0, rows)],
                                  sem.at[sl]).wait()

    @pl.when(i >= 2)
    def _():
        wait_slot(slot, used_ref[i - 2])

    idx = idxt_ref[...]
    e_iota = lax.broadcasted_iota(I32, (n_exp, tile), 0)
    hits = [idx[k:k + 1, :] == e_iota for k in range(TOP_K)]
    onehot = jnp.zeros((n_exp, tile), F32)
    for h in hits:
        onehot = onehot + h.astype(F32)
    upper = (lax.broadcasted_iota(I32, (tile, tile), 0)
             < lax.broadcasted_iota(I32, (tile, tile), 1)).astype(BF16)
    pre = jnp.dot(onehot.astype(BF16), upper, preferred_element_type=F32)
    cnt = jnp.sum(onehot, axis=1, keepdims=True)
    cnt8 = jnp.ceil(cnt / ROW_ALIGN) * ROW_ALIGN
    lower = (lax.broadcasted_iota(I32, (n_exp, n_exp), 1)
             < lax.broadcasted_iota(I32, (n_exp, n_exp), 0)).astype(BF16)
    off = jnp.dot(lower, jnp.broadcast_to(cnt8, (n_exp, LANES)).astype(BF16),
                  preferred_element_type=F32)[:, 0:1]
    base = off + pre
    r_iota = lax.broadcasted_iota(I32, (loc_rows, tile), 0)
    perm = jnp.zeros((loc_rows, tile), F32)
    for h in hits:
        dest = jnp.sum(jnp.where(h, base, 0.0), axis=0, keepdims=True).astype(I32)
        perm = jnp.where(r_iota == dest, 1.0, perm)
    loc_ref[slot] = jnp.dot(perm.astype(BF16), x_ref[...], preferred_element_type=F32)

    def issue_run(e, carry):
        rows = _aligned(len_ref[i * n_exp + e])

        @pl.when(rows > 0)
        def _():
            pltpu.make_async_copy(
                loc_ref.at[slot, pl.ds(_aligned(loc0_ref[i * n_exp + e]), rows)],
                xs_ref.at[pl.ds(_aligned(glb0_ref[i * n_exp + e]), rows)],
                sem.at[slot]).start()
        return carry
    lax.fori_loop(0, n_exp, issue_run, 0)

    @pl.when(i == n - 1)
    def _():
        zero_ref[...] = jnp.zeros(zero_ref.shape, F32)
        max_blocks = xs_ref.shape[0] // EXPERT_ROWS

        def zero_tail(e, total):
            rows = _aligned(taillen_ref[e])

            @pl.when(rows > 0)
            def _():
                pltpu.make_async_copy(zero_ref.at[pl.ds(0, rows)],
                                      xs_ref.at[pl.ds(_aligned(tail0_ref[e]), rows)],
                                      zsem.at[0]).start()
            return total + rows
        tail_rows = lax.fori_loop(0, n_exp, zero_tail, jnp.int32(0))

        def zero_block(blk, carry):
            pltpu.make_async_copy(
                zero_ref,
                xs_ref.at[pl.ds(pl.multiple_of(blk * EXPERT_ROWS, EXPERT_ROWS), EXPERT_ROWS)],
                zsem.at[1]).start()
            return carry
        lax.fori_loop(nblk_ref[0], max_blocks, zero_block, 0)

        @pl.when(i >= 1)
        def _():
            wait_slot(1 - slot, used_ref[i - 1])
        wait_slot(slot, used_ref[i])

        def wait_rows(rows, s):
            rows = _aligned(rows)

            @pl.when(rows > 0)
            def _():
                pltpu.make_async_copy(xs_ref.at[pl.ds(0, rows)], xs_ref.at[pl.ds(0, rows)],
                                      zsem.at[s]).wait()
        wait_rows(tail_rows, 0)
        wait_rows((max_blocks - nblk_ref[0]) * EXPERT_ROWS, 1)


def _dispatch(hn2, idxt, tabs, n_exp, loc_rows, max_blocks):
    n_tok, d = hn2.shape
    n_tiles = n_tok // MOE_TILE
    return pl.pallas_call(
        functools.partial(_dispatch_kernel, n_exp),
        out_shape=jax.ShapeDtypeStruct((max_blocks * EXPERT_ROWS, d), F32),
        grid_spec=pltpu.PrefetchScalarGridSpec(
            num_scalar_prefetch=7,
            grid=(n_tiles,),
            in_specs=[
                pl.BlockSpec((MOE_TILE, d), lambda i, *_: (i, 0)),
                pl.BlockSpec((TOP_K, MOE_TILE), lambda i, *_: (0, i)),
            ],
            out_specs=pl.BlockSpec(memory_space=pl.ANY),
            scratch_shapes=[
                pltpu.VMEM((2, loc_rows, d), F32),
                pltpu.VMEM((EXPERT_ROWS, d), F32),
                pltpu.SemaphoreType.DMA((2,)),
                pltpu.SemaphoreType.DMA((2,)),
            ],
        ),
        compiler_params=_cparams(("arbitrary",)),
        name="moe_dispatch",
    )(tabs["run_loc"], tabs["run_glb"], tabs["run_len"], tabs["rows_used"],
      tabs["tail_glb"], tabs["tail_len"], tabs["n_blocks"], hn2, idxt)


def _expert_kernel(layer, be_ref, first_ref, slot_ref, next_ref, nb_ref,
                   x_ref, wgu_hbm, bgu_ref, wdn_hbm, bdn_ref, y_ref,
                   wgu_f32, wdn_f32, wgu_bf, wdn_bf, act_ref, sem):
    de = wdn_bf.shape[1]
    b = pl.program_id(0)
    n_blocks = nb_ref[0]

    def weight_copies(e, s):
        return (pltpu.make_async_copy(wgu_hbm.at[layer, e], wgu_f32.at[s], sem.at[s, 0]),
                pltpu.make_async_copy(wdn_hbm.at[layer, e], wdn_f32.at[s], sem.at[s, 1]))

    @pl.when(b == 0)
    def _():
        for cp in weight_copies(be_ref[0], 0):
            cp.start()

    @pl.when(first_ref[b] == 1)
    def _():
        s = slot_ref[b]
        for cp in weight_copies(be_ref[b], s):
            cp.wait()

        @pl.when(next_ref[b] >= 0)
        def _():
            for cp in weight_copies(next_ref[b], 1 - s):
                cp.start()
        cw = 512
        for c0 in range(0, wgu_bf.shape[1], cw):
            wgu_bf[:, c0:c0 + cw] = wgu_f32[s, :, c0:c0 + cw].astype(BF16)
        for c0 in range(0, wdn_bf.shape[2], cw):
            wdn_bf[s, :, c0:c0 + cw] = wdn_f32[s, :, c0:c0 + cw].astype(BF16)

    def gate_up():
        xb = x_ref[...].astype(BF16)
        g = jnp.dot(xb, wgu_bf[:, :de], preferred_element_type=F32) + bgu_ref[:, :de]
        u = jnp.dot(xb, wgu_bf[:, de:], preferred_element_type=F32) + bgu_ref[:, de:]
        g = jnp.minimum(g, SWIGLU_LIMIT)
        u = jnp.clip(u, -SWIGLU_LIMIT, SWIGLU_LIMIT)
        act_ref[...] = ((u + 1.0) * g * jax.nn.sigmoid(SWIGLU_ALPHA * g)).astype(BF16)

    def down():
        w = wdn_bf[slot_ref[jnp.maximum(b - 1, 0)]]
        y_ref[...] = jnp.dot(act_ref[...], w, preferred_element_type=F32) + bdn_ref[...]

    @pl.when(b == 0)
    def _():
        gate_up()

    @pl.when((b >= 1) & (b < n_blocks))
    def _():
        down()
        gate_up()

    @pl.when(b == n_blocks)
    def _():
        down()

    @pl.when(b > n_blocks)
    def _():
        y_ref[...] = jnp.zeros(y_ref.shape, y_ref.dtype)


def _experts(xs, tabs, layer, wgu, bgu, wdn, bdn):
    rows, d = xs.shape
    depth, n_exp, _, de2 = wgu.shape
    de = de2 // 2
    max_blocks = rows // EXPERT_ROWS
    row_map = lambda b, be, fi, sl, nx, nb: (jnp.minimum(b, nb[0] - 1), 0)
    exp_map = lambda b, be, fi, sl, nx, nb: (layer, be[b], 0, 0)
    prev_exp_map = lambda b, be, fi, sl, nx, nb: (layer, be[jnp.maximum(b - 1, 0)], 0, 0)
    return pl.pallas_call(
        functools.partial(_expert_kernel, layer),
        out_shape=jax.ShapeDtypeStruct((rows, d), F32),
        grid_spec=pltpu.PrefetchScalarGridSpec(
            num_scalar_prefetch=5,
            grid=(max_blocks + 1,),
            in_specs=[
                pl.BlockSpec((EXPERT_ROWS, d), row_map),
                pl.BlockSpec(memory_space=pl.ANY),
                pl.BlockSpec((None, None, 1, de2), exp_map),
                pl.BlockSpec(memory_space=pl.ANY),
                pl.BlockSpec((None, None, 1, d), prev_exp_map),
            ],
            out_specs=pl.BlockSpec((EXPERT_ROWS, d), lambda b, *_: (jnp.maximum(b - 1, 0), 0)),
            scratch_shapes=[
                pltpu.VMEM((2, d, de2), F32), pltpu.VMEM((2, de, d), F32),
                pltpu.VMEM((d, de2), BF16), pltpu.VMEM((2, de, d), BF16),
                pltpu.VMEM((EXPERT_ROWS, de), BF16),
                pltpu.SemaphoreType.DMA((2, 2)),
            ],
        ),
        compiler_params=_cparams(("arbitrary",)),
        name="moe_experts",
    )(tabs["blk_exp"], tabs["blk_first"], tabs["blk_slot"], tabs["blk_next"], tabs["n_blocks"],
      xs, wgu, bgu.reshape(depth, n_exp, 1, de2), wdn, bdn.reshape(depth, n_exp, 1, d))


def _combine_kernel(final_norm, n_exp, loc0_ref, glb0_ref, len_ref, used_ref,
                    route_ref, x1_ref, g2_ref, fg_ref, ys_ref, out_ref, loc_ref, sem):
    i = pl.program_id(0)
    n = pl.num_programs(0)
    slot = i % 2
    tile = x1_ref.shape[0]
    loc_rows = loc_ref.shape[1]

    def issue_tile(t, sl):
        def issue_run(e, carry):
            rows = _aligned(len_ref[t * n_exp + e])

            @pl.when(rows > 0)
            def _():
                pltpu.make_async_copy(
                    ys_ref.at[pl.ds(_aligned(glb0_ref[t * n_exp + e]), rows)],
                    loc_ref.at[sl, pl.ds(_aligned(loc0_ref[t * n_exp + e]), rows)],
                    sem.at[sl]).start()
            return carry
        lax.fori_loop(0, n_exp, issue_run, 0)

    @pl.when(i == 0)
    def _():
        issue_tile(0, 0)

    @pl.when(i + 1 < n)
    def _():
        issue_tile(i + 1, 1 - slot)

    route = route_ref[...]
    e_iota = lax.broadcasted_iota(I32, (tile, n_exp), 1)
    hits = [route[:, k:k + 1].astype(I32) == e_iota for k in range(TOP_K)]
    gates = [route[:, TOP_K + k:TOP_K + k + 1] for k in range(TOP_K)]
    onehot = jnp.zeros((tile, n_exp), F32)
    for h in hits:
        onehot = onehot + h.astype(F32)
    lower = (lax.broadcasted_iota(I32, (tile, tile), 1)
             < lax.broadcasted_iota(I32, (tile, tile), 0)).astype(BF16)
    pre = jnp.dot(lower, onehot.astype(BF16), preferred_element_type=F32)
    cnt = jnp.sum(onehot, axis=0, keepdims=True)
    cnt8 = jnp.ceil(cnt / ROW_ALIGN) * ROW_ALIGN
    upper = (lax.broadcasted_iota(I32, (n_exp, n_exp), 0)
             < lax.broadcasted_iota(I32, (n_exp, n_exp), 1)).astype(BF16)
    off = jnp.dot(jnp.broadcast_to(cnt8, (SUBLANES, n_exp)).astype(BF16), upper,
                  preferred_element_type=F32)[0:1, :]
    base = off + pre
    r_iota = lax.broadcasted_iota(I32, (tile, loc_rows), 1)
    comb = jnp.zeros((tile, loc_rows), F32)
    for h, gk in zip(hits, gates):
        dest = jnp.sum(jnp.where(h, base, 0.0), axis=1, keepdims=True).astype(I32)
        comb = jnp.where(r_iota == dest, gk, comb)

    used_rows = _aligned(used_ref[i])

    @pl.when(used_rows > 0)
    def _():
        pltpu.make_async_copy(ys_ref.at[pl.ds(0, used_rows)],
                              loc_ref.at[slot, pl.ds(0, used_rows)], sem.at[slot]).wait()

    row_ok = lax.broadcasted_iota(I32, (loc_rows, 1), 0) < used_rows
    yl = jnp.where(row_ok, loc_ref[slot], 0.0).astype(BF16)
    moe = jnp.dot(comb.astype(BF16), yl, preferred_element_type=F32)
    out = x1_ref[...] + g2_ref[...] * moe
    if final_norm:
        out = _rms(out, fg_ref[...])
    out_ref[...] = out


def _combine(ys, route, x1, gate2, final_g, tabs, n_exp, loc_rows, final_norm):
    b, s, d = x1.shape
    n_tok = b * s
    n_tiles = n_tok // MOE_TILE
    tiles_per_seq = s // MOE_TILE
    out = pl.pallas_call(
        functools.partial(_combine_kernel, final_norm, n_exp),
        out_shape=jax.ShapeDtypeStruct((n_tok, d), F32),
        grid_spec=pltpu.PrefetchScalarGridSpec(
            num_scalar_prefetch=4,
            grid=(n_tiles,),
            in_specs=[
                pl.BlockSpec((MOE_TILE, LANES), lambda i, *_: (i, 0)),
                pl.BlockSpec((MOE_TILE, d), lambda i, *_: (i, 0)),
                pl.BlockSpec((None, 1, d), lambda i, *_: (i // tiles_per_seq, 0, 0)),
                pl.BlockSpec((1, d), lambda i, *_: (0, 0)),
                pl.BlockSpec(memory_space=pl.ANY),
            ],
            out_specs=pl.BlockSpec((MOE_TILE, d), lambda i, *_: (i, 0)),
            scratch_shapes=[
                pltpu.VMEM((2, loc_rows, d), F32),
                pltpu.SemaphoreType.DMA((2,)),
            ],
        ),
        compiler_params=_cparams(("arbitrary",)),
        name="moe_combine",
    )(tabs["run_loc"], tabs["run_glb"], tabs["run_len"], tabs["rows_used"],
      route, x1.reshape(n_tok, d), gate2, final_g, ys)
    return out.reshape(b, s, d)


def _moe(x1, hn2, idxt, route, cnt, gate2, final_g, layer, wgu, bgu, wdn, bdn, final_norm):
    b, s, d = x1.shape
    n_tok = b * s
    n_exp = wgu.shape[1]
    loc_rows, max_blocks = _moe_dims(n_tok, n_exp)
    tabs = _routing_tables(cnt, max_blocks)
    xs = _dispatch(hn2.reshape(n_tok, d), idxt, tabs, n_exp, loc_rows, max_blocks)
    ys = _experts(xs, tabs, layer, wgu, bgu, wdn, bdn)
    return _combine(ys, route, x1, gate2, final_g, tabs, n_exp, loc_rows, final_norm)


def _qkv_kernel(n_heads, scale, x_ref, cc_ref, ss_ref, kvg_ref, wdkv_ref, ckvg_ref, wuk_ref, wuv_ref,
                n1g_ref, sh1_ref, sc1_ref, wdq_ref, cqg_ref, wuq_ref,
                q_ref, k_ref, v_ref):
    x = x_ref[...]
    cc = cc_ref[...]
    ss = ss_ref[...]
    r_kv = ckvg_ref.shape[1]
    hk = _rms(x, kvg_ref[...]).astype(BF16)
    lat = jnp.dot(hk, wdkv_ref[...], preferred_element_type=F32)
    ckv = _rms(lat[:, :r_kv], ckvg_ref[...]).astype(BF16)
    krot = lat[:, r_kv:r_kv + LANES] * cc + lat[:, r_kv + LANES:r_kv + 2 * LANES] * ss
    kn = jnp.dot(ckv, wuk_ref[...], preferred_element_type=F32)
    vt = lax.dot_general(wuv_ref[...], ckv, (((1,), (1,)), ((), ())), preferred_element_type=F32)
    hq = (_rms(x, n1g_ref[...]) * (1.0 + sc1_ref[...]) + sh1_ref[...]).astype(BF16)
    cq = _rms(jnp.dot(hq, wdq_ref[...], preferred_element_type=F32), cqg_ref[...]).astype(BF16)
    qq = jnp.dot(cq, wuq_ref[...], preferred_element_type=F32) * scale
    hd = n_heads * LANES
    ts = x.shape[0]
    ones_rows = (lax.broadcasted_iota(I32, (V_ONES_ROWS, ts), 0) == 0).astype(BF16)
    for h in range(n_heads):
        sl = slice(h * LANES, (h + 1) * LANES)
        k_ref[h, :, 0:LANES] = kn[:, sl].astype(BF16)
        k_ref[h, :, LANES:2 * LANES] = krot.astype(BF16)
        v_ref[h, 0:LANES, :] = vt[h * LANES:(h + 1) * LANES, :].astype(BF16)
        v_ref[h, LANES:LANES + V_ONES_ROWS, :] = ones_rows
        q_ref[h, :, 0:LANES] = qq[:, sl].astype(BF16)
        qrot = qq[:, hd + h * LANES:hd + (h + 1) * LANES] * cc \
            + qq[:, 2 * hd + h * LANES:2 * hd + (h + 1) * LANES] * ss
        q_ref[h, :, LANES:2 * LANES] = qrot.astype(BF16)


def _qkv(x, cc, ss, kvg, wdkv, ckvg, wuk, wuv, n1g, sh1, sc1, wdq, cqg, wuq, n_heads, scale, ts):
    b, s, d = x.shape
    full = lambda a: pl.BlockSpec(a.shape, lambda i, j: (0,) * a.ndim)
    hspec = lambda w: pl.BlockSpec((None, n_heads, ts, w), lambda i, j: (i, 0, j, 0))
    return pl.pallas_call(
        functools.partial(_qkv_kernel, n_heads, scale),
        out_shape=[
            jax.ShapeDtypeStruct((b, n_heads, s, 2 * LANES), BF16),
            jax.ShapeDtypeStruct((b, n_heads, s, 2 * LANES), BF16),
            jax.ShapeDtypeStruct((b, n_heads, s // ts, LANES + V_ONES_ROWS, ts), BF16),
        ],
        grid=(b, s // ts),
        in_specs=[
            pl.BlockSpec((None, ts, d), lambda i, j: (i, j, 0)),
            pl.BlockSpec((None, ts, LANES), lambda i, j: (i, j, 0)),
            pl.BlockSpec((None, ts, LANES), lambda i, j: (i, j, 0)),
            full(kvg), full(wdkv), full(ckvg), full(wuk), full(wuv),
            full(n1g), _bvec_spec(d), _bvec_spec(d), full(wdq), full(cqg), full(wuq),
        ],
        out_specs=[hspec(2 * LANES), hspec(2 * LANES),
                   pl.BlockSpec((None, n_heads, None, LANES + V_ONES_ROWS, ts),
                                lambda i, j: (i, 0, j, 0, 0))],
        compiler_params=_cparams(("arbitrary", "arbitrary")),
        name="mla_qkv",
    )(x, cc, ss, kvg, wdkv, ckvg, wuk, wuv, n1g, sh1, sc1, wdq, cqg, wuq)


def _attn_kernel(q_ref, k_ref, vt_ref, o_ref, m_ref, acc_ref):
    tq = q_ref.shape[0]
    tk = vt_ref.shape[2]
    ratio = tq // tk
    dv = o_ref.shape[1]
    qi = pl.program_id(2)
    m_ref[...] = jnp.full(m_ref.shape, -jnp.inf, F32)
    acc_ref[...] = jnp.zeros(acc_ref.shape, F32)

    def scores(ki, cols, on_diagonal):
        start = pl.multiple_of(ki * tk, tk)
        k = k_ref[pl.ds(start, tk), :]
        st = lax.dot_general(k, q_ref[cols, :], (((1,), (1,)), ((), ())),
                             preferred_element_type=F32)
        if on_diagonal:
            kc = lax.broadcasted_iota(I32, st.shape, 0) // CHUNK
            qc = lax.broadcasted_iota(I32, st.shape, 1) // CHUNK
            st = jnp.where(kc <= qc, st, -jnp.inf)
        return st

    def update(ki, st, m_old, acc_old):
        m_new = jnp.maximum(m_old, jnp.max(st, axis=0, keepdims=True))
        p = jnp.exp2(st - m_new).astype(BF16)
        alpha = jnp.exp2(m_old - m_new)
        acc_new = alpha * acc_old + jnp.dot(vt_ref[ki], p, preferred_element_type=F32)
        return m_new, acc_new

    def step(groups):
        sts = [[scores(ki, cols, on_diagonal) for ki, on_diagonal in tiles]
               for cols, tiles in groups]
        for (cols, tiles), group_sts in zip(groups, sts):
            state = (m_ref[:, cols], acc_ref[:, cols])
            for (ki, _), st in zip(tiles, group_sts):
                state = update(ki, st, *state)
            m_ref[:, cols], acc_ref[:, cols] = state

    n_below = qi * ratio
    everything = slice(0, tq)
    step([(slice(g * tk, (g + 1) * tk), [(n_below + a, a == g) for a in range(g + 1)])
          for g in range(ratio)])

    def fast_step(first, count):
        m = m_ref[...]
        pv = None
        top = None
        for t in range(count):
            st = scores(first + t, everything, False)
            tmax = jnp.max(st, axis=0, keepdims=True)
            top = tmax if top is None else jnp.maximum(top, tmax)
            p = jnp.exp2(st - m).astype(BF16)
            part = jnp.dot(vt_ref[first + t], p, preferred_element_type=F32)
            pv = part if pv is None else pv + part
        safe = jnp.max(top - m) <= ATTN_MAX_SLACK

        @pl.when(safe)
        def _():
            acc_ref[...] += pv

        @pl.when(jnp.logical_not(safe))
        def _():
            def redo(t, carry):
                step([(everything, [(first + t, False)])])
                return carry
            lax.fori_loop(0, count, redo, 0)

    def body(j, carry):
        fast_step(ATTN_UNROLL * j, ATTN_UNROLL)
        return carry
    trips = n_below // ATTN_UNROLL
    lax.fori_loop(0, trips, body, 0)

    for rem in range(math.gcd(ratio, ATTN_UNROLL), ATTN_UNROLL, math.gcd(ratio, ATTN_UNROLL)):
        @pl.when(n_below % ATTN_UNROLL == rem)
        def _(rem=rem):
            fast_step(trips * ATTN_UNROLL, rem)

    o_ref[...] = (acc_ref[0:dv, :] / acc_ref[dv:dv + 1, :]).T.astype(o_ref.dtype)


def _attention(q, k, vt, tq):
    b, h, s, dk = q.shape
    nk, dv_ext, tk = vt.shape[2:]
    dv = dv_ext - V_ONES_ROWS
    assert tk % CHUNK == 0 and s % tq == 0 and tq % tk == 0
    return pl.pallas_call(
        _attn_kernel,
        out_shape=jax.ShapeDtypeStruct((b, s, h * dv), BF16),
        grid=(b, h, s // tq),
        in_specs=[
            pl.BlockSpec((None, None, tq, dk), lambda i, j, t: (i, j, t, 0)),
            pl.BlockSpec((None, None, s, dk), lambda i, j, t: (i, j, 0, 0)),
            pl.BlockSpec((None, None, nk, dv_ext, tk), lambda i, j, t: (i, j, 0, 0, 0)),
        ],
        out_specs=pl.BlockSpec((None, tq, dv), lambda i, j, t: (i, t, j)),
        scratch_shapes=[pltpu.VMEM((1, tq), F32), pltpu.VMEM((dv_ext, tq), F32)],
        compiler_params=_cparams(("arbitrary", "arbitrary", "arbitrary")),
        name="mla_attention",
    )(q, k, vt)


def _swap_halves(w):
    half = w.shape[-1] // 2
    return jnp.concatenate([w[..., half:], w[..., :half]], axis=-1)


def _pad_lanes(w):
    pad = LANES - w.shape[-1]
    return jnp.concatenate([w, jnp.zeros(w.shape[:-1] + (pad,), w.dtype)], axis=-1)


def kernel(x, c, positions, mod_w, mod_b, norm1_g, norm2_g, conv_pw1_w, conv_pw1_b, conv_dw_w, conv_dw_b, conv_ln_g, conv_ln_b, conv_pw2_w, conv_pw2_b, kv_norm_g, w_dkv, ckv_norm_g, w_uk, w_uv, w_dq, cq_norm_g, w_uq, w_o, router_w, router_b, exp_w_gu, exp_b_gu, exp_w_dn, exp_b_dn, final_g):
    b, s, d = x.shape
    n_heads, nope = w_uk.shape[1], w_uk.shape[2]
    r_kv = ckv_norm_g.shape[0]
    rope = w_dkv.shape[1] - r_kv
    vdim = w_uv.shape[2]
    n_exp = router_w.shape[2]
    assert nope == LANES and vdim == LANES and rope <= LANES and d % LANES == 0
    ts = min(512, s)
    tq = min(1024, s)

    mod = _modulation(c, mod_w, mod_b)
    mods = [[m.reshape(b, 1, d) for m in jnp.split(mod[l], 6, axis=-1)] for l in range(2)]
    row = lambda v: v.reshape(1, -1)

    half = rope // 2
    inv = jnp.exp(-(2.0 * math.log(ROPE_THETA) / rope) * jnp.arange(half, dtype=F32))
    ang = positions.astype(F32)[..., None] * inv
    cos, sin = jnp.cos(ang), jnp.sin(ang)
    cc = _pad_lanes(jnp.concatenate([cos, cos], axis=-1))
    ss = _pad_lanes(jnp.concatenate([-sin, sin], axis=-1))

    sh1, sc1, gt1, sh2, sc2, gt2 = mods[0]
    x1, hn2, idxt, route, cnt = _mix0(
        x, row(norm1_g[0]), sh1, sc1, gt1,
        conv_pw1_w[0].astype(BF16), row(conv_pw1_b[0]), conv_dw_w[0], row(conv_dw_b[0]),
        row(conv_ln_g[0]), row(conv_ln_b[0]), conv_pw2_w[0].astype(BF16), row(conv_pw2_b[0]),
        row(norm2_g[0]), sh2, sc2, router_w[0].T, router_b[0].reshape(n_exp, 1), ts)
    cnt = cnt[:, :ts // MOE_TILE].reshape(-1, n_exp)
    x2 = _moe(x1, hn2, idxt, route, cnt, gt2, row(final_g),
              0, exp_w_gu, exp_b_gu, exp_w_dn, exp_b_dn, False)

    sh1, sc1, gt1, sh2, sc2, gt2 = mods[1]
    wdkv_rope = w_dkv[:, r_kv:]
    wdkv_ext = jnp.concatenate(
        [w_dkv[:, :r_kv], _pad_lanes(wdkv_rope), _pad_lanes(_swap_halves(wdkv_rope))], axis=-1)
    wuq = w_uq[0]
    r_q = wuq.shape[0]
    wuq_rope = wuq[:, :, nope:]
    wuq_ext = jnp.concatenate([
        wuq[:, :, :nope].reshape(r_q, n_heads * LANES),
        _pad_lanes(wuq_rope).reshape(r_q, n_heads * LANES),
        _pad_lanes(_swap_halves(wuq_rope)).reshape(r_q, n_heads * LANES)], axis=-1)
    scale = float((nope + rope) ** -0.5 * math.log2(math.e))
    q, k, v = _qkv(
        x2, cc, ss, row(kv_norm_g), wdkv_ext.astype(BF16), row(ckv_norm_g),
        w_uk.reshape(r_kv, n_heads * nope).astype(BF16), w_uv.reshape(r_kv, n_heads * vdim).T.astype(BF16),
        row(norm1_g[1]), sh1, sc1, w_dq[0].astype(BF16), row(cq_norm_g[0]), wuq_ext.astype(BF16),
        n_heads, scale, ts)
    o = _attention(q, k, v, tq)

    x3, hn2, idxt, route, cnt = _mix1(
        x2, o, gt1, w_o[0].astype(BF16), row(norm2_g[1]), sh2, sc2,
        router_w[1].T, router_b[1].reshape(n_exp, 1), ts)
    cnt = cnt[:, :ts // MOE_TILE].reshape(-1, n_exp)
    return _moe(x3, hn2, idxt, route, cnt, gt2, row(final_g),
                1, exp_w_gu, exp_b_gu, exp_w_dn, exp_b_dn, True)
```

```python
import functools
import math

import jax
import jax.numpy as jnp
from jax import lax
from jax.experimental import pallas as pl
from jax.experimental.pallas import tpu as pltpu

CHUNK = 64
TOP_K = 4
ROPE_THETA = 10000.0
SWIGLU_ALPHA = 1.702
SWIGLU_LIMIT = 7.0
EPS = 1e-6

LANES = 128
SUBLANES = 8
VMEM_LIMIT_BYTES = 56 * 1024 * 1024

ROW_ALIGN = SUBLANES
MOE_TILE = 256
EXPERT_ROWS = 512
CONV_HALO = 32
V_ONES_ROWS = 2 * SUBLANES
ATTN_UNROLL = 4
ATTN_MAX_SLACK = 60.0

F32 = jnp.float32
BF16 = jnp.bfloat16
I32 = jnp.int32


def _cparams(sem):
    return pltpu.CompilerParams(dimension_semantics=sem, vmem_limit_bytes=VMEM_LIMIT_BYTES)


def _rms(x, g):
    return x * lax.rsqrt(jnp.mean(x * x, axis=-1, keepdims=True) + EPS) * g


def _round_up(a, m):
    return (a + m - 1) // m * m


def _mod_kernel(c_ref, w_ref, b_ref, o_ref):
    c = c_ref[...]
    ca = c * jax.nn.sigmoid(c)
    o_ref[...] = jnp.dot(ca, w_ref[...], preferred_element_type=F32) + b_ref[...]


def _modulation(c, mod_w, mod_b):
    depth, d, d6 = mod_w.shape
    b = c.shape[0]
    bp = _round_up(b, SUBLANES)
    cp = jnp.zeros((bp, d), F32).at[:b].set(c)
    tn = d6 // 4
    out = pl.pallas_call(
        _mod_kernel,
        out_shape=jax.ShapeDtypeStruct((depth, bp, d6), F32),
        grid=(depth, d6 // tn),
        in_specs=[
            pl.BlockSpec((bp, d), lambda l, j: (0, 0)),
            pl.BlockSpec((None, d, tn), lambda l, j: (l, 0, j)),
            pl.BlockSpec((None, 1, tn), lambda l, j: (l, 0, j)),
        ],
        out_specs=pl.BlockSpec((None, bp, tn), lambda l, j: (l, 0, j)),
        compiler_params=_cparams(("arbitrary", "arbitrary")),
        name="adaln_modulation",
    )(cp, mod_w, mod_b.reshape(depth, 1, d6))
    return out[:, :b]


def _pre_moe(x1, g2, sh2, sc2, rwt, rb, earlier, hn_ref, destt_ref, route_ref, cnt_ref):
    ts = x1.shape[0]
    n_exp = rwt.shape[0]
    hn = _rms(x1, g2) * (1.0 + sc2) + sh2
    hn_ref[...] = hn.astype(BF16)
    def split(v):
        hi = v.astype(BF16)
        return hi, (v - hi.astype(F32)).astype(BF16)
    nt = lambda a, b: lax.dot_general(a, b, (((1,), (1,)), ((), ())), preferred_element_type=F32)
    w_hi, w_lo = split(rwt)
    h_hi, h_lo = split(hn)
    logits = nt(w_hi, h_hi) + (nt(w_hi, h_lo) + nt(w_lo, h_hi)) + rb
    e_iota = lax.broadcasted_iota(I32, (n_exp, ts), 0)
    vals, idxs = [], []
    cur = logits
    for _ in range(TOP_K):
        m = jnp.max(cur, axis=0, keepdims=True)
        i = jnp.min(jnp.where(cur == m, e_iota, n_exp), axis=0, keepdims=True)
        vals.append(m)
        idxs.append(i)
        cur = jnp.where(e_iota == i, -jnp.inf, cur)
    exps = [jnp.exp(v - vals[0]) for v in vals]
    den = exps[0]
    for e in exps[1:]:
        den = den + e
    gates = [e / den for e in exps]
    hits = [e_iota == i for i in idxs]
    onehot = jnp.zeros((n_exp, ts), F32)
    for h in hits:
        onehot = onehot + h.astype(F32)
    rank = jnp.dot(onehot.astype(BF16), earlier, preferred_element_type=F32)
    lower = (lax.broadcasted_iota(I32, (n_exp, n_exp), 1)
             < lax.broadcasted_iota(I32, (n_exp, n_exp), 0)).astype(BF16)
    dests = [[] for _ in range(TOP_K)]
    for t0 in range(0, ts, MOE_TILE):
        cols = slice(t0, t0 + MOE_TILE)
        cnt = jnp.sum(onehot[:, cols], axis=1, keepdims=True)
        cnt8 = jnp.ceil(cnt / ROW_ALIGN) * ROW_ALIGN
        off = jnp.dot(lower, jnp.broadcast_to(cnt8, (n_exp, LANES)).astype(BF16),
                      preferred_element_type=F32)[:, 0:1]
        base = off + rank[:, cols]
        for k, h in enumerate(hits):
            dests[k].append(jnp.sum(jnp.where(h[:, cols], base, 0.0), axis=0, keepdims=True))
    dests = [jnp.concatenate(parts, axis=1) for parts in dests]
    destt_ref[...] = jnp.concatenate(dests, axis=0).astype(I32)
    rows = jnp.concatenate(dests + gates + [jnp.zeros((LANES - 2 * TOP_K, ts), F32)], axis=0)
    route_ref[...] = rows.T
    sel = (lax.broadcasted_iota(I32, (SUBLANES, ts), 1) // MOE_TILE
           == lax.broadcasted_iota(I32, (SUBLANES, ts), 0)).astype(BF16)
    cnt = lax.dot_general(sel, onehot.astype(BF16), (((1,), (1,)), ((), ())),
                          preferred_element_type=F32)
    cnt_ref[...] = cnt.astype(I32)


def _earlier_in_tile(ts):
    t = jnp.arange(ts, dtype=I32)
    same_tile = (t[:, None] // MOE_TILE) == (t[None, :] // MOE_TILE)
    return ((t[:, None] < t[None, :]) & same_tile).astype(BF16)


def _pre_moe_specs(b, s, d, ts, n_exp):
    ns = s // ts
    out_shape = [
        jax.ShapeDtypeStruct((b, s, d), F32),
        jax.ShapeDtypeStruct((b, s, d), BF16),
        jax.ShapeDtypeStruct((TOP_K, b * s), I32),
        jax.ShapeDtypeStruct((b * s, LANES), F32),
        jax.ShapeDtypeStruct((b * ns, SUBLANES, n_exp), I32),
    ]
    out_specs = [
        pl.BlockSpec((None, ts, d), lambda i, j: (i, j, 0)),
        pl.BlockSpec((None, ts, d), lambda i, j: (i, j, 0)),
        pl.BlockSpec((TOP_K, ts), lambda i, j: (0, i * ns + j)),
        pl.BlockSpec((ts, LANES), lambda i, j: (i * ns + j, 0)),
        pl.BlockSpec((None, SUBLANES, n_exp), lambda i, j: (i * ns + j, 0, 0)),
    ]
    return out_shape, out_specs


def _vec_spec(d):
    return pl.BlockSpec((1, d), lambda i, j: (0, 0))


def _bvec_spec(d):
    return pl.BlockSpec((None, 1, d), lambda i, j: (i, 0, 0))


def _mix0_kernel(x_ref, n1g_ref, sh1_ref, sc1_ref, gt1_ref,
                 pw1w_ref, pw1b_ref, dww_ref, dwb_ref, lng_ref, lnb_ref,
                 pw2w_ref, pw2b_ref, n2g_ref, sh2_ref, sc2_ref, rwt_ref, rb_ref, earlier_ref,
                 x1_ref, hn_ref, destt_ref, route_ref, cnt_ref, buf_ref, conv_ref, shift_ref):
    ts, d = x_ref.shape
    width = dww_ref.shape[0]
    x = x_ref[...]
    hn = _rms(x, n1g_ref[...]) * (1.0 + sc1_ref[...]) + sh1_ref[...]
    hb = hn.astype(BF16)
    a = jnp.dot(hb, pw1w_ref[:, :d], preferred_element_type=F32) + pw1b_ref[:, :d]
    g = jnp.dot(hb, pw1w_ref[:, d:], preferred_element_type=F32) + pw1b_ref[:, d:]
    glu = a * jax.nn.sigmoid(g)

    @pl.when(pl.program_id(1) == 0)
    def _():
        buf_ref[0:CONV_HALO, :] = jnp.zeros((CONV_HALO, d), F32)

    buf_ref[CONV_HALO:, :] = glu
    base = CONV_HALO - (width - 1)
    rc = 32
    lc = min(512, d)
    sh_rows = shift_ref.shape[1]
    sub = lax.broadcasted_iota(I32, (SUBLANES, lc), 0)
    for c0 in range(0, d, lc):
        for r in range(1, SUBLANES):
            from_this = sub < SUBLANES - r
            cur = pltpu.roll(buf_ref[0:SUBLANES, c0:c0 + lc], SUBLANES - r, axis=0)
            for j0 in range(0, sh_rows, SUBLANES):
                nxt = pltpu.roll(buf_ref[j0 + SUBLANES:j0 + 2 * SUBLANES, c0:c0 + lc],
                                 SUBLANES - r, axis=0)
                shift_ref[r - 1, j0:j0 + SUBLANES, :] = jnp.where(from_this, cur, nxt)
                cur = nxt
        for r0 in range(0, ts, rc):
            acc = jnp.zeros((rc, lc), F32)
            for k in range(width):
                q8, r = divmod(base + k, SUBLANES)
                lo = q8 * SUBLANES + r0
                if r == 0:
                    win = buf_ref[lo:lo + rc, c0:c0 + lc]
                else:
                    win = shift_ref[r - 1, lo:lo + rc, :]
                acc = acc + dww_ref[k:k + 1, c0:c0 + lc] * win
            conv_ref[r0:r0 + rc, c0:c0 + lc] = acc
    buf_ref[0:CONV_HALO, :] = buf_ref[ts:ts + CONV_HALO, :]
    u = conv_ref[...] + dwb_ref[...]
    mu = jnp.mean(u, axis=-1, keepdims=True)
    dlt = u - mu
    var = jnp.mean(dlt * dlt, axis=-1, keepdims=True)
    u = dlt * lax.rsqrt(var + EPS) * lng_ref[...] + lnb_ref[...]
    u = u * jax.nn.sigmoid(u)
    y = jnp.dot(u.astype(BF16), pw2w_ref[...], preferred_element_type=F32) + pw2b_ref[...]
    x1 = x + gt1_ref[...] * y
    x1_ref[...] = x1
    _pre_moe(x1, n2g_ref[...], sh2_ref[...], sc2_ref[...], rwt_ref[...], rb_ref[...],
             earlier_ref[...], hn_ref, destt_ref, route_ref, cnt_ref)


def _mix0(x, n1g, sh1, sc1, gt1, pw1w, pw1b, dww, dwb, lng, lnb, pw2w, pw2b,
          n2g, sh2, sc2, rwt, rb, ts):
    b, s, d = x.shape
    n_exp = rwt.shape[0]
    width = dww.shape[0]
    assert width - 1 <= CONV_HALO and ts % MOE_TILE == 0 and ts // MOE_TILE <= SUBLANES
    out_shape, out_specs = _pre_moe_specs(b, s, d, ts, n_exp)
    full = lambda shp: pl.BlockSpec(shp, lambda i, j: (0,) * len(shp))
    return pl.pallas_call(
        _mix0_kernel,
        out_shape=out_shape,
        grid=(b, s // ts),
        in_specs=[
            pl.BlockSpec((None, ts, d), lambda i, j: (i, j, 0)),
            _vec_spec(d), _bvec_spec(d), _bvec_spec(d), _bvec_spec(d),
            full((d, 2 * d)), full((1, 2 * d)), full((width, d)), full((1, d)),
            full((1, d)), full((1, d)), full((d, d)), full((1, d)),
            _vec_spec(d), _bvec_spec(d), _bvec_spec(d),
            full((n_exp, d)), full((n_exp, 1)), full((ts, ts)),
        ],
        out_specs=out_specs,
        scratch_shapes=[
            pltpu.VMEM((CONV_HALO + ts, d), F32),
            pltpu.VMEM((ts, d), F32),
            pltpu.VMEM((SUBLANES - 1, CONV_HALO + ts - SUBLANES, min(512, d)), F32),
        ],
        compiler_params=_cparams(("arbitrary", "arbitrary")),
        name="conformer_mixer",
    )(x, n1g, sh1, sc1, gt1, pw1w, pw1b, dww, dwb, lng, lnb, pw2w, pw2b,
      n2g, sh2, sc2, rwt, rb, _earlier_in_tile(ts))


def _mix1_kernel(x_ref, o_ref, gt1_ref, wo_ref, n2g_ref, sh2_ref, sc2_ref, rwt_ref, rb_ref,
                 earlier_ref, x1_ref, hn_ref, destt_ref, route_ref, cnt_ref):
    y = jnp.dot(o_ref[...], wo_ref[...], preferred_element_type=F32)
    x1 = x_ref[...] + gt1_ref[...] * y
    x1_ref[...] = x1
    _pre_moe(x1, n2g_ref[...], sh2_ref[...], sc2_ref[...], rwt_ref[...], rb_ref[...],
             earlier_ref[...], hn_ref, destt_ref, route_ref, cnt_ref)


def _mix1(x, o, gt1, wo, n2g, sh2, sc2, rwt, rb, ts):
    b, s, d = x.shape
    n_exp = rwt.shape[0]
    do = o.shape[-1]
    assert ts % MOE_TILE == 0 and ts // MOE_TILE <= SUBLANES
    out_shape, out_specs = _pre_moe_specs(b, s, d, ts, n_exp)
    full = lambda shp: pl.BlockSpec(shp, lambda i, j: (0,) * len(shp))
    return pl.pallas_call(
        _mix1_kernel,
        out_shape=out_shape,
        grid=(b, s // ts),
        in_specs=[
            pl.BlockSpec((None, ts, d), lambda i, j: (i, j, 0)),
            pl.BlockSpec((None, ts, do), lambda i, j: (i, j, 0)),
            _bvec_spec(d), full((do, d)),
            _vec_spec(d), _bvec_spec(d), _bvec_spec(d),
            full((n_exp, d)), full((n_exp, 1)), full((ts, ts)),
        ],
        out_specs=out_specs,
        compiler_params=_cparams(("arbitrary", "arbitrary")),
        name="attn_out_mixer",
    )(x, o, gt1, wo, n2g, sh2, sc2, rwt, rb, _earlier_in_tile(ts))


def _moe_dims(n_tok, n_exp):
    n_tiles = n_tok // MOE_TILE
    loc_rows = _round_up(TOP_K * MOE_TILE + n_exp * (ROW_ALIGN - 1), LANES)
    max_rows = TOP_K * n_tok + n_tiles * n_exp * (ROW_ALIGN - 1) + n_exp * (EXPERT_ROWS - ROW_ALIGN)
    max_blocks = -(-max_rows // EXPERT_ROWS)
    return loc_rows, max_blocks


def _routing_tables(cnt, max_blocks):
    n_tiles, n_exp = cnt.shape
    cnt8 = _round_up(cnt, ROW_ALIGN)
    off = jnp.cumsum(cnt8, axis=1) - cnt8
    seg_len = cnt8.sum(axis=0)
    seg_pad = _round_up(seg_len, EXPERT_ROWS)
    seg_end = jnp.cumsum(seg_pad)
    seg_start = seg_end - seg_pad
    run_start = seg_start[None, :] + jnp.cumsum(cnt8, axis=0) - cnt8
    n_blocks = (seg_end[-1] // EXPERT_ROWS).astype(I32)
    blk_row = jnp.arange(max_blocks, dtype=I32) * EXPERT_ROWS
    blk_row = jnp.minimum(blk_row, seg_end[-1] - EXPERT_ROWS)
    blk_exp = jnp.minimum((blk_row[:, None] >= seg_end[None, :]).sum(axis=-1), n_exp - 1).astype(I32)
    blk = jnp.arange(max_blocks, dtype=I32)
    prev_exp = jnp.concatenate([jnp.full((1,), -1, I32), blk_exp[:-1]])
    blk_first = ((blk_exp != prev_exp) & (blk < n_blocks)).astype(I32)
    blk_slot = ((jnp.cumsum(blk_first) - 1) % 2).astype(I32)
    e_ids = jnp.arange(n_exp, dtype=I32)
    later = (e_ids[None, :] > e_ids[:, None]) & (seg_pad[None, :] > 0)
    next_exp = jnp.min(jnp.where(later, e_ids[None, :], n_exp), axis=1)
    next_exp = jnp.where(next_exp < n_exp, next_exp, -1)
    blk_next = jnp.sum(jnp.where(blk_exp[:, None] == e_ids[None, :], next_exp[None, :], 0), axis=1)
    flat = lambda a: a.reshape(-1).astype(I32)
    return dict(
        run_loc=flat(off),
        run_glb=flat(run_start),
        run_len=flat(cnt8),
        rows_used=flat(cnt8.sum(axis=1)),
        tail_glb=flat(seg_start + seg_len),
        tail_len=flat(seg_pad - seg_len),
        blk_exp=blk_exp, blk_first=blk_first, blk_slot=blk_slot, blk_next=flat(blk_next),
        n_blocks=n_blocks.reshape(1))


def _aligned(v):
    return pl.multiple_of(v, ROW_ALIGN)


def _dispatch_kernel(n_exp, loc0_ref, glb0_ref, len_ref, used_ref, tail0_ref, taillen_ref, nblk_ref,
                     x_ref, destt_ref, xs_ref, loc_ref, zero_ref, sem, zsem):
    i = pl.program_id(0)
    n = pl.num_programs(0)
    slot = i % 2
    tile = x_ref.shape[0]
    loc_rows = loc_ref.shape[1]

    def wait_slot(sl, rows):
        rows = _aligned(rows)

        @pl.when(rows > 0)
        def _():
            pltpu.make_async_copy(loc_ref.at[sl, pl.ds(0, rows)], xs_ref.at[pl.ds(0, rows)],
                                  sem.at[sl]).wait()

    @pl.when(i >= 2)
    def _():
        wait_slot(slot, used_ref[i - 2])

    r_iota = lax.broadcasted_iota(I32, (loc_rows, tile), 0)
    perm = jnp.zeros((loc_rows, tile), F32)
    for k in range(TOP_K):
        perm = jnp.where(r_iota == destt_ref[k:k + 1, :], 1.0, perm)
    loc_ref[slot] = jnp.dot(perm.astype(BF16), x_ref[...], preferred_element_type=F32)

    def issue_run(e, carry):
        rows = _aligned(len_ref[i * n_exp + e])

        @pl.when(rows > 0)
        def _():
            pltpu.make_async_copy(
                loc_ref.at[slot, pl.ds(_aligned(loc0_ref[i * n_exp + e]), rows)],
                xs_ref.at[pl.ds(_aligned(glb0_ref[i * n_exp + e]), rows)],
                sem.at[slot]).start()
        return carry
    lax.fori_loop(0, n_exp, issue_run, 0)

    @pl.when(i == n - 1)
    def _():
        zero_ref[...] = jnp.zeros(zero_ref.shape, F32)
        max_blocks = xs_ref.shape[0] // EXPERT_ROWS

        def zero_tail(e, total):
            rows = _aligned(taillen_ref[e])

            @pl.when(rows > 0)
            def _():
                pltpu.make_async_copy(zero_ref.at[pl.ds(0, rows)],
                                      xs_ref.at[pl.ds(_aligned(tail0_ref[e]), rows)],
                                      zsem.at[0]).start()
            return total + rows
        tail_rows = lax.fori_loop(0, n_exp, zero_tail, jnp.int32(0))

        def zero_block(blk, carry):
            pltpu.make_async_copy(
                zero_ref,
                xs_ref.at[pl.ds(pl.multiple_of(blk * EXPERT_ROWS, EXPERT_ROWS), EXPERT_ROWS)],
                zsem.at[1]).start()
            return carry
        lax.fori_loop(nblk_ref[0], max_blocks, zero_block, 0)

        @pl.when(i >= 1)
        def _():
            wait_slot(1 - slot, used_ref[i - 1])
        wait_slot(slot, used_ref[i])

        def wait_rows(rows, s):
            rows = _aligned(rows)

            @pl.when(rows > 0)
            def _():
                pltpu.make_async_copy(xs_ref.at[pl.ds(0, rows)], xs_ref.at[pl.ds(0, rows)],
                                      zsem.at[s]).wait()
        wait_rows(tail_rows, 0)
        wait_rows((max_blocks - nblk_ref[0]) * EXPERT_ROWS, 1)


def _dispatch(hn2, idxt, tabs, n_exp, loc_rows, max_blocks):
    n_tok, d = hn2.shape
    n_tiles = n_tok // MOE_TILE
    return pl.pallas_call(
        functools.partial(_dispatch_kernel, n_exp),
        out_shape=jax.ShapeDtypeStruct((max_blocks * EXPERT_ROWS, d), F32),
        grid_spec=pltpu.PrefetchScalarGridSpec(
            num_scalar_prefetch=7,
            grid=(n_tiles,),
            in_specs=[
                pl.BlockSpec((MOE_TILE, d), lambda i, *_: (i, 0)),
                pl.BlockSpec((TOP_K, MOE_TILE), lambda i, *_: (0, i)),
            ],
            out_specs=pl.BlockSpec(memory_space=pl.ANY),
            scratch_shapes=[
                pltpu.VMEM((2, loc_rows, d), F32),
                pltpu.VMEM((EXPERT_ROWS, d), F32),
                pltpu.SemaphoreType.DMA((2,)),
                pltpu.SemaphoreType.DMA((2,)),
            ],
        ),
        compiler_params=_cparams(("arbitrary",)),
        name="moe_dispatch",
    )(tabs["run_loc"], tabs["run_glb"], tabs["run_len"], tabs["rows_used"],
      tabs["tail_glb"], tabs["tail_len"], tabs["n_blocks"], hn2, idxt)


def _expert_kernel(layer, be_ref, first_ref, slot_ref, next_ref, nb_ref,
                   x_ref, wgu_hbm, bgu_ref, wdn_hbm, bdn_ref, y_ref,
                   wgu_f32, wdn_f32, wgu_bf, wdn_bf, sem):
    de = wdn_bf.shape[0]
    b = pl.program_id(0)

    def weight_copies(e, s):
        return (pltpu.make_async_copy(wgu_hbm.at[layer, e], wgu_f32.at[s], sem.at[s, 0]),
                pltpu.make_async_copy(wdn_hbm.at[layer, e], wdn_f32.at[s], sem.at[s, 1]))

    @pl.when(b == 0)
    def _():
        for cp in weight_copies(be_ref[0], 0):
            cp.start()

    @pl.when(first_ref[b] == 1)
    def _():
        s = slot_ref[b]
        for cp in weight_copies(be_ref[b], s):
            cp.wait()

        @pl.when(next_ref[b] >= 0)
        def _():
            for cp in weight_copies(next_ref[b], 1 - s):
                cp.start()
        cw = 512
        for c0 in range(0, wgu_bf.shape[1], cw):
            wgu_bf[:, c0:c0 + cw] = wgu_f32[s, :, c0:c0 + cw].astype(BF16)
        for c0 in range(0, wdn_bf.shape[1], cw):
            wdn_bf[:, c0:c0 + cw] = wdn_f32[s, :, c0:c0 + cw].astype(BF16)

    @pl.when(b < nb_ref[0])
    def _():
        xb = x_ref[...].astype(BF16)
        g = jnp.dot(xb, wgu_bf[:, :de], preferred_element_type=F32) + bgu_ref[:, :de]
        u = jnp.dot(xb, wgu_bf[:, de:], preferred_element_type=F32) + bgu_ref[:, de:]
        g = jnp.minimum(g, SWIGLU_LIMIT)
        u = jnp.clip(u, -SWIGLU_LIMIT, SWIGLU_LIMIT)
        a = (u + 1.0) * g * jax.nn.sigmoid(SWIGLU_ALPHA * g)
        y_ref[...] = jnp.dot(a.astype(BF16), wdn_bf[...], preferred_element_type=F32) + bdn_ref[...]

    @pl.when(pl.program_id(0) >= nb_ref[0])
    def _():
        y_ref[...] = jnp.zeros(y_ref.shape, y_ref.dtype)


def _experts(xs, tabs, layer, wgu, bgu, wdn, bdn):
    rows, d = xs.shape
    depth, n_exp, _, de2 = wgu.shape
    de = de2 // 2
    max_blocks = rows // EXPERT_ROWS
    row_map = lambda b, be, fi, sl, nx, nb: (jnp.minimum(b, nb[0] - 1), 0)
    exp_map = lambda b, be, fi, sl, nx, nb: (layer, be[b], 0, 0)
    return pl.pallas_call(
        functools.partial(_expert_kernel, layer),
        out_shape=jax.ShapeDtypeStruct((rows, d), F32),
        grid_spec=pltpu.PrefetchScalarGridSpec(
            num_scalar_prefetch=5,
            grid=(max_blocks,),
            in_specs=[
                pl.BlockSpec((EXPERT_ROWS, d), row_map),
                pl.BlockSpec(memory_space=pl.ANY),
                pl.BlockSpec((None, None, 1, de2), exp_map),
                pl.BlockSpec(memory_space=pl.ANY),
                pl.BlockSpec((None, None, 1, d), exp_map),
            ],
            out_specs=pl.BlockSpec((EXPERT_ROWS, d), lambda b, *_: (b, 0)),
            scratch_shapes=[
                pltpu.VMEM((2, d, de2), F32), pltpu.VMEM((2, de, d), F32),
                pltpu.VMEM((d, de2), BF16), pltpu.VMEM((de, d), BF16),
                pltpu.SemaphoreType.DMA((2, 2)),
            ],
        ),
        compiler_params=_cparams(("arbitrary",)),
        name="moe_experts",
    )(tabs["blk_exp"], tabs["blk_first"], tabs["blk_slot"], tabs["blk_next"], tabs["n_blocks"],
      xs, wgu, bgu.reshape(depth, n_exp, 1, de2), wdn, bdn.reshape(depth, n_exp, 1, d))


def _combine_kernel(final_norm, n_exp, loc0_ref, glb0_ref, len_ref, used_ref,
                    route_ref, x1_ref, g2_ref, fg_ref, ys_ref, out_ref, loc_ref, sem):
    i = pl.program_id(0)
    n = pl.num_programs(0)
    slot = i % 2
    tile = x1_ref.shape[0]
    loc_rows = loc_ref.shape[1]

    def issue_tile(t, sl):
        def issue_run(e, carry):
            rows = _aligned(len_ref[t * n_exp + e])

            @pl.when(rows > 0)
            def _():
                pltpu.make_async_copy(
                    ys_ref.at[pl.ds(_aligned(glb0_ref[t * n_exp + e]), rows)],
                    loc_ref.at[sl, pl.ds(_aligned(loc0_ref[t * n_exp + e]), rows)],
                    sem.at[sl]).start()
            return carry
        lax.fori_loop(0, n_exp, issue_run, 0)

    @pl.when(i == 0)
    def _():
        issue_tile(0, 0)

    @pl.when(i + 1 < n)
    def _():
        issue_tile(i + 1, 1 - slot)

    route = route_ref[...]
    r_iota = lax.broadcasted_iota(I32, (tile, loc_rows), 1)
    comb = jnp.zeros((tile, loc_rows), F32)
    for k in range(TOP_K):
        dest = route[:, k:k + 1].astype(I32)
        comb = jnp.where(r_iota == dest, route[:, TOP_K + k:TOP_K + k + 1], comb)

    used_rows = _aligned(used_ref[i])

    @pl.when(used_rows > 0)
    def _():
        pltpu.make_async_copy(ys_ref.at[pl.ds(0, used_rows)],
                              loc_ref.at[slot, pl.ds(0, used_rows)], sem.at[slot]).wait()

    row_ok = lax.broadcasted_iota(I32, (loc_rows, 1), 0) < used_rows
    yl = jnp.where(row_ok, loc_ref[slot], 0.0).astype(BF16)
    moe = jnp.dot(comb.astype(BF16), yl, preferred_element_type=F32)
    out = x1_ref[...] + g2_ref[...] * moe
    if final_norm:
        out = _rms(out, fg_ref[...])
    out_ref[...] = out


def _combine(ys, route, x1, gate2, final_g, tabs, n_exp, loc_rows, final_norm):
    b, s, d = x1.shape
    n_tok = b * s
    n_tiles = n_tok // MOE_TILE
    tiles_per_seq = s // MOE_TILE
    out = pl.pallas_call(
        functools.partial(_combine_kernel, final_norm, n_exp),
        out_shape=jax.ShapeDtypeStruct((n_tok, d), F32),
        grid_spec=pltpu.PrefetchScalarGridSpec(
            num_scalar_prefetch=4,
            grid=(n_tiles,),
            in_specs=[
                pl.BlockSpec((MOE_TILE, LANES), lambda i, *_: (i, 0)),
                pl.BlockSpec((MOE_TILE, d), lambda i, *_: (i, 0)),
                pl.BlockSpec((None, 1, d), lambda i, *_: (i // tiles_per_seq, 0, 0)),
                pl.BlockSpec((1, d), lambda i, *_: (0, 0)),
                pl.BlockSpec(memory_space=pl.ANY),
            ],
            out_specs=pl.BlockSpec((MOE_TILE, d), lambda i, *_: (i, 0)),
            scratch_shapes=[
                pltpu.VMEM((2, loc_rows, d), F32),
                pltpu.SemaphoreType.DMA((2,)),
            ],
        ),
        compiler_params=_cparams(("arbitrary",)),
        name="moe_combine",
    )(tabs["run_loc"], tabs["run_glb"], tabs["run_len"], tabs["rows_used"],
      route, x1.reshape(n_tok, d), gate2, final_g, ys)
    return out.reshape(b, s, d)


def _moe(x1, hn2, idxt, route, cnt, gate2, final_g, layer, wgu, bgu, wdn, bdn, final_norm):
    b, s, d = x1.shape
    n_tok = b * s
    n_exp = wgu.shape[1]
    loc_rows, max_blocks = _moe_dims(n_tok, n_exp)
    tabs = _routing_tables(cnt, max_blocks)
    xs = _dispatch(hn2.reshape(n_tok, d), idxt, tabs, n_exp, loc_rows, max_blocks)
    ys = _experts(xs, tabs, layer, wgu, bgu, wdn, bdn)
    return _combine(ys, route, x1, gate2, final_g, tabs, n_exp, loc_rows, final_norm)


def _qkv_kernel(n_heads, scale, x_ref, cc_ref, ss_ref, kvg_ref, wdkv_ref, ckvg_ref, wuk_ref, wuv_ref,
                n1g_ref, sh1_ref, sc1_ref, wdq_ref, cqg_ref, wuq_ref,
                q_ref, k_ref, v_ref):
    x = x_ref[...]
    cc = cc_ref[...]
    ss = ss_ref[...]
    r_kv = ckvg_ref.shape[1]
    hk = _rms(x, kvg_ref[...]).astype(BF16)
    lat = jnp.dot(hk, wdkv_ref[...], preferred_element_type=F32)
    ckv = _rms(lat[:, :r_kv], ckvg_ref[...]).astype(BF16)
    krot = lat[:, r_kv:r_kv + LANES] * cc + lat[:, r_kv + LANES:r_kv + 2 * LANES] * ss
    kn = jnp.dot(ckv, wuk_ref[...], preferred_element_type=F32)
    vt = lax.dot_general(wuv_ref[...], ckv, (((1,), (1,)), ((), ())), preferred_element_type=F32)
    hq = (_rms(x, n1g_ref[...]) * (1.0 + sc1_ref[...]) + sh1_ref[...]).astype(BF16)
    cq = _rms(jnp.dot(hq, wdq_ref[...], preferred_element_type=F32), cqg_ref[...]).astype(BF16)
    qq = jnp.dot(cq, wuq_ref[...], preferred_element_type=F32) * scale
    hd = n_heads * LANES
    ts = x.shape[0]
    ones_rows = (lax.broadcasted_iota(I32, (V_ONES_ROWS, ts), 0) == 0).astype(BF16)
    for h in range(n_heads):
        sl = slice(h * LANES, (h + 1) * LANES)
        k_ref[h, :, 0:LANES] = kn[:, sl].astype(BF16)
        k_ref[h, :, LANES:2 * LANES] = krot.astype(BF16)
        v_ref[h, 0:LANES, :] = vt[h * LANES:(h + 1) * LANES, :].astype(BF16)
        v_ref[h, LANES:LANES + V_ONES_ROWS, :] = ones_rows
        q_ref[h, :, 0:LANES] = qq[:, sl].astype(BF16)
        qrot = qq[:, hd + h * LANES:hd + (h + 1) * LANES] * cc \
            + qq[:, 2 * hd + h * LANES:2 * hd + (h + 1) * LANES] * ss
        q_ref[h, :, LANES:2 * LANES] = qrot.astype(BF16)


def _qkv(x, cc, ss, kvg, wdkv, ckvg, wuk, wuv, n1g, sh1, sc1, wdq, cqg, wuq, n_heads, scale, ts):
    b, s, d = x.shape
    full = lambda a: pl.BlockSpec(a.shape, lambda i, j: (0,) * a.ndim)
    hspec = lambda w: pl.BlockSpec((None, n_heads, ts, w), lambda i, j: (i, 0, j, 0))
    return pl.pallas_call(
        functools.partial(_qkv_kernel, n_heads, scale),
        out_shape=[
            jax.ShapeDtypeStruct((b, n_heads, s, 2 * LANES), BF16),
            jax.ShapeDtypeStruct((b, n_heads, s, 2 * LANES), BF16),
            jax.ShapeDtypeStruct((b, n_heads, s // ts, LANES + V_ONES_ROWS, ts), BF16),
        ],
        grid=(b, s // ts),
        in_specs=[
            pl.BlockSpec((None, ts, d), lambda i, j: (i, j, 0)),
            pl.BlockSpec((None, ts, LANES), lambda i, j: (i, j, 0)),
            pl.BlockSpec((None, ts, LANES), lambda i, j: (i, j, 0)),
            full(kvg), full(wdkv), full(ckvg), full(wuk), full(wuv),
            full(n1g), _bvec_spec(d), _bvec_spec(d), full(wdq), full(cqg), full(wuq),
        ],
        out_specs=[hspec(2 * LANES), hspec(2 * LANES),
                   pl.BlockSpec((None, n_heads, None, LANES + V_ONES_ROWS, ts),
                                lambda i, j: (i, 0, j, 0, 0))],
        compiler_params=_cparams(("arbitrary", "arbitrary")),
        name="mla_qkv",
    )(x, cc, ss, kvg, wdkv, ckvg, wuk, wuv, n1g, sh1, sc1, wdq, cqg, wuq)


def _attn_kernel(q_ref, k_ref, vt_ref, o_ref, m_ref, acc_ref):
    tq = q_ref.shape[0]
    tk = vt_ref.shape[2]
    ratio = tq // tk
    dv = o_ref.shape[1]
    qi = pl.program_id(2)
    m_ref[...] = jnp.full(m_ref.shape, -jnp.inf, F32)
    acc_ref[...] = jnp.zeros(acc_ref.shape, F32)

    def scores(ki, cols, on_diagonal):
        start = pl.multiple_of(ki * tk, tk)
        k = k_ref[pl.ds(start, tk), :]
        st = lax.dot_general(k, q_ref[cols, :], (((1,), (1,)), ((), ())),
                             preferred_element_type=F32)
        if on_diagonal:
            kc = lax.broadcasted_iota(I32, st.shape, 0) // CHUNK
            qc = lax.broadcasted_iota(I32, st.shape, 1) // CHUNK
            st = jnp.where(kc <= qc, st, -jnp.inf)
        return st

    def update(ki, st, m_old, acc_old):
        m_new = jnp.maximum(m_old, jnp.max(st, axis=0, keepdims=True))
        p = jnp.exp2(st - m_new).astype(BF16)
        alpha = jnp.exp2(m_old - m_new)
        acc_new = alpha * acc_old + jnp.dot(vt_ref[ki], p, preferred_element_type=F32)
        return m_new, acc_new

    def step(groups):
        sts = [[scores(ki, cols, on_diagonal) for ki, on_diagonal in tiles]
               for cols, tiles in groups]
        for (cols, tiles), group_sts in zip(groups, sts):
            state = (m_ref[:, cols], acc_ref[:, cols])
            for (ki, _), st in zip(tiles, group_sts):
                state = update(ki, st, *state)
            m_ref[:, cols], acc_ref[:, cols] = state

    n_below = qi * ratio
    everything = slice(0, tq)
    step([(slice(g * tk, (g + 1) * tk), [(n_below + a, a == g) for a in range(g + 1)])
          for g in range(ratio)])

    def fast_step(first, count):
        m = m_ref[...]
        pv = None
        top = None
        for t in range(count):
            st = scores(first + t, everything, False)
            tmax = jnp.max(st, axis=0, keepdims=True)
            top = tmax if top is None else jnp.maximum(top, tmax)
            p = jnp.exp2(st - m).astype(BF16)
            part = jnp.dot(vt_ref[first + t], p, preferred_element_type=F32)
            pv = part if pv is None else pv + part
        safe = jnp.max(top - m) <= ATTN_MAX_SLACK

        @pl.when(safe)
        def _():
            acc_ref[...] += pv

        @pl.when(jnp.logical_not(safe))
        def _():
            def redo(t, carry):
                step([(everything, [(first + t, False)])])
                return carry
            lax.fori_loop(0, count, redo, 0)

    def body(j, carry):
        fast_step(ATTN_UNROLL * j, ATTN_UNROLL)
        return carry
    trips = n_below // ATTN_UNROLL
    lax.fori_loop(0, trips, body, 0)

    for rem in range(math.gcd(ratio, ATTN_UNROLL), ATTN_UNROLL, math.gcd(ratio, ATTN_UNROLL)):
        @pl.when(n_below % ATTN_UNROLL == rem)
        def _(rem=rem):
            fast_step(trips * ATTN_UNROLL, rem)

    o_ref[...] = (acc_ref[0:dv, :] / acc_ref[dv:dv + 1, :]).T.astype(o_ref.dtype)


def _attention(q, k, vt, tq):
    b, h, s, dk = q.shape
    nk, dv_ext, tk = vt.shape[2:]
    dv = dv_ext - V_ONES_ROWS
    assert tk % CHUNK == 0 and s % tq == 0 and tq % tk == 0
    return pl.pallas_call(
        _attn_kernel,
        out_shape=jax.ShapeDtypeStruct((b, s, h * dv), BF16),
        grid=(b, h, s // tq),
        in_specs=[
            pl.BlockSpec((None, None, tq, dk), lambda i, j, t: (i, j, t, 0)),
            pl.BlockSpec((None, None, s, dk), lambda i, j, t: (i, j, 0, 0)),
            pl.BlockSpec((None, None, nk, dv_ext, tk), lambda i, j, t: (i, j, 0, 0, 0)),
        ],
        out_specs=pl.BlockSpec((None, tq, dv), lambda i, j, t: (i, t, j)),
        scratch_shapes=[pltpu.VMEM((1, tq), F32), pltpu.VMEM((dv_ext, tq), F32)],
        compiler_params=_cparams(("arbitrary", "arbitrary", "arbitrary")),
        name="mla_attention",
    )(q, k, vt)


def _swap_halves(w):
    half = w.shape[-1] // 2
    return jnp.concatenate([w[..., half:], w[..., :half]], axis=-1)


def _pad_lanes(w):
    pad = LANES - w.shape[-1]
    return jnp.concatenate([w, jnp.zeros(w.shape[:-1] + (pad,), w.dtype)], axis=-1)


def kernel(x, c, positions, mod_w, mod_b, norm1_g, norm2_g, conv_pw1_w, conv_pw1_b, conv_dw_w, conv_dw_b, conv_ln_g, conv_ln_b, conv_pw2_w, conv_pw2_b, kv_norm_g, w_dkv, ckv_norm_g, w_uk, w_uv, w_dq, cq_norm_g, w_uq, w_o, router_w, router_b, exp_w_gu, exp_b_gu, exp_w_dn, exp_b_dn, final_g):
    b, s, d = x.shape
    n_heads, nope = w_uk.shape[1], w_uk.shape[2]
    r_kv = ckv_norm_g.shape[0]
    rope = w_dkv.shape[1] - r_kv
    vdim = w_uv.shape[2]
    n_exp = router_w.shape[2]
    assert nope == LANES and vdim == LANES and rope <= LANES and d % LANES == 0
    ts = min(512, s)
    tq = min(1024, s)

    mod = _modulation(c, mod_w, mod_b)
    mods = [[m.reshape(b, 1, d) for m in jnp.split(mod[l], 6, axis=-1)] for l in range(2)]
    row = lambda v: v.reshape(1, -1)

    half = rope // 2
    inv = jnp.exp(-(2.0 * math.log(ROPE_THETA) / rope) * jnp.arange(half, dtype=F32))
    ang = positions.astype(F32)[..., None] * inv
    cos, sin = jnp.cos(ang), jnp.sin(ang)
    cc = _pad_lanes(jnp.concatenate([cos, cos], axis=-1))
    ss = _pad_lanes(jnp.concatenate([-sin, sin], axis=-1))

    sh1, sc1, gt1, sh2, sc2, gt2 = mods[0]
    x1, hn2, idxt, route, cnt = _mix0(
        x, row(norm1_g[0]), sh1, sc1, gt1,
        conv_pw1_w[0].astype(BF16), row(conv_pw1_b[0]), conv_dw_w[0], row(conv_dw_b[0]),
        row(conv_ln_g[0]), row(conv_ln_b[0]), conv_pw2_w[0].astype(BF16), row(conv_pw2_b[0]),
        row(norm2_g[0]), sh2, sc2, router_w[0].T, router_b[0].reshape(n_exp, 1), ts)
    cnt = cnt[:, :ts // MOE_TILE].reshape(-1, n_exp)
    x2 = _moe(x1, hn2, idxt, route, cnt, gt2, row(final_g),
              0, exp_w_gu, exp_b_gu, exp_w_dn, exp_b_dn, False)

    sh1, sc1, gt1, sh2, sc2, gt2 = mods[1]
    wdkv_rope = w_dkv[:, r_kv:]
    wdkv_ext = jnp.concatenate(
        [w_dkv[:, :r_kv], _pad_lanes(wdkv_rope), _pad_lanes(_swap_halves(wdkv_rope))], axis=-1)
    wuq = w_uq[0]
    r_q = wuq.shape[0]
    wuq_rope = wuq[:, :, nope:]
    wuq_ext = jnp.concatenate([
        wuq[:, :, :nope].reshape(r_q, n_heads * LANES),
        _pad_lanes(wuq_rope).reshape(r_q, n_heads * LANES),
        _pad_lanes(_swap_halves(wuq_rope)).reshape(r_q, n_heads * LANES)], axis=-1)
    scale = float((nope + rope) ** -0.5 * math.log2(math.e))
    q, k, v = _qkv(
        x2, cc, ss, row(kv_norm_g), wdkv_ext.astype(BF16), row(ckv_norm_g),
        w_uk.reshape(r_kv, n_heads * nope).astype(BF16), w_uv.reshape(r_kv, n_heads * vdim).T.astype(BF16),
        row(norm1_g[1]), sh1, sc1, w_dq[0].astype(BF16), row(cq_norm_g[0]), wuq_ext.astype(BF16),
        n_heads, scale, ts)
    o = _attention(q, k, v, tq)

    x3, hn2, idxt, route, cnt = _mix1(
        x2, o, gt1, w_o[0].astype(BF16), row(norm2_g[1]), sh2, sc2,
        router_w[1].T, router_b[1].reshape(n_exp, 1), ts)
    cnt = cnt[:, :ts // MOE_TILE].reshape(-1, n_exp)
    return _moe(x3, hn2, idxt, route, cnt, gt2, row(final_g),
                1, exp_w_gu, exp_b_gu, exp_w_dn, exp_b_dn, True)
```

```python
import functools
import math

import jax
import jax.numpy as jnp
from jax import lax
from jax.experimental import pallas as pl
from jax.experimental.pallas import tpu as pltpu

CHUNK = 64
TOP_K = 4
ROPE_THETA = 10000.0
SWIGLU_ALPHA = 1.702
SWIGLU_LIMIT = 7.0
EPS = 1e-6

LANES = 128
SUBLANES = 8
VMEM_LIMIT_BYTES = 56 * 1024 * 1024

ROW_ALIGN = SUBLANES
MOE_TILE = 256
EXPERT_ROWS = 512
CONV_HALO = 32
V_ONES_ROWS = 2 * SUBLANES
ATTN_UNROLL = 4
ATTN_MAX_SLACK = 60.0

F32 = jnp.float32
BF16 = jnp.bfloat16
I32 = jnp.int32


def _cparams(sem):
    return pltpu.CompilerParams(dimension_semantics=sem, vmem_limit_bytes=VMEM_LIMIT_BYTES)


def _rms(x, g):
    return x * lax.rsqrt(jnp.mean(x * x, axis=-1, keepdims=True) + EPS) * g


def _round_up(a, m):
    return (a + m - 1) // m * m


def _mod_kernel(c_ref, w_ref, b_ref, o_ref):
    c = c_ref[...]
    ca = c * jax.nn.sigmoid(c)
    o_ref[...] = jnp.dot(ca, w_ref[...], preferred_element_type=F32) + b_ref[...]


def _modulation(c, mod_w, mod_b):
    depth, d, d6 = mod_w.shape
    b = c.shape[0]
    bp = _round_up(b, SUBLANES)
    cp = jnp.zeros((bp, d), F32).at[:b].set(c)
    tn = d6 // 4
    out = pl.pallas_call(
        _mod_kernel,
        out_shape=jax.ShapeDtypeStruct((depth, bp, d6), F32),
        grid=(depth, d6 // tn),
        in_specs=[
            pl.BlockSpec((bp, d), lambda l, j: (0, 0)),
            pl.BlockSpec((None, d, tn), lambda l, j: (l, 0, j)),
            pl.BlockSpec((None, 1, tn), lambda l, j: (l, 0, j)),
        ],
        out_specs=pl.BlockSpec((None, bp, tn), lambda l, j: (l, 0, j)),
        compiler_params=_cparams(("arbitrary", "arbitrary")),
        name="adaln_modulation",
    )(cp, mod_w, mod_b.reshape(depth, 1, d6))
    return out[:, :b]


def _pre_moe(x1, g2, sh2, sc2, rwt, rb, earlier, hn_ref, destt_ref, route_ref, cnt_ref):
    ts = x1.shape[0]
    n_exp = rwt.shape[0]
    hn = _rms(x1, g2) * (1.0 + sc2) + sh2
    hn_ref[...] = hn.astype(BF16)
    def split(v):
        hi = v.astype(BF16)
        return hi, (v - hi.astype(F32)).astype(BF16)
    nt = lambda a, b: lax.dot_general(a, b, (((1,), (1,)), ((), ())), preferred_element_type=F32)
    w_hi, w_lo = split(rwt)
    h_hi, h_lo = split(hn)
    logits = nt(w_hi, h_hi) + (nt(w_hi, h_lo) + nt(w_lo, h_hi)) + rb
    e_iota = lax.broadcasted_iota(I32, (n_exp, ts), 0)
    vals, idxs = [], []
    cur = logits
    for _ in range(TOP_K):
        m = jnp.max(cur, axis=0, keepdims=True)
        i = jnp.min(jnp.where(cur == m, e_iota, n_exp), axis=0, keepdims=True)
        vals.append(m)
        idxs.append(i)
        cur = jnp.where(e_iota == i, -jnp.inf, cur)
    exps = [jnp.exp(v - vals[0]) for v in vals]
    den = exps[0]
    for e in exps[1:]:
        den = den + e
    gates = [e / den for e in exps]
    hits = [e_iota == i for i in idxs]
    onehot = jnp.zeros((n_exp, ts), F32)
    for h in hits:
        onehot = onehot + h.astype(F32)
    rank = jnp.dot(onehot.astype(BF16), earlier, preferred_element_type=F32)
    lower = (lax.broadcasted_iota(I32, (n_exp, n_exp), 1)
             < lax.broadcasted_iota(I32, (n_exp, n_exp), 0)).astype(BF16)
    dests = [[] for _ in range(TOP_K)]
    for t0 in range(0, ts, MOE_TILE):
        cols = slice(t0, t0 + MOE_TILE)
        cnt = jnp.sum(onehot[:, cols], axis=1, keepdims=True)
        cnt8 = jnp.ceil(cnt / ROW_ALIGN) * ROW_ALIGN
        off = jnp.dot(lower, jnp.broadcast_to(cnt8, (n_exp, LANES)).astype(BF16),
                      preferred_element_type=F32)[:, 0:1]
        base = off + rank[:, cols]
        for k, h in enumerate(hits):
            dests[k].append(jnp.sum(jnp.where(h[:, cols], base, 0.0), axis=0, keepdims=True))
    dests = [jnp.concatenate(parts, axis=1) for parts in dests]
    destt_ref[...] = jnp.concatenate(dests, axis=0).astype(I32)
    rows = jnp.concatenate(dests + gates + [jnp.zeros((LANES - 2 * TOP_K, ts), F32)], axis=0)
    route_ref[...] = rows.T
    sel = (lax.broadcasted_iota(I32, (SUBLANES, ts), 1) // MOE_TILE
           == lax.broadcasted_iota(I32, (SUBLANES, ts), 0)).astype(BF16)
    cnt = lax.dot_general(sel, onehot.astype(BF16), (((1,), (1,)), ((), ())),
                          preferred_element_type=F32)
    cnt_ref[...] = cnt.astype(I32)


def _earlier_in_tile(ts):
    t = jnp.arange(ts, dtype=I32)
    same_tile = (t[:, None] // MOE_TILE) == (t[None, :] // MOE_TILE)
    return ((t[:, None] < t[None, :]) & same_tile).astype(BF16)


def _pre_moe_specs(b, s, d, ts, n_exp):
    ns = s // ts
    out_shape = [
        jax.ShapeDtypeStruct((b, s, d), F32),
        jax.ShapeDtypeStruct((b, s, d), BF16),
        jax.ShapeDtypeStruct((TOP_K, b * s), I32),
        jax.ShapeDtypeStruct((b * s, LANES), F32),
        jax.ShapeDtypeStruct((b * ns, SUBLANES, n_exp), I32),
    ]
    out_specs = [
        pl.BlockSpec((None, ts, d), lambda i, j: (i, j, 0)),
        pl.BlockSpec((None, ts, d), lambda i, j: (i, j, 0)),
        pl.BlockSpec((TOP_K, ts), lambda i, j: (0, i * ns + j)),
        pl.BlockSpec((ts, LANES), lambda i, j: (i * ns + j, 0)),
        pl.BlockSpec((None, SUBLANES, n_exp), lambda i, j: (i * ns + j, 0, 0)),
    ]
    return out_shape, out_specs


def _vec_spec(d):
    return pl.BlockSpec((1, d), lambda i, j: (0, 0))


def _bvec_spec(d):
    return pl.BlockSpec((None, 1, d), lambda i, j: (i, 0, 0))


def _mix0_kernel(x_ref, n1g_ref, sh1_ref, sc1_ref, gt1_ref,
                 pw1w_ref, pw1b_ref, dww_ref, dwb_ref, lng_ref, lnb_ref,
                 pw2w_ref, pw2b_ref, n2g_ref, sh2_ref, sc2_ref, rwt_ref, rb_ref, earlier_ref,
                 x1_ref, hn_ref, destt_ref, route_ref, cnt_ref, buf_ref, conv_ref, shift_ref):
    ts, d = x_ref.shape
    width = dww_ref.shape[0]
    x = x_ref[...]
    hn = _rms(x, n1g_ref[...]) * (1.0 + sc1_ref[...]) + sh1_ref[...]
    hb = hn.astype(BF16)
    a = jnp.dot(hb, pw1w_ref[:, :d], preferred_element_type=F32) + pw1b_ref[:, :d]
    g = jnp.dot(hb, pw1w_ref[:, d:], preferred_element_type=F32) + pw1b_ref[:, d:]
    glu = a * jax.nn.sigmoid(g)

    @pl.when(pl.program_id(1) == 0)
    def _():
        buf_ref[0:CONV_HALO, :] = jnp.zeros((CONV_HALO, d), F32)

    buf_ref[CONV_HALO:, :] = glu
    base = CONV_HALO - (width - 1)
    rc = 32
    lc = min(512, d)
    sh_rows = shift_ref.shape[1]
    sub = lax.broadcasted_iota(I32, (SUBLANES, lc), 0)
    for c0 in range(0, d, lc):
        for r in range(1, SUBLANES):
            from_this = sub < SUBLANES - r
            cur = pltpu.roll(buf_ref[0:SUBLANES, c0:c0 + lc], SUBLANES - r, axis=0)
            for j0 in range(0, sh_rows, SUBLANES):
                nxt = pltpu.roll(buf_ref[j0 + SUBLANES:j0 + 2 * SUBLANES, c0:c0 + lc],
                                 SUBLANES - r, axis=0)
                shift_ref[r - 1, j0:j0 + SUBLANES, :] = jnp.where(from_this, cur, nxt)
                cur = nxt
        for r0 in range(0, ts, rc):
            acc = jnp.zeros((rc, lc), F32)
            for k in range(width):
                q8, r = divmod(base + k, SUBLANES)
                lo = q8 * SUBLANES + r0
                if r == 0:
                    win = buf_ref[lo:lo + rc, c0:c0 + lc]
                else:
                    win = shift_ref[r - 1, lo:lo + rc, :]
                acc = acc + dww_ref[k:k + 1, c0:c0 + lc] * win
            conv_ref[r0:r0 + rc, c0:c0 + lc] = acc
    buf_ref[0:CONV_HALO, :] = buf_ref[ts:ts + CONV_HALO, :]
    u = conv_ref[...] + dwb_ref[...]
    mu = jnp.mean(u, axis=-1, keepdims=True)
    dlt = u - mu
    var = jnp.mean(dlt * dlt, axis=-1, keepdims=True)
    u = dlt * lax.rsqrt(var + EPS) * lng_ref[...] + lnb_ref[...]
    u = u * jax.nn.sigmoid(u)
    y = jnp.dot(u.astype(BF16), pw2w_ref[...], preferred_element_type=F32) + pw2b_ref[...]
    x1 = x + gt1_ref[...] * y
    x1_ref[...] = x1
    _pre_moe(x1, n2g_ref[...], sh2_ref[...], sc2_ref[...], rwt_ref[...], rb_ref[...],
             earlier_ref[...], hn_ref, destt_ref, route_ref, cnt_ref)


def _mix0(x, n1g, sh1, sc1, gt1, pw1w, pw1b, dww, dwb, lng, lnb, pw2w, pw2b,
          n2g, sh2, sc2, rwt, rb, ts):
    b, s, d = x.shape
    n_exp = rwt.shape[0]
    width = dww.shape[0]
    assert width - 1 <= CONV_HALO and ts % MOE_TILE == 0 and ts // MOE_TILE <= SUBLANES
    out_shape, out_specs = _pre_moe_specs(b, s, d, ts, n_exp)
    full = lambda shp: pl.BlockSpec(shp, lambda i, j: (0,) * len(shp))
    return pl.pallas_call(
        _mix0_kernel,
        out_shape=out_shape,
        grid=(b, s // ts),
        in_specs=[
            pl.BlockSpec((None, ts, d), lambda i, j: (i, j, 0)),
            _vec_spec(d), _bvec_spec(d), _bvec_spec(d), _bvec_spec(d),
            full((d, 2 * d)), full((1, 2 * d)), full((width, d)), full((1, d)),
            full((1, d)), full((1, d)), full((d, d)), full((1, d)),
            _vec_spec(d), _bvec_spec(d), _bvec_spec(d),
            full((n_exp, d)), full((n_exp, 1)), full((ts, ts)),
        ],
        out_specs=out_specs,
        scratch_shapes=[
            pltpu.VMEM((CONV_HALO + ts, d), F32),
            pltpu.VMEM((ts, d), F32),
            pltpu.VMEM((SUBLANES - 1, CONV_HALO + ts - SUBLANES, min(512, d)), F32),
        ],
        compiler_params=_cparams(("arbitrary", "arbitrary")),
        name="conformer_mixer",
    )(x, n1g, sh1, sc1, gt1, pw1w, pw1b, dww, dwb, lng, lnb, pw2w, pw2b,
      n2g, sh2, sc2, rwt, rb, _earlier_in_tile(ts))


def _mix1_kernel(x_ref, o_ref, gt1_ref, wo_ref, n2g_ref, sh2_ref, sc2_ref, rwt_ref, rb_ref,
                 earlier_ref, x1_ref, hn_ref, destt_ref, route_ref, cnt_ref):
    y = jnp.dot(o_ref[...], wo_ref[...], preferred_element_type=F32)
    x1 = x_ref[...] + gt1_ref[...] * y
    x1_ref[...] = x1
    _pre_moe(x1, n2g_ref[...], sh2_ref[...], sc2_ref[...], rwt_ref[...], rb_ref[...],
             earlier_ref[...], hn_ref, destt_ref, route_ref, cnt_ref)


def _mix1(x, o, gt1, wo, n2g, sh2, sc2, rwt, rb, ts):
    b, s, d = x.shape
    n_exp = rwt.shape[0]
    do = o.shape[-1]
    assert ts % MOE_TILE == 0 and ts // MOE_TILE <= SUBLANES
    out_shape, out_specs = _pre_moe_specs(b, s, d, ts, n_exp)
    full = lambda shp: pl.BlockSpec(shp, lambda i, j: (0,) * len(shp))
    return pl.pallas_call(
        _mix1_kernel,
        out_shape=out_shape,
        grid=(b, s // ts),
        in_specs=[
            pl.BlockSpec((None, ts, d), lambda i, j: (i, j, 0)),
            pl.BlockSpec((None, ts, do), lambda i, j: (i, j, 0)),
            _bvec_spec(d), full((do, d)),
            _vec_spec(d), _bvec_spec(d), _bvec_spec(d),
            full((n_exp, d)), full((n_exp, 1)), full((ts, ts)),
        ],
        out_specs=out_specs,
        compiler_params=_cparams(("arbitrary", "arbitrary")),
        name="attn_out_mixer",
    )(x, o, gt1, wo, n2g, sh2, sc2, rwt, rb, _earlier_in_tile(ts))


def _moe_dims(n_tok, n_exp):
    n_tiles = n_tok // MOE_TILE
    loc_rows = _round_up(TOP_K * MOE_TILE + n_exp * (ROW_ALIGN - 1), LANES)
    max_rows = TOP_K * n_tok + n_tiles * n_exp * (ROW_ALIGN - 1) + n_exp * (EXPERT_ROWS - ROW_ALIGN)
    max_blocks = -(-max_rows // EXPERT_ROWS)
    return loc_rows, max_blocks


def _routing_tables(cnt, max_blocks):
    n_tiles, n_exp = cnt.shape
    cnt8 = _round_up(cnt, ROW_ALIGN)
    off = jnp.cumsum(cnt8, axis=1) - cnt8
    seg_len = cnt8.sum(axis=0)
    seg_pad = _round_up(seg_len, EXPERT_ROWS)
    seg_end = jnp.cumsum(seg_pad)
    seg_start = seg_end - seg_pad
    run_start = seg_start[None, :] + jnp.cumsum(cnt8, axis=0) - cnt8
    n_blocks = (seg_end[-1] // EXPERT_ROWS).astype(I32)
    blk_row = jnp.arange(max_blocks, dtype=I32) * EXPERT_ROWS
    blk_row = jnp.minimum(blk_row, seg_end[-1] - EXPERT_ROWS)
    blk_exp = jnp.minimum((blk_row[:, None] >= seg_end[None, :]).sum(axis=-1), n_exp - 1).astype(I32)
    blk = jnp.arange(max_blocks, dtype=I32)
    prev_exp = jnp.concatenate([jnp.full((1,), -1, I32), blk_exp[:-1]])
    blk_first = ((blk_exp != prev_exp) & (blk < n_blocks)).astype(I32)
    blk_slot = ((jnp.cumsum(blk_first) - 1) % 2).astype(I32)
    e_ids = jnp.arange(n_exp, dtype=I32)
    later = (e_ids[None, :] > e_ids[:, None]) & (seg_pad[None, :] > 0)
    next_exp = jnp.min(jnp.where(later, e_ids[None, :], n_exp), axis=1)
    next_exp = jnp.where(next_exp < n_exp, next_exp, -1)
    blk_next = jnp.sum(jnp.where(blk_exp[:, None] == e_ids[None, :], next_exp[None, :], 0), axis=1)
    flat = lambda a: a.reshape(-1).astype(I32)
    return dict(
        run_loc=flat(off),
        run_glb=flat(run_start),
        run_len=flat(cnt8),
        rows_used=flat(cnt8.sum(axis=1)),
        tail_glb=flat(seg_start + seg_len),
        tail_len=flat(seg_pad - seg_len),
        blk_exp=blk_exp, blk_first=blk_first, blk_slot=blk_slot, blk_next=flat(blk_next),
        n_blocks=n_blocks.reshape(1))


def _aligned(v):
    return pl.multiple_of(v, ROW_ALIGN)


def _dispatch_kernel(n_exp, loc0_ref, glb0_ref, len_ref, used_ref, tail0_ref, taillen_ref, nblk_ref,
                     x_ref, destt_ref, xs_ref, loc_ref, zero_ref, sem, zsem):
    i = pl.program_id(0)
    n = pl.num_programs(0)
    slot = i % 2
    tile = x_ref.shape[0]
    loc_rows = loc_ref.shape[1]

    def wait_slot(sl, rows):
        rows = _aligned(rows)

        @pl.when(rows > 0)
        def _():
            pltpu.make_async_copy(loc_ref.at[sl, pl.ds(0, rows)], xs_ref.at[pl.ds(0, rows)],
                                  sem.at[sl]).wait()

    @pl.when(i >= 2)
    def _():
        wait_slot(slot, used_ref[i - 2])

    r_iota = lax.broadcasted_iota(I32, (loc_rows, tile), 0)
    perm = jnp.zeros((loc_rows, tile), F32)
    for k in range(TOP_K):
        perm = jnp.where(r_iota == destt_ref[k:k + 1, :], 1.0, perm)
    loc_ref[slot] = jnp.dot(perm.astype(BF16), x_ref[...], preferred_element_type=F32)

    def issue_run(e, carry):
        rows = _aligned(len_ref[i * n_exp + e])

        @pl.when(rows > 0)
        def _():
            pltpu.make_async_copy(
                loc_ref.at[slot, pl.ds(_aligned(loc0_ref[i * n_exp + e]), rows)],
                xs_ref.at[pl.ds(_aligned(glb0_ref[i * n_exp + e]), rows)],
                sem.at[slot]).start()
        return carry
    lax.fori_loop(0, n_exp, issue_run, 0)

    @pl.when(i == n - 1)
    def _():
        zero_ref[...] = jnp.zeros(zero_ref.shape, F32)
        max_blocks = xs_ref.shape[0] // EXPERT_ROWS

        def zero_tail(e, total):
            rows = _aligned(taillen_ref[e])

            @pl.when(rows > 0)
            def _():
                pltpu.make_async_copy(zero_ref.at[pl.ds(0, rows)],
                                      xs_ref.at[pl.ds(_aligned(tail0_ref[e]), rows)],
                                      zsem.at[0]).start()
            return total + rows
        tail_rows = lax.fori_loop(0, n_exp, zero_tail, jnp.int32(0))

        def zero_block(blk, carry):
            pltpu.make_async_copy(
                zero_ref,
                xs_ref.at[pl.ds(pl.multiple_of(blk * EXPERT_ROWS, EXPERT_ROWS), EXPERT_ROWS)],
                zsem.at[1]).start()
            return carry
        lax.fori_loop(nblk_ref[0], max_blocks, zero_block, 0)

        @pl.when(i >= 1)
        def _():
            wait_slot(1 - slot, used_ref[i - 1])
        wait_slot(slot, used_ref[i])

        def wait_rows(rows, s):
            rows = _aligned(rows)

            @pl.when(rows > 0)
            def _():
                pltpu.make_async_copy(xs_ref.at[pl.ds(0, rows)], xs_ref.at[pl.ds(0, rows)],
                                      zsem.at[s]).wait()
        wait_rows(tail_rows, 0)
        wait_rows((max_blocks - nblk_ref[0]) * EXPERT_ROWS, 1)


def _dispatch(hn2, idxt, tabs, n_exp, loc_rows, max_blocks):
    n_tok, d = hn2.shape
    n_tiles = n_tok // MOE_TILE
    return pl.pallas_call(
        functools.partial(_dispatch_kernel, n_exp),
        out_shape=jax.ShapeDtypeStruct((max_blocks * EXPERT_ROWS, d), F32),
        grid_spec=pltpu.PrefetchScalarGridSpec(
            num_scalar_prefetch=7,
            grid=(n_tiles,),
            in_specs=[
                pl.BlockSpec((MOE_TILE, d), lambda i, *_: (i, 0)),
                pl.BlockSpec((TOP_K, MOE_TILE), lambda i, *_: (0, i)),
            ],
            out_specs=pl.BlockSpec(memory_space=pl.ANY),
            scratch_shapes=[
                pltpu.VMEM((2, loc_rows, d), F32),
                pltpu.VMEM((EXPERT_ROWS, d), F32),
                pltpu.SemaphoreType.DMA((2,)),
                pltpu.SemaphoreType.DMA((2,)),
            ],
        ),
        compiler_params=_cparams(("arbitrary",)),
        name="moe_dispatch",
    )(tabs["run_loc"], tabs["run_glb"], tabs["run_len"], tabs["rows_used"],
      tabs["tail_glb"], tabs["tail_len"], tabs["n_blocks"], hn2, idxt)


def _expert_kernel(layer, be_ref, first_ref, slot_ref, next_ref, nb_ref,
                   x_ref, wgu_hbm, bgu_ref, wdn_hbm, bdn_ref, y_ref,
                   wgu_f32, wdn_f32, wgu_bf, wdn_bf, sem):
    de = wdn_bf.shape[0]
    b = pl.program_id(0)

    def weight_copies(e, s):
        return (pltpu.make_async_copy(wgu_hbm.at[layer, e], wgu_f32.at[s], sem.at[s, 0]),
                pltpu.make_async_copy(wdn_hbm.at[layer, e], wdn_f32.at[s], sem.at[s, 1]))

    @pl.when(b == 0)
    def _():
        for cp in weight_copies(be_ref[0], 0):
            cp.start()

    @pl.when(first_ref[b] == 1)
    def _():
        s = slot_ref[b]
        for cp in weight_copies(be_ref[b], s):
            cp.wait()

        @pl.when(next_ref[b] >= 0)
        def _():
            for cp in weight_copies(next_ref[b], 1 - s):
                cp.start()
        cw = 512
        for c0 in range(0, wgu_bf.shape[1], cw):
            wgu_bf[:, c0:c0 + cw] = wgu_f32[s, :, c0:c0 + cw].astype(BF16)
        for c0 in range(0, wdn_bf.shape[1], cw):
            wdn_bf[:, c0:c0 + cw] = wdn_f32[s, :, c0:c0 + cw].astype(BF16)

    @pl.when(b < nb_ref[0])
    def _():
        xb = x_ref[...].astype(BF16)
        g = jnp.dot(xb, wgu_bf[:, :de], preferred_element_type=F32) + bgu_ref[:, :de]
        u = jnp.dot(xb, wgu_bf[:, de:], preferred_element_type=F32) + bgu_ref[:, de:]
        g = jnp.minimum(g, SWIGLU_LIMIT)
        u = jnp.clip(u, -SWIGLU_LIMIT, SWIGLU_LIMIT)
        a = (u + 1.0) * g * jax.nn.sigmoid(SWIGLU_ALPHA * g)
        y_ref[...] = jnp.dot(a.astype(BF16), wdn_bf[...], preferred_element_type=F32) + bdn_ref[...]

    @pl.when(pl.program_id(0) >= nb_ref[0])
    def _():
        y_ref[...] = jnp.zeros(y_ref.shape, y_ref.dtype)


def _experts(xs, tabs, layer, wgu, bgu, wdn, bdn):
    rows, d = xs.shape
    depth, n_exp, _, de2 = wgu.shape
    de = de2 // 2
    max_blocks = rows // EXPERT_ROWS
    row_map = lambda b, be, fi, sl, nx, nb: (jnp.minimum(b, nb[0] - 1), 0)
    exp_map = lambda b, be, fi, sl, nx, nb: (layer, be[b], 0, 0)
    return pl.pallas_call(
        functools.partial(_expert_kernel, layer),
        out_shape=jax.ShapeDtypeStruct((rows, d), F32),
        grid_spec=pltpu.PrefetchScalarGridSpec(
            num_scalar_prefetch=5,
            grid=(max_blocks,),
            in_specs=[
                pl.BlockSpec((EXPERT_ROWS, d), row_map),
                pl.BlockSpec(memory_space=pl.ANY),
                pl.BlockSpec((None, None, 1, de2), exp_map),
                pl.BlockSpec(memory_space=pl.ANY),
                pl.BlockSpec((None, None, 1, d), exp_map),
            ],
            out_specs=pl.BlockSpec((EXPERT_ROWS, d), lambda b, *_: (b, 0)),
            scratch_shapes=[
                pltpu.VMEM((2, d, de2), F32), pltpu.VMEM((2, de, d), F32),
                pltpu.VMEM((d, de2), BF16), pltpu.VMEM((de, d), BF16),
                pltpu.SemaphoreType.DMA((2, 2)),
            ],
        ),
        compiler_params=_cparams(("arbitrary",)),
        name="moe_experts",
    )(tabs["blk_exp"], tabs["blk_first"], tabs["blk_slot"], tabs["blk_next"], tabs["n_blocks"],
      xs, wgu, bgu.reshape(depth, n_exp, 1, de2), wdn, bdn.reshape(depth, n_exp, 1, d))


def _combine_kernel(final_norm, n_exp, loc0_ref, glb0_ref, len_ref, used_ref,
                    route_ref, x1_ref, g2_ref, fg_ref, ys_ref, out_ref, loc_ref, sem):
    i = pl.program_id(0)
    n = pl.num_programs(0)
    slot = i % 2
    tile = x1_ref.shape[0]
    loc_rows = loc_ref.shape[1]

    def issue_tile(t, sl):
        def issue_run(e, carry):
            rows = _aligned(len_ref[t * n_exp + e])

            @pl.when(rows > 0)
            def _():
                pltpu.make_async_copy(
                    ys_ref.at[pl.ds(_aligned(glb0_ref[t * n_exp + e]), rows)],
                    loc_ref.at[sl, pl.ds(_aligned(loc0_ref[t * n_exp + e]), rows)],
                    sem.at[sl]).start()
            return carry
        lax.fori_loop(0, n_exp, issue_run, 0)

    @pl.when(i == 0)
    def _():
        issue_tile(0, 0)

    @pl.when(i + 1 < n)
    def _():
        issue_tile(i + 1, 1 - slot)

    route = route_ref[...]
    r_iota = lax.broadcasted_iota(I32, (tile, loc_rows), 1)
    comb = jnp.zeros((tile, loc_rows), F32)
    for k in range(TOP_K):
        dest = route[:, k:k + 1].astype(I32)
        comb = jnp.where(r_iota == dest, route[:, TOP_K + k:TOP_K + k + 1], comb)

    used_rows = _aligned(used_ref[i])

    @pl.when(used_rows > 0)
    def _():
        pltpu.make_async_copy(ys_ref.at[pl.ds(0, used_rows)],
                              loc_ref.at[slot, pl.ds(0, used_rows)], sem.at[slot]).wait()

    row_ok = lax.broadcasted_iota(I32, (loc_rows, 1), 0) < used_rows
    yl = jnp.where(row_ok, loc_ref[slot], 0.0).astype(BF16)
    moe = jnp.dot(comb.astype(BF16), yl, preferred_element_type=F32)
    out = x1_ref[...] + g2_ref[...] * moe
    if final_norm:
        out = _rms(out, fg_ref[...])
    out_ref[...] = out


def _combine(ys, route, x1, gate2, final_g, tabs, n_exp, loc_rows, final_norm):
    b, s, d = x1.shape
    n_tok = b * s
    n_tiles = n_tok // MOE_TILE
    tiles_per_seq = s // MOE_TILE
    out = pl.pallas_call(
        functools.partial(_combine_kernel, final_norm, n_exp),
        out_shape=jax.ShapeDtypeStruct((n_tok, d), F32),
        grid_spec=pltpu.PrefetchScalarGridSpec(
            num_scalar_prefetch=4,
            grid=(n_tiles,),
            in_specs=[
                pl.BlockSpec((MOE_TILE, LANES), lambda i, *_: (i, 0)),
                pl.BlockSpec((MOE_TILE, d), lambda i, *_: (i, 0)),
                pl.BlockSpec((None, 1, d), lambda i, *_: (i // tiles_per_seq, 0, 0)),
                pl.BlockSpec((1, d), lambda i, *_: (0, 0)),
                pl.BlockSpec(memory_space=pl.ANY),
            ],
            out_specs=pl.BlockSpec((MOE_TILE, d), lambda i, *_: (i, 0)),
            scratch_shapes=[
                pltpu.VMEM((2, loc_rows, d), F32),
                pltpu.SemaphoreType.DMA((2,)),
            ],
        ),
        compiler_params=_cparams(("arbitrary",)),
        name="moe_combine",
    )(tabs["run_loc"], tabs["run_glb"], tabs["run_len"], tabs["rows_used"],
      route, x1.reshape(n_tok, d), gate2, final_g, ys)
    return out.reshape(b, s, d)


def _moe(x1, hn2, idxt, route, cnt, gate2, final_g, layer, wgu, bgu, wdn, bdn, final_norm):
    b, s, d = x1.shape
    n_tok = b * s
    n_exp = wgu.shape[1]
    loc_rows, max_blocks = _moe_dims(n_tok, n_exp)
    tabs = _routing_tables(cnt, max_blocks)
    xs = _dispatch(hn2.reshape(n_tok, d), idxt, tabs, n_exp, loc_rows, max_blocks)
    ys = _experts(xs, tabs, layer, wgu, bgu, wdn, bdn)
    return _combine(ys, route, x1, gate2, final_g, tabs, n_exp, loc_rows, final_norm)


def _qkv_kernel(n_heads, rope, scale, x_ref, cc_ref, ss_ref, kvg_ref, wdkv_ref, ckvg_ref, wuk_ref, wuv_ref,
                n1g_ref, sh1_ref, sc1_ref, wdq_ref, cqg_ref, wuq_ref,
                q_ref, k_ref, v_ref):
    x = x_ref[...]
    cc = cc_ref[...]
    ss = ss_ref[...]
    r_kv = ckvg_ref.shape[1]
    hk = _rms(x, kvg_ref[...]).astype(BF16)
    lat = jnp.dot(hk, wdkv_ref[...], preferred_element_type=F32)
    ckv = _rms(lat[:, :r_kv], ckvg_ref[...]).astype(BF16)
    krot = lat[:, r_kv:r_kv + LANES] * cc + lat[:, r_kv + LANES:r_kv + 2 * LANES] * ss
    kn = jnp.dot(ckv, wuk_ref[...], preferred_element_type=F32)
    vt = lax.dot_general(wuv_ref[...], ckv, (((1,), (1,)), ((), ())), preferred_element_type=F32)
    hq = (_rms(x, n1g_ref[...]) * (1.0 + sc1_ref[...]) + sh1_ref[...]).astype(BF16)
    cq = _rms(jnp.dot(hq, wdq_ref[...], preferred_element_type=F32), cqg_ref[...]).astype(BF16)
    qq = jnp.dot(cq, wuq_ref[...], preferred_element_type=F32) * scale
    hd = n_heads * LANES
    ts = x.shape[0]
    ones_rows = (lax.broadcasted_iota(I32, (V_ONES_ROWS, ts), 0) == 0).astype(BF16)
    for h in range(n_heads):
        sl = slice(h * LANES, (h + 1) * LANES)
        k_ref[h, :, 0:LANES] = kn[:, sl].astype(BF16)
        k_ref[h, :, LANES:2 * LANES] = krot.astype(BF16)
        v_ref[h, 0:LANES, :] = vt[h * LANES:(h + 1) * LANES, :].astype(BF16)
        v_ref[h, LANES:LANES + V_ONES_ROWS, :] = ones_rows
        q_ref[h, :, 0:LANES] = qq[:, sl].astype(BF16)
    per_tile = LANES // rope
    cc_rep, ss_rep = cc, ss
    for u in range(1, per_tile):
        cc_rep = cc_rep + pltpu.roll(cc, u * rope, axis=1)
        ss_rep = ss_rep + pltpu.roll(ss, u * rope, axis=1)
    in_head = lax.broadcasted_iota(I32, (ts, LANES), 1) < rope
    n_rope = n_heads * rope
    for j in range(n_heads // per_tile):
        cols = slice(hd + j * LANES, hd + (j + 1) * LANES)
        swapped = slice(hd + n_rope + j * LANES, hd + n_rope + (j + 1) * LANES)
        rot = qq[:, cols] * cc_rep + qq[:, swapped] * ss_rep
        for u in range(per_tile):
            piece = rot if u == 0 else pltpu.roll(rot, LANES - u * rope, axis=1)
            q_ref[j * per_tile + u, :, LANES:2 * LANES] = jnp.where(in_head, piece, 0.0).astype(BF16)


def _qkv(x, cc, ss, kvg, wdkv, ckvg, wuk, wuv, n1g, sh1, sc1, wdq, cqg, wuq, n_heads, rope, scale, ts):
    b, s, d = x.shape
    assert LANES % rope == 0 and n_heads % (LANES // rope) == 0
    full = lambda a: pl.BlockSpec(a.shape, lambda i, j: (0,) * a.ndim)
    hspec = lambda w: pl.BlockSpec((None, n_heads, ts, w), lambda i, j: (i, 0, j, 0))
    return pl.pallas_call(
        functools.partial(_qkv_kernel, n_heads, rope, scale),
        out_shape=[
            jax.ShapeDtypeStruct((b, n_heads, s, 2 * LANES), BF16),
            jax.ShapeDtypeStruct((b, n_heads, s, 2 * LANES), BF16),
            jax.ShapeDtypeStruct((b, n_heads, s // ts, LANES + V_ONES_ROWS, ts), BF16),
        ],
        grid=(b, s // ts),
        in_specs=[
            pl.BlockSpec((None, ts, d), lambda i, j: (i, j, 0)),
            pl.BlockSpec((None, ts, LANES), lambda i, j: (i, j, 0)),
            pl.BlockSpec((None, ts, LANES), lambda i, j: (i, j, 0)),
            full(kvg), full(wdkv), full(ckvg), full(wuk), full(wuv),
            full(n1g), _bvec_spec(d), _bvec_spec(d), full(wdq), full(cqg), full(wuq),
        ],
        out_specs=[hspec(2 * LANES), hspec(2 * LANES),
                   pl.BlockSpec((None, n_heads, None, LANES + V_ONES_ROWS, ts),
                                lambda i, j: (i, 0, j, 0, 0))],
        compiler_params=_cparams(("arbitrary", "arbitrary")),
        name="mla_qkv",
    )(x, cc, ss, kvg, wdkv, ckvg, wuk, wuv, n1g, sh1, sc1, wdq, cqg, wuq)


def _attn_kernel(q_ref, k_ref, vt_ref, o_ref, m_ref, acc_ref):
    tq = q_ref.shape[0]
    tk = vt_ref.shape[2]
    ratio = tq // tk
    dv = o_ref.shape[1]
    qi = pl.program_id(2)
    m_ref[...] = jnp.full(m_ref.shape, -jnp.inf, F32)
    acc_ref[...] = jnp.zeros(acc_ref.shape, F32)

    def scores(ki, cols, on_diagonal):
        start = pl.multiple_of(ki * tk, tk)
        k = k_ref[pl.ds(start, tk), :]
        st = lax.dot_general(k, q_ref[cols, :], (((1,), (1,)), ((), ())),
                             preferred_element_type=F32)
        if on_diagonal:
            kc = lax.broadcasted_iota(I32, st.shape, 0) // CHUNK
            qc = lax.broadcasted_iota(I32, st.shape, 1) // CHUNK
            st = jnp.where(kc <= qc, st, -jnp.inf)
        return st

    def update(ki, st, m_old, acc_old):
        m_new = jnp.maximum(m_old, jnp.max(st, axis=0, keepdims=True))
        p = jnp.exp2(st - m_new).astype(BF16)
        alpha = jnp.exp2(m_old - m_new)
        acc_new = alpha * acc_old + jnp.dot(vt_ref[ki], p, preferred_element_type=F32)
        return m_new, acc_new

    def step(groups):
        sts = [[scores(ki, cols, on_diagonal) for ki, on_diagonal in tiles]
               for cols, tiles in groups]
        for (cols, tiles), group_sts in zip(groups, sts):
            state = (m_ref[:, cols], acc_ref[:, cols])
            for (ki, _), st in zip(tiles, group_sts):
                state = update(ki, st, *state)
            m_ref[:, cols], acc_ref[:, cols] = state

    n_below = qi * ratio
    everything = slice(0, tq)
    step([(slice(g * tk, (g + 1) * tk), [(n_below + a, a == g) for a in range(g + 1)])
          for g in range(ratio)])

    def fast_step(first, count):
        m = m_ref[...]
        pv = None
        top = None
        for t in range(count):
            st = scores(first + t, everything, False)
            tmax = jnp.max(st, axis=0, keepdims=True)
            top = tmax if top is None else jnp.maximum(top, tmax)
            p = jnp.exp2(st - m).astype(BF16)
            part = jnp.dot(vt_ref[first + t], p, preferred_element_type=F32)
            pv = part if pv is None else pv + part
        safe = jnp.max(top - m) <= ATTN_MAX_SLACK

        @pl.when(safe)
        def _():
            acc_ref[...] += pv

        @pl.when(jnp.logical_not(safe))
        def _():
            def redo(t, carry):
                step([(everything, [(first + t, False)])])
                return carry
            lax.fori_loop(0, count, redo, 0)

    def body(j, carry):
        fast_step(ATTN_UNROLL * j, ATTN_UNROLL)
        return carry
    trips = n_below // ATTN_UNROLL
    lax.fori_loop(0, trips, body, 0)

    for rem in range(math.gcd(ratio, ATTN_UNROLL), ATTN_UNROLL, math.gcd(ratio, ATTN_UNROLL)):
        @pl.when(n_below % ATTN_UNROLL == rem)
        def _(rem=rem):
            fast_step(trips * ATTN_UNROLL, rem)

    o_ref[...] = (acc_ref[0:dv, :] / acc_ref[dv:dv + 1, :]).T.astype(o_ref.dtype)


def _attention(q, k, vt, tq):
    b, h, s, dk = q.shape
    nk, dv_ext, tk = vt.shape[2:]
    dv = dv_ext - V_ONES_ROWS
    assert tk % CHUNK == 0 and s % tq == 0 and tq % tk == 0
    return pl.pallas_call(
        _attn_kernel,
        out_shape=jax.ShapeDtypeStruct((b, s, h * dv), BF16),
        grid=(b, h, s // tq),
        in_specs=[
            pl.BlockSpec((None, None, tq, dk), lambda i, j, t: (i, j, t, 0)),
            pl.BlockSpec((None, None, s, dk), lambda i, j, t: (i, j, 0, 0)),
            pl.BlockSpec((None, None, nk, dv_ext, tk), lambda i, j, t: (i, j, 0, 0, 0)),
        ],
        out_specs=pl.BlockSpec((None, tq, dv), lambda i, j, t: (i, t, j)),
        scratch_shapes=[pltpu.VMEM((1, tq), F32), pltpu.VMEM((dv_ext, tq), F32)],
        compiler_params=_cparams(("arbitrary", "arbitrary", "arbitrary")),
        name="mla_attention",
    )(q, k, vt)


def _swap_halves(w):
    half = w.shape[-1] // 2
    return jnp.concatenate([w[..., half:], w[..., :half]], axis=-1)


def _pad_lanes(w):
    pad = LANES - w.shape[-1]
    return jnp.concatenate([w, jnp.zeros(w.shape[:-1] + (pad,), w.dtype)], axis=-1)


def kernel(x, c, positions, mod_w, mod_b, norm1_g, norm2_g, conv_pw1_w, conv_pw1_b, conv_dw_w, conv_dw_b, conv_ln_g, conv_ln_b, conv_pw2_w, conv_pw2_b, kv_norm_g, w_dkv, ckv_norm_g, w_uk, w_uv, w_dq, cq_norm_g, w_uq, w_o, router_w, router_b, exp_w_gu, exp_b_gu, exp_w_dn, exp_b_dn, final_g):
    b, s, d = x.shape
    n_heads, nope = w_uk.shape[1], w_uk.shape[2]
    r_kv = ckv_norm_g.shape[0]
    rope = w_dkv.shape[1] - r_kv
    vdim = w_uv.shape[2]
    n_exp = router_w.shape[2]
    assert nope == LANES and vdim == LANES and rope <= LANES and d % LANES == 0
    ts = min(512, s)
    tq = min(1024, s)

    mod = _modulation(c, mod_w, mod_b)
    mods = [[m.reshape(b, 1, d) for m in jnp.split(mod[l], 6, axis=-1)] for l in range(2)]
    row = lambda v: v.reshape(1, -1)

    half = rope // 2
    inv = jnp.exp(-(2.0 * math.log(ROPE_THETA) / rope) * jnp.arange(half, dtype=F32))
    ang = positions.astype(F32)[..., None] * inv
    cos, sin = jnp.cos(ang), jnp.sin(ang)
    cc = _pad_lanes(jnp.concatenate([cos, cos], axis=-1))
    ss = _pad_lanes(jnp.concatenate([-sin, sin], axis=-1))

    sh1, sc1, gt1, sh2, sc2, gt2 = mods[0]
    x1, hn2, idxt, route, cnt = _mix0(
        x, row(norm1_g[0]), sh1, sc1, gt1,
        conv_pw1_w[0].astype(BF16), row(conv_pw1_b[0]), conv_dw_w[0], row(conv_dw_b[0]),
        row(conv_ln_g[0]), row(conv_ln_b[0]), conv_pw2_w[0].astype(BF16), row(conv_pw2_b[0]),
        row(norm2_g[0]), sh2, sc2, router_w[0].T, router_b[0].reshape(n_exp, 1), ts)
    cnt = cnt[:, :ts // MOE_TILE].reshape(-1, n_exp)
    x2 = _moe(x1, hn2, idxt, route, cnt, gt2, row(final_g),
              0, exp_w_gu, exp_b_gu, exp_w_dn, exp_b_dn, False)

    sh1, sc1, gt1, sh2, sc2, gt2 = mods[1]
    wdkv_rope = w_dkv[:, r_kv:]
    wdkv_ext = jnp.concatenate(
        [w_dkv[:, :r_kv], _pad_lanes(wdkv_rope), _pad_lanes(_swap_halves(wdkv_rope))], axis=-1)
    wuq = w_uq[0]
    r_q = wuq.shape[0]
    wuq_rope = wuq[:, :, nope:]
    wuq_ext = jnp.concatenate([
        wuq[:, :, :nope].reshape(r_q, n_heads * LANES),
        wuq_rope.reshape(r_q, n_heads * rope),
        _swap_halves(wuq_rope).reshape(r_q, n_heads * rope)], axis=-1)
    scale = float((nope + rope) ** -0.5 * math.log2(math.e))
    q, k, v = _qkv(
        x2, cc, ss, row(kv_norm_g), wdkv_ext.astype(BF16), row(ckv_norm_g),
        w_uk.reshape(r_kv, n_heads * nope).astype(BF16), w_uv.reshape(r_kv, n_heads * vdim).T.astype(BF16),
        row(norm1_g[1]), sh1, sc1, w_dq[0].astype(BF16), row(cq_norm_g[0]), wuq_ext.astype(BF16),
        n_heads, rope, scale, ts)
    o = _attention(q, k, v, tq)

    x3, hn2, idxt, route, cnt = _mix1(
        x2, o, gt1, w_o[0].astype(BF16), row(norm2_g[1]), sh2, sc2,
        router_w[1].T, router_b[1].reshape(n_exp, 1), ts)
    cnt = cnt[:, :ts // MOE_TILE].reshape(-1, n_exp)
    return _moe(x3, hn2, idxt, route, cnt, gt2, row(final_g),
                1, exp_w_gu, exp_b_gu, exp_w_dn, exp_b_dn, True)
```

```python
import functools
import math

import jax
import jax.numpy as jnp
from jax import lax
from jax.experimental import pallas as pl
from jax.experimental.pallas import tpu as pltpu

CHUNK = 64
TOP_K = 4
ROPE_THETA = 10000.0
SWIGLU_ALPHA = 1.702
SWIGLU_LIMIT = 7.0
EPS = 1e-6

LANES = 128
SUBLANES = 8
VMEM_LIMIT_BYTES = 56 * 1024 * 1024

ROW_ALIGN = SUBLANES
MOE_TILE = 256
EXPERT_ROWS = 512
CONV_HALO = 32
V_ONES_ROWS = 2 * SUBLANES
ATTN_UNROLL = 4
ATTN_MAX_SLACK = 60.0

F32 = jnp.float32
BF16 = jnp.bfloat16
I32 = jnp.int32


def _cparams(sem):
    return pltpu.CompilerParams(dimension_semantics=sem, vmem_limit_bytes=VMEM_LIMIT_BYTES)


def _rms(x, g):
    return x * lax.rsqrt(jnp.mean(x * x, axis=-1, keepdims=True) + EPS) * g


def _round_up(a, m):
    return (a + m - 1) // m * m


def _mod_kernel(c_ref, w_ref, b_ref, o_ref):
    c = c_ref[...]
    ca = c * jax.nn.sigmoid(c)
    o_ref[...] = jnp.dot(ca, w_ref[...], preferred_element_type=F32) + b_ref[...]


def _modulation(c, mod_w, mod_b):
    depth, d, d6 = mod_w.shape
    b = c.shape[0]
    bp = _round_up(b, SUBLANES)
    cp = jnp.zeros((bp, d), F32).at[:b].set(c)
    tn = d6 // 4
    out = pl.pallas_call(
        _mod_kernel,
        out_shape=jax.ShapeDtypeStruct((depth, bp, d6), F32),
        grid=(depth, d6 // tn),
        in_specs=[
            pl.BlockSpec((bp, d), lambda l, j: (0, 0)),
            pl.BlockSpec((None, d, tn), lambda l, j: (l, 0, j)),
            pl.BlockSpec((None, 1, tn), lambda l, j: (l, 0, j)),
        ],
        out_specs=pl.BlockSpec((None, bp, tn), lambda l, j: (l, 0, j)),
        compiler_params=_cparams(("arbitrary", "arbitrary")),
        name="adaln_modulation",
    )(cp, mod_w, mod_b.reshape(depth, 1, d6))
    return out[:, :b]


def _pre_moe(x1, g2, sh2, sc2, rwt, rb, earlier, hn_ref, destt_ref, route_ref, cnt_ref):
    ts = x1.shape[0]
    n_exp = rwt.shape[0]
    hn = _rms(x1, g2) * (1.0 + sc2) + sh2
    hn_ref[...] = hn.astype(BF16)
    def split(v):
        hi = v.astype(BF16)
        return hi, (v - hi.astype(F32)).astype(BF16)
    nt = lambda a, b: lax.dot_general(a, b, (((1,), (1,)), ((), ())), preferred_element_type=F32)
    w_hi, w_lo = split(rwt)
    h_hi, h_lo = split(hn)
    logits = nt(w_hi, h_hi) + (nt(w_hi, h_lo) + nt(w_lo, h_hi)) + rb
    e_iota = lax.broadcasted_iota(I32, (n_exp, ts), 0)
    vals, idxs = [], []
    cur = logits
    for _ in range(TOP_K):
        m = jnp.max(cur, axis=0, keepdims=True)
        i = jnp.min(jnp.where(cur == m, e_iota, n_exp), axis=0, keepdims=True)
        vals.append(m)
        idxs.append(i)
        cur = jnp.where(e_iota == i, -jnp.inf, cur)
    exps = [jnp.exp(v - vals[0]) for v in vals]
    den = exps[0]
    for e in exps[1:]:
        den = den + e
    gates = [e / den for e in exps]
    hits = [e_iota == i for i in idxs]
    onehot = jnp.zeros((n_exp, ts), F32)
    for h in hits:
        onehot = onehot + h.astype(F32)
    rank = jnp.dot(onehot.astype(BF16), earlier, preferred_element_type=F32)
    lower = (lax.broadcasted_iota(I32, (n_exp, n_exp), 1)
             < lax.broadcasted_iota(I32, (n_exp, n_exp), 0)).astype(BF16)
    dests = [[] for _ in range(TOP_K)]
    for t0 in range(0, ts, MOE_TILE):
        cols = slice(t0, t0 + MOE_TILE)
        cnt = jnp.sum(onehot[:, cols], axis=1, keepdims=True)
        cnt8 = jnp.ceil(cnt / ROW_ALIGN) * ROW_ALIGN
        off = jnp.dot(lower, jnp.broadcast_to(cnt8, (n_exp, LANES)).astype(BF16),
                      preferred_element_type=F32)[:, 0:1]
        base = off + rank[:, cols]
        for k, h in enumerate(hits):
            dests[k].append(jnp.sum(jnp.where(h[:, cols], base, 0.0), axis=0, keepdims=True))
    dests = [jnp.concatenate(parts, axis=1) for parts in dests]
    destt_ref[...] = jnp.concatenate(dests, axis=0).astype(I32)
    rows = jnp.concatenate(dests + gates + [jnp.zeros((LANES - 2 * TOP_K, ts), F32)], axis=0)
    route_ref[...] = rows.T
    sel = (lax.broadcasted_iota(I32, (SUBLANES, ts), 1) // MOE_TILE
           == lax.broadcasted_iota(I32, (SUBLANES, ts), 0)).astype(BF16)
    cnt = lax.dot_general(sel, onehot.astype(BF16), (((1,), (1,)), ((), ())),
                          preferred_element_type=F32)
    cnt_ref[...] = cnt.astype(I32)


def _earlier_in_tile(ts):
    t = jnp.arange(ts, dtype=I32)
    same_tile = (t[:, None] // MOE_TILE) == (t[None, :] // MOE_TILE)
    return ((t[:, None] < t[None, :]) & same_tile).astype(BF16)


def _pre_moe_specs(b, s, d, ts, n_exp):
    ns = s // ts
    out_shape = [
        jax.ShapeDtypeStruct((b, s, d), F32),
        jax.ShapeDtypeStruct((b, s, d), BF16),
        jax.ShapeDtypeStruct((TOP_K, b * s), I32),
        jax.ShapeDtypeStruct((b * s, LANES), F32),
        jax.ShapeDtypeStruct((b * ns, SUBLANES, n_exp), I32),
    ]
    out_specs = [
        pl.BlockSpec((None, ts, d), lambda i, j: (i, j, 0)),
        pl.BlockSpec((None, ts, d), lambda i, j: (i, j, 0)),
        pl.BlockSpec((TOP_K, ts), lambda i, j: (0, i * ns + j)),
        pl.BlockSpec((ts, LANES), lambda i, j: (i * ns + j, 0)),
        pl.BlockSpec((None, SUBLANES, n_exp), lambda i, j: (i * ns + j, 0, 0)),
    ]
    return out_shape, out_specs


def _vec_spec(d):
    return pl.BlockSpec((1, d), lambda i, j: (0, 0))


def _bvec_spec(d):
    return pl.BlockSpec((None, 1, d), lambda i, j: (i, 0, 0))


def _mix0_kernel(x_ref, n1g_ref, sh1_ref, sc1_ref, gt1_ref,
                 pw1w_ref, pw1b_ref, dww_ref, dwb_ref, lng_ref, lnb_ref,
                 pw2w_ref, pw2b_ref, n2g_ref, sh2_ref, sc2_ref, rwt_ref, rb_ref, earlier_ref,
                 x1_ref, hn_ref, destt_ref, route_ref, cnt_ref, buf_ref, conv_ref, shift_ref):
    ts, d = x_ref.shape
    width = dww_ref.shape[0]
    x = x_ref[...]
    hn = _rms(x, n1g_ref[...]) * (1.0 + sc1_ref[...]) + sh1_ref[...]
    hb = hn.astype(BF16)
    a = jnp.dot(hb, pw1w_ref[:, :d], preferred_element_type=F32) + pw1b_ref[:, :d]
    g = jnp.dot(hb, pw1w_ref[:, d:], preferred_element_type=F32) + pw1b_ref[:, d:]
    glu = a * jax.nn.sigmoid(g)

    @pl.when(pl.program_id(1) == 0)
    def _():
        buf_ref[0:CONV_HALO, :] = jnp.zeros((CONV_HALO, d), F32)

    buf_ref[CONV_HALO:, :] = glu
    base = CONV_HALO - (width - 1)
    rc = 32
    lc = min(512, d)
    sh_rows = shift_ref.shape[1]
    sub = lax.broadcasted_iota(I32, (SUBLANES, lc), 0)
    for c0 in range(0, d, lc):
        for r in range(1, SUBLANES):
            from_this = sub < SUBLANES - r
            cur = pltpu.roll(buf_ref[0:SUBLANES, c0:c0 + lc], SUBLANES - r, axis=0)
            for j0 in range(0, sh_rows, SUBLANES):
                nxt = pltpu.roll(buf_ref[j0 + SUBLANES:j0 + 2 * SUBLANES, c0:c0 + lc],
                                 SUBLANES - r, axis=0)
                shift_ref[r - 1, j0:j0 + SUBLANES, :] = jnp.where(from_this, cur, nxt)
                cur = nxt
        for r0 in range(0, ts, rc):
            acc = jnp.zeros((rc, lc), F32)
            for k in range(width):
                q8, r = divmod(base + k, SUBLANES)
                lo = q8 * SUBLANES + r0
                if r == 0:
                    win = buf_ref[lo:lo + rc, c0:c0 + lc]
                else:
                    win = shift_ref[r - 1, lo:lo + rc, :]
                acc = acc + dww_ref[k:k + 1, c0:c0 + lc] * win
            conv_ref[r0:r0 + rc, c0:c0 + lc] = acc
    buf_ref[0:CONV_HALO, :] = buf_ref[ts:ts + CONV_HALO, :]
    u = conv_ref[...] + dwb_ref[...]
    mu = jnp.mean(u, axis=-1, keepdims=True)
    dlt = u - mu
    var = jnp.mean(dlt * dlt, axis=-1, keepdims=True)
    u = dlt * lax.rsqrt(var + EPS) * lng_ref[...] + lnb_ref[...]
    u = u * jax.nn.sigmoid(u)
    y = jnp.dot(u.astype(BF16), pw2w_ref[...], preferred_element_type=F32) + pw2b_ref[...]
    x1 = x + gt1_ref[...] * y
    x1_ref[...] = x1
    _pre_moe(x1, n2g_ref[...], sh2_ref[...], sc2_ref[...], rwt_ref[...], rb_ref[...],
             earlier_ref[...], hn_ref, destt_ref, route_ref, cnt_ref)


def _mix0(x, n1g, sh1, sc1, gt1, pw1w, pw1b, dww, dwb, lng, lnb, pw2w, pw2b,
          n2g, sh2, sc2, rwt, rb, ts):
    b, s, d = x.shape
    n_exp = rwt.shape[0]
    width = dww.shape[0]
    assert width - 1 <= CONV_HALO and ts % MOE_TILE == 0 and ts // MOE_TILE <= SUBLANES
    out_shape, out_specs = _pre_moe_specs(b, s, d, ts, n_exp)
    full = lambda shp: pl.BlockSpec(shp, lambda i, j: (0,) * len(shp))
    return pl.pallas_call(
        _mix0_kernel,
        out_shape=out_shape,
        grid=(b, s // ts),
        in_specs=[
            pl.BlockSpec((None, ts, d), lambda i, j: (i, j, 0)),
            _vec_spec(d), _bvec_spec(d), _bvec_spec(d), _bvec_spec(d),
            full((d, 2 * d)), full((1, 2 * d)), full((width, d)), full((1, d)),
            full((1, d)), full((1, d)), full((d, d)), full((1, d)),
            _vec_spec(d), _bvec_spec(d), _bvec_spec(d),
            full((n_exp, d)), full((n_exp, 1)), full((ts, ts)),
        ],
        out_specs=out_specs,
        scratch_shapes=[
            pltpu.VMEM((CONV_HALO + ts, d), F32),
            pltpu.VMEM((ts, d), F32),
            pltpu.VMEM((SUBLANES - 1, CONV_HALO + ts - SUBLANES, min(512, d)), F32),
        ],
        compiler_params=_cparams(("arbitrary", "arbitrary")),
        name="conformer_mixer",
    )(x, n1g, sh1, sc1, gt1, pw1w, pw1b, dww, dwb, lng, lnb, pw2w, pw2b,
      n2g, sh2, sc2, rwt, rb, _earlier_in_tile(ts))


def _mix1_kernel(x_ref, o_ref, gt1_ref, wo_ref, n2g_ref, sh2_ref, sc2_ref, rwt_ref, rb_ref,
                 earlier_ref, x1_ref, hn_ref, destt_ref, route_ref, cnt_ref):
    y = jnp.dot(o_ref[...], wo_ref[...], preferred_element_type=F32)
    x1 = x_ref[...] + gt1_ref[...] * y
    x1_ref[...] = x1
    _pre_moe(x1, n2g_ref[...], sh2_ref[...], sc2_ref[...], rwt_ref[...], rb_ref[...],
             earlier_ref[...], hn_ref, destt_ref, route_ref, cnt_ref)


def _mix1(x, o, gt1, wo, n2g, sh2, sc2, rwt, rb, ts):
    b, s, d = x.shape
    n_exp = rwt.shape[0]
    do = o.shape[-1]
    assert ts % MOE_TILE == 0 and ts // MOE_TILE <= SUBLANES
    out_shape, out_specs = _pre_moe_specs(b, s, d, ts, n_exp)
    full = lambda shp: pl.BlockSpec(shp, lambda i, j: (0,) * len(shp))
    return pl.pallas_call(
        _mix1_kernel,
        out_shape=out_shape,
        grid=(b, s // ts),
        in_specs=[
            pl.BlockSpec((None, ts, d), lambda i, j: (i, j, 0)),
            pl.BlockSpec((None, ts, do), lambda i, j: (i, j, 0)),
            _bvec_spec(d), full((do, d)),
            _vec_spec(d), _bvec_spec(d), _bvec_spec(d),
            full((n_exp, d)), full((n_exp, 1)), full((ts, ts)),
        ],
        out_specs=out_specs,
        compiler_params=_cparams(("arbitrary", "arbitrary")),
        name="attn_out_mixer",
    )(x, o, gt1, wo, n2g, sh2, sc2, rwt, rb, _earlier_in_tile(ts))


def _moe_dims(n_tok, n_exp):
    n_tiles = n_tok // MOE_TILE
    loc_rows = _round_up(TOP_K * MOE_TILE + n_exp * (ROW_ALIGN - 1), LANES)
    max_rows = TOP_K * n_tok + n_tiles * n_exp * (ROW_ALIGN - 1) + n_exp * (EXPERT_ROWS - ROW_ALIGN)
    max_blocks = -(-max_rows // EXPERT_ROWS)
    return loc_rows, max_blocks


def _routing_tables(cnt, max_blocks):
    n_tiles, n_exp = cnt.shape
    cnt8 = _round_up(cnt, ROW_ALIGN)
    off = jnp.cumsum(cnt8, axis=1) - cnt8
    seg_len = cnt8.sum(axis=0)
    seg_pad = _round_up(seg_len, EXPERT_ROWS)
    seg_end = jnp.cumsum(seg_pad)
    seg_start = seg_end - seg_pad
    run_start = seg_start[None, :] + jnp.cumsum(cnt8, axis=0) - cnt8
    n_blocks = (seg_end[-1] // EXPERT_ROWS).astype(I32)
    blk_row = jnp.arange(max_blocks, dtype=I32) * EXPERT_ROWS
    blk_row = jnp.minimum(blk_row, seg_end[-1] - EXPERT_ROWS)
    blk_exp = jnp.minimum((blk_row[:, None] >= seg_end[None, :]).sum(axis=-1), n_exp - 1).astype(I32)
    blk = jnp.arange(max_blocks, dtype=I32)
    prev_exp = jnp.concatenate([jnp.full((1,), -1, I32), blk_exp[:-1]])
    blk_first = ((blk_exp != prev_exp) & (blk < n_blocks)).astype(I32)
    blk_slot = ((jnp.cumsum(blk_first) - 1) % 2).astype(I32)
    e_ids = jnp.arange(n_exp, dtype=I32)
    later = (e_ids[None, :] > e_ids[:, None]) & (seg_pad[None, :] > 0)
    next_exp = jnp.min(jnp.where(later, e_ids[None, :], n_exp), axis=1)
    next_exp = jnp.where(next_exp < n_exp, next_exp, -1)
    blk_next = jnp.sum(jnp.where(blk_exp[:, None] == e_ids[None, :], next_exp[None, :], 0), axis=1)
    flat = lambda a: a.reshape(-1).astype(I32)
    return dict(
        run_loc=flat(off),
        run_glb=flat(run_start),
        run_len=flat(cnt8),
        rows_used=flat(cnt8.sum(axis=1)),
        tail_glb=flat(seg_start + seg_len),
        tail_len=flat(seg_pad - seg_len),
        blk_exp=blk_exp, blk_first=blk_first, blk_slot=blk_slot, blk_next=flat(blk_next),
        n_blocks=n_blocks.reshape(1))


def _aligned(v):
    return pl.multiple_of(v, ROW_ALIGN)


def _dispatch_kernel(n_exp, loc0_ref, glb0_ref, len_ref, used_ref, tail0_ref, taillen_ref, nblk_ref,
                     x_ref, destt_ref, xs_ref, loc_ref, zero_ref, sem, zsem):
    i = pl.program_id(0)
    n = pl.num_programs(0)
    slot = i % 2
    tile = x_ref.shape[0]
    loc_rows = loc_ref.shape[1]

    def wait_slot(sl, rows):
        rows = _aligned(rows)

        @pl.when(rows > 0)
        def _():
            pltpu.make_async_copy(loc_ref.at[sl, pl.ds(0, rows)], xs_ref.at[pl.ds(0, rows)],
                                  sem.at[sl]).wait()

    @pl.when(i >= 2)
    def _():
        wait_slot(slot, used_ref[i - 2])

    r_iota = lax.broadcasted_iota(I32, (loc_rows, tile), 0)
    perm = jnp.zeros((loc_rows, tile), F32)
    for k in range(TOP_K):
        perm = jnp.where(r_iota == destt_ref[k:k + 1, :], 1.0, perm)
    loc_ref[slot] = jnp.dot(perm.astype(BF16), x_ref[...], preferred_element_type=F32)

    def issue_run(e, carry):
        rows = _aligned(len_ref[i * n_exp + e])

        @pl.when(rows > 0)
        def _():
            pltpu.make_async_copy(
                loc_ref.at[slot, pl.ds(_aligned(loc0_ref[i * n_exp + e]), rows)],
                xs_ref.at[pl.ds(_aligned(glb0_ref[i * n_exp + e]), rows)],
                sem.at[slot]).start()
        return carry
    lax.fori_loop(0, n_exp, issue_run, 0)

    @pl.when(i == n - 1)
    def _():
        zero_ref[...] = jnp.zeros(zero_ref.shape, F32)
        max_blocks = xs_ref.shape[0] // EXPERT_ROWS

        def zero_tail(e, total):
            rows = _aligned(taillen_ref[e])

            @pl.when(rows > 0)
            def _():
                pltpu.make_async_copy(zero_ref.at[pl.ds(0, rows)],
                                      xs_ref.at[pl.ds(_aligned(tail0_ref[e]), rows)],
                                      zsem.at[0]).start()
            return total + rows
        tail_rows = lax.fori_loop(0, n_exp, zero_tail, jnp.int32(0))

        def zero_block(blk, carry):
            pltpu.make_async_copy(
                zero_ref,
                xs_ref.at[pl.ds(pl.multiple_of(blk * EXPERT_ROWS, EXPERT_ROWS), EXPERT_ROWS)],
                zsem.at[1]).start()
            return carry
        lax.fori_loop(nblk_ref[0], max_blocks, zero_block, 0)

        @pl.when(i >= 1)
        def _():
            wait_slot(1 - slot, used_ref[i - 1])
        wait_slot(slot, used_ref[i])

        def wait_rows(rows, s):
            rows = _aligned(rows)

            @pl.when(rows > 0)
            def _():
                pltpu.make_async_copy(xs_ref.at[pl.ds(0, rows)], xs_ref.at[pl.ds(0, rows)],
                                      zsem.at[s]).wait()
        wait_rows(tail_rows, 0)
        wait_rows((max_blocks - nblk_ref[0]) * EXPERT_ROWS, 1)


def _dispatch(hn2, idxt, tabs, n_exp, loc_rows, max_blocks):
    n_tok, d = hn2.shape
    n_tiles = n_tok // MOE_TILE
    return pl.pallas_call(
        functools.partial(_dispatch_kernel, n_exp),
        out_shape=jax.ShapeDtypeStruct((max_blocks * EXPERT_ROWS, d), F32),
        grid_spec=pltpu.PrefetchScalarGridSpec(
            num_scalar_prefetch=7,
            grid=(n_tiles,),
            in_specs=[
                pl.BlockSpec((MOE_TILE, d), lambda i, *_: (i, 0)),
                pl.BlockSpec((TOP_K, MOE_TILE), lambda i, *_: (0, i)),
            ],
            out_specs=pl.BlockSpec(memory_space=pl.ANY),
            scratch_shapes=[
                pltpu.VMEM((2, loc_rows, d), F32),
                pltpu.VMEM((EXPERT_ROWS, d), F32),
                pltpu.SemaphoreType.DMA((2,)),
                pltpu.SemaphoreType.DMA((2,)),
            ],
        ),
        compiler_params=_cparams(("arbitrary",)),
        name="moe_dispatch",
    )(tabs["run_loc"], tabs["run_glb"], tabs["run_len"], tabs["rows_used"],
      tabs["tail_glb"], tabs["tail_len"], tabs["n_blocks"], hn2, idxt)


def _expert_kernel(layer, be_ref, first_ref, slot_ref, next_ref, nb_ref,
                   x_ref, wgu_hbm, bgu_ref, wdn_hbm, bdn_ref, y_ref,
                   wgu_f32, wdn_f32, wgu_bf, wdn_bf, sem):
    de = wdn_bf.shape[0]
    b = pl.program_id(0)

    def weight_copies(e, s):
        return (pltpu.make_async_copy(wgu_hbm.at[layer, e], wgu_f32.at[s], sem.at[s, 0]),
                pltpu.make_async_copy(wdn_hbm.at[layer, e], wdn_f32.at[s], sem.at[s, 1]))

    @pl.when(b == 0)
    def _():
        for cp in weight_copies(be_ref[0], 0):
            cp.start()

    @pl.when(first_ref[b] == 1)
    def _():
        s = slot_ref[b]
        for cp in weight_copies(be_ref[b], s):
            cp.wait()

        @pl.when(next_ref[b] >= 0)
        def _():
            for cp in weight_copies(next_ref[b], 1 - s):
                cp.start()
        cw = 512
        for c0 in range(0, wgu_bf.shape[1], cw):
            wgu_bf[:, c0:c0 + cw] = wgu_f32[s, :, c0:c0 + cw].astype(BF16)
        for c0 in range(0, wdn_bf.shape[1], cw):
            wdn_bf[:, c0:c0 + cw] = wdn_f32[s, :, c0:c0 + cw].astype(BF16)

    @pl.when(b < nb_ref[0])
    def _():
        xb = x_ref[...].astype(BF16)
        g = jnp.dot(xb, wgu_bf[:, :de], preferred_element_type=F32) + bgu_ref[:, :de]
        u = jnp.dot(xb, wgu_bf[:, de:], preferred_element_type=F32) + bgu_ref[:, de:]
        g = jnp.minimum(g, SWIGLU_LIMIT)
        u = jnp.clip(u, -SWIGLU_LIMIT, SWIGLU_LIMIT)
        a = (u + 1.0) * g * jax.nn.sigmoid(SWIGLU_ALPHA * g)
        y_ref[...] = jnp.dot(a.astype(BF16), wdn_bf[...], preferred_element_type=F32) + bdn_ref[...]

    @pl.when(pl.program_id(0) >= nb_ref[0])
    def _():
        y_ref[...] = jnp.zeros(y_ref.shape, y_ref.dtype)


def _experts(xs, tabs, layer, wgu, bgu, wdn, bdn):
    rows, d = xs.shape
    depth, n_exp, _, de2 = wgu.shape
    de = de2 // 2
    max_blocks = rows // EXPERT_ROWS
    row_map = lambda b, be, fi, sl, nx, nb: (jnp.minimum(b, nb[0] - 1), 0)
    exp_map = lambda b, be, fi, sl, nx, nb: (layer, be[b], 0, 0)
    return pl.pallas_call(
        functools.partial(_expert_kernel, layer),
        out_shape=jax.ShapeDtypeStruct((rows, d), F32),
        grid_spec=pltpu.PrefetchScalarGridSpec(
            num_scalar_prefetch=5,
            grid=(max_blocks,),
            in_specs=[
                pl.BlockSpec((EXPERT_ROWS, d), row_map),
                pl.BlockSpec(memory_space=pl.ANY),
                pl.BlockSpec((None, None, 1, de2), exp_map),
                pl.BlockSpec(memory_space=pl.ANY),
                pl.BlockSpec((None, None, 1, d), exp_map),
            ],
            out_specs=pl.BlockSpec((EXPERT_ROWS, d), lambda b, *_: (b, 0)),
            scratch_shapes=[
                pltpu.VMEM((2, d, de2), F32), pltpu.VMEM((2, de, d), F32),
                pltpu.VMEM((d, de2), BF16), pltpu.VMEM((de, d), BF16),
                pltpu.SemaphoreType.DMA((2, 2)),
            ],
        ),
        compiler_params=_cparams(("arbitrary",)),
        name="moe_experts",
    )(tabs["blk_exp"], tabs["blk_first"], tabs["blk_slot"], tabs["blk_next"], tabs["n_blocks"],
      xs, wgu, bgu.reshape(depth, n_exp, 1, de2), wdn, bdn.reshape(depth, n_exp, 1, d))


def _combine_kernel(final_norm, n_exp, loc0_ref, glb0_ref, len_ref, used_ref,
                    route_ref, x1_ref, g2_ref, fg_ref, ys_ref, out_ref, loc_ref, sem):
    i = pl.program_id(0)
    n = pl.num_programs(0)
    slot = i % 2
    tile = x1_ref.shape[0]
    loc_rows = loc_ref.shape[1]

    def issue_tile(t, sl):
        def issue_run(e, carry):
            rows = _aligned(len_ref[t * n_exp + e])

            @pl.when(rows > 0)
            def _():
                pltpu.make_async_copy(
                    ys_ref.at[pl.ds(_aligned(glb0_ref[t * n_exp + e]), rows)],
                    loc_ref.at[sl, pl.ds(_aligned(loc0_ref[t * n_exp + e]), rows)],
                    sem.at[sl]).start()
            return carry
        lax.fori_loop(0, n_exp, issue_run, 0)

    @pl.when(i == 0)
    def _():
        issue_tile(0, 0)

    @pl.when(i + 1 < n)
    def _():
        issue_tile(i + 1, 1 - slot)

    route = route_ref[...]
    r_iota = lax.broadcasted_iota(I32, (tile, loc_rows), 1)
    comb = jnp.zeros((tile, loc_rows), F32)
    for k in range(TOP_K):
        dest = route[:, k:k + 1].astype(I32)
        comb = jnp.where(r_iota == dest, route[:, TOP_K + k:TOP_K + k + 1], comb)

    used_rows = _aligned(used_ref[i])

    @pl.when(used_rows > 0)
    def _():
        pltpu.make_async_copy(ys_ref.at[pl.ds(0, used_rows)],
                              loc_ref.at[slot, pl.ds(0, used_rows)], sem.at[slot]).wait()

    row_ok = lax.broadcasted_iota(I32, (loc_rows, 1), 0) < used_rows
    yl = jnp.where(row_ok, loc_ref[slot], 0.0).astype(BF16)
    moe = jnp.dot(comb.astype(BF16), yl, preferred_element_type=F32)
    out = x1_ref[...] + g2_ref[...] * moe
    if final_norm:
        out = _rms(out, fg_ref[...])
    out_ref[...] = out


def _combine(ys, route, x1, gate2, final_g, tabs, n_exp, loc_rows, final_norm):
    b, s, d = x1.shape
    n_tok = b * s
    n_tiles = n_tok // MOE_TILE
    tiles_per_seq = s // MOE_TILE
    out = pl.pallas_call(
        functools.partial(_combine_kernel, final_norm, n_exp),
        out_shape=jax.ShapeDtypeStruct((n_tok, d), F32),
        grid_spec=pltpu.PrefetchScalarGridSpec(
            num_scalar_prefetch=4,
            grid=(n_tiles,),
            in_specs=[
                pl.BlockSpec((MOE_TILE, LANES), lambda i, *_: (i, 0)),
                pl.BlockSpec((MOE_TILE, d), lambda i, *_: (i, 0)),
                pl.BlockSpec((None, 1, d), lambda i, *_: (i // tiles_per_seq, 0, 0)),
                pl.BlockSpec((1, d), lambda i, *_: (0, 0)),
                pl.BlockSpec(memory_space=pl.ANY),
            ],
            out_specs=pl.BlockSpec((MOE_TILE, d), lambda i, *_: (i, 0)),
            scratch_shapes=[
                pltpu.VMEM((2, loc_rows, d), F32),
                pltpu.SemaphoreType.DMA((2,)),
            ],
        ),
        compiler_params=_cparams(("arbitrary",)),
        name="moe_combine",
    )(tabs["run_loc"], tabs["run_glb"], tabs["run_len"], tabs["rows_used"],
      route, x1.reshape(n_tok, d), gate2, final_g, ys)
    return out.reshape(b, s, d)


def _moe(x1, hn2, idxt, route, cnt, gate2, final_g, layer, wgu, bgu, wdn, bdn, final_norm):
    b, s, d = x1.shape
    n_tok = b * s
    n_exp = wgu.shape[1]
    loc_rows, max_blocks = _moe_dims(n_tok, n_exp)
    tabs = _routing_tables(cnt, max_blocks)
    xs = _dispatch(hn2.reshape(n_tok, d), idxt, tabs, n_exp, loc_rows, max_blocks)
    ys = _experts(xs, tabs, layer, wgu, bgu, wdn, bdn)
    return _combine(ys, route, x1, gate2, final_g, tabs, n_exp, loc_rows, final_norm)


def _qkv_kernel(n_heads, rope, scale, x_ref, trig_ref, kvg_ref, wdkv_ref, ckvg_ref, wuk_ref, wuv_ref,
                n1g_ref, sh1_ref, sc1_ref, wdq_ref, cqg_ref, wuq_ref,
                q_ref, k_ref, v_ref):
    x = x_ref[...]
    half = rope // 2
    trig = trig_ref[...]
    lane = lax.broadcasted_iota(I32, trig.shape, 1)
    cc = jnp.where(lane < half, trig,
                   jnp.where(lane < rope, pltpu.roll(trig, half, axis=1), 0.0))
    ss = jnp.where(lane < half, -pltpu.roll(trig, LANES - half, axis=1),
                   jnp.where(lane < rope, trig, 0.0))
    r_kv = ckvg_ref.shape[1]
    hk = _rms(x, kvg_ref[...]).astype(BF16)
    lat = jnp.dot(hk, wdkv_ref[...], preferred_element_type=F32)
    ckv = _rms(lat[:, :r_kv], ckvg_ref[...]).astype(BF16)
    krot = lat[:, r_kv:r_kv + LANES] * cc + lat[:, r_kv + LANES:r_kv + 2 * LANES] * ss
    kn = jnp.dot(ckv, wuk_ref[...], preferred_element_type=F32)
    vt = lax.dot_general(wuv_ref[...], ckv, (((1,), (1,)), ((), ())), preferred_element_type=F32)
    hq = (_rms(x, n1g_ref[...]) * (1.0 + sc1_ref[...]) + sh1_ref[...]).astype(BF16)
    cq = _rms(jnp.dot(hq, wdq_ref[...], preferred_element_type=F32), cqg_ref[...]).astype(BF16)
    qq = jnp.dot(cq, wuq_ref[...], preferred_element_type=F32) * scale
    hd = n_heads * LANES
    ts = x.shape[0]
    ones_rows = (lax.broadcasted_iota(I32, (V_ONES_ROWS, ts), 0) == 0).astype(BF16)
    for h in range(n_heads):
        sl = slice(h * LANES, (h + 1) * LANES)
        k_ref[h, :, 0:LANES] = kn[:, sl].astype(BF16)
        k_ref[h, :, LANES:2 * LANES] = krot.astype(BF16)
        v_ref[h, 0:LANES, :] = vt[h * LANES:(h + 1) * LANES, :].astype(BF16)
        v_ref[h, LANES:LANES + V_ONES_ROWS, :] = ones_rows
        q_ref[h, :, 0:LANES] = qq[:, sl].astype(BF16)
    per_tile = LANES // rope
    cc_rep, ss_rep = cc, ss
    for u in range(1, per_tile):
        cc_rep = cc_rep + pltpu.roll(cc, u * rope, axis=1)
        ss_rep = ss_rep + pltpu.roll(ss, u * rope, axis=1)
    in_head = lax.broadcasted_iota(I32, (ts, LANES), 1) < rope
    n_rope = n_heads * rope
    for j in range(n_heads // per_tile):
        cols = slice(hd + j * LANES, hd + (j + 1) * LANES)
        swapped = slice(hd + n_rope + j * LANES, hd + n_rope + (j + 1) * LANES)
        rot = qq[:, cols] * cc_rep + qq[:, swapped] * ss_rep
        for u in range(per_tile):
            piece = rot if u == 0 else pltpu.roll(rot, LANES - u * rope, axis=1)
            q_ref[j * per_tile + u, :, LANES:2 * LANES] = jnp.where(in_head, piece, 0.0).astype(BF16)


def _qkv(x, trig, kvg, wdkv, ckvg, wuk, wuv, n1g, sh1, sc1, wdq, cqg, wuq, n_heads, rope, scale, ts):
    b, s, d = x.shape
    assert LANES % rope == 0 and n_heads % (LANES // rope) == 0
    full = lambda a: pl.BlockSpec(a.shape, lambda i, j: (0,) * a.ndim)
    hspec = lambda w: pl.BlockSpec((None, n_heads, ts, w), lambda i, j: (i, 0, j, 0))
    return pl.pallas_call(
        functools.partial(_qkv_kernel, n_heads, rope, scale),
        out_shape=[
            jax.ShapeDtypeStruct((b, n_heads, s, 2 * LANES), BF16),
            jax.ShapeDtypeStruct((b, n_heads, s, 2 * LANES), BF16),
            jax.ShapeDtypeStruct((b, n_heads, s // ts, LANES + V_ONES_ROWS, ts), BF16),
        ],
        grid=(b, s // ts),
        in_specs=[
            pl.BlockSpec((None, ts, d), lambda i, j: (i, j, 0)),
            pl.BlockSpec((None, ts, LANES), lambda i, j: (i, j, 0)),
            full(kvg), full(wdkv), full(ckvg), full(wuk), full(wuv),
            full(n1g), _bvec_spec(d), _bvec_spec(d), full(wdq), full(cqg), full(wuq),
        ],
        out_specs=[hspec(2 * LANES), hspec(2 * LANES),
                   pl.BlockSpec((None, n_heads, None, LANES + V_ONES_ROWS, ts),
                                lambda i, j: (i, 0, j, 0, 0))],
        compiler_params=_cparams(("arbitrary", "arbitrary")),
        name="mla_qkv",
    )(x, trig, kvg, wdkv, ckvg, wuk, wuv, n1g, sh1, sc1, wdq, cqg, wuq)


def _attn_kernel(q_ref, k_ref, vt_ref, o_ref, m_ref, acc_ref):
    tq = q_ref.shape[0]
    tk = vt_ref.shape[2]
    ratio = tq // tk
    dv = o_ref.shape[1]
    qi = pl.program_id(2)
    m_ref[...] = jnp.full(m_ref.shape, -jnp.inf, F32)
    acc_ref[...] = jnp.zeros(acc_ref.shape, F32)

    def scores(ki, cols, on_diagonal):
        start = pl.multiple_of(ki * tk, tk)
        k = k_ref[pl.ds(start, tk), :]
        st = lax.dot_general(k, q_ref[cols, :], (((1,), (1,)), ((), ())),
                             preferred_element_type=F32)
        if on_diagonal:
            kc = lax.broadcasted_iota(I32, st.shape, 0) // CHUNK
            qc = lax.broadcasted_iota(I32, st.shape, 1) // CHUNK
            st = jnp.where(kc <= qc, st, -jnp.inf)
        return st

    def update(ki, st, m_old, acc_old):
        m_new = jnp.maximum(m_old, jnp.max(st, axis=0, keepdims=True))
        p = jnp.exp2(st - m_new).astype(BF16)
        alpha = jnp.exp2(m_old - m_new)
        acc_new = alpha * acc_old + jnp.dot(vt_ref[ki], p, preferred_element_type=F32)
        return m_new, acc_new

    def step(groups):
        sts = [[scores(ki, cols, on_diagonal) for ki, on_diagonal in tiles]
               for cols, tiles in groups]
        for (cols, tiles), group_sts in zip(groups, sts):
            state = (m_ref[:, cols], acc_ref[:, cols])
            for (ki, _), st in zip(tiles, group_sts):
                state = update(ki, st, *state)
            m_ref[:, cols], acc_ref[:, cols] = state

    n_below = qi * ratio
    everything = slice(0, tq)
    step([(slice(g * tk, (g + 1) * tk), [(n_below + a, a == g) for a in range(g + 1)])
          for g in range(ratio)])

    def fast_step(first, count):
        m = m_ref[...]
        pv = None
        top = None
        for t in range(count):
            st = scores(first + t, everything, False)
            tmax = jnp.max(st, axis=0, keepdims=True)
            top = tmax if top is None else jnp.maximum(top, tmax)
            p = jnp.exp2(st - m).astype(BF16)
            part = jnp.dot(vt_ref[first + t], p, preferred_element_type=F32)
            pv = part if pv is None else pv + part
        safe = jnp.max(top - m) <= ATTN_MAX_SLACK

        @pl.when(safe)
        def _():
            acc_ref[...] += pv

        @pl.when(jnp.logical_not(safe))
        def _():
            def redo(t, carry):
                step([(everything, [(first + t, False)])])
                return carry
            lax.fori_loop(0, count, redo, 0)

    def body(j, carry):
        fast_step(ATTN_UNROLL * j, ATTN_UNROLL)
        return carry
    trips = n_below // ATTN_UNROLL
    lax.fori_loop(0, trips, body, 0)

    for rem in range(math.gcd(ratio, ATTN_UNROLL), ATTN_UNROLL, math.gcd(ratio, ATTN_UNROLL)):
        @pl.when(n_below % ATTN_UNROLL == rem)
        def _(rem=rem):
            fast_step(trips * ATTN_UNROLL, rem)

    o_ref[...] = (acc_ref[0:dv, :] / acc_ref[dv:dv + 1, :]).T.astype(o_ref.dtype)


def _attention(q, k, vt, tq):
    b, h, s, dk = q.shape
    nk, dv_ext, tk = vt.shape[2:]
    dv = dv_ext - V_ONES_ROWS
    assert tk % CHUNK == 0 and s % tq == 0 and tq % tk == 0
    return pl.pallas_call(
        _attn_kernel,
        out_shape=jax.ShapeDtypeStruct((b, s, h * dv), BF16),
        grid=(b, h, s // tq),
        in_specs=[
            pl.BlockSpec((None, None, tq, dk), lambda i, j, t: (i, j, t, 0)),
            pl.BlockSpec((None, None, s, dk), lambda i, j, t: (i, j, 0, 0)),
            pl.BlockSpec((None, None, nk, dv_ext, tk), lambda i, j, t: (i, j, 0, 0, 0)),
        ],
        out_specs=pl.BlockSpec((None, tq, dv), lambda i, j, t: (i, t, j)),
        scratch_shapes=[pltpu.VMEM((1, tq), F32), pltpu.VMEM((dv_ext, tq), F32)],
        compiler_params=_cparams(("arbitrary", "arbitrary", "arbitrary")),
        name="mla_attention",
    )(q, k, vt)


def _swap_halves(w):
    half = w.shape[-1] // 2
    return jnp.concatenate([w[..., half:], w[..., :half]], axis=-1)


def _pad_lanes(w):
    pad = LANES - w.shape[-1]
    return jnp.concatenate([w, jnp.zeros(w.shape[:-1] + (pad,), w.dtype)], axis=-1)


def kernel(x, c, positions, mod_w, mod_b, norm1_g, norm2_g, conv_pw1_w, conv_pw1_b, conv_dw_w, conv_dw_b, conv_ln_g, conv_ln_b, conv_pw2_w, conv_pw2_b, kv_norm_g, w_dkv, ckv_norm_g, w_uk, w_uv, w_dq, cq_norm_g, w_uq, w_o, router_w, router_b, exp_w_gu, exp_b_gu, exp_w_dn, exp_b_dn, final_g):
    b, s, d = x.shape
    n_heads, nope = w_uk.shape[1], w_uk.shape[2]
    r_kv = ckv_norm_g.shape[0]
    rope = w_dkv.shape[1] - r_kv
    vdim = w_uv.shape[2]
    n_exp = router_w.shape[2]
    assert nope == LANES and vdim == LANES and rope <= LANES and d % LANES == 0
    ts = min(512, s)
    tq = min(1024, s)

    mod = _modulation(c, mod_w, mod_b)
    mods = [[m.reshape(b, 1, d) for m in jnp.split(mod[l], 6, axis=-1)] for l in range(2)]
    row = lambda v: v.reshape(1, -1)

    half = rope // 2
    inv = jnp.exp(-(2.0 * math.log(ROPE_THETA) / rope) * jnp.arange(half, dtype=F32))
    ang = positions.astype(F32)[..., None] * inv
    trig = _pad_lanes(jnp.concatenate([jnp.cos(ang), jnp.sin(ang)], axis=-1))

    sh1, sc1, gt1, sh2, sc2, gt2 = mods[0]
    x1, hn2, idxt, route, cnt = _mix0(
        x, row(norm1_g[0]), sh1, sc1, gt1,
        conv_pw1_w[0].astype(BF16), row(conv_pw1_b[0]), conv_dw_w[0], row(conv_dw_b[0]),
        row(conv_ln_g[0]), row(conv_ln_b[0]), conv_pw2_w[0].astype(BF16), row(conv_pw2_b[0]),
        row(norm2_g[0]), sh2, sc2, router_w[0].T, router_b[0].reshape(n_exp, 1), ts)
    cnt = cnt[:, :ts // MOE_TILE].reshape(-1, n_exp)
    x2 = _moe(x1, hn2, idxt, route, cnt, gt2, row(final_g),
              0, exp_w_gu, exp_b_gu, exp_w_dn, exp_b_dn, False)

    sh1, sc1, gt1, sh2, sc2, gt2 = mods[1]
    wdkv_rope = w_dkv[:, r_kv:]
    wdkv_ext = jnp.concatenate(
        [w_dkv[:, :r_kv], _pad_lanes(wdkv_rope), _pad_lanes(_swap_halves(wdkv_rope))], axis=-1)
    wuq = w_uq[0]
    r_q = wuq.shape[0]
    wuq_rope = wuq[:, :, nope:]
    wuq_ext = jnp.concatenate([
        wuq[:, :, :nope].reshape(r_q, n_heads * LANES),
        wuq_rope.reshape(r_q, n_heads * rope),
        _swap_halves(wuq_rope).reshape(r_q, n_heads * rope)], axis=-1)
    scale = float((nope + rope) ** -0.5 * math.log2(math.e))
    q, k, v = _qkv(
        x2, trig, row(kv_norm_g), wdkv_ext.astype(BF16), row(ckv_norm_g),
        w_uk.reshape(r_kv, n_heads * nope).astype(BF16), w_uv.reshape(r_kv, n_heads * vdim).T.astype(BF16),
        row(norm1_g[1]), sh1, sc1, w_dq[0].astype(BF16), row(cq_norm_g[0]), wuq_ext.astype(BF16),
        n_heads, rope, scale, ts)
    o = _attention(q, k, v, tq)

    x3, hn2, idxt, route, cnt = _mix1(
        x2, o, gt1, w_o[0].astype(BF16), row(norm2_g[1]), sh2, sc2,
        router_w[1].T, router_b[1].reshape(n_exp, 1), ts)
    cnt = cnt[:, :ts // MOE_TILE].reshape(-1, n_exp)
    return _moe(x3, hn2, idxt, route, cnt, gt2, row(final_g),
                1, exp_w_gu, exp_b_gu, exp_w_dn, exp_b_dn, True)
```

```python
import functools
import math

import jax
import jax.numpy as jnp
from jax import lax
from jax.experimental import pallas as pl
from jax.experimental.pallas import tpu as pltpu

CHUNK = 64
TOP_K = 4
ROPE_THETA = 10000.0
SWIGLU_ALPHA = 1.702
SWIGLU_LIMIT = 7.0
EPS = 1e-6

LANES = 128
SUBLANES = 8
VMEM_LIMIT_BYTES = 56 * 1024 * 1024

ROW_ALIGN = SUBLANES
MOE_TILE = 256
EXPERT_ROWS = 512
CONV_HALO = 32
V_ONES_ROWS = 2 * SUBLANES
ATTN_UNROLL = 4
ATTN_MAX_SLACK = 60.0

F32 = jnp.float32
BF16 = jnp.bfloat16
I32 = jnp.int32


def _cparams(sem):
    return pltpu.CompilerParams(dimension_semantics=sem, vmem_limit_bytes=VMEM_LIMIT_BYTES)


def _rms(x, g):
    return x * lax.rsqrt(jnp.mean(x * x, axis=-1, keepdims=True) + EPS) * g


def _round_up(a, m):
    return (a + m - 1) // m * m


def _mod_kernel(c_ref, w_ref, b_ref, o_ref):
    c = c_ref[...]
    ca = c * jax.nn.sigmoid(c)
    o_ref[...] = jnp.dot(ca, w_ref[...], preferred_element_type=F32) + b_ref[...]


def _modulation(c, mod_w, mod_b):
    depth, d, d6 = mod_w.shape
    b = c.shape[0]
    bp = _round_up(b, SUBLANES)
    cp = jnp.zeros((bp, d), F32).at[:b].set(c)
    tn = d6 // 4
    out = pl.pallas_call(
        _mod_kernel,
        out_shape=jax.ShapeDtypeStruct((depth, bp, d6), F32),
        grid=(depth, d6 // tn),
        in_specs=[
            pl.BlockSpec((bp, d), lambda l, j: (0, 0)),
            pl.BlockSpec((None, d, tn), lambda l, j: (l, 0, j)),
            pl.BlockSpec((None, 1, tn), lambda l, j: (l, 0, j)),
        ],
        out_specs=pl.BlockSpec((None, bp, tn), lambda l, j: (l, 0, j)),
        compiler_params=_cparams(("arbitrary", "arbitrary")),
        name="adaln_modulation",
    )(cp, mod_w, mod_b.reshape(depth, 1, d6))
    return out[:, :b]


def _pre_moe(x1, g2, sh2, sc2, rwt, rb, earlier, hn_ref, destt_ref, route_ref, cnt_ref):
    ts = x1.shape[0]
    n_exp = rwt.shape[0]
    hn = _rms(x1, g2) * (1.0 + sc2) + sh2
    hn_ref[...] = hn.astype(BF16)
    def split(v):
        hi = v.astype(BF16)
        return hi, (v - hi.astype(F32)).astype(BF16)
    nt = lambda a, b: lax.dot_general(a, b, (((1,), (1,)), ((), ())), preferred_element_type=F32)
    w_hi, w_lo = split(rwt)
    h_hi, h_lo = split(hn)
    logits = nt(w_hi, h_hi) + (nt(w_hi, h_lo) + nt(w_lo, h_hi)) + rb
    e_iota = lax.broadcasted_iota(I32, (n_exp, ts), 0)
    vals, idxs = [], []
    cur = logits
    for _ in range(TOP_K):
        m = jnp.max(cur, axis=0, keepdims=True)
        i = jnp.min(jnp.where(cur == m, e_iota, n_exp), axis=0, keepdims=True)
        vals.append(m)
        idxs.append(i)
        cur = jnp.where(e_iota == i, -jnp.inf, cur)
    exps = [jnp.exp(v - vals[0]) for v in vals]
    den = exps[0]
    for e in exps[1:]:
        den = den + e
    gates = [e / den for e in exps]
    hits = [e_iota == i for i in idxs]
    onehot = jnp.zeros((n_exp, ts), F32)
    for h in hits:
        onehot = onehot + h.astype(F32)
    rank = jnp.dot(onehot.astype(BF16), earlier, preferred_element_type=F32)
    lower = (lax.broadcasted_iota(I32, (n_exp, n_exp), 1)
             < lax.broadcasted_iota(I32, (n_exp, n_exp), 0)).astype(BF16)
    dests = [[] for _ in range(TOP_K)]
    for t0 in range(0, ts, MOE_TILE):
        cols = slice(t0, t0 + MOE_TILE)
        cnt = jnp.sum(onehot[:, cols], axis=1, keepdims=True)
        cnt8 = jnp.ceil(cnt / ROW_ALIGN) * ROW_ALIGN
        off = jnp.dot(lower, jnp.broadcast_to(cnt8, (n_exp, LANES)).astype(BF16),
                      preferred_element_type=F32)[:, 0:1]
        base = off + rank[:, cols]
        for k, h in enumerate(hits):
            dests[k].append(jnp.sum(jnp.where(h[:, cols], base, 0.0), axis=0, keepdims=True))
    dests = [jnp.concatenate(parts, axis=1) for parts in dests]
    destt_ref[...] = jnp.concatenate(dests, axis=0).astype(I32)
    rows = jnp.concatenate(dests + gates + [jnp.zeros((LANES - 2 * TOP_K, ts), F32)], axis=0)
    route_ref[...] = rows.T
    sel = (lax.broadcasted_iota(I32, (SUBLANES, ts), 1) // MOE_TILE
           == lax.broadcasted_iota(I32, (SUBLANES, ts), 0)).astype(BF16)
    cnt = lax.dot_general(sel, onehot.astype(BF16), (((1,), (1,)), ((), ())),
                          preferred_element_type=F32)
    cnt_ref[...] = cnt.astype(I32)


def _earlier_in_tile(ts):
    t = jnp.arange(ts, dtype=I32)
    same_tile = (t[:, None] // MOE_TILE) == (t[None, :] // MOE_TILE)
    return ((t[:, None] < t[None, :]) & same_tile).astype(BF16)


def _pre_moe_specs(b, s, d, ts, n_exp):
    ns = s // ts
    out_shape = [
        jax.ShapeDtypeStruct((b, s, d), F32),
        jax.ShapeDtypeStruct((b, s, d), BF16),
        jax.ShapeDtypeStruct((TOP_K, b * s), I32),
        jax.ShapeDtypeStruct((b * s, LANES), F32),
        jax.ShapeDtypeStruct((b * ns, SUBLANES, n_exp), I32),
    ]
    out_specs = [
        pl.BlockSpec((None, ts, d), lambda i, j: (i, j, 0)),
        pl.BlockSpec((None, ts, d), lambda i, j: (i, j, 0)),
        pl.BlockSpec((TOP_K, ts), lambda i, j: (0, i * ns + j)),
        pl.BlockSpec((ts, LANES), lambda i, j: (i * ns + j, 0)),
        pl.BlockSpec((None, SUBLANES, n_exp), lambda i, j: (i * ns + j, 0, 0)),
    ]
    return out_shape, out_specs


def _vec_spec(d):
    return pl.BlockSpec((1, d), lambda i, j: (0, 0))


def _bvec_spec(d):
    return pl.BlockSpec((None, 1, d), lambda i, j: (i, 0, 0))


def _mix0_kernel(x_ref, n1g_ref, sh1_ref, sc1_ref, gt1_ref,
                 pw1w_ref, pw1b_ref, dww_ref, dwb_ref, lng_ref, lnb_ref,
                 pw2w_ref, pw2b_ref, n2g_ref, sh2_ref, sc2_ref, rwt_ref, rb_ref, earlier_ref,
                 x1_ref, hn_ref, destt_ref, route_ref, cnt_ref, buf_ref, conv_ref, shift_ref):
    ts, d = x_ref.shape
    width = dww_ref.shape[0]
    x = x_ref[...]
    hn = _rms(x, n1g_ref[...]) * (1.0 + sc1_ref[...]) + sh1_ref[...]
    hb = hn.astype(BF16)
    a = jnp.dot(hb, pw1w_ref[:, :d], preferred_element_type=F32) + pw1b_ref[:, :d]
    g = jnp.dot(hb, pw1w_ref[:, d:], preferred_element_type=F32) + pw1b_ref[:, d:]
    glu = a * jax.nn.sigmoid(g)

    @pl.when(pl.program_id(1) == 0)
    def _():
        buf_ref[0:CONV_HALO, :] = jnp.zeros((CONV_HALO, d), F32)

    buf_ref[CONV_HALO:, :] = glu
    base = CONV_HALO - (width - 1)
    rc = 32
    lc = min(512, d)
    sh_rows = shift_ref.shape[1]
    sub = lax.broadcasted_iota(I32, (SUBLANES, lc), 0)
    for c0 in range(0, d, lc):
        for r in range(1, SUBLANES):
            from_this = sub < SUBLANES - r
            cur = pltpu.roll(buf_ref[0:SUBLANES, c0:c0 + lc], SUBLANES - r, axis=0)
            for j0 in range(0, sh_rows, SUBLANES):
                nxt = pltpu.roll(buf_ref[j0 + SUBLANES:j0 + 2 * SUBLANES, c0:c0 + lc],
                                 SUBLANES - r, axis=0)
                shift_ref[r - 1, j0:j0 + SUBLANES, :] = jnp.where(from_this, cur, nxt)
                cur = nxt
        for r0 in range(0, ts, rc):
            acc = jnp.zeros((rc, lc), F32)
            for k in range(width):
                q8, r = divmod(base + k, SUBLANES)
                lo = q8 * SUBLANES + r0
                if r == 0:
                    win = buf_ref[lo:lo + rc, c0:c0 + lc]
                else:
                    win = shift_ref[r - 1, lo:lo + rc, :]
                acc = acc + dww_ref[k:k + 1, c0:c0 + lc] * win
            conv_ref[r0:r0 + rc, c0:c0 + lc] = acc
    buf_ref[0:CONV_HALO, :] = buf_ref[ts:ts + CONV_HALO, :]
    u = conv_ref[...] + dwb_ref[...]
    mu = jnp.mean(u, axis=-1, keepdims=True)
    dlt = u - mu
    var = jnp.mean(dlt * dlt, axis=-1, keepdims=True)
    u = dlt * lax.rsqrt(var + EPS) * lng_ref[...] + lnb_ref[...]
    u = u * jax.nn.sigmoid(u)
    y = jnp.dot(u.astype(BF16), pw2w_ref[...], preferred_element_type=F32) + pw2b_ref[...]
    x1 = x + gt1_ref[...] * y
    x1_ref[...] = x1
    _pre_moe(x1, n2g_ref[...], sh2_ref[...], sc2_ref[...], rwt_ref[...], rb_ref[...],
             earlier_ref[...], hn_ref, destt_ref, route_ref, cnt_ref)


def _mix0(x, n1g, sh1, sc1, gt1, pw1w, pw1b, dww, dwb, lng, lnb, pw2w, pw2b,
          n2g, sh2, sc2, rwt, rb, ts):
    b, s, d = x.shape
    n_exp = rwt.shape[0]
    width = dww.shape[0]
    assert width - 1 <= CONV_HALO and ts % MOE_TILE == 0 and ts // MOE_TILE <= SUBLANES
    out_shape, out_specs = _pre_moe_specs(b, s, d, ts, n_exp)
    full = lambda shp: pl.BlockSpec(shp, lambda i, j: (0,) * len(shp))
    return pl.pallas_call(
        _mix0_kernel,
        out_shape=out_shape,
        grid=(b, s // ts),
        in_specs=[
            pl.BlockSpec((None, ts, d), lambda i, j: (i, j, 0)),
            _vec_spec(d), _bvec_spec(d), _bvec_spec(d), _bvec_spec(d),
            full((d, 2 * d)), full((1, 2 * d)), full((width, d)), full((1, d)),
            full((1, d)), full((1, d)), full((d, d)), full((1, d)),
            _vec_spec(d), _bvec_spec(d), _bvec_spec(d),
            full((n_exp, d)), full((n_exp, 1)), full((ts, ts)),
        ],
        out_specs=out_specs,
        scratch_shapes=[
            pltpu.VMEM((CONV_HALO + ts, d), F32),
            pltpu.VMEM((ts, d), F32),
            pltpu.VMEM((SUBLANES - 1, CONV_HALO + ts - SUBLANES, min(512, d)), F32),
        ],
        compiler_params=_cparams(("arbitrary", "arbitrary")),
        name="conformer_mixer",
    )(x, n1g, sh1, sc1, gt1, pw1w, pw1b, dww, dwb, lng, lnb, pw2w, pw2b,
      n2g, sh2, sc2, rwt, rb, _earlier_in_tile(ts))


def _mix1_kernel(x_ref, o_ref, gt1_ref, wo_ref, n2g_ref, sh2_ref, sc2_ref, rwt_ref, rb_ref,
                 earlier_ref, x1_ref, hn_ref, destt_ref, route_ref, cnt_ref):
    y = jnp.dot(o_ref[...], wo_ref[...], preferred_element_type=F32)
    x1 = x_ref[...] + gt1_ref[...] * y
    x1_ref[...] = x1
    _pre_moe(x1, n2g_ref[...], sh2_ref[...], sc2_ref[...], rwt_ref[...], rb_ref[...],
             earlier_ref[...], hn_ref, destt_ref, route_ref, cnt_ref)


def _mix1(x, o, gt1, wo, n2g, sh2, sc2, rwt, rb, ts):
    b, s, d = x.shape
    n_exp = rwt.shape[0]
    do = o.shape[-1]
    assert ts % MOE_TILE == 0 and ts // MOE_TILE <= SUBLANES
    out_shape, out_specs = _pre_moe_specs(b, s, d, ts, n_exp)
    full = lambda shp: pl.BlockSpec(shp, lambda i, j: (0,) * len(shp))
    return pl.pallas_call(
        _mix1_kernel,
        out_shape=out_shape,
        grid=(b, s // ts),
        in_specs=[
            pl.BlockSpec((None, ts, d), lambda i, j: (i, j, 0)),
            pl.BlockSpec((None, ts, do), lambda i, j: (i, j, 0)),
            _bvec_spec(d), full((do, d)),
            _vec_spec(d), _bvec_spec(d), _bvec_spec(d),
            full((n_exp, d)), full((n_exp, 1)), full((ts, ts)),
        ],
        out_specs=out_specs,
        compiler_params=_cparams(("arbitrary", "arbitrary")),
        name="attn_out_mixer",
    )(x, o, gt1, wo, n2g, sh2, sc2, rwt, rb, _earlier_in_tile(ts))


def _moe_dims(n_tok, n_exp):
    n_tiles = n_tok // MOE_TILE
    loc_rows = _round_up(TOP_K * MOE_TILE + n_exp * (ROW_ALIGN - 1), LANES)
    max_rows = TOP_K * n_tok + n_tiles * n_exp * (ROW_ALIGN - 1) + n_exp * (EXPERT_ROWS - ROW_ALIGN)
    max_blocks = -(-max_rows // EXPERT_ROWS)
    return loc_rows, max_blocks


def _routing_tables(cnt, max_blocks):
    n_tiles, n_exp = cnt.shape
    cnt8 = _round_up(cnt, ROW_ALIGN)
    off = jnp.cumsum(cnt8, axis=1) - cnt8
    seg_len = cnt8.sum(axis=0)
    seg_pad = _round_up(seg_len, EXPERT_ROWS)
    seg_end = jnp.cumsum(seg_pad)
    seg_start = seg_end - seg_pad
    run_start = seg_start[None, :] + jnp.cumsum(cnt8, axis=0) - cnt8
    n_blocks = (seg_end[-1] // EXPERT_ROWS).astype(I32)
    blk_row = jnp.arange(max_blocks, dtype=I32) * EXPERT_ROWS
    blk_row = jnp.minimum(blk_row, seg_end[-1] - EXPERT_ROWS)
    blk_exp = jnp.minimum((blk_row[:, None] >= seg_end[None, :]).sum(axis=-1), n_exp - 1).astype(I32)
    blk = jnp.arange(max_blocks, dtype=I32)
    prev_exp = jnp.concatenate([jnp.full((1,), -1, I32), blk_exp[:-1]])
    blk_first = ((blk_exp != prev_exp) & (blk < n_blocks)).astype(I32)
    blk_slot = ((jnp.cumsum(blk_first) - 1) % 2).astype(I32)
    e_ids = jnp.arange(n_exp, dtype=I32)
    later = (e_ids[None, :] > e_ids[:, None]) & (seg_pad[None, :] > 0)
    next_exp = jnp.min(jnp.where(later, e_ids[None, :], n_exp), axis=1)
    next_exp = jnp.where(next_exp < n_exp, next_exp, -1)
    blk_next = jnp.sum(jnp.where(blk_exp[:, None] == e_ids[None, :], next_exp[None, :], 0), axis=1)
    flat = lambda a: a.reshape(-1).astype(I32)
    return dict(
        run_loc=flat(off),
        run_glb=flat(run_start),
        run_len=flat(cnt8),
        rows_used=flat(cnt8.sum(axis=1)),
        tail_glb=flat(seg_start + seg_len),
        tail_len=flat(seg_pad - seg_len),
        blk_exp=blk_exp, blk_first=blk_first, blk_slot=blk_slot, blk_next=flat(blk_next),
        n_blocks=n_blocks.reshape(1))


def _aligned(v):
    return pl.multiple_of(v, ROW_ALIGN)


def _dispatch_kernel(n_exp, loc0_ref, glb0_ref, len_ref, used_ref, tail0_ref, taillen_ref, nblk_ref,
                     x_ref, destt_ref, xs_ref, loc_ref, zero_ref, sem, zsem):
    i = pl.program_id(0)
    n = pl.num_programs(0)
    slot = i % 2
    tile = x_ref.shape[0]
    loc_rows = loc_ref.shape[1]

    def wait_slot(sl, rows):
        rows = _aligned(rows)

        @pl.when(rows > 0)
        def _():
            pltpu.make_async_copy(loc_ref.at[sl, pl.ds(0, rows)], xs_ref.at[pl.ds(0, rows)],
                                  sem.at[sl]).wait()

    @pl.when(i >= 2)
    def _():
        wait_slot(slot, used_ref[i - 2])

    r_iota = lax.broadcasted_iota(I32, (loc_rows, tile), 0)
    perm = jnp.zeros((loc_rows, tile), F32)
    for k in range(TOP_K):
        perm = jnp.where(r_iota == destt_ref[k:k + 1, :], 1.0, perm)
    loc_ref[slot] = jnp.dot(perm.astype(BF16), x_ref[...], preferred_element_type=F32)

    def issue_run(e, carry):
        rows = _aligned(len_ref[i * n_exp + e])

        @pl.when(rows > 0)
        def _():
            pltpu.make_async_copy(
                loc_ref.at[slot, pl.ds(_aligned(loc0_ref[i * n_exp + e]), rows)],
                xs_ref.at[pl.ds(_aligned(glb0_ref[i * n_exp + e]), rows)],
                sem.at[slot]).start()
        return carry
    lax.fori_loop(0, n_exp, issue_run, 0)

    @pl.when(i == n - 1)
    def _():
        zero_ref[...] = jnp.zeros(zero_ref.shape, F32)
        max_blocks = xs_ref.shape[0] // EXPERT_ROWS

        def zero_tail(e, total):
            rows = _aligned(taillen_ref[e])

            @pl.when(rows > 0)
            def _():
                pltpu.make_async_copy(zero_ref.at[pl.ds(0, rows)],
                                      xs_ref.at[pl.ds(_aligned(tail0_ref[e]), rows)],
                                      zsem.at[0]).start()
            return total + rows
        tail_rows = lax.fori_loop(0, n_exp, zero_tail, jnp.int32(0))

        def zero_block(blk, carry):
            pltpu.make_async_copy(
                zero_ref,
                xs_ref.at[pl.ds(pl.multiple_of(blk * EXPERT_ROWS, EXPERT_ROWS), EXPERT_ROWS)],
                zsem.at[1]).start()
            return carry
        lax.fori_loop(nblk_ref[0], max_blocks, zero_block, 0)

        @pl.when(i >= 1)
        def _():
            wait_slot(1 - slot, used_ref[i - 1])
        wait_slot(slot, used_ref[i])

        def wait_rows(rows, s):
            rows = _aligned(rows)

            @pl.when(rows > 0)
            def _():
                pltpu.make_async_copy(xs_ref.at[pl.ds(0, rows)], xs_ref.at[pl.ds(0, rows)],
                                      zsem.at[s]).wait()
        wait_rows(tail_rows, 0)
        wait_rows((max_blocks - nblk_ref[0]) * EXPERT_ROWS, 1)


def _dispatch(hn2, destt, tabs, n_exp, loc_rows, max_blocks):
    n_tok, d = hn2.shape
    n_tiles = n_tok // MOE_TILE
    return pl.pallas_call(
        functools.partial(_dispatch_kernel, n_exp),
        out_shape=jax.ShapeDtypeStruct((max_blocks * EXPERT_ROWS, d), F32),
        grid_spec=pltpu.PrefetchScalarGridSpec(
            num_scalar_prefetch=7,
            grid=(n_tiles,),
            in_specs=[
                pl.BlockSpec((MOE_TILE, d), lambda i, *_: (i, 0)),
                pl.BlockSpec((TOP_K, MOE_TILE), lambda i, *_: (0, i)),
            ],
            out_specs=pl.BlockSpec(memory_space=pl.ANY),
            scratch_shapes=[
                pltpu.VMEM((2, loc_rows, d), F32),
                pltpu.VMEM((EXPERT_ROWS, d), F32),
                pltpu.SemaphoreType.DMA((2,)),
                pltpu.SemaphoreType.DMA((2,)),
            ],
        ),
        compiler_params=_cparams(("arbitrary",)),
        name="moe_dispatch",
    )(tabs["run_loc"], tabs["run_glb"], tabs["run_len"], tabs["rows_used"],
      tabs["tail_glb"], tabs["tail_len"], tabs["n_blocks"], hn2, destt)


def _expert_kernel(layer, be_ref, first_ref, slot_ref, next_ref, nb_ref,
                   x_ref, wgu_hbm, bgu_ref, wdn_hbm, bdn_ref, y_ref,
                   wgu_f32, wdn_f32, wgu_bf, wdn_bf, sem):
    de = wdn_bf.shape[0]
    b = pl.program_id(0)

    def weight_copies(e, s):
        return (pltpu.make_async_copy(wgu_hbm.at[layer, e], wgu_f32.at[s], sem.at[s, 0]),
                pltpu.make_async_copy(wdn_hbm.at[layer, e], wdn_f32.at[s], sem.at[s, 1]))

    @pl.when(b == 0)
    def _():
        for cp in weight_copies(be_ref[0], 0):
            cp.start()

    @pl.when(first_ref[b] == 1)
    def _():
        s = slot_ref[b]
        for cp in weight_copies(be_ref[b], s):
            cp.wait()

        @pl.when(next_ref[b] >= 0)
        def _():
            for cp in weight_copies(next_ref[b], 1 - s):
                cp.start()
        cw = 512
        for c0 in range(0, wgu_bf.shape[1], cw):
            wgu_bf[:, c0:c0 + cw] = wgu_f32[s, :, c0:c0 + cw].astype(BF16)
        for c0 in range(0, wdn_bf.shape[1], cw):
            wdn_bf[:, c0:c0 + cw] = wdn_f32[s, :, c0:c0 + cw].astype(BF16)

    @pl.when(b < nb_ref[0])
    def _():
        xb = x_ref[...].astype(BF16)
        g = jnp.dot(xb, wgu_bf[:, :de], preferred_element_type=F32) + bgu_ref[:, :de]
        u = jnp.dot(xb, wgu_bf[:, de:], preferred_element_type=F32) + bgu_ref[:, de:]
        g = jnp.minimum(g, SWIGLU_LIMIT)
        u = jnp.clip(u, -SWIGLU_LIMIT, SWIGLU_LIMIT)
        a = (u + 1.0) * g * jax.nn.sigmoid(SWIGLU_ALPHA * g)
        y_ref[...] = jnp.dot(a.astype(BF16), wdn_bf[...], preferred_element_type=F32) + bdn_ref[...]

    @pl.when(pl.program_id(0) >= nb_ref[0])
    def _():
        y_ref[...] = jnp.zeros(y_ref.shape, y_ref.dtype)


def _experts(xs, tabs, layer, wgu, bgu, wdn, bdn):
    rows, d = xs.shape
    depth, n_exp, _, de2 = wgu.shape
    de = de2 // 2
    max_blocks = rows // EXPERT_ROWS
    row_map = lambda b, be, fi, sl, nx, nb: (jnp.minimum(b, nb[0] - 1), 0)
    exp_map = lambda b, be, fi, sl, nx, nb: (layer, be[b], 0, 0)
    return pl.pallas_call(
        functools.partial(_expert_kernel, layer),
        out_shape=jax.ShapeDtypeStruct((rows, d), F32),
        grid_spec=pltpu.PrefetchScalarGridSpec(
            num_scalar_prefetch=5,
            grid=(max_blocks,),
            in_specs=[
                pl.BlockSpec((EXPERT_ROWS, d), row_map),
                pl.BlockSpec(memory_space=pl.ANY),
                pl.BlockSpec((None, None, 1, de2), exp_map),
                pl.BlockSpec(memory_space=pl.ANY),
                pl.BlockSpec((None, None, 1, d), exp_map),
            ],
            out_specs=pl.BlockSpec((EXPERT_ROWS, d), lambda b, *_: (b, 0)),
            scratch_shapes=[
                pltpu.VMEM((2, d, de2), F32), pltpu.VMEM((2, de, d), F32),
                pltpu.VMEM((d, de2), BF16), pltpu.VMEM((de, d), BF16),
                pltpu.SemaphoreType.DMA((2, 2)),
            ],
        ),
        compiler_params=_cparams(("arbitrary",)),
        name="moe_experts",
    )(tabs["blk_exp"], tabs["blk_first"], tabs["blk_slot"], tabs["blk_next"], tabs["n_blocks"],
      xs, wgu, bgu.reshape(depth, n_exp, 1, de2), wdn, bdn.reshape(depth, n_exp, 1, d))


def _combine_kernel(final_norm, n_exp, loc0_ref, glb0_ref, len_ref, used_ref,
                    route_ref, x1_ref, g2_ref, fg_ref, ys_ref, out_ref, loc_ref, sem):
    i = pl.program_id(0)
    n = pl.num_programs(0)
    slot = i % 2
    tile = x1_ref.shape[0]
    loc_rows = loc_ref.shape[1]

    def issue_tile(t, sl):
        def issue_run(e, carry):
            rows = _aligned(len_ref[t * n_exp + e])

            @pl.when(rows > 0)
            def _():
                pltpu.make_async_copy(
                    ys_ref.at[pl.ds(_aligned(glb0_ref[t * n_exp + e]), rows)],
                    loc_ref.at[sl, pl.ds(_aligned(loc0_ref[t * n_exp + e]), rows)],
                    sem.at[sl]).start()
            return carry
        lax.fori_loop(0, n_exp, issue_run, 0)

    @pl.when(i == 0)
    def _():
        issue_tile(0, 0)

    @pl.when(i + 1 < n)
    def _():
        issue_tile(i + 1, 1 - slot)

    route = route_ref[...]
    r_iota = lax.broadcasted_iota(I32, (tile, loc_rows), 1)
    comb = jnp.zeros((tile, loc_rows), F32)
    for k in range(TOP_K):
        dest = route[:, k:k + 1].astype(I32)
        comb = jnp.where(r_iota == dest, route[:, TOP_K + k:TOP_K + k + 1], comb)

    used_rows = _aligned(used_ref[i])

    @pl.when(used_rows > 0)
    def _():
        pltpu.make_async_copy(ys_ref.at[pl.ds(0, used_rows)],
                              loc_ref.at[slot, pl.ds(0, used_rows)], sem.at[slot]).wait()

    row_ok = lax.broadcasted_iota(I32, (loc_rows, 1), 0) < used_rows
    yl = jnp.where(row_ok, loc_ref[slot], 0.0).astype(BF16)
    moe = jnp.dot(comb.astype(BF16), yl, preferred_element_type=F32)
    out = x1_ref[...] + g2_ref[...] * moe
    if final_norm:
        out = _rms(out, fg_ref[...])
    out_ref[...] = out


def _combine(ys, route, x1, gate2, final_g, tabs, n_exp, loc_rows, final_norm):
    b, s, d = x1.shape
    n_tok = b * s
    n_tiles = n_tok // MOE_TILE
    tiles_per_seq = s // MOE_TILE
    out = pl.pallas_call(
        functools.partial(_combine_kernel, final_norm, n_exp),
        out_shape=jax.ShapeDtypeStruct((n_tok, d), F32),
        grid_spec=pltpu.PrefetchScalarGridSpec(
            num_scalar_prefetch=4,
            grid=(n_tiles,),
            in_specs=[
                pl.BlockSpec((MOE_TILE, LANES), lambda i, *_: (i, 0)),
                pl.BlockSpec((MOE_TILE, d), lambda i, *_: (i, 0)),
                pl.BlockSpec((None, 1, d), lambda i, *_: (i // tiles_per_seq, 0, 0)),
                pl.BlockSpec((1, d), lambda i, *_: (0, 0)),
                pl.BlockSpec(memory_space=pl.ANY),
            ],
            out_specs=pl.BlockSpec((MOE_TILE, d), lambda i, *_: (i, 0)),
            scratch_shapes=[
                pltpu.VMEM((2, loc_rows, d), F32),
                pltpu.SemaphoreType.DMA((2,)),
            ],
        ),
        compiler_params=_cparams(("arbitrary",)),
        name="moe_combine",
    )(tabs["run_loc"], tabs["run_glb"], tabs["run_len"], tabs["rows_used"],
      route, x1.reshape(n_tok, d), gate2, final_g, ys)
    return out.reshape(b, s, d)


def _moe(x1, hn2, destt, route, cnt, gate2, final_g, layer, wgu, bgu, wdn, bdn, final_norm):
    b, s, d = x1.shape
    n_tok = b * s
    n_exp = wgu.shape[1]
    loc_rows, max_blocks = _moe_dims(n_tok, n_exp)
    tabs = _routing_tables(cnt, max_blocks)
    xs = _dispatch(hn2.reshape(n_tok, d), destt, tabs, n_exp, loc_rows, max_blocks)
    ys = _experts(xs, tabs, layer, wgu, bgu, wdn, bdn)
    return _combine(ys, route, x1, gate2, final_g, tabs, n_exp, loc_rows, final_norm)


def _qkv_kernel(n_heads, rope, scale, x_ref, trig_ref, kvg_ref, wdkv_ref, ckvg_ref, wuk_ref, wuv_ref,
                n1g_ref, sh1_ref, sc1_ref, wdq_ref, cqg_ref, wuq_ref,
                q_ref, k_ref, v_ref):
    x = x_ref[...]
    half = rope // 2
    trig = trig_ref[...]
    lane = lax.broadcasted_iota(I32, trig.shape, 1)
    cc = jnp.where(lane < half, trig,
                   jnp.where(lane < rope, pltpu.roll(trig, half, axis=1), 0.0))
    ss = jnp.where(lane < half, -pltpu.roll(trig, LANES - half, axis=1),
                   jnp.where(lane < rope, trig, 0.0))
    r_kv = ckvg_ref.shape[1]
    hk = _rms(x, kvg_ref[...]).astype(BF16)
    lat = jnp.dot(hk, wdkv_ref[...], preferred_element_type=F32)
    ckv = _rms(lat[:, :r_kv], ckvg_ref[...]).astype(BF16)
    krot = lat[:, r_kv:r_kv + LANES] * cc + lat[:, r_kv + LANES:r_kv + 2 * LANES] * ss
    kn = jnp.dot(ckv, wuk_ref[...], preferred_element_type=F32)
    vt = lax.dot_general(wuv_ref[...], ckv, (((1,), (1,)), ((), ())), preferred_element_type=F32)
    hq = (_rms(x, n1g_ref[...]) * (1.0 + sc1_ref[...]) + sh1_ref[...]).astype(BF16)
    cq = _rms(jnp.dot(hq, wdq_ref[...], preferred_element_type=F32), cqg_ref[...]).astype(BF16)
    qq = jnp.dot(cq, wuq_ref[...], preferred_element_type=F32) * scale
    hd = n_heads * LANES
    ts = x.shape[0]
    ones_rows = (lax.broadcasted_iota(I32, (V_ONES_ROWS, ts), 0) == 0).astype(BF16)
    for h in range(n_heads):
        sl = slice(h * LANES, (h + 1) * LANES)
        k_ref[h, :, 0:LANES] = kn[:, sl].astype(BF16)
        k_ref[h, :, LANES:2 * LANES] = krot.astype(BF16)
        v_ref[h, 0:LANES, :] = vt[h * LANES:(h + 1) * LANES, :].astype(BF16)
        v_ref[h, LANES:LANES + V_ONES_ROWS, :] = ones_rows
        q_ref[h, :, 0:LANES] = qq[:, sl].astype(BF16)
    per_tile = LANES // rope
    cc_rep, ss_rep = cc, ss
    for u in range(1, per_tile):
        cc_rep = cc_rep + pltpu.roll(cc, u * rope, axis=1)
        ss_rep = ss_rep + pltpu.roll(ss, u * rope, axis=1)
    in_head = lax.broadcasted_iota(I32, (ts, LANES), 1) < rope
    n_rope = n_heads * rope
    for j in range(n_heads // per_tile):
        cols = slice(hd + j * LANES, hd + (j + 1) * LANES)
        swapped = slice(hd + n_rope + j * LANES, hd + n_rope + (j + 1) * LANES)
        rot = qq[:, cols] * cc_rep + qq[:, swapped] * ss_rep
        for u in range(per_tile):
            piece = rot if u == 0 else pltpu.roll(rot, LANES - u * rope, axis=1)
            q_ref[j * per_tile + u, :, LANES:2 * LANES] = jnp.where(in_head, piece, 0.0).astype(BF16)


def _qkv(x, trig, kvg, wdkv, ckvg, wuk, wuv, n1g, sh1, sc1, wdq, cqg, wuq, n_heads, rope, scale, ts):
    b, s, d = x.shape
    assert LANES % rope == 0 and n_heads % (LANES // rope) == 0
    full = lambda a: pl.BlockSpec(a.shape, lambda i, j: (0,) * a.ndim)
    hspec = lambda w: pl.BlockSpec((None, n_heads, ts, w), lambda i, j: (i, 0, j, 0))
    return pl.pallas_call(
        functools.partial(_qkv_kernel, n_heads, rope, scale),
        out_shape=[
            jax.ShapeDtypeStruct((b, n_heads, s, 2 * LANES), BF16),
            jax.ShapeDtypeStruct((b, n_heads, s, 2 * LANES), BF16),
            jax.ShapeDtypeStruct((b, n_heads, s // ts, LANES + V_ONES_ROWS, ts), BF16),
        ],
        grid=(b, s // ts),
        in_specs=[
            pl.BlockSpec((None, ts, d), lambda i, j: (i, j, 0)),
            pl.BlockSpec((None, ts, LANES), lambda i, j: (i, j, 0)),
            full(kvg), full(wdkv), full(ckvg), full(wuk), full(wuv),
            full(n1g), _bvec_spec(d), _bvec_spec(d), full(wdq), full(cqg), full(wuq),
        ],
        out_specs=[hspec(2 * LANES), hspec(2 * LANES),
                   pl.BlockSpec((None, n_heads, None, LANES + V_ONES_ROWS, ts),
                                lambda i, j: (i, 0, j, 0, 0))],
        compiler_params=_cparams(("arbitrary", "arbitrary")),
        name="mla_qkv",
    )(x, trig, kvg, wdkv, ckvg, wuk, wuv, n1g, sh1, sc1, wdq, cqg, wuq)


def _attn_kernel(q_ref, k_ref, vt_ref, o_ref, m_ref, acc_ref):
    tq = q_ref.shape[0]
    tk = vt_ref.shape[2]
    ratio = tq // tk
    dv = o_ref.shape[1]
    qi = pl.program_id(2)
    m_ref[...] = jnp.full(m_ref.shape, -jnp.inf, F32)
    acc_ref[...] = jnp.zeros(acc_ref.shape, F32)

    def scores(ki, cols, on_diagonal):
        start = pl.multiple_of(ki * tk, tk)
        k = k_ref[pl.ds(start, tk), :]
        st = lax.dot_general(k, q_ref[cols, :], (((1,), (1,)), ((), ())),
                             preferred_element_type=F32)
        if on_diagonal:
            kc = lax.broadcasted_iota(I32, st.shape, 0) // CHUNK
            qc = lax.broadcasted_iota(I32, st.shape, 1) // CHUNK
            st = jnp.where(kc <= qc, st, -jnp.inf)
        return st

    def update(ki, st, m_old, acc_old):
        m_new = jnp.maximum(m_old, jnp.max(st, axis=0, keepdims=True))
        p = jnp.exp2(st - m_new).astype(BF16)
        alpha = jnp.exp2(m_old - m_new)
        acc_new = alpha * acc_old + jnp.dot(vt_ref[ki], p, preferred_element_type=F32)
        return m_new, acc_new

    def step(groups):
        sts = [[scores(ki, cols, on_diagonal) for ki, on_diagonal in tiles]
               for cols, tiles in groups]
        for (cols, tiles), group_sts in zip(groups, sts):
            state = (m_ref[:, cols], acc_ref[:, cols])
            for (ki, _), st in zip(tiles, group_sts):
                state = update(ki, st, *state)
            m_ref[:, cols], acc_ref[:, cols] = state

    n_below = qi * ratio
    everything = slice(0, tq)
    step([(slice(g * tk, (g + 1) * tk), [(n_below + a, a == g) for a in range(g + 1)])
          for g in range(ratio)])

    def fast_step(first, count):
        m = m_ref[...]
        pv = None
        top = None
        for t in range(count):
            st = scores(first + t, everything, False)
            tmax = jnp.max(st, axis=0, keepdims=True)
            top = tmax if top is None else jnp.maximum(top, tmax)
            p = jnp.exp2(st - m).astype(BF16)
            part = jnp.dot(vt_ref[first + t], p, preferred_element_type=F32)
            pv = part if pv is None else pv + part
        safe = jnp.max(top - m) <= ATTN_MAX_SLACK

        @pl.when(safe)
        def _():
            acc_ref[...] += pv

        @pl.when(jnp.logical_not(safe))
        def _():
            def redo(t, carry):
                step([(everything, [(first + t, False)])])
                return carry
            lax.fori_loop(0, count, redo, 0)

    def body(j, carry):
        fast_step(ATTN_UNROLL * j, ATTN_UNROLL)
        return carry
    trips = n_below // ATTN_UNROLL
    lax.fori_loop(0, trips, body, 0)

    for rem in range(math.gcd(ratio, ATTN_UNROLL), ATTN_UNROLL, math.gcd(ratio, ATTN_UNROLL)):
        @pl.when(n_below % ATTN_UNROLL == rem)
        def _(rem=rem):
            fast_step(trips * ATTN_UNROLL, rem)

    o_ref[...] = (acc_ref[0:dv, :] / acc_ref[dv:dv + 1, :]).T.astype(o_ref.dtype)


def _attention(q, k, vt, tq):
    b, h, s, dk = q.shape
    nk, dv_ext, tk = vt.shape[2:]
    dv = dv_ext - V_ONES_ROWS
    assert tk % CHUNK == 0 and s % tq == 0 and tq % tk == 0
    return pl.pallas_call(
        _attn_kernel,
        out_shape=jax.ShapeDtypeStruct((b, s, h * dv), BF16),
        grid=(b, h, s // tq),
        in_specs=[
            pl.BlockSpec((None, None, tq, dk), lambda i, j, t: (i, j, t, 0)),
            pl.BlockSpec((None, None, s, dk), lambda i, j, t: (i, j, 0, 0)),
            pl.BlockSpec((None, None, nk, dv_ext, tk), lambda i, j, t: (i, j, 0, 0, 0)),
        ],
        out_specs=pl.BlockSpec((None, tq, dv), lambda i, j, t: (i, t, j)),
        scratch_shapes=[pltpu.VMEM((1, tq), F32), pltpu.VMEM((dv_ext, tq), F32)],
        compiler_params=_cparams(("arbitrary", "arbitrary", "arbitrary")),
        name="mla_attention",
    )(q, k, vt)


def _swap_halves(w):
    half = w.shape[-1] // 2
    return jnp.concatenate([w[..., half:], w[..., :half]], axis=-1)


def _pad_lanes(w):
    pad = LANES - w.shape[-1]
    return jnp.concatenate([w, jnp.zeros(w.shape[:-1] + (pad,), w.dtype)], axis=-1)


def kernel(x, c, positions, mod_w, mod_b, norm1_g, norm2_g, conv_pw1_w, conv_pw1_b, conv_dw_w, conv_dw_b, conv_ln_g, conv_ln_b, conv_pw2_w, conv_pw2_b, kv_norm_g, w_dkv, ckv_norm_g, w_uk, w_uv, w_dq, cq_norm_g, w_uq, w_o, router_w, router_b, exp_w_gu, exp_b_gu, exp_w_dn, exp_b_dn, final_g):
    b, s, d = x.shape
    n_heads, nope = w_uk.shape[1], w_uk.shape[2]
    r_kv = ckv_norm_g.shape[0]
    rope = w_dkv.shape[1] - r_kv
    vdim = w_uv.shape[2]
    n_exp = router_w.shape[2]
    assert nope == LANES and vdim == LANES and rope <= LANES and d % LANES == 0
    ts = min(512, s)
    tq = min(1024, s)

    mod = _modulation(c, mod_w, mod_b)
    mods = [[m.reshape(b, 1, d) for m in jnp.split(mod[l], 6, axis=-1)] for l in range(2)]
    row = lambda v: v.reshape(1, -1)

    half = rope // 2
    inv = jnp.exp(-(2.0 * math.log(ROPE_THETA) / rope) * jnp.arange(half, dtype=F32))
    ang = positions.astype(F32)[..., None] * inv
    trig = _pad_lanes(jnp.concatenate([jnp.cos(ang), jnp.sin(ang)], axis=-1))

    sh1, sc1, gt1, sh2, sc2, gt2 = mods[0]
    x1, hn2, destt, route, cnt = _mix0(
        x, row(norm1_g[0]), sh1, sc1, gt1,
        conv_pw1_w[0].astype(BF16), row(conv_pw1_b[0]), conv_dw_w[0], row(conv_dw_b[0]),
        row(conv_ln_g[0]), row(conv_ln_b[0]), conv_pw2_w[0].astype(BF16), row(conv_pw2_b[0]),
        row(norm2_g[0]), sh2, sc2, router_w[0].T, router_b[0].reshape(n_exp, 1), ts)
    cnt = cnt[:, :ts // MOE_TILE].reshape(-1, n_exp)
    x2 = _moe(x1, hn2, destt, route, cnt, gt2, row(final_g),
              0, exp_w_gu, exp_b_gu, exp_w_dn, exp_b_dn, False)

    sh1, sc1, gt1, sh2, sc2, gt2 = mods[1]
    wdkv_rope = w_dkv[:, r_kv:]
    wdkv_ext = jnp.concatenate(
        [w_dkv[:, :r_kv], _pad_lanes(wdkv_rope), _pad_lanes(_swap_halves(wdkv_rope))], axis=-1)
    wuq = w_uq[0]
    r_q = wuq.shape[0]
    wuq_rope = wuq[:, :, nope:]
    wuq_ext = jnp.concatenate([
        wuq[:, :, :nope].reshape(r_q, n_heads * LANES),
        wuq_rope.reshape(r_q, n_heads * rope),
        _swap_halves(wuq_rope).reshape(r_q, n_heads * rope)], axis=-1)
    scale = float((nope + rope) ** -0.5 * math.log2(math.e))
    q, k, v = _qkv(
        x2, trig, row(kv_norm_g), wdkv_ext.astype(BF16), row(ckv_norm_g),
        w_uk.reshape(r_kv, n_heads * nope).astype(BF16), w_uv.reshape(r_kv, n_heads * vdim).T.astype(BF16),
        row(norm1_g[1]), sh1, sc1, w_dq[0].astype(BF16), row(cq_norm_g[0]), wuq_ext.astype(BF16),
        n_heads, rope, scale, ts)
    o = _attention(q, k, v, tq)

    x3, hn2, destt, route, cnt = _mix1(
        x2, o, gt1, w_o[0].astype(BF16), row(norm2_g[1]), sh2, sc2,
        router_w[1].T, router_b[1].reshape(n_exp, 1), ts)
    cnt = cnt[:, :ts // MOE_TILE].reshape(-1, n_exp)
    return _moe(x3, hn2, destt, route, cnt, gt2, row(final_g),
                1, exp_w_gu, exp_b_gu, exp_w_dn, exp_b_dn, True)
```

```python
import functools
import math

import jax
import jax.numpy as jnp
from jax import lax
from jax.experimental import pallas as pl
from jax.experimental.pallas import tpu as pltpu

CHUNK = 64
TOP_K = 4
ROPE_THETA = 10000.0
SWIGLU_ALPHA = 1.702
SWIGLU_LIMIT = 7.0
EPS = 1e-6

LANES = 128
SUBLANES = 8
VMEM_LIMIT_BYTES = 56 * 1024 * 1024

ROW_ALIGN = SUBLANES
MOE_TILE = 256
EXPERT_ROWS = 512
CONV_HALO = 32
V_ONES_ROWS = 2 * SUBLANES
ATTN_UNROLL = 4
ATTN_MAX_SLACK = 60.0

F32 = jnp.float32
BF16 = jnp.bfloat16
I32 = jnp.int32


def _cparams(sem):
    return pltpu.CompilerParams(dimension_semantics=sem, vmem_limit_bytes=VMEM_LIMIT_BYTES)


def _rms(x, g):
    return x * lax.rsqrt(jnp.mean(x * x, axis=-1, keepdims=True) + EPS) * g


def _round_up(a, m):
    return (a + m - 1) // m * m


def _mod_kernel(c_ref, w_ref, b_ref, o_ref):
    c = c_ref[...]
    ca = c * jax.nn.sigmoid(c)
    o_ref[...] = jnp.dot(ca, w_ref[...], preferred_element_type=F32) + b_ref[...]


def _modulation(c, mod_w, mod_b):
    depth, d, d6 = mod_w.shape
    b = c.shape[0]
    bp = _round_up(b, SUBLANES)
    cp = jnp.zeros((bp, d), F32).at[:b].set(c)
    tn = d6 // 4
    out = pl.pallas_call(
        _mod_kernel,
        out_shape=jax.ShapeDtypeStruct((depth, bp, d6), F32),
        grid=(depth, d6 // tn),
        in_specs=[
            pl.BlockSpec((bp, d), lambda l, j: (0, 0)),
            pl.BlockSpec((None, d, tn), lambda l, j: (l, 0, j)),
            pl.BlockSpec((None, 1, tn), lambda l, j: (l, 0, j)),
        ],
        out_specs=pl.BlockSpec((None, bp, tn), lambda l, j: (l, 0, j)),
        compiler_params=_cparams(("arbitrary", "arbitrary")),
        name="adaln_modulation",
    )(cp, mod_w, mod_b.reshape(depth, 1, d6))
    return out[:, :b]


def _pre_moe(x1, g2, sh2, sc2, rwt, rb, earlier, hn_ref, destt_ref, route_ref, cnt_ref):
    ts = x1.shape[0]
    n_exp = rwt.shape[0]
    hn = _rms(x1, g2) * (1.0 + sc2) + sh2
    hn_ref[...] = hn.astype(BF16)
    def split(v):
        hi = v.astype(BF16)
        return hi, (v - hi.astype(F32)).astype(BF16)
    nt = lambda a, b: lax.dot_general(a, b, (((1,), (1,)), ((), ())), preferred_element_type=F32)
    w_hi, w_lo = split(rwt)
    h_hi, h_lo = split(hn)
    logits = nt(w_hi, h_hi) + (nt(w_hi, h_lo) + nt(w_lo, h_hi)) + rb
    e_iota = lax.broadcasted_iota(I32, (n_exp, ts), 0)
    vals, idxs = [], []
    cur = logits
    for _ in range(TOP_K):
        m = jnp.max(cur, axis=0, keepdims=True)
        i = jnp.min(jnp.where(cur == m, e_iota, n_exp), axis=0, keepdims=True)
        vals.append(m)
        idxs.append(i)
        cur = jnp.where(e_iota == i, -jnp.inf, cur)
    exps = [jnp.exp(v - vals[0]) for v in vals]
    den = exps[0]
    for e in exps[1:]:
        den = den + e
    gates = [e / den for e in exps]
    hits = [e_iota == i for i in idxs]
    onehot = jnp.zeros((n_exp, ts), F32)
    for h in hits:
        onehot = onehot + h.astype(F32)
    rank = jnp.dot(onehot.astype(BF16), earlier, preferred_element_type=F32)
    lower = (lax.broadcasted_iota(I32, (n_exp, n_exp), 1)
             < lax.broadcasted_iota(I32, (n_exp, n_exp), 0)).astype(BF16)
    dests = [[] for _ in range(TOP_K)]
    for t0 in range(0, ts, MOE_TILE):
        cols = slice(t0, t0 + MOE_TILE)
        cnt = jnp.sum(onehot[:, cols], axis=1, keepdims=True)
        cnt8 = jnp.ceil(cnt / ROW_ALIGN) * ROW_ALIGN
        off = jnp.dot(lower, jnp.broadcast_to(cnt8, (n_exp, LANES)).astype(BF16),
                      preferred_element_type=F32)[:, 0:1]
        base = off + rank[:, cols]
        for k, h in enumerate(hits):
            dests[k].append(jnp.sum(jnp.where(h[:, cols], base, 0.0), axis=0, keepdims=True))
    dests = [jnp.concatenate(parts, axis=1) for parts in dests]
    destt_ref[...] = jnp.concatenate(dests, axis=0).astype(I32)
    rows = jnp.concatenate(dests + gates + [jnp.zeros((LANES - 2 * TOP_K, ts), F32)], axis=0)
    route_ref[...] = rows.T
    sel = (lax.broadcasted_iota(I32, (SUBLANES, ts), 1) // MOE_TILE
           == lax.broadcasted_iota(I32, (SUBLANES, ts), 0)).astype(BF16)
    cnt = lax.dot_general(sel, onehot.astype(BF16), (((1,), (1,)), ((), ())),
                          preferred_element_type=F32)
    cnt_ref[...] = cnt.astype(I32)


def _earlier_in_tile(ts):
    t = jnp.arange(ts, dtype=I32)
    same_tile = (t[:, None] // MOE_TILE) == (t[None, :] // MOE_TILE)
    return ((t[:, None] < t[None, :]) & same_tile).astype(BF16)


def _pre_moe_specs(b, s, d, ts, n_exp):
    ns = s // ts
    out_shape = [
        jax.ShapeDtypeStruct((b, s, d), F32),
        jax.ShapeDtypeStruct((b, s, d), BF16),
        jax.ShapeDtypeStruct((TOP_K, b * s), I32),
        jax.ShapeDtypeStruct((b * s, LANES), F32),
        jax.ShapeDtypeStruct((b * ns, SUBLANES, n_exp), I32),
    ]
    out_specs = [
        pl.BlockSpec((None, ts, d), lambda i, j: (i, j, 0)),
        pl.BlockSpec((None, ts, d), lambda i, j: (i, j, 0)),
        pl.BlockSpec((TOP_K, ts), lambda i, j: (0, i * ns + j)),
        pl.BlockSpec((ts, LANES), lambda i, j: (i * ns + j, 0)),
        pl.BlockSpec((None, SUBLANES, n_exp), lambda i, j: (i * ns + j, 0, 0)),
    ]
    return out_shape, out_specs


def _vec_spec(d):
    return pl.BlockSpec((1, d), lambda i, j: (0, 0))


def _bvec_spec(d):
    return pl.BlockSpec((None, 1, d), lambda i, j: (i, 0, 0))


def _mix0_kernel(x_ref, n1g_ref, sh1_ref, sc1_ref, gt1_ref,
                 pw1w_ref, pw1b_ref, dww_ref, dwb_ref, lng_ref, lnb_ref,
                 pw2w_ref, pw2b_ref, n2g_ref, sh2_ref, sc2_ref, rwt_ref, rb_ref, earlier_ref,
                 x1_ref, hn_ref, destt_ref, route_ref, cnt_ref, buf_ref, conv_ref, shift_ref):
    ts, d = x_ref.shape
    width = dww_ref.shape[0]
    x = x_ref[...]
    hn = _rms(x, n1g_ref[...]) * (1.0 + sc1_ref[...]) + sh1_ref[...]
    hb = hn.astype(BF16)
    a = jnp.dot(hb, pw1w_ref[:, :d], preferred_element_type=F32) + pw1b_ref[:, :d]
    g = jnp.dot(hb, pw1w_ref[:, d:], preferred_element_type=F32) + pw1b_ref[:, d:]
    glu = a * jax.nn.sigmoid(g)

    @pl.when(pl.program_id(1) == 0)
    def _():
        buf_ref[0:CONV_HALO, :] = jnp.zeros((CONV_HALO, d), F32)

    buf_ref[CONV_HALO:, :] = glu
    base = CONV_HALO - (width - 1)
    rc = 32
    lc = min(512, d)
    sh_rows = shift_ref.shape[1]
    sub = lax.broadcasted_iota(I32, (SUBLANES, lc), 0)
    for c0 in range(0, d, lc):
        for r in range(1, SUBLANES):
            from_this = sub < SUBLANES - r
            cur = pltpu.roll(buf_ref[0:SUBLANES, c0:c0 + lc], SUBLANES - r, axis=0)
            for j0 in range(0, sh_rows, SUBLANES):
                nxt = pltpu.roll(buf_ref[j0 + SUBLANES:j0 + 2 * SUBLANES, c0:c0 + lc],
                                 SUBLANES - r, axis=0)
                shift_ref[r - 1, j0:j0 + SUBLANES, :] = jnp.where(from_this, cur, nxt)
                cur = nxt
        for r0 in range(0, ts, rc):
            acc = jnp.zeros((rc, lc), F32)
            for k in range(width):
                q8, r = divmod(base + k, SUBLANES)
                lo = q8 * SUBLANES + r0
                if r == 0:
                    win = buf_ref[lo:lo + rc, c0:c0 + lc]
                else:
                    win = shift_ref[r - 1, lo:lo + rc, :]
                acc = acc + dww_ref[k:k + 1, c0:c0 + lc] * win
            conv_ref[r0:r0 + rc, c0:c0 + lc] = acc
    buf_ref[0:CONV_HALO, :] = buf_ref[ts:ts + CONV_HALO, :]
    u = conv_ref[...] + dwb_ref[...]
    mu = jnp.mean(u, axis=-1, keepdims=True)
    dlt = u - mu
    var = jnp.mean(dlt * dlt, axis=-1, keepdims=True)
    u = dlt * lax.rsqrt(var + EPS) * lng_ref[...] + lnb_ref[...]
    u = u * jax.nn.sigmoid(u)
    y = jnp.dot(u.astype(BF16), pw2w_ref[...], preferred_element_type=F32) + pw2b_ref[...]
    x1 = x + gt1_ref[...] * y
    x1_ref[...] = x1
    _pre_moe(x1, n2g_ref[...], sh2_ref[...], sc2_ref[...], rwt_ref[...], rb_ref[...],
             earlier_ref[...], hn_ref, destt_ref, route_ref, cnt_ref)


def _mix0(x, n1g, sh1, sc1, gt1, pw1w, pw1b, dww, dwb, lng, lnb, pw2w, pw2b,
          n2g, sh2, sc2, rwt, rb, ts):
    b, s, d = x.shape
    n_exp = rwt.shape[0]
    width = dww.shape[0]
    assert width - 1 <= CONV_HALO and ts % MOE_TILE == 0 and ts // MOE_TILE <= SUBLANES
    out_shape, out_specs = _pre_moe_specs(b, s, d, ts, n_exp)
    full = lambda shp: pl.BlockSpec(shp, lambda i, j: (0,) * len(shp))
    return pl.pallas_call(
        _mix0_kernel,
        out_shape=out_shape,
        grid=(b, s // ts),
        in_specs=[
            pl.BlockSpec((None, ts, d), lambda i, j: (i, j, 0)),
            _vec_spec(d), _bvec_spec(d), _bvec_spec(d), _bvec_spec(d),
            full((d, 2 * d)), full((1, 2 * d)), full((width, d)), full((1, d)),
            full((1, d)), full((1, d)), full((d, d)), full((1, d)),
            _vec_spec(d), _bvec_spec(d), _bvec_spec(d),
            full((n_exp, d)), full((n_exp, 1)), full((ts, ts)),
        ],
        out_specs=out_specs,
        scratch_shapes=[
            pltpu.VMEM((CONV_HALO + ts, d), F32),
            pltpu.VMEM((ts, d), F32),
            pltpu.VMEM((SUBLANES - 1, CONV_HALO + ts - SUBLANES, min(512, d)), F32),
        ],
        compiler_params=_cparams(("arbitrary", "arbitrary")),
        name="conformer_mixer",
    )(x, n1g, sh1, sc1, gt1, pw1w, pw1b, dww, dwb, lng, lnb, pw2w, pw2b,
      n2g, sh2, sc2, rwt, rb, _earlier_in_tile(ts))


def _mix1_kernel(x_ref, o_ref, gt1_ref, wo_ref, n2g_ref, sh2_ref, sc2_ref, rwt_ref, rb_ref,
                 earlier_ref, x1_ref, hn_ref, destt_ref, route_ref, cnt_ref):
    y = jnp.dot(o_ref[...], wo_ref[...], preferred_element_type=F32)
    x1 = x_ref[...] + gt1_ref[...] * y
    x1_ref[...] = x1
    _pre_moe(x1, n2g_ref[...], sh2_ref[...], sc2_ref[...], rwt_ref[...], rb_ref[...],
             earlier_ref[...], hn_ref, destt_ref, route_ref, cnt_ref)


def _mix1(x, o, gt1, wo, n2g, sh2, sc2, rwt, rb, ts):
    b, s, d = x.shape
    n_exp = rwt.shape[0]
    do = o.shape[-1]
    assert ts % MOE_TILE == 0 and ts // MOE_TILE <= SUBLANES
    out_shape, out_specs = _pre_moe_specs(b, s, d, ts, n_exp)
    full = lambda shp: pl.BlockSpec(shp, lambda i, j: (0,) * len(shp))
    return pl.pallas_call(
        _mix1_kernel,
        out_shape=out_shape,
        grid=(b, s // ts),
        in_specs=[
            pl.BlockSpec((None, ts, d), lambda i, j: (i, j, 0)),
            pl.BlockSpec((None, ts, do), lambda i, j: (i, j, 0)),
            _bvec_spec(d), full((do, d)),
            _vec_spec(d), _bvec_spec(d), _bvec_spec(d),
            full((n_exp, d)), full((n_exp, 1)), full((ts, ts)),
        ],
        out_specs=out_specs,
        compiler_params=_cparams(("arbitrary", "arbitrary")),
        name="attn_out_mixer",
    )(x, o, gt1, wo, n2g, sh2, sc2, rwt, rb, _earlier_in_tile(ts))


def _moe_dims(n_tok, n_exp):
    n_tiles = n_tok // MOE_TILE
    loc_rows = _round_up(TOP_K * MOE_TILE + n_exp * (ROW_ALIGN - 1), LANES)
    max_rows = TOP_K * n_tok + n_tiles * n_exp * (ROW_ALIGN - 1) + n_exp * (EXPERT_ROWS - ROW_ALIGN)
    max_blocks = -(-max_rows // EXPERT_ROWS)
    return loc_rows, max_blocks


def _routing_tables(cnt, max_blocks):
    n_tiles, n_exp = cnt.shape
    cnt8 = _round_up(cnt, ROW_ALIGN)
    off = jnp.cumsum(cnt8, axis=1) - cnt8
    seg_len = cnt8.sum(axis=0)
    seg_pad = _round_up(seg_len, EXPERT_ROWS)
    seg_end = jnp.cumsum(seg_pad)
    seg_start = seg_end - seg_pad
    run_start = seg_start[None, :] + jnp.cumsum(cnt8, axis=0) - cnt8
    n_blocks = (seg_end[-1] // EXPERT_ROWS).astype(I32)
    blk_row = jnp.arange(max_blocks, dtype=I32) * EXPERT_ROWS
    blk_row = jnp.minimum(blk_row, seg_end[-1] - EXPERT_ROWS)
    blk_exp = jnp.minimum((blk_row[:, None] >= seg_end[None, :]).sum(axis=-1), n_exp - 1).astype(I32)
    blk = jnp.arange(max_blocks, dtype=I32)
    prev_exp = jnp.concatenate([jnp.full((1,), -1, I32), blk_exp[:-1]])
    blk_first = ((blk_exp != prev_exp) & (blk < n_blocks)).astype(I32)
    blk_slot = ((jnp.cumsum(blk_first) - 1) % 2).astype(I32)
    e_ids = jnp.arange(n_exp, dtype=I32)
    later = (e_ids[None, :] > e_ids[:, None]) & (seg_pad[None, :] > 0)
    next_exp = jnp.min(jnp.where(later, e_ids[None, :], n_exp), axis=1)
    next_exp = jnp.where(next_exp < n_exp, next_exp, -1)
    blk_next = jnp.sum(jnp.where(blk_exp[:, None] == e_ids[None, :], next_exp[None, :], 0), axis=1)
    flat = lambda a: a.reshape(-1).astype(I32)
    return dict(
        run_loc=flat(off),
        run_glb=flat(run_start),
        run_len=flat(cnt8),
        rows_used=flat(cnt8.sum(axis=1)),
        tail_glb=flat(seg_start + seg_len),
        tail_len=flat(seg_pad - seg_len),
        blk_exp=blk_exp, blk_first=blk_first, blk_slot=blk_slot, blk_next=flat(blk_next),
        n_blocks=n_blocks.reshape(1))


def _aligned(v):
    return pl.multiple_of(v, ROW_ALIGN)


def _dispatch_kernel(n_exp, loc0_ref, glb0_ref, len_ref, used_ref, tail0_ref, taillen_ref, nblk_ref,
                     x_ref, destt_ref, xs_ref, loc_ref, zero_ref, sem, zsem):
    i = pl.program_id(0)
    n = pl.num_programs(0)
    slot = i % 2
    tile = x_ref.shape[0]
    loc_rows = loc_ref.shape[1]

    def wait_slot(sl, rows):
        rows = _aligned(rows)

        @pl.when(rows > 0)
        def _():
            pltpu.make_async_copy(loc_ref.at[sl, pl.ds(0, rows)], xs_ref.at[pl.ds(0, rows)],
                                  sem.at[sl]).wait()

    @pl.when(i >= 2)
    def _():
        wait_slot(slot, used_ref[i - 2])

    max_blocks = xs_ref.shape[0] // EXPERT_ROWS

    @pl.when(i == 0)
    def _():
        zero_ref[...] = jnp.zeros(zero_ref.shape, F32)

        def zero_tail(e, carry):
            rows = _aligned(taillen_ref[e])

            @pl.when(rows > 0)
            def _():
                pltpu.make_async_copy(zero_ref.at[pl.ds(0, rows)],
                                      xs_ref.at[pl.ds(_aligned(tail0_ref[e]), rows)],
                                      zsem.at[0]).start()
            return carry
        lax.fori_loop(0, n_exp, zero_tail, 0)

        def zero_block(blk, carry):
            pltpu.make_async_copy(
                zero_ref,
                xs_ref.at[pl.ds(pl.multiple_of(blk * EXPERT_ROWS, EXPERT_ROWS), EXPERT_ROWS)],
                zsem.at[1]).start()
            return carry
        lax.fori_loop(nblk_ref[0], max_blocks, zero_block, 0)

    r_iota = lax.broadcasted_iota(I32, (loc_rows, tile), 0)
    perm = jnp.zeros((loc_rows, tile), F32)
    for k in range(TOP_K):
        perm = jnp.where(r_iota == destt_ref[k:k + 1, :], 1.0, perm)
    loc_ref[slot] = jnp.dot(perm.astype(BF16), x_ref[...], preferred_element_type=F32)

    def issue_run(e, carry):
        rows = _aligned(len_ref[i * n_exp + e])

        @pl.when(rows > 0)
        def _():
            pltpu.make_async_copy(
                loc_ref.at[slot, pl.ds(_aligned(loc0_ref[i * n_exp + e]), rows)],
                xs_ref.at[pl.ds(_aligned(glb0_ref[i * n_exp + e]), rows)],
                sem.at[slot]).start()
        return carry
    lax.fori_loop(0, n_exp, issue_run, 0)

    @pl.when(i == n - 1)
    def _():
        tail_rows = lax.fori_loop(0, n_exp, lambda e, total: total + taillen_ref[e], jnp.int32(0))

        @pl.when(i >= 1)
        def _():
            wait_slot(1 - slot, used_ref[i - 1])
        wait_slot(slot, used_ref[i])

        def wait_rows(rows, s):
            rows = _aligned(rows)

            @pl.when(rows > 0)
            def _():
                pltpu.make_async_copy(xs_ref.at[pl.ds(0, rows)], xs_ref.at[pl.ds(0, rows)],
                                      zsem.at[s]).wait()
        wait_rows(tail_rows, 0)
        wait_rows((max_blocks - nblk_ref[0]) * EXPERT_ROWS, 1)


def _dispatch(hn2, destt, tabs, n_exp, loc_rows, max_blocks):
    n_tok, d = hn2.shape
    n_tiles = n_tok // MOE_TILE
    return pl.pallas_call(
        functools.partial(_dispatch_kernel, n_exp),
        out_shape=jax.ShapeDtypeStruct((max_blocks * EXPERT_ROWS, d), F32),
        grid_spec=pltpu.PrefetchScalarGridSpec(
            num_scalar_prefetch=7,
            grid=(n_tiles,),
            in_specs=[
                pl.BlockSpec((MOE_TILE, d), lambda i, *_: (i, 0)),
                pl.BlockSpec((TOP_K, MOE_TILE), lambda i, *_: (0, i)),
            ],
            out_specs=pl.BlockSpec(memory_space=pl.ANY),
            scratch_shapes=[
                pltpu.VMEM((2, loc_rows, d), F32),
                pltpu.VMEM((EXPERT_ROWS, d), F32),
                pltpu.SemaphoreType.DMA((2,)),
                pltpu.SemaphoreType.DMA((2,)),
            ],
        ),
        compiler_params=_cparams(("arbitrary",)),
        name="moe_dispatch",
    )(tabs["run_loc"], tabs["run_glb"], tabs["run_len"], tabs["rows_used"],
      tabs["tail_glb"], tabs["tail_len"], tabs["n_blocks"], hn2, destt)


def _expert_kernel(layer, be_ref, first_ref, slot_ref, next_ref, nb_ref,
                   x_ref, wgu_hbm, bgu_ref, wdn_hbm, bdn_ref, y_ref,
                   wgu_f32, wdn_f32, wgu_bf, wdn_bf, sem):
    de = wdn_bf.shape[0]
    b = pl.program_id(0)

    def weight_copies(e, s):
        return (pltpu.make_async_copy(wgu_hbm.at[layer, e], wgu_f32.at[s], sem.at[s, 0]),
                pltpu.make_async_copy(wdn_hbm.at[layer, e], wdn_f32.at[s], sem.at[s, 1]))

    @pl.when(b == 0)
    def _():
        for cp in weight_copies(be_ref[0], 0):
            cp.start()

    @pl.when(first_ref[b] == 1)
    def _():
        s = slot_ref[b]
        for cp in weight_copies(be_ref[b], s):
            cp.wait()

        @pl.when(next_ref[b] >= 0)
        def _():
            for cp in weight_copies(next_ref[b], 1 - s):
                cp.start()
        cw = 512
        for c0 in range(0, wgu_bf.shape[1], cw):
            wgu_bf[:, c0:c0 + cw] = wgu_f32[s, :, c0:c0 + cw].astype(BF16)
        for c0 in range(0, wdn_bf.shape[1], cw):
            wdn_bf[:, c0:c0 + cw] = wdn_f32[s, :, c0:c0 + cw].astype(BF16)

    @pl.when(b < nb_ref[0])
    def _():
        xb = x_ref[...].astype(BF16)
        g = jnp.dot(xb, wgu_bf[:, :de], preferred_element_type=F32) + bgu_ref[:, :de]
        u = jnp.dot(xb, wgu_bf[:, de:], preferred_element_type=F32) + bgu_ref[:, de:]
        g = jnp.minimum(g, SWIGLU_LIMIT)
        u = jnp.clip(u, -SWIGLU_LIMIT, SWIGLU_LIMIT)
        a = (u + 1.0) * g * jax.nn.sigmoid(SWIGLU_ALPHA * g)
        y_ref[...] = jnp.dot(a.astype(BF16), wdn_bf[...], preferred_element_type=F32) + bdn_ref[...]

    @pl.when(pl.program_id(0) >= nb_ref[0])
    def _():
        y_ref[...] = jnp.zeros(y_ref.shape, y_ref.dtype)


def _experts(xs, tabs, layer, wgu, bgu, wdn, bdn):
    rows, d = xs.shape
    depth, n_exp, _, de2 = wgu.shape
    de = de2 // 2
    max_blocks = rows // EXPERT_ROWS
    row_map = lambda b, be, fi, sl, nx, nb: (jnp.minimum(b, nb[0] - 1), 0)
    exp_map = lambda b, be, fi, sl, nx, nb: (layer, be[b], 0, 0)
    return pl.pallas_call(
        functools.partial(_expert_kernel, layer),
        out_shape=jax.ShapeDtypeStruct((rows, d), F32),
        grid_spec=pltpu.PrefetchScalarGridSpec(
            num_scalar_prefetch=5,
            grid=(max_blocks,),
            in_specs=[
                pl.BlockSpec((EXPERT_ROWS, d), row_map),
                pl.BlockSpec(memory_space=pl.ANY),
                pl.BlockSpec((None, None, 1, de2), exp_map),
                pl.BlockSpec(memory_space=pl.ANY),
                pl.BlockSpec((None, None, 1, d), exp_map),
            ],
            out_specs=pl.BlockSpec((EXPERT_ROWS, d), lambda b, *_: (b, 0)),
            scratch_shapes=[
                pltpu.VMEM((2, d, de2), F32), pltpu.VMEM((2, de, d), F32),
                pltpu.VMEM((d, de2), BF16), pltpu.VMEM((de, d), BF16),
                pltpu.SemaphoreType.DMA((2, 2)),
            ],
        ),
        compiler_params=_cparams(("arbitrary",)),
        name="moe_experts",
    )(tabs["blk_exp"], tabs["blk_first"], tabs["blk_slot"], tabs["blk_next"], tabs["n_blocks"],
      xs, wgu, bgu.reshape(depth, n_exp, 1, de2), wdn, bdn.reshape(depth, n_exp, 1, d))


def _combine_kernel(final_norm, n_exp, loc0_ref, glb0_ref, len_ref, used_ref,
                    route_ref, x1_ref, g2_ref, fg_ref, ys_ref, out_ref, loc_ref, sem):
    i = pl.program_id(0)
    n = pl.num_programs(0)
    slot = i % 2
    tile = x1_ref.shape[0]
    loc_rows = loc_ref.shape[1]

    def issue_tile(t, sl):
        def issue_run(e, carry):
            rows = _aligned(len_ref[t * n_exp + e])

            @pl.when(rows > 0)
            def _():
                pltpu.make_async_copy(
                    ys_ref.at[pl.ds(_aligned(glb0_ref[t * n_exp + e]), rows)],
                    loc_ref.at[sl, pl.ds(_aligned(loc0_ref[t * n_exp + e]), rows)],
                    sem.at[sl]).start()
            return carry
        lax.fori_loop(0, n_exp, issue_run, 0)

    @pl.when(i == 0)
    def _():
        issue_tile(0, 0)

    @pl.when(i + 1 < n)
    def _():
        issue_tile(i + 1, 1 - slot)

    route = route_ref[...]
    r_iota = lax.broadcasted_iota(I32, (tile, loc_rows), 1)
    comb = jnp.zeros((tile, loc_rows), F32)
    for k in range(TOP_K):
        dest = route[:, k:k + 1].astype(I32)
        comb = jnp.where(r_iota == dest, route[:, TOP_K + k:TOP_K + k + 1], comb)

    used_rows = _aligned(used_ref[i])

    @pl.when(used_rows > 0)
    def _():
        pltpu.make_async_copy(ys_ref.at[pl.ds(0, used_rows)],
                              loc_ref.at[slot, pl.ds(0, used_rows)], sem.at[slot]).wait()

    row_ok = lax.broadcasted_iota(I32, (loc_rows, 1), 0) < used_rows
    yl = jnp.where(row_ok, loc_ref[slot], 0.0).astype(BF16)
    moe = jnp.dot(comb.astype(BF16), yl, preferred_element_type=F32)
    out = x1_ref[...] + g2_ref[...] * moe
    if final_norm:
        out = _rms(out, fg_ref[...])
    out_ref[...] = out


def _combine(ys, route, x1, gate2, final_g, tabs, n_exp, loc_rows, final_norm):
    b, s, d = x1.shape
    n_tok = b * s
    n_tiles = n_tok // MOE_TILE
    tiles_per_seq = s // MOE_TILE
    out = pl.pallas_call(
        functools.partial(_combine_kernel, final_norm, n_exp),
        out_shape=jax.ShapeDtypeStruct((n_tok, d), F32),
        grid_spec=pltpu.PrefetchScalarGridSpec(
            num_scalar_prefetch=4,
            grid=(n_tiles,),
            in_specs=[
                pl.BlockSpec((MOE_TILE, LANES), lambda i, *_: (i, 0)),
                pl.BlockSpec((MOE_TILE, d), lambda i, *_: (i, 0)),
                pl.BlockSpec((None, 1, d), lambda i, *_: (i // tiles_per_seq, 0, 0)),
                pl.BlockSpec((1, d), lambda i, *_: (0, 0)),
                pl.BlockSpec(memory_space=pl.ANY),
            ],
            out_specs=pl.BlockSpec((MOE_TILE, d), lambda i, *_: (i, 0)),
            scratch_shapes=[
                pltpu.VMEM((2, loc_rows, d), F32),
                pltpu.SemaphoreType.DMA((2,)),
            ],
        ),
        compiler_params=_cparams(("arbitrary",)),
        name="moe_combine",
    )(tabs["run_loc"], tabs["run_glb"], tabs["run_len"], tabs["rows_used"],
      route, x1.reshape(n_tok, d), gate2, final_g, ys)
    return out.reshape(b, s, d)


def _moe(x1, hn2, destt, route, cnt, gate2, final_g, layer, wgu, bgu, wdn, bdn, final_norm):
    b, s, d = x1.shape
    n_tok = b * s
    n_exp = wgu.shape[1]
    loc_rows, max_blocks = _moe_dims(n_tok, n_exp)
    tabs = _routing_tables(cnt, max_blocks)
    xs = _dispatch(hn2.reshape(n_tok, d), destt, tabs, n_exp, loc_rows, max_blocks)
    ys = _experts(xs, tabs, layer, wgu, bgu, wdn, bdn)
    return _combine(ys, route, x1, gate2, final_g, tabs, n_exp, loc_rows, final_norm)


def _qkv_kernel(n_heads, rope, scale, x_ref, trig_ref, kvg_ref, wdkv_ref, ckvg_ref, wuk_ref, wuv_ref,
                n1g_ref, sh1_ref, sc1_ref, wdq_ref, cqg_ref, wuq_ref,
                q_ref, k_ref, v_ref):
    x = x_ref[...]
    half = rope // 2
    trig = trig_ref[...]
    lane = lax.broadcasted_iota(I32, trig.shape, 1)
    cc = jnp.where(lane < half, trig,
                   jnp.where(lane < rope, pltpu.roll(trig, half, axis=1), 0.0))
    ss = jnp.where(lane < half, -pltpu.roll(trig, LANES - half, axis=1),
                   jnp.where(lane < rope, trig, 0.0))
    r_kv = ckvg_ref.shape[1]
    hk = _rms(x, kvg_ref[...]).astype(BF16)
    lat = jnp.dot(hk, wdkv_ref[...], preferred_element_type=F32)
    ckv = _rms(lat[:, :r_kv], ckvg_ref[...]).astype(BF16)
    krot = lat[:, r_kv:r_kv + LANES] * cc + lat[:, r_kv + LANES:r_kv + 2 * LANES] * ss
    kn = jnp.dot(ckv, wuk_ref[...], preferred_element_type=F32)
    vt = lax.dot_general(wuv_ref[...], ckv, (((1,), (1,)), ((), ())), preferred_element_type=F32)
    hq = (_rms(x, n1g_ref[...]) * (1.0 + sc1_ref[...]) + sh1_ref[...]).astype(BF16)
    cq = _rms(jnp.dot(hq, wdq_ref[...], preferred_element_type=F32), cqg_ref[...]).astype(BF16)
    qq = jnp.dot(cq, wuq_ref[...], preferred_element_type=F32) * scale
    hd = n_heads * LANES
    ts = x.shape[0]
    ones_rows = (lax.broadcasted_iota(I32, (V_ONES_ROWS, ts), 0) == 0).astype(BF16)
    for h in range(n_heads):
        sl = slice(h * LANES, (h + 1) * LANES)
        k_ref[h, :, 0:LANES] = kn[:, sl].astype(BF16)
        k_ref[h, :, LANES:2 * LANES] = krot.astype(BF16)
        v_ref[h, 0:LANES, :] = vt[h * LANES:(h + 1) * LANES, :].astype(BF16)
        v_ref[h, LANES:LANES + V_ONES_ROWS, :] = ones_rows
        q_ref[h, :, 0:LANES] = qq[:, sl].astype(BF16)
    per_tile = LANES // rope
    cc_rep, ss_rep = cc, ss
    for u in range(1, per_tile):
        cc_rep = cc_rep + pltpu.roll(cc, u * rope, axis=1)
        ss_rep = ss_rep + pltpu.roll(ss, u * rope, axis=1)
    in_head = lax.broadcasted_iota(I32, (ts, LANES), 1) < rope
    n_rope = n_heads * rope
    for j in range(n_heads // per_tile):
        cols = slice(hd + j * LANES, hd + (j + 1) * LANES)
        swapped = slice(hd + n_rope + j * LANES, hd + n_rope + (j + 1) * LANES)
        rot = qq[:, cols] * cc_rep + qq[:, swapped] * ss_rep
        for u in range(per_tile):
            piece = rot if u == 0 else pltpu.roll(rot, LANES - u * rope, axis=1)
            q_ref[j * per_tile + u, :, LANES:2 * LANES] = jnp.where(in_head, piece, 0.0).astype(BF16)


def _qkv(x, trig, kvg, wdkv, ckvg, wuk, wuv, n1g, sh1, sc1, wdq, cqg, wuq, n_heads, rope, scale, ts):
    b, s, d = x.shape
    assert LANES % rope == 0 and n_heads % (LANES // rope) == 0
    full = lambda a: pl.BlockSpec(a.shape, lambda i, j: (0,) * a.ndim)
    hspec = lambda w: pl.BlockSpec((None, n_heads, ts, w), lambda i, j: (i, 0, j, 0))
    return pl.pallas_call(
        functools.partial(_qkv_kernel, n_heads, rope, scale),
        out_shape=[
            jax.ShapeDtypeStruct((b, n_heads, s, 2 * LANES), BF16),
            jax.ShapeDtypeStruct((b, n_heads, s, 2 * LANES), BF16),
            jax.ShapeDtypeStruct((b, n_heads, s // ts, LANES + V_ONES_ROWS, ts), BF16),
        ],
        grid=(b, s // ts),
        in_specs=[
            pl.BlockSpec((None, ts, d), lambda i, j: (i, j, 0)),
            pl.BlockSpec((None, ts, LANES), lambda i, j: (i, j, 0)),
            full(kvg), full(wdkv), full(ckvg), full(wuk), full(wuv),
            full(n1g), _bvec_spec(d), _bvec_spec(d), full(wdq), full(cqg), full(wuq),
        ],
        out_specs=[hspec(2 * LANES), hspec(2 * LANES),
                   pl.BlockSpec((None, n_heads, None, LANES + V_ONES_ROWS, ts),
                                lambda i, j: (i, 0, j, 0, 0))],
        compiler_params=_cparams(("arbitrary", "arbitrary")),
        name="mla_qkv",
    )(x, trig, kvg, wdkv, ckvg, wuk, wuv, n1g, sh1, sc1, wdq, cqg, wuq)


def _attn_kernel(q_ref, k_ref, vt_ref, o_ref, m_ref, acc_ref):
    tq = q_ref.shape[0]
    tk = vt_ref.shape[2]
    ratio = tq // tk
    dv = o_ref.shape[1]
    qi = pl.program_id(2)
    m_ref[...] = jnp.full(m_ref.shape, -jnp.inf, F32)
    acc_ref[...] = jnp.zeros(acc_ref.shape, F32)

    def scores(ki, cols, on_diagonal):
        start = pl.multiple_of(ki * tk, tk)
        k = k_ref[pl.ds(start, tk), :]
        st = lax.dot_general(k, q_ref[cols, :], (((1,), (1,)), ((), ())),
                             preferred_element_type=F32)
        if on_diagonal:
            kc = lax.broadcasted_iota(I32, st.shape, 0) // CHUNK
            qc = lax.broadcasted_iota(I32, st.shape, 1) // CHUNK
            st = jnp.where(kc <= qc, st, -jnp.inf)
        return st

    def update(ki, st, m_old, acc_old):
        m_new = jnp.maximum(m_old, jnp.max(st, axis=0, keepdims=True))
        p = jnp.exp2(st - m_new).astype(BF16)
        alpha = jnp.exp2(m_old - m_new)
        acc_new = alpha * acc_old + jnp.dot(vt_ref[ki], p, preferred_element_type=F32)
        return m_new, acc_new

    def step(groups):
        sts = [[scores(ki, cols, on_diagonal) for ki, on_diagonal in tiles]
               for cols, tiles in groups]
        for (cols, tiles), group_sts in zip(groups, sts):
            state = (m_ref[:, cols], acc_ref[:, cols])
            for (ki, _), st in zip(tiles, group_sts):
                state = update(ki, st, *state)
            m_ref[:, cols], acc_ref[:, cols] = state

    n_below = qi * ratio
    everything = slice(0, tq)
    step([(slice(g * tk, (g + 1) * tk), [(n_below + a, a == g) for a in range(g + 1)])
          for g in range(ratio)])

    def fast_step(first, count):
        m = m_ref[...]
        pv = None
        top = None
        for t in range(count):
            st = scores(first + t, everything, False)
            tmax = jnp.max(st, axis=0, keepdims=True)
            top = tmax if top is None else jnp.maximum(top, tmax)
            p = jnp.exp2(st - m).astype(BF16)
            part = jnp.dot(vt_ref[first + t], p, preferred_element_type=F32)
            pv = part if pv is None else pv + part
        safe = jnp.max(top - m) <= ATTN_MAX_SLACK

        @pl.when(safe)
        def _():
            acc_ref[...] += pv

        @pl.when(jnp.logical_not(safe))
        def _():
            def redo(t, carry):
                step([(everything, [(first + t, False)])])
                return carry
            lax.fori_loop(0, count, redo, 0)

    def body(j, carry):
        fast_step(ATTN_UNROLL * j, ATTN_UNROLL)
        return carry
    trips = n_below // ATTN_UNROLL
    lax.fori_loop(0, trips, body, 0)

    for rem in range(math.gcd(ratio, ATTN_UNROLL), ATTN_UNROLL, math.gcd(ratio, ATTN_UNROLL)):
        @pl.when(n_below % ATTN_UNROLL == rem)
        def _(rem=rem):
            fast_step(trips * ATTN_UNROLL, rem)

    o_ref[...] = (acc_ref[0:dv, :] / acc_ref[dv:dv + 1, :]).T.astype(o_ref.dtype)


def _attention(q, k, vt, tq):
    b, h, s, dk = q.shape
    nk, dv_ext, tk = vt.shape[2:]
    dv = dv_ext - V_ONES_ROWS
    assert tk % CHUNK == 0 and s % tq == 0 and tq % tk == 0
    return pl.pallas_call(
        _attn_kernel,
        out_shape=jax.ShapeDtypeStruct((b, s, h * dv), BF16),
        grid=(b, h, s // tq),
        in_specs=[
            pl.BlockSpec((None, None, tq, dk), lambda i, j, t: (i, j, t, 0)),
            pl.BlockSpec((None, None, s, dk), lambda i, j, t: (i, j, 0, 0)),
            pl.BlockSpec((None, None, nk, dv_ext, tk), lambda i, j, t: (i, j, 0, 0, 0)),
        ],
        out_specs=pl.BlockSpec((None, tq, dv), lambda i, j, t: (i, t, j)),
        scratch_shapes=[pltpu.VMEM((1, tq), F32), pltpu.VMEM((dv_ext, tq), F32)],
        compiler_params=_cparams(("arbitrary", "arbitrary", "arbitrary")),
        name="mla_attention",
    )(q, k, vt)


def _swap_halves(w):
    half = w.shape[-1] // 2
    return jnp.concatenate([w[..., half:], w[..., :half]], axis=-1)


def _pad_lanes(w):
    pad = LANES - w.shape[-1]
    return jnp.concatenate([w, jnp.zeros(w.shape[:-1] + (pad,), w.dtype)], axis=-1)


def kernel(x, c, positions, mod_w, mod_b, norm1_g, norm2_g, conv_pw1_w, conv_pw1_b, conv_dw_w, conv_dw_b, conv_ln_g, conv_ln_b, conv_pw2_w, conv_pw2_b, kv_norm_g, w_dkv, ckv_norm_g, w_uk, w_uv, w_dq, cq_norm_g, w_uq, w_o, router_w, router_b, exp_w_gu, exp_b_gu, exp_w_dn, exp_b_dn, final_g):
    b, s, d = x.shape
    n_heads, nope = w_uk.shape[1], w_uk.shape[2]
    r_kv = ckv_norm_g.shape[0]
    rope = w_dkv.shape[1] - r_kv
    vdim = w_uv.shape[2]
    n_exp = router_w.shape[2]
    assert nope == LANES and vdim == LANES and rope <= LANES and d % LANES == 0
    ts = min(512, s)
    tq = min(1024, s)

    mod = _modulation(c, mod_w, mod_b)
    mods = [[m.reshape(b, 1, d) for m in jnp.split(mod[l], 6, axis=-1)] for l in range(2)]
    row = lambda v: v.reshape(1, -1)

    half = rope // 2
    inv = jnp.exp(-(2.0 * math.log(ROPE_THETA) / rope) * jnp.arange(half, dtype=F32))
    ang = positions.astype(F32)[..., None] * inv
    trig = _pad_lanes(jnp.concatenate([jnp.cos(ang), jnp.sin(ang)], axis=-1))

    sh1, sc1, gt1, sh2, sc2, gt2 = mods[0]
    x1, hn2, destt, route, cnt = _mix0(
        x, row(norm1_g[0]), sh1, sc1, gt1,
        conv_pw1_w[0].astype(BF16), row(conv_pw1_b[0]), conv_dw_w[0], row(conv_dw_b[0]),
        row(conv_ln_g[0]), row(conv_ln_b[0]), conv_pw2_w[0].astype(BF16), row(conv_pw2_b[0]),
        row(norm2_g[0]), sh2, sc2, router_w[0].T, router_b[0].reshape(n_exp, 1), ts)
    cnt = cnt[:, :ts // MOE_TILE].reshape(-1, n_exp)
    x2 = _moe(x1, hn2, destt, route, cnt, gt2, row(final_g),
              0, exp_w_gu, exp_b_gu, exp_w_dn, exp_b_dn, False)

    sh1, sc1, gt1, sh2, sc2, gt2 = mods[1]
    wdkv_rope = w_dkv[:, r_kv:]
    wdkv_ext = jnp.concatenate(
        [w_dkv[:, :r_kv], _pad_lanes(wdkv_rope), _pad_lanes(_swap_halves(wdkv_rope))], axis=-1)
    wuq = w_uq[0]
    r_q = wuq.shape[0]
    wuq_rope = wuq[:, :, nope:]
    wuq_ext = jnp.concatenate([
        wuq[:, :, :nope].reshape(r_q, n_heads * LANES),
        wuq_rope.reshape(r_q, n_heads * rope),
        _swap_halves(wuq_rope).reshape(r_q, n_heads * rope)], axis=-1)
    scale = float((nope + rope) ** -0.5 * math.log2(math.e))
    q, k, v = _qkv(
        x2, trig, row(kv_norm_g), wdkv_ext.astype(BF16), row(ckv_norm_g),
        w_uk.reshape(r_kv, n_heads * nope).astype(BF16), w_uv.reshape(r_kv, n_heads * vdim).T.astype(BF16),
        row(norm1_g[1]), sh1, sc1, w_dq[0].astype(BF16), row(cq_norm_g[0]), wuq_ext.astype(BF16),
        n_heads, rope, scale, ts)
    o = _attention(q, k, v, tq)

    x3, hn2, destt, route, cnt = _mix1(
        x2, o, gt1, w_o[0].astype(BF16), row(norm2_g[1]), sh2, sc2,
        router_w[1].T, router_b[1].reshape(n_exp, 1), ts)
    cnt = cnt[:, :ts // MOE_TILE].reshape(-1, n_exp)
    return _moe(x3, hn2, destt, route, cnt, gt2, row(final_g),
                1, exp_w_gu, exp_b_gu, exp_w_dn, exp_b_dn, True)
```

```python
import functools
import math

import jax
import jax.numpy as jnp
from jax import lax
from jax.experimental import pallas as pl
from jax.experimental.pallas import tpu as pltpu

CHUNK = 64
TOP_K = 4
ROPE_THETA = 10000.0
SWIGLU_ALPHA = 1.702
SWIGLU_LIMIT = 7.0
EPS = 1e-6

LANES = 128
SUBLANES = 8
VMEM_LIMIT_BYTES = 56 * 1024 * 1024

ROW_ALIGN = SUBLANES
MOE_TILE = 256
EXPERT_ROWS = 512
CONV_HALO = 32
V_ONES_ROWS = 2 * SUBLANES
ATTN_UNROLL = 4
DMA_PRIORITIES = 2
ATTN_MAX_SLACK = 60.0

F32 = jnp.float32
BF16 = jnp.bfloat16
I32 = jnp.int32


def _cparams(sem):
    return pltpu.CompilerParams(dimension_semantics=sem, vmem_limit_bytes=VMEM_LIMIT_BYTES)


def _rms(x, g):
    return x * lax.rsqrt(jnp.mean(x * x, axis=-1, keepdims=True) + EPS) * g


def _round_up(a, m):
    return (a + m - 1) // m * m


def _mod_kernel(c_ref, w_ref, b_ref, o_ref):
    c = c_ref[...]
    ca = c * jax.nn.sigmoid(c)
    o_ref[...] = jnp.dot(ca, w_ref[...], preferred_element_type=F32) + b_ref[...]


def _modulation(c, mod_w, mod_b):
    depth, d, d6 = mod_w.shape
    b = c.shape[0]
    bp = _round_up(b, SUBLANES)
    cp = jnp.zeros((bp, d), F32).at[:b].set(c)
    tn = d6 // 4
    out = pl.pallas_call(
        _mod_kernel,
        out_shape=jax.ShapeDtypeStruct((depth, bp, d6), F32),
        grid=(depth, d6 // tn),
        in_specs=[
            pl.BlockSpec((bp, d), lambda l, j: (0, 0)),
            pl.BlockSpec((None, d, tn), lambda l, j: (l, 0, j)),
            pl.BlockSpec((None, 1, tn), lambda l, j: (l, 0, j)),
        ],
        out_specs=pl.BlockSpec((None, bp, tn), lambda l, j: (l, 0, j)),
        compiler_params=_cparams(("arbitrary", "arbitrary")),
        name="adaln_modulation",
    )(cp, mod_w, mod_b.reshape(depth, 1, d6))
    return out[:, :b]


def _pre_moe(x1, g2, sh2, sc2, rwt, rb, earlier, hn_ref, destt_ref, route_ref, cnt_ref):
    ts = x1.shape[0]
    n_exp = rwt.shape[0]
    hn = _rms(x1, g2) * (1.0 + sc2) + sh2
    hn_ref[...] = hn.astype(BF16)
    def split(v):
        hi = v.astype(BF16)
        return hi, (v - hi.astype(F32)).astype(BF16)
    nt = lambda a, b: lax.dot_general(a, b, (((1,), (1,)), ((), ())), preferred_element_type=F32)
    w_hi, w_lo = split(rwt)
    h_hi, h_lo = split(hn)
    logits = nt(w_hi, h_hi) + (nt(w_hi, h_lo) + nt(w_lo, h_hi)) + rb
    e_iota = lax.broadcasted_iota(I32, (n_exp, ts), 0)
    vals, idxs = [], []
    cur = logits
    for _ in range(TOP_K):
        m = jnp.max(cur, axis=0, keepdims=True)
        i = jnp.min(jnp.where(cur == m, e_iota, n_exp), axis=0, keepdims=True)
        vals.append(m)
        idxs.append(i)
        cur = jnp.where(e_iota == i, -jnp.inf, cur)
    exps = [jnp.exp(v - vals[0]) for v in vals]
    den = exps[0]
    for e in exps[1:]:
        den = den + e
    gates = [e / den for e in exps]
    hits = [e_iota == i for i in idxs]
    onehot = jnp.zeros((n_exp, ts), F32)
    for h in hits:
        onehot = onehot + h.astype(F32)
    rank = jnp.dot(onehot.astype(BF16), earlier, preferred_element_type=F32)
    lower = (lax.broadcasted_iota(I32, (n_exp, n_exp), 1)
             < lax.broadcasted_iota(I32, (n_exp, n_exp), 0)).astype(BF16)
    dests = [[] for _ in range(TOP_K)]
    for t0 in range(0, ts, MOE_TILE):
        cols = slice(t0, t0 + MOE_TILE)
        cnt = jnp.sum(onehot[:, cols], axis=1, keepdims=True)
        cnt8 = jnp.ceil(cnt / ROW_ALIGN) * ROW_ALIGN
        off = jnp.dot(lower, jnp.broadcast_to(cnt8, (n_exp, LANES)).astype(BF16),
                      preferred_element_type=F32)[:, 0:1]
        base = off + rank[:, cols]
        for k, h in enumerate(hits):
            dests[k].append(jnp.sum(jnp.where(h[:, cols], base, 0.0), axis=0, keepdims=True))
    dests = [jnp.concatenate(parts, axis=1) for parts in dests]
    destt_ref[...] = jnp.concatenate(dests, axis=0).astype(I32)
    rows = jnp.concatenate(dests + gates + [jnp.zeros((LANES - 2 * TOP_K, ts), F32)], axis=0)
    route_ref[...] = rows.T
    sel = (lax.broadcasted_iota(I32, (SUBLANES, ts), 1) // MOE_TILE
           == lax.broadcasted_iota(I32, (SUBLANES, ts), 0)).astype(BF16)
    cnt = lax.dot_general(sel, onehot.astype(BF16), (((1,), (1,)), ((), ())),
                          preferred_element_type=F32)
    cnt_ref[...] = cnt.astype(I32)


def _earlier_in_tile(ts):
    t = jnp.arange(ts, dtype=I32)
    same_tile = (t[:, None] // MOE_TILE) == (t[None, :] // MOE_TILE)
    return ((t[:, None] < t[None, :]) & same_tile).astype(BF16)


def _pre_moe_specs(b, s, d, ts, n_exp):
    ns = s // ts
    out_shape = [
        jax.ShapeDtypeStruct((b, s, d), F32),
        jax.ShapeDtypeStruct((b, s, d), BF16),
        jax.ShapeDtypeStruct((TOP_K, b * s), I32),
        jax.ShapeDtypeStruct((b * s, LANES), F32),
        jax.ShapeDtypeStruct((b * ns, SUBLANES, n_exp), I32),
    ]
    out_specs = [
        pl.BlockSpec((None, ts, d), lambda i, j: (i, j, 0)),
        pl.BlockSpec((None, ts, d), lambda i, j: (i, j, 0)),
        pl.BlockSpec((TOP_K, ts), lambda i, j: (0, i * ns + j)),
        pl.BlockSpec((ts, LANES), lambda i, j: (i * ns + j, 0)),
        pl.BlockSpec((None, SUBLANES, n_exp), lambda i, j: (i * ns + j, 0, 0)),
    ]
    return out_shape, out_specs


def _vec_spec(d):
    return pl.BlockSpec((1, d), lambda i, j: (0, 0))


def _bvec_spec(d):
    return pl.BlockSpec((None, 1, d), lambda i, j: (i, 0, 0))


def _mix0_kernel(x_ref, n1g_ref, sh1_ref, sc1_ref, gt1_ref,
                 pw1w_ref, pw1b_ref, dww_ref, dwb_ref, lng_ref, lnb_ref,
                 pw2w_ref, pw2b_ref, n2g_ref, sh2_ref, sc2_ref, rwt_ref, rb_ref, earlier_ref,
                 x1_ref, hn_ref, destt_ref, route_ref, cnt_ref, buf_ref, conv_ref, shift_ref):
    ts, d = x_ref.shape
    width = dww_ref.shape[0]
    x = x_ref[...]
    hn = _rms(x, n1g_ref[...]) * (1.0 + sc1_ref[...]) + sh1_ref[...]
    hb = hn.astype(BF16)
    a = jnp.dot(hb, pw1w_ref[:, :d], preferred_element_type=F32) + pw1b_ref[:, :d]
    g = jnp.dot(hb, pw1w_ref[:, d:], preferred_element_type=F32) + pw1b_ref[:, d:]
    glu = a * jax.nn.sigmoid(g)

    @pl.when(pl.program_id(1) == 0)
    def _():
        buf_ref[0:CONV_HALO, :] = jnp.zeros((CONV_HALO, d), F32)

    buf_ref[CONV_HALO:, :] = glu
    base = CONV_HALO - (width - 1)
    rc = 32
    lc = min(512, d)
    sh_rows = shift_ref.shape[1]
    sub = lax.broadcasted_iota(I32, (SUBLANES, lc), 0)
    for c0 in range(0, d, lc):
        for r in range(1, SUBLANES):
            from_this = sub < SUBLANES - r
            cur = pltpu.roll(buf_ref[0:SUBLANES, c0:c0 + lc], SUBLANES - r, axis=0)
            for j0 in range(0, sh_rows, SUBLANES):
                nxt = pltpu.roll(buf_ref[j0 + SUBLANES:j0 + 2 * SUBLANES, c0:c0 + lc],
                                 SUBLANES - r, axis=0)
                shift_ref[r - 1, j0:j0 + SUBLANES, :] = jnp.where(from_this, cur, nxt)
                cur = nxt
        for r0 in range(0, ts, rc):
            acc = jnp.zeros((rc, lc), F32)
            for k in range(width):
                q8, r = divmod(base + k, SUBLANES)
                lo = q8 * SUBLANES + r0
                if r == 0:
                    win = buf_ref[lo:lo + rc, c0:c0 + lc]
                else:
                    win = shift_ref[r - 1, lo:lo + rc, :]
                acc = acc + dww_ref[k:k + 1, c0:c0 + lc] * win
            conv_ref[r0:r0 + rc, c0:c0 + lc] = acc
    buf_ref[0:CONV_HALO, :] = buf_ref[ts:ts + CONV_HALO, :]
    u = conv_ref[...] + dwb_ref[...]
    mu = jnp.mean(u, axis=-1, keepdims=True)
    dlt = u - mu
    var = jnp.mean(dlt * dlt, axis=-1, keepdims=True)
    u = dlt * lax.rsqrt(var + EPS) * lng_ref[...] + lnb_ref[...]
    u = u * jax.nn.sigmoid(u)
    y = jnp.dot(u.astype(BF16), pw2w_ref[...], preferred_element_type=F32) + pw2b_ref[...]
    x1 = x + gt1_ref[...] * y
    x1_ref[...] = x1
    _pre_moe(x1, n2g_ref[...], sh2_ref[...], sc2_ref[...], rwt_ref[...], rb_ref[...],
             earlier_ref[...], hn_ref, destt_ref, route_ref, cnt_ref)


def _mix0(x, n1g, sh1, sc1, gt1, pw1w, pw1b, dww, dwb, lng, lnb, pw2w, pw2b,
          n2g, sh2, sc2, rwt, rb, ts):
    b, s, d = x.shape
    n_exp = rwt.shape[0]
    width = dww.shape[0]
    assert width - 1 <= CONV_HALO and ts % MOE_TILE == 0 and ts // MOE_TILE <= SUBLANES
    out_shape, out_specs = _pre_moe_specs(b, s, d, ts, n_exp)
    full = lambda shp: pl.BlockSpec(shp, lambda i, j: (0,) * len(shp))
    return pl.pallas_call(
        _mix0_kernel,
        out_shape=out_shape,
        grid=(b, s // ts),
        in_specs=[
            pl.BlockSpec((None, ts, d), lambda i, j: (i, j, 0)),
            _vec_spec(d), _bvec_spec(d), _bvec_spec(d), _bvec_spec(d),
            full((d, 2 * d)), full((1, 2 * d)), full((width, d)), full((1, d)),
            full((1, d)), full((1, d)), full((d, d)), full((1, d)),
            _vec_spec(d), _bvec_spec(d), _bvec_spec(d),
            full((n_exp, d)), full((n_exp, 1)), full((ts, ts)),
        ],
        out_specs=out_specs,
        scratch_shapes=[
            pltpu.VMEM((CONV_HALO + ts, d), F32),
            pltpu.VMEM((ts, d), F32),
            pltpu.VMEM((SUBLANES - 1, CONV_HALO + ts - SUBLANES, min(512, d)), F32),
        ],
        compiler_params=_cparams(("arbitrary", "arbitrary")),
        name="conformer_mixer",
    )(x, n1g, sh1, sc1, gt1, pw1w, pw1b, dww, dwb, lng, lnb, pw2w, pw2b,
      n2g, sh2, sc2, rwt, rb, _earlier_in_tile(ts))


def _mix1_kernel(x_ref, o_ref, gt1_ref, wo_ref, n2g_ref, sh2_ref, sc2_ref, rwt_ref, rb_ref,
                 earlier_ref, x1_ref, hn_ref, destt_ref, route_ref, cnt_ref):
    y = jnp.dot(o_ref[...], wo_ref[...], preferred_element_type=F32)
    x1 = x_ref[...] + gt1_ref[...] * y
    x1_ref[...] = x1
    _pre_moe(x1, n2g_ref[...], sh2_ref[...], sc2_ref[...], rwt_ref[...], rb_ref[...],
             earlier_ref[...], hn_ref, destt_ref, route_ref, cnt_ref)


def _mix1(x, o, gt1, wo, n2g, sh2, sc2, rwt, rb, ts):
    b, s, d = x.shape
    n_exp = rwt.shape[0]
    do = o.shape[-1]
    assert ts % MOE_TILE == 0 and ts // MOE_TILE <= SUBLANES
    out_shape, out_specs = _pre_moe_specs(b, s, d, ts, n_exp)
    full = lambda shp: pl.BlockSpec(shp, lambda i, j: (0,) * len(shp))
    return pl.pallas_call(
        _mix1_kernel,
        out_shape=out_shape,
        grid=(b, s // ts),
        in_specs=[
            pl.BlockSpec((None, ts, d), lambda i, j: (i, j, 0)),
            pl.BlockSpec((None, ts, do), lambda i, j: (i, j, 0)),
            _bvec_spec(d), full((do, d)),
            _vec_spec(d), _bvec_spec(d), _bvec_spec(d),
            full((n_exp, d)), full((n_exp, 1)), full((ts, ts)),
        ],
        out_specs=out_specs,
        compiler_params=_cparams(("arbitrary", "arbitrary")),
        name="attn_out_mixer",
    )(x, o, gt1, wo, n2g, sh2, sc2, rwt, rb, _earlier_in_tile(ts))


def _moe_dims(n_tok, n_exp):
    n_tiles = n_tok // MOE_TILE
    loc_rows = _round_up(TOP_K * MOE_TILE + n_exp * (ROW_ALIGN - 1), LANES)
    max_rows = TOP_K * n_tok + n_tiles * n_exp * (ROW_ALIGN - 1) + n_exp * (EXPERT_ROWS - ROW_ALIGN)
    max_blocks = -(-max_rows // EXPERT_ROWS)
    return loc_rows, max_blocks


def _routing_tables(cnt, max_blocks):
    n_tiles, n_exp = cnt.shape
    cnt8 = _round_up(cnt, ROW_ALIGN)
    off = jnp.cumsum(cnt8, axis=1) - cnt8
    seg_len = cnt8.sum(axis=0)
    seg_pad = _round_up(seg_len, EXPERT_ROWS)
    seg_end = jnp.cumsum(seg_pad)
    seg_start = seg_end - seg_pad
    run_start = seg_start[None, :] + jnp.cumsum(cnt8, axis=0) - cnt8
    n_blocks = (seg_end[-1] // EXPERT_ROWS).astype(I32)
    blk_row = jnp.arange(max_blocks, dtype=I32) * EXPERT_ROWS
    blk_row = jnp.minimum(blk_row, seg_end[-1] - EXPERT_ROWS)
    blk_exp = jnp.minimum((blk_row[:, None] >= seg_end[None, :]).sum(axis=-1), n_exp - 1).astype(I32)
    blk = jnp.arange(max_blocks, dtype=I32)
    prev_exp = jnp.concatenate([jnp.full((1,), -1, I32), blk_exp[:-1]])
    blk_first = ((blk_exp != prev_exp) & (blk < n_blocks)).astype(I32)
    blk_slot = ((jnp.cumsum(blk_first) - 1) % 2).astype(I32)
    e_ids = jnp.arange(n_exp, dtype=I32)
    later = (e_ids[None, :] > e_ids[:, None]) & (seg_pad[None, :] > 0)
    next_exp = jnp.min(jnp.where(later, e_ids[None, :], n_exp), axis=1)
    next_exp = jnp.where(next_exp < n_exp, next_exp, -1)
    blk_next = jnp.sum(jnp.where(blk_exp[:, None] == e_ids[None, :], next_exp[None, :], 0), axis=1)
    flat = lambda a: a.reshape(-1).astype(I32)
    return dict(
        run_loc=flat(off),
        run_glb=flat(run_start),
        run_len=flat(cnt8),
        rows_used=flat(cnt8.sum(axis=1)),
        tail_glb=flat(seg_start + seg_len),
        tail_len=flat(seg_pad - seg_len),
        blk_exp=blk_exp, blk_first=blk_first, blk_slot=blk_slot, blk_next=flat(blk_next),
        n_blocks=n_blocks.reshape(1))


def _aligned(v):
    return pl.multiple_of(v, ROW_ALIGN)


def _dispatch_kernel(n_exp, loc0_ref, glb0_ref, len_ref, used_ref, tail0_ref, taillen_ref, nblk_ref,
                     x_ref, destt_ref, xs_ref, loc_ref, zero_ref, sem, zsem):
    i = pl.program_id(0)
    n = pl.num_programs(0)
    slot = i % 2
    tile = x_ref.shape[0]
    loc_rows = loc_ref.shape[1]

    def wait_slot(sl, rows):
        rows = _aligned(rows)

        @pl.when(rows > 0)
        def _():
            pltpu.make_async_copy(loc_ref.at[sl, pl.ds(0, rows)], xs_ref.at[pl.ds(0, rows)],
                                  sem.at[sl]).wait()

    @pl.when(i >= 2)
    def _():
        wait_slot(slot, used_ref[i - 2])

    max_blocks = xs_ref.shape[0] // EXPERT_ROWS

    @pl.when(i == 0)
    def _():
        zero_ref[...] = jnp.zeros(zero_ref.shape, F32)

        def zero_tail(e, carry):
            rows = _aligned(taillen_ref[e])

            @pl.when(rows > 0)
            def _():
                pltpu.make_async_copy(zero_ref.at[pl.ds(0, rows)],
                                      xs_ref.at[pl.ds(_aligned(tail0_ref[e]), rows)],
                                      zsem.at[0]).start()
            return carry
        lax.fori_loop(0, n_exp, zero_tail, 0)

        def zero_block(blk, carry):
            pltpu.make_async_copy(
                zero_ref,
                xs_ref.at[pl.ds(pl.multiple_of(blk * EXPERT_ROWS, EXPERT_ROWS), EXPERT_ROWS)],
                zsem.at[1]).start()
            return carry
        lax.fori_loop(nblk_ref[0], max_blocks, zero_block, 0)

    r_iota = lax.broadcasted_iota(I32, (loc_rows, tile), 0)
    perm = jnp.zeros((loc_rows, tile), F32)
    for k in range(TOP_K):
        perm = jnp.where(r_iota == destt_ref[k:k + 1, :], 1.0, perm)
    loc_ref[slot] = jnp.dot(perm.astype(BF16), x_ref[...], preferred_element_type=F32)

    def issue_run(e, priority):
        rows = _aligned(len_ref[i * n_exp + e])

        @pl.when(rows > 0)
        def _():
            pltpu.make_async_copy(
                loc_ref.at[slot, pl.ds(_aligned(loc0_ref[i * n_exp + e]), rows)],
                xs_ref.at[pl.ds(_aligned(glb0_ref[i * n_exp + e]), rows)],
                sem.at[slot]).start(priority=priority)

    def issue_pair(j, carry):
        for priority in range(DMA_PRIORITIES):
            issue_run(j * DMA_PRIORITIES + priority, priority)
        return carry
    lax.fori_loop(0, n_exp // DMA_PRIORITIES, issue_pair, 0)

    @pl.when(i == n - 1)
    def _():
        tail_rows = lax.fori_loop(0, n_exp, lambda e, total: total + taillen_ref[e], jnp.int32(0))

        @pl.when(i >= 1)
        def _():
            wait_slot(1 - slot, used_ref[i - 1])
        wait_slot(slot, used_ref[i])

        def wait_rows(rows, s):
            rows = _aligned(rows)

            @pl.when(rows > 0)
            def _():
                pltpu.make_async_copy(xs_ref.at[pl.ds(0, rows)], xs_ref.at[pl.ds(0, rows)],
                                      zsem.at[s]).wait()
        wait_rows(tail_rows, 0)
        wait_rows((max_blocks - nblk_ref[0]) * EXPERT_ROWS, 1)


def _dispatch(hn2, destt, tabs, n_exp, loc_rows, max_blocks):
    n_tok, d = hn2.shape
    n_tiles = n_tok // MOE_TILE
    return pl.pallas_call(
        functools.partial(_dispatch_kernel, n_exp),
        out_shape=jax.ShapeDtypeStruct((max_blocks * EXPERT_ROWS, d), F32),
        grid_spec=pltpu.PrefetchScalarGridSpec(
            num_scalar_prefetch=7,
            grid=(n_tiles,),
            in_specs=[
                pl.BlockSpec((MOE_TILE, d), lambda i, *_: (i, 0)),
                pl.BlockSpec((TOP_K, MOE_TILE), lambda i, *_: (0, i)),
            ],
            out_specs=pl.BlockSpec(memory_space=pl.ANY),
            scratch_shapes=[
                pltpu.VMEM((2, loc_rows, d), F32),
                pltpu.VMEM((EXPERT_ROWS, d), F32),
                pltpu.SemaphoreType.DMA((2,)),
                pltpu.SemaphoreType.DMA((2,)),
            ],
        ),
        compiler_params=_cparams(("arbitrary",)),
        name="moe_dispatch",
    )(tabs["run_loc"], tabs["run_glb"], tabs["run_len"], tabs["rows_used"],
      tabs["tail_glb"], tabs["tail_len"], tabs["n_blocks"], hn2, destt)


def _expert_kernel(layer, be_ref, first_ref, slot_ref, next_ref, nb_ref,
                   x_ref, wgu_hbm, bgu_ref, wdn_hbm, bdn_ref, y_ref,
                   wgu_f32, wdn_f32, wgu_bf, wdn_bf, sem):
    de = wdn_bf.shape[0]
    b = pl.program_id(0)

    def weight_copies(e, s):
        return (pltpu.make_async_copy(wgu_hbm.at[layer, e], wgu_f32.at[s], sem.at[s, 0]),
                pltpu.make_async_copy(wdn_hbm.at[layer, e], wdn_f32.at[s], sem.at[s, 1]))

    @pl.when(b == 0)
    def _():
        for cp in weight_copies(be_ref[0], 0):
            cp.start()

    @pl.when(first_ref[b] == 1)
    def _():
        s = slot_ref[b]
        for cp in weight_copies(be_ref[b], s):
            cp.wait()

        @pl.when(next_ref[b] >= 0)
        def _():
            for cp in weight_copies(next_ref[b], 1 - s):
                cp.start()
        cw = 512
        for c0 in range(0, wgu_bf.shape[1], cw):
            wgu_bf[:, c0:c0 + cw] = wgu_f32[s, :, c0:c0 + cw].astype(BF16)
        for c0 in range(0, wdn_bf.shape[1], cw):
            wdn_bf[:, c0:c0 + cw] = wdn_f32[s, :, c0:c0 + cw].astype(BF16)

    @pl.when(b < nb_ref[0])
    def _():
        xb = x_ref[...].astype(BF16)
        g = jnp.dot(xb, wgu_bf[:, :de], preferred_element_type=F32) + bgu_ref[:, :de]
        u = jnp.dot(xb, wgu_bf[:, de:], preferred_element_type=F32) + bgu_ref[:, de:]
        g = jnp.minimum(g, SWIGLU_LIMIT)
        u = jnp.clip(u, -SWIGLU_LIMIT, SWIGLU_LIMIT)
        a = (u + 1.0) * g * jax.nn.sigmoid(SWIGLU_ALPHA * g)
        y_ref[...] = jnp.dot(a.astype(BF16), wdn_bf[...], preferred_element_type=F32) + bdn_ref[...]

    @pl.when(pl.program_id(0) >= nb_ref[0])
    def _():
        y_ref[...] = jnp.zeros(y_ref.shape, y_ref.dtype)


def _experts(xs, tabs, layer, wgu, bgu, wdn, bdn):
    rows, d = xs.shape
    depth, n_exp, _, de2 = wgu.shape
    de = de2 // 2
    max_blocks = rows // EXPERT_ROWS
    row_map = lambda b, be, fi, sl, nx, nb: (jnp.minimum(b, nb[0] - 1), 0)
    exp_map = lambda b, be, fi, sl, nx, nb: (layer, be[b], 0, 0)
    return pl.pallas_call(
        functools.partial(_expert_kernel, layer),
        out_shape=jax.ShapeDtypeStruct((rows, d), F32),
        grid_spec=pltpu.PrefetchScalarGridSpec(
            num_scalar_prefetch=5,
            grid=(max_blocks,),
            in_specs=[
                pl.BlockSpec((EXPERT_ROWS, d), row_map),
                pl.BlockSpec(memory_space=pl.ANY),
                pl.BlockSpec((None, None, 1, de2), exp_map),
                pl.BlockSpec(memory_space=pl.ANY),
                pl.BlockSpec((None, None, 1, d), exp_map),
            ],
            out_specs=pl.BlockSpec((EXPERT_ROWS, d), lambda b, *_: (b, 0)),
            scratch_shapes=[
                pltpu.VMEM((2, d, de2), F32), pltpu.VMEM((2, de, d), F32),
                pltpu.VMEM((d, de2), BF16), pltpu.VMEM((de, d), BF16),
                pltpu.SemaphoreType.DMA((2, 2)),
            ],
        ),
        compiler_params=_cparams(("arbitrary",)),
        name="moe_experts",
    )(tabs["blk_exp"], tabs["blk_first"], tabs["blk_slot"], tabs["blk_next"], tabs["n_blocks"],
      xs, wgu, bgu.reshape(depth, n_exp, 1, de2), wdn, bdn.reshape(depth, n_exp, 1, d))


def _combine_kernel(final_norm, n_exp, loc0_ref, glb0_ref, len_ref, used_ref,
                    route_ref, x1_ref, g2_ref, fg_ref, ys_ref, out_ref, loc_ref, sem):
    i = pl.program_id(0)
    n = pl.num_programs(0)
    slot = i % 2
    tile = x1_ref.shape[0]
    loc_rows = loc_ref.shape[1]

    def issue_tile(t, sl):
        def issue_run(e, priority):
            rows = _aligned(len_ref[t * n_exp + e])

            @pl.when(rows > 0)
            def _():
                pltpu.make_async_copy(
                    ys_ref.at[pl.ds(_aligned(glb0_ref[t * n_exp + e]), rows)],
                    loc_ref.at[sl, pl.ds(_aligned(loc0_ref[t * n_exp + e]), rows)],
                    sem.at[sl]).start(priority=priority)

        def issue_pair(j, carry):
            for priority in range(DMA_PRIORITIES):
                issue_run(j * DMA_PRIORITIES + priority, priority)
            return carry
        lax.fori_loop(0, n_exp // DMA_PRIORITIES, issue_pair, 0)

    @pl.when(i == 0)
    def _():
        issue_tile(0, 0)

    @pl.when(i + 1 < n)
    def _():
        issue_tile(i + 1, 1 - slot)

    route = route_ref[...]
    r_iota = lax.broadcasted_iota(I32, (tile, loc_rows), 1)
    comb = jnp.zeros((tile, loc_rows), F32)
    for k in range(TOP_K):
        dest = route[:, k:k + 1].astype(I32)
        comb = jnp.where(r_iota == dest, route[:, TOP_K + k:TOP_K + k + 1], comb)

    used_rows = _aligned(used_ref[i])

    @pl.when(used_rows > 0)
    def _():
        pltpu.make_async_copy(ys_ref.at[pl.ds(0, used_rows)],
                              loc_ref.at[slot, pl.ds(0, used_rows)], sem.at[slot]).wait()

    row_ok = lax.broadcasted_iota(I32, (loc_rows, 1), 0) < used_rows
    yl = jnp.where(row_ok, loc_ref[slot], 0.0).astype(BF16)
    moe = jnp.dot(comb.astype(BF16), yl, preferred_element_type=F32)
    out = x1_ref[...] + g2_ref[...] * moe
    if final_norm:
        out = _rms(out, fg_ref[...])
    out_ref[...] = out


def _combine(ys, route, x1, gate2, final_g, tabs, n_exp, loc_rows, final_norm):
    b, s, d = x1.shape
    n_tok = b * s
    n_tiles = n_tok // MOE_TILE
    tiles_per_seq = s // MOE_TILE
    out = pl.pallas_call(
        functools.partial(_combine_kernel, final_norm, n_exp),
        out_shape=jax.ShapeDtypeStruct((n_tok, d), F32),
        grid_spec=pltpu.PrefetchScalarGridSpec(
            num_scalar_prefetch=4,
            grid=(n_tiles,),
            in_specs=[
                pl.BlockSpec((MOE_TILE, LANES), lambda i, *_: (i, 0)),
                pl.BlockSpec((MOE_TILE, d), lambda i, *_: (i, 0)),
                pl.BlockSpec((None, 1, d), lambda i, *_: (i // tiles_per_seq, 0, 0)),
                pl.BlockSpec((1, d), lambda i, *_: (0, 0)),
                pl.BlockSpec(memory_space=pl.ANY),
            ],
            out_specs=pl.BlockSpec((MOE_TILE, d), lambda i, *_: (i, 0)),
            scratch_shapes=[
                pltpu.VMEM((2, loc_rows, d), F32),
                pltpu.SemaphoreType.DMA((2,)),
            ],
        ),
        compiler_params=_cparams(("arbitrary",)),
        name="moe_combine",
    )(tabs["run_loc"], tabs["run_glb"], tabs["run_len"], tabs["rows_used"],
      route, x1.reshape(n_tok, d), gate2, final_g, ys)
    return out.reshape(b, s, d)


def _moe(x1, hn2, destt, route, cnt, gate2, final_g, layer, wgu, bgu, wdn, bdn, final_norm):
    b, s, d = x1.shape
    n_tok = b * s
    n_exp = wgu.shape[1]
    assert n_exp % DMA_PRIORITIES == 0
    loc_rows, max_blocks = _moe_dims(n_tok, n_exp)
    tabs = _routing_tables(cnt, max_blocks)
    xs = _dispatch(hn2.reshape(n_tok, d), destt, tabs, n_exp, loc_rows, max_blocks)
    ys = _experts(xs, tabs, layer, wgu, bgu, wdn, bdn)
    return _combine(ys, route, x1, gate2, final_g, tabs, n_exp, loc_rows, final_norm)


def _qkv_kernel(n_heads, rope, scale, x_ref, trig_ref, kvg_ref, wdkv_ref, ckvg_ref, wuk_ref, wuv_ref,
                n1g_ref, sh1_ref, sc1_ref, wdq_ref, cqg_ref, wuq_ref,
                q_ref, k_ref, v_ref):
    x = x_ref[...]
    half = rope // 2
    trig = trig_ref[...]
    lane = lax.broadcasted_iota(I32, trig.shape, 1)
    cc = jnp.where(lane < half, trig,
                   jnp.where(lane < rope, pltpu.roll(trig, half, axis=1), 0.0))
    ss = jnp.where(lane < half, -pltpu.roll(trig, LANES - half, axis=1),
                   jnp.where(lane < rope, trig, 0.0))
    r_kv = ckvg_ref.shape[1]
    hk = _rms(x, kvg_ref[...]).astype(BF16)
    lat = jnp.dot(hk, wdkv_ref[...], preferred_element_type=F32)
    ckv = _rms(lat[:, :r_kv], ckvg_ref[...]).astype(BF16)
    krot = lat[:, r_kv:r_kv + LANES] * cc + lat[:, r_kv + LANES:r_kv + 2 * LANES] * ss
    kn = jnp.dot(ckv, wuk_ref[...], preferred_element_type=F32)
    vt = lax.dot_general(wuv_ref[...], ckv, (((1,), (1,)), ((), ())), preferred_element_type=F32)
    hq = (_rms(x, n1g_ref[...]) * (1.0 + sc1_ref[...]) + sh1_ref[...]).astype(BF16)
    cq = _rms(jnp.dot(hq, wdq_ref[...], preferred_element_type=F32), cqg_ref[...]).astype(BF16)
    qq = jnp.dot(cq, wuq_ref[...], preferred_element_type=F32) * scale
    hd = n_heads * LANES
    ts = x.shape[0]
    ones_rows = (lax.broadcasted_iota(I32, (V_ONES_ROWS, ts), 0) == 0).astype(BF16)
    for h in range(n_heads):
        sl = slice(h * LANES, (h + 1) * LANES)
        k_ref[h, :, 0:LANES] = kn[:, sl].astype(BF16)
        k_ref[h, :, LANES:2 * LANES] = krot.astype(BF16)
        v_ref[h, 0:LANES, :] = vt[h * LANES:(h + 1) * LANES, :].astype(BF16)
        v_ref[h, LANES:LANES + V_ONES_ROWS, :] = ones_rows
        q_ref[h, :, 0:LANES] = qq[:, sl].astype(BF16)
    per_tile = LANES // rope
    cc_rep, ss_rep = cc, ss
    for u in range(1, per_tile):
        cc_rep = cc_rep + pltpu.roll(cc, u * rope, axis=1)
        ss_rep = ss_rep + pltpu.roll(ss, u * rope, axis=1)
    in_head = lax.broadcasted_iota(I32, (ts, LANES), 1) < rope
    n_rope = n_heads * rope
    for j in range(n_heads // per_tile):
        cols = slice(hd + j * LANES, hd + (j + 1) * LANES)
        swapped = slice(hd + n_rope + j * LANES, hd + n_rope + (j + 1) * LANES)
        rot = qq[:, cols] * cc_rep + qq[:, swapped] * ss_rep
        for u in range(per_tile):
            piece = rot if u == 0 else pltpu.roll(rot, LANES - u * rope, axis=1)
            q_ref[j * per_tile + u, :, LANES:2 * LANES] = jnp.where(in_head, piece, 0.0).astype(BF16)


def _qkv(x, trig, kvg, wdkv, ckvg, wuk, wuv, n1g, sh1, sc1, wdq, cqg, wuq, n_heads, rope, scale, ts):
    b, s, d = x.shape
    assert LANES % rope == 0 and n_heads % (LANES // rope) == 0
    full = lambda a: pl.BlockSpec(a.shape, lambda i, j: (0,) * a.ndim)
    hspec = lambda w: pl.BlockSpec((None, n_heads, ts, w), lambda i, j: (i, 0, j, 0))
    return pl.pallas_call(
        functools.partial(_qkv_kernel, n_heads, rope, scale),
        out_shape=[
            jax.ShapeDtypeStruct((b, n_heads, s, 2 * LANES), BF16),
            jax.ShapeDtypeStruct((b, n_heads, s, 2 * LANES), BF16),
            jax.ShapeDtypeStruct((b, n_heads, s // ts, LANES + V_ONES_ROWS, ts), BF16),
        ],
        grid=(b, s // ts),
        in_specs=[
            pl.BlockSpec((None, ts, d), lambda i, j: (i, j, 0)),
            pl.BlockSpec((None, ts, LANES), lambda i, j: (i, j, 0)),
            full(kvg), full(wdkv), full(ckvg), full(wuk), full(wuv),
            full(n1g), _bvec_spec(d), _bvec_spec(d), full(wdq), full(cqg), full(wuq),
        ],
        out_specs=[hspec(2 * LANES), hspec(2 * LANES),
                   pl.BlockSpec((None, n_heads, None, LANES + V_ONES_ROWS, ts),
                                lambda i, j: (i, 0, j, 0, 0))],
        compiler_params=_cparams(("arbitrary", "arbitrary")),
        name="mla_qkv",
    )(x, trig, kvg, wdkv, ckvg, wuk, wuv, n1g, sh1, sc1, wdq, cqg, wuq)


def _attn_kernel(q_ref, k_ref, vt_ref, o_ref, m_ref, acc_ref):
    tq = q_ref.shape[0]
    tk = vt_ref.shape[2]
    ratio = tq // tk
    dv = o_ref.shape[1]
    qi = pl.program_id(2)
    m_ref[...] = jnp.full(m_ref.shape, -jnp.inf, F32)
    acc_ref[...] = jnp.zeros(acc_ref.shape, F32)

    def scores(ki, cols, on_diagonal):
        start = pl.multiple_of(ki * tk, tk)
        k = k_ref[pl.ds(start, tk), :]
        st = lax.dot_general(k, q_ref[cols, :], (((1,), (1,)), ((), ())),
                             preferred_element_type=F32)
        if on_diagonal:
            kc = lax.broadcasted_iota(I32, st.shape, 0) // CHUNK
            qc = lax.broadcasted_iota(I32, st.shape, 1) // CHUNK
            st = jnp.where(kc <= qc, st, -jnp.inf)
        return st

    def update(ki, st, m_old, acc_old):
        m_new = jnp.maximum(m_old, jnp.max(st, axis=0, keepdims=True))
        p = jnp.exp2(st - m_new).astype(BF16)
        alpha = jnp.exp2(m_old - m_new)
        acc_new = alpha * acc_old + jnp.dot(vt_ref[ki], p, preferred_element_type=F32)
        return m_new, acc_new

    def step(groups):
        sts = [[scores(ki, cols, on_diagonal) for ki, on_diagonal in tiles]
               for cols, tiles in groups]
        for (cols, tiles), group_sts in zip(groups, sts):
            state = (m_ref[:, cols], acc_ref[:, cols])
            for (ki, _), st in zip(tiles, group_sts):
                state = update(ki, st, *state)
            m_ref[:, cols], acc_ref[:, cols] = state

    n_below = qi * ratio
    everything = slice(0, tq)
    step([(slice(g * tk, (g + 1) * tk), [(n_below + a, a == g) for a in range(g + 1)])
          for g in range(ratio)])

    def fast_step(first, count):
        m = m_ref[...]
        pv = None
        top = None
        for t in range(count):
            st = scores(first + t, everything, False)
            tmax = jnp.max(st, axis=0, keepdims=True)
            top = tmax if top is None else jnp.maximum(top, tmax)
            p = jnp.exp2(st - m).astype(BF16)
            part = jnp.dot(vt_ref[first + t], p, preferred_element_type=F32)
            pv = part if pv is None else pv + part
        safe = jnp.max(top - m) <= ATTN_MAX_SLACK

        @pl.when(safe)
        def _():
            acc_ref[...] += pv

        @pl.when(jnp.logical_not(safe))
        def _():
            def redo(t, carry):
                step([(everything, [(first + t, False)])])
                return carry
            lax.fori_loop(0, count, redo, 0)

    def body(j, carry):
        fast_step(ATTN_UNROLL * j, ATTN_UNROLL)
        return carry
    trips = n_below // ATTN_UNROLL
    lax.fori_loop(0, trips, body, 0)

    for rem in range(math.gcd(ratio, ATTN_UNROLL), ATTN_UNROLL, math.gcd(ratio, ATTN_UNROLL)):
        @pl.when(n_below % ATTN_UNROLL == rem)
        def _(rem=rem):
            fast_step(trips * ATTN_UNROLL, rem)

    o_ref[...] = (acc_ref[0:dv, :] / acc_ref[dv:dv + 1, :]).T.astype(o_ref.dtype)


def _attention(q, k, vt, tq):
    b, h, s, dk = q.shape
    nk, dv_ext, tk = vt.shape[2:]
    dv = dv_ext - V_ONES_ROWS
    assert tk % CHUNK == 0 and s % tq == 0 and tq % tk == 0
    return pl.pallas_call(
        _attn_kernel,
        out_shape=jax.ShapeDtypeStruct((b, s, h * dv), BF16),
        grid=(b, h, s // tq),
        in_specs=[
            pl.BlockSpec((None, None, tq, dk), lambda i, j, t: (i, j, t, 0)),
            pl.BlockSpec((None, None, s, dk), lambda i, j, t: (i, j, 0, 0)),
            pl.BlockSpec((None, None, nk, dv_ext, tk), lambda i, j, t: (i, j, 0, 0, 0)),
        ],
        out_specs=pl.BlockSpec((None, tq, dv), lambda i, j, t: (i, t, j)),
        scratch_shapes=[pltpu.VMEM((1, tq), F32), pltpu.VMEM((dv_ext, tq), F32)],
        compiler_params=_cparams(("arbitrary", "arbitrary", "arbitrary")),
        name="mla_attention",
    )(q, k, vt)


def _swap_halves(w):
    half = w.shape[-1] // 2
    return jnp.concatenate([w[..., half:], w[..., :half]], axis=-1)


def _pad_lanes(w):
    pad = LANES - w.shape[-1]
    return jnp.concatenate([w, jnp.zeros(w.shape[:-1] + (pad,), w.dtype)], axis=-1)


def kernel(x, c, positions, mod_w, mod_b, norm1_g, norm2_g, conv_pw1_w, conv_pw1_b, conv_dw_w, conv_dw_b, conv_ln_g, conv_ln_b, conv_pw2_w, conv_pw2_b, kv_norm_g, w_dkv, ckv_norm_g, w_uk, w_uv, w_dq, cq_norm_g, w_uq, w_o, router_w, router_b, exp_w_gu, exp_b_gu, exp_w_dn, exp_b_dn, final_g):
    b, s, d = x.shape
    n_heads, nope = w_uk.shape[1], w_uk.shape[2]
    r_kv = ckv_norm_g.shape[0]
    rope = w_dkv.shape[1] - r_kv
    vdim = w_uv.shape[2]
    n_exp = router_w.shape[2]
    assert nope == LANES and vdim == LANES and rope <= LANES and d % LANES == 0
    ts = min(512, s)
    tq = min(1024, s)

    mod = _modulation(c, mod_w, mod_b)
    mods = [[m.reshape(b, 1, d) for m in jnp.split(mod[l], 6, axis=-1)] for l in range(2)]
    row = lambda v: v.reshape(1, -1)

    half = rope // 2
    inv = jnp.exp(-(2.0 * math.log(ROPE_THETA) / rope) * jnp.arange(half, dtype=F32))
    ang = positions.astype(F32)[..., None] * inv
    trig = _pad_lanes(jnp.concatenate([jnp.cos(ang), jnp.sin(ang)], axis=-1))

    sh1, sc1, gt1, sh2, sc2, gt2 = mods[0]
    x1, hn2, destt, route, cnt = _mix0(
        x, row(norm1_g[0]), sh1, sc1, gt1,
        conv_pw1_w[0].astype(BF16), row(conv_pw1_b[0]), conv_dw_w[0], row(conv_dw_b[0]),
        row(conv_ln_g[0]), row(conv_ln_b[0]), conv_pw2_w[0].astype(BF16), row(conv_pw2_b[0]),
        row(norm2_g[0]), sh2, sc2, router_w[0].T, router_b[0].reshape(n_exp, 1), ts)
    cnt = cnt[:, :ts // MOE_TILE].reshape(-1, n_exp)
    x2 = _moe(x1, hn2, destt, route, cnt, gt2, row(final_g),
              0, exp_w_gu, exp_b_gu, exp_w_dn, exp_b_dn, False)

    sh1, sc1, gt1, sh2, sc2, gt2 = mods[1]
    wdkv_rope = w_dkv[:, r_kv:]
    wdkv_ext = jnp.concatenate(
        [w_dkv[:, :r_kv], _pad_lanes(wdkv_rope), _pad_lanes(_swap_halves(wdkv_rope))], axis=-1)
    wuq = w_uq[0]
    r_q = wuq.shape[0]
    wuq_rope = wuq[:, :, nope:]
    wuq_ext = jnp.concatenate([
        wuq[:, :, :nope].reshape(r_q, n_heads * LANES),
        wuq_rope.reshape(r_q, n_heads * rope),
        _swap_halves(wuq_rope).reshape(r_q, n_heads * rope)], axis=-1)
    scale = float((nope + rope) ** -0.5 * math.log2(math.e))
    q, k, v = _qkv(
        x2, trig, row(kv_norm_g), wdkv_ext.astype(BF16), row(ckv_norm_g),
        w_uk.reshape(r_kv, n_heads * nope).astype(BF16), w_uv.reshape(r_kv, n_heads * vdim).T.astype(BF16),
        row(norm1_g[1]), sh1, sc1, w_dq[0].astype(BF16), row(cq_norm_g[0]), wuq_ext.astype(BF16),
        n_heads, rope, scale, ts)
    o = _attention(q, k, v, tq)

    x3, hn2, destt, route, cnt = _mix1(
        x2, o, gt1, w_o[0].astype(BF16), row(norm2_g[1]), sh2, sc2,
        router_w[1].T, router_b[1].reshape(n_exp, 1), ts)
    cnt = cnt[:, :ts // MOE_TILE].reshape(-1, n_exp)
    return _moe(x3, hn2, destt, route, cnt, gt2, row(final_g),
                1, exp_w_gu, exp_b_gu, exp_w_dn, exp_b_dn, True)
```
